```python
import jax
import jax.numpy as jnp
from jax import lax
import numpy as np

D_MODEL = 1024
BATCH = 16
SEQ = 256
DEPTH = 4
DEC_BATCH = 8
DEC_SEQ = 1024
PAST_LEN = 256

GRID_W = 64
HEAD_DIM = 64
NA_HEADS = 4
GQA_Q_HEADS = 8
GQA_KV_HEADS = 2
MLSTM_HEADS = 4
NA_WIN_ROWS = 8
NA_WIN_COLS = 16
Q_BLOCK = 128
MLSTM_CHUNK = 64
ROPE_BASE = 10000.0
EPS = 1e-6
NEG = -1e30
NA_W = NA_HEADS * HEAD_DIM
GQA_QW = GQA_Q_HEADS * HEAD_DIM
GQA_KW = GQA_KV_HEADS * HEAD_DIM
ML_W = MLSTM_HEADS * HEAD_DIM
N_GATES = 4 * MLSTM_HEADS
MIX_WIDTH = NA_W + GQA_QW + ML_W
IN_SIZES = (NA_W, NA_W, NA_W, GQA_QW, GQA_KW, GQA_KW, ML_W, ML_W, ML_W, ML_W, N_GATES)
IN_WIDTH = 3 * NA_W + GQA_QW + 2 * GQA_KW + 4 * ML_W + N_GATES
FF_HIDDEN = ((8 * D_MODEL + 3 * 256 - 1) // (3 * 256)) * 256

kernel_name = 'hybrid_na_gqa_mlstm_diffusion_step'


def rms_norm(x, g):
    xf = x.astype(jnp.float32)
    y = xf * lax.rsqrt(jnp.mean(xf * xf, axis=-1, keepdims=True) + EPS)
    return (y * g.astype(jnp.float32)).astype(x.dtype)


def modulate(h, shift, scale):
    return h * (1 + scale) + shift


def adaln(cvec, w, b):
    mod = jnp.einsum('nd,de->ne', jax.nn.silu(cvec), w) + b
    return jnp.split(mod[:, None, :], 6, axis=-1)


def split_heads(x, n_heads):
    B, T, _ = x.shape
    return x.reshape(B, T, n_heads, HEAD_DIM).transpose(0, 2, 1, 3)


def merge_heads(x):
    B, H, T, d = x.shape
    return x.transpose(0, 2, 1, 3).reshape(B, T, H * d)


def rope_2d(x):
    T = x.shape[2]
    half = HEAD_DIM // 2
    quarter = half // 2
    inv = 1.0 / (ROPE_BASE ** (jnp.arange(quarter, dtype=jnp.float32) / quarter))
    t = jnp.arange(T)
    row = (t // GRID_W).astype(jnp.float32)
    col = (t % GRID_W).astype(jnp.float32)

    def rot(xh, pos):
        ang = pos[:, None] * inv[None, :]
        cos, sin = jnp.cos(ang), jnp.sin(ang)
        x1 = xh[..., :quarter].astype(jnp.float32)
        x2 = xh[..., quarter:].astype(jnp.float32)
        return jnp.concatenate([x1 * cos - x2 * sin, x2 * cos + x1 * sin], axis=-1)

    out = jnp.concatenate([rot(x[..., :half], row), rot(x[..., half:], col)], axis=-1)
    return out.astype(x.dtype)


def block_attention(q, k, v):
    B, Hq, T, d = q.shape
    Hk = k.shape[1]
    G = Hq // Hk
    NB = T // Q_BLOCK
    qb = q.reshape(B, Hk, G, NB, Q_BLOCK, d).transpose(3, 0, 1, 2, 4, 5)
    scale = d ** -0.5

    def one_block(qi):
        s = jnp.einsum('bhgqd,bhkd->bhgqk', qi, k).astype(jnp.float32) * scale
        p = jax.nn.softmax(s, axis=-1).astype(v.dtype)
        return jnp.einsum('bhgqk,bhkd->bhgqd', p, v)

    o = lax.map(one_block, qb)
    return o.transpose(1, 2, 3, 0, 4, 5).reshape(B, Hq, T, d)


def na_geometry(rows):
    kr = min(NA_WIN_ROWS, rows)
    r = np.arange(rows)
    r0 = np.clip(r - kr // 2, 0, rows - kr)
    row_idx = r0[:, None] + np.arange(kr)[None, :]
    cq = np.arange(GRID_W)
    c0 = np.clip(cq - NA_WIN_COLS // 2, 0, GRID_W - NA_WIN_COLS)
    col_mask = (cq[None, :] >= c0[:, None]) & (cq[None, :] < c0[:, None] + NA_WIN_COLS)
    dr = row_idx - r[:, None] + NA_WIN_ROWS - 1
    dc = np.clip(cq[None, :] - cq[:, None], -(NA_WIN_COLS - 1), NA_WIN_COLS - 1) + NA_WIN_COLS - 1
    return row_idx, col_mask, dr, dc


def na_latent(q, k, v, kc, vc, bias_table):
    B, H, T, d = q.shape
    rows = T // GRID_W
    row_idx, col_mask, dr, dc = na_geometry(rows)
    kr = row_idx.shape[1]
    scale = d ** -0.5
    qg = q.reshape(B, H, rows, GRID_W, d)
    kg = k.reshape(B, H, rows, GRID_W, d)[:, :, row_idx]
    vg = v.reshape(B, H, rows, GRID_W, d)[:, :, row_idx]
    s_win = jnp.einsum('bhrqd,bhrkwd->bhrqkw', qg, kg).astype(jnp.float32) * scale
    bias = bias_table[:, dr[:, :, None, None], dc[None, None, :, :]].transpose(0, 1, 3, 2, 4)
    s_win = jnp.where(col_mask[:, None, :], s_win + bias.astype(jnp.float32), NEG)
    s_ctx = jnp.einsum('bhrqd,bhld->bhrql', qg, kc).astype(jnp.float32) * scale
    nw = kr * GRID_W
    s = jnp.concatenate([s_win.reshape(B, H, rows, GRID_W, nw), s_ctx], axis=-1)
    p = jax.nn.softmax(s, axis=-1).astype(v.dtype)
    p_win = p[..., :nw].reshape(B, H, rows, GRID_W, kr, GRID_W)
    o = (jnp.einsum('bhrqkw,bhrkwd->bhrqd', p_win, vg)
         + jnp.einsum('bhrql,bhld->bhrqd', p[..., nw:], vc))
    return o.reshape(B, H, T, d)


def mlstm_scan(q, k, v, log_i, log_f, C0, n0, m0):
    B, H, T, d = q.shape
    L = MLSTM_CHUNK
    NC = T // L
    to_chunks = lambda x: jnp.moveaxis(x.reshape(B, H, NC, L, *x.shape[3:]), 2, 0)
    causal = jnp.tril(jnp.ones((L, L), dtype=bool))

    def chunk_step(carry, inp):
        C, n, m = carry
        qc, kc, vc, ic, fc = inp
        b = jnp.cumsum(fc, axis=-1)
        logw = jnp.where(causal, b[..., :, None] - b[..., None, :] + ic[..., None, :], -jnp.inf)
        m_inter = b + m[..., None]
        m_t = jnp.maximum(jnp.max(logw, axis=-1), m_inter)
        w = jnp.exp(logw - m_t[..., None])
        s = jnp.einsum('bhtk,bhsk->bhts', qc, kc) * w
        decay = jnp.exp(m_inter - m_t)
        num = (jnp.einsum('bhts,bhsv->bhtv', s, vc)
               + decay[..., None] * jnp.einsum('bhvk,bhtk->bhtv', C, qc))
        den = jnp.sum(s, axis=-1) + decay * jnp.einsum('bhk,bhtk->bht', n, qc)
        h = num / jnp.maximum(jnp.abs(den), jnp.exp(-m_t))[..., None]
        b_end = b[..., -1]
        logw_end = b_end[..., None] - b + ic
        m_new = jnp.maximum(b_end + m, jnp.max(logw_end, axis=-1))
        w_end = jnp.exp(logw_end - m_new[..., None])
        carry_decay = jnp.exp(b_end + m - m_new)
        C_new = carry_decay[..., None, None] * C + jnp.einsum('bhs,bhsv,bhsk->bhvk', w_end, vc, kc)
        n_new = carry_decay[..., None] * n + jnp.einsum('bhs,bhsk->bhk', w_end, kc)
        return (C_new, n_new, m_new), h

    xs = (to_chunks(q), to_chunks(k), to_chunks(v), to_chunks(log_i), to_chunks(log_f))
    (C, n, m), hs = lax.scan(chunk_step, (C0, n0, m0), xs)
    h = jnp.moveaxis(hs, 0, 2).reshape(B, H, T, d)
    return h, (C, n, m)


def mlstm_bidir(q, k, v, gates, st_fwd, st_bwd):
    B, T, _ = gates.shape
    g = gates.astype(jnp.float32).reshape(B, T, 4, MLSTM_HEADS).transpose(2, 0, 3, 1)
    f32 = lambda st: tuple(s.astype(jnp.float32) for s in st)
    h_f, fin_f = mlstm_scan(q, k, v, g[0], jax.nn.log_sigmoid(g[1]), *f32(st_fwd))
    flip = lambda x: jnp.flip(x, axis=2)
    h_b, fin_b = mlstm_scan(flip(q), flip(k), flip(v), flip(g[2]), flip(jax.nn.log_sigmoid(g[3])), *f32(st_bwd))
    return h_f + flip(h_b), fin_f, fin_b


def mlstm_output(h, o_pre, gain):
    B, H, T, d = h.shape
    hf = h.transpose(0, 2, 1, 3)
    hn = hf * lax.rsqrt(jnp.mean(hf * hf, axis=-1, keepdims=True) + EPS) * gain.reshape(H, d).astype(jnp.float32)
    return (hn.reshape(B, T, H * d) * jax.nn.sigmoid(o_pre.astype(jnp.float32))).astype(o_pre.dtype)


def token_mixers(h, w_in_l, b_gates_l, w_out_l, g_qk_l, g_ml_l, na_bias_l, ctx):
    offs = [int(o) for o in np.cumsum(IN_SIZES)[:-1]]
    z = jnp.einsum('btd,de->bte', h, w_in_l)
    nq, nk, nv, gq, gk, gv, mq, mk, mv, mo, mg = jnp.split(z, offs, axis=-1)
    mg = mg + b_gates_l
    nq, nk, nv = split_heads(nq, NA_HEADS), split_heads(nk, NA_HEADS), split_heads(nv, NA_HEADS)
    gq = rms_norm(split_heads(gq, GQA_Q_HEADS), g_qk_l[0])
    gk = rms_norm(split_heads(gk, GQA_KV_HEADS), g_qk_l[1])
    gv = split_heads(gv, GQA_KV_HEADS)
    mq = split_heads(mq, MLSTM_HEADS).astype(jnp.float32)
    mk = split_heads(mk, MLSTM_HEADS).astype(jnp.float32) * (HEAD_DIM ** -0.5)
    mv = split_heads(mv, MLSTM_HEADS).astype(jnp.float32)
    B = h.shape[0]
    if ctx is None:
        o_na = block_attention(nq, nk, nv)
        o_gqa = block_attention(gq, gk, gv)
        zero = (jnp.zeros((B, MLSTM_HEADS, HEAD_DIM, HEAD_DIM), jnp.float32),
                jnp.zeros((B, MLSTM_HEADS, HEAD_DIM), jnp.float32),
                jnp.zeros((B, MLSTM_HEADS), jnp.float32))
        st_f0, st_b0 = zero, zero
    else:
        na_kv, gqa_kv, C, n, m = ctx
        o_na = na_latent(nq, nk, nv, na_kv[:, 0], na_kv[:, 1], na_bias_l)
        k_all = jnp.concatenate([rope_2d(gk), gqa_kv[:, 0].astype(gk.dtype)], axis=2)
        v_all = jnp.concatenate([gv, gqa_kv[:, 1].astype(gv.dtype)], axis=2)
        o_gqa = block_attention(rope_2d(gq), k_all, v_all)
        st_f0 = (C[:, 0], n[:, 0], m[:, 0])
        st_b0 = (C[:, 1], n[:, 1], m[:, 1])
    h_ml, fin_f, fin_b = mlstm_bidir(mq, mk, mv, mg, st_f0, st_b0)
    o_ml = mlstm_output(h_ml, mo, g_ml_l)
    mix = jnp.concatenate([merge_heads(o_na), merge_heads(o_gqa), o_ml], axis=-1)
    out = jnp.einsum('bte,ed->btd', mix, w_out_l)
    if ctx is None:
        new_ctx = (jnp.stack([nk, nv], axis=1), jnp.stack([gk, gv], axis=1),
                   jnp.stack([fin_f[0], fin_b[0]], axis=1), jnp.stack([fin_f[1], fin_b[1]], axis=1),
                   jnp.stack([fin_f[2], fin_b[2]], axis=1))
    else:
        new_ctx = None
    return out, new_ctx


def swiglu(h, w_gu, w_down):
    gate, up = jnp.split(jnp.einsum('btd,df->btf', h, w_gu), 2, axis=-1)
    return jnp.einsum('btf,fd->btd', jax.nn.silu(gate) * up, w_down)


def trunk_layer(x, mods, mixer_w, g_norm_l, w_gu_l, w_down_l, ctx):
    sh1, sc1, ga1, sh2, sc2, ga2 = mods
    h = modulate(rms_norm(x, g_norm_l[0]), sh1, sc1)
    mix, new_ctx = token_mixers(h, *mixer_w, ctx)
    x = x + ga1 * rms_norm(mix, g_norm_l[1])
    f = swiglu(modulate(rms_norm(x, g_norm_l[2]), sh2, sc2), w_gu_l, w_down_l)
    x = x + ga2 * rms_norm(f, g_norm_l[3])
    return x, new_ctx


def setup_inputs(seed: int = 0) -> dict:
    key = jax.random.key(seed)
    ks = jax.random.split(key, 20)
    nrm = lambda k, shape, s: jax.random.normal(k, shape, jnp.float32) * s
    gate_offset = jnp.array([0.0, 3.0, 0.0, 3.0], jnp.float32)[None, :, None]
    return {
        'x_prompt': nrm(ks[0], (BATCH, SEQ, D_MODEL), 1.0),
        'x_sample': nrm(ks[1], (DEC_BATCH, DEC_SEQ, D_MODEL), 1.0),
        'cache_na_kv': nrm(ks[2], (DEC_BATCH, DEPTH, 2, NA_HEADS, PAST_LEN, HEAD_DIM), 1.0),
        'cache_gqa_kv': nrm(ks[3], (DEC_BATCH, DEPTH, 2, GQA_KV_HEADS, PAST_LEN, HEAD_DIM), 1.0),
        'state_mlstm_C': nrm(ks[4], (DEC_BATCH, DEPTH, 2, MLSTM_HEADS, HEAD_DIM, HEAD_DIM), HEAD_DIM ** -0.5),
        'state_mlstm_n': nrm(ks[5], (DEC_BATCH, DEPTH, 2, MLSTM_HEADS, HEAD_DIM), 0.5),
        'state_mlstm_m': nrm(ks[6], (DEC_BATCH, DEPTH, 2, MLSTM_HEADS), 1.0),
        'c': nrm(ks[7], (DEC_BATCH, D_MODEL), 1.0),
        'c_ctx': nrm(ks[8], (D_MODEL,), 1.0),
        'w_in': nrm(ks[9], (DEPTH, D_MODEL, IN_WIDTH), D_MODEL ** -0.5),
        'b_gates': (nrm(ks[10], (DEPTH, 4, MLSTM_HEADS), 0.1) + gate_offset).reshape(DEPTH, N_GATES),
        'w_out': nrm(ks[11], (DEPTH, MIX_WIDTH, D_MODEL), MIX_WIDTH ** -0.5),
        'g_norm': 1.0 + nrm(ks[12], (DEPTH, 4, D_MODEL), 0.02),
        'g_qk': 1.0 + nrm(ks[13], (DEPTH, 2, HEAD_DIM), 0.02),
        'g_mlstm': 1.0 + nrm(ks[14], (DEPTH, ML_W), 0.02),
        'na_bias': nrm(ks[15], (DEPTH, NA_HEADS, 2 * NA_WIN_ROWS - 1, 2 * NA_WIN_COLS - 1), 0.02),
        'w_ada': nrm(ks[16], (DEPTH, D_MODEL, 6 * D_MODEL), 0.5 * D_MODEL ** -0.5),
        'b_ada': nrm(ks[17], (DEPTH, 6 * D_MODEL), 0.02),
        'w_gu': nrm(ks[18], (DEPTH, D_MODEL, 2 * FF_HIDDEN), D_MODEL ** -0.5),
        'w_down': nrm(ks[19], (DEPTH, FF_HIDDEN, D_MODEL), FF_HIDDEN ** -0.5),
    }


def reference(x_prompt, x_sample, cache_na_kv, cache_gqa_kv, state_mlstm_C, state_mlstm_n, state_mlstm_m,
              c, c_ctx, w_in, b_gates, w_out, g_norm, g_qk, g_mlstm, na_bias, w_ada, b_ada, w_gu, w_down):
    xp, xs = x_prompt, x_sample
    na_l, gqa_l, C_l, n_l, m_l = [], [], [], [], []
    for l in range(DEPTH):
        mixer_w = (w_in[l], b_gates[l], w_out[l], g_qk[l], g_mlstm[l], na_bias[l])
        xp, (na_kv, gqa_kv, Cs, ns, ms) = trunk_layer(
            xp, adaln(c_ctx[None, :], w_ada[l], b_ada[l]), mixer_w, g_norm[l], w_gu[l], w_down[l], None)
        na_l.append(na_kv)
        gqa_l.append(gqa_kv)
        C_l.append(Cs)
        n_l.append(ns)
        m_l.append(ms)
        ctx = (cache_na_kv[:, l], cache_gqa_kv[:, l], state_mlstm_C[:, l], state_mlstm_n[:, l], state_mlstm_m[:, l])
        xs, _ = trunk_layer(xs, adaln(c, w_ada[l], b_ada[l]), mixer_w, g_norm[l], w_gu[l], w_down[l], ctx)
    new_na_kv = jnp.stack(na_l, axis=1)
    new_gqa_kv = jnp.stack(gqa_l, axis=1)
    new_C = jnp.stack(C_l, axis=1)
    new_n = jnp.stack(n_l, axis=1)
    new_m = jnp.stack(m_l, axis=1)
    return (xp, xs, new_na_kv, new_gqa_kv, new_C, new_n, new_m)
```

```python
import functools

import jax
import jax.numpy as jnp
import numpy as np
from jax import lax
from jax.experimental import pallas as pl
from jax.experimental.pallas import tpu as pltpu

D_MODEL = 1024
DEPTH = 4
GRID_W = 64
HEAD_DIM = 64
NA_HEADS = 4
GQA_Q_HEADS = 8
GQA_KV_HEADS = 2
GQA_GROUP = GQA_Q_HEADS // GQA_KV_HEADS
MLSTM_HEADS = 4
NA_WIN_ROWS = 8
NA_WIN_COLS = 16
MLSTM_CHUNK = 64
ROPE_BASE = 10000.0
EPS = 1e-6
NEG = -1e30
NA_W = NA_HEADS * HEAD_DIM
GQA_QW = GQA_Q_HEADS * HEAD_DIM
GQA_KW = GQA_KV_HEADS * HEAD_DIM
ML_W = MLSTM_HEADS * HEAD_DIM
N_GATES = 4 * MLSTM_HEADS
FF_HIDDEN = ((8 * D_MODEL + 3 * 256 - 1) // (3 * 256)) * 256
QK_SCALE = HEAD_DIM ** -0.5

ZNA_W = 3 * NA_W
ZGQ_W = GQA_QW + 2 * GQA_KW
ZML_W = 4 * ML_W
OFF_GQ = ZNA_W
OFF_ML = ZNA_W + ZGQ_W
OFF_GATES = OFF_ML + ZML_W

LANES = 128
N_MOD_ROWS = 16
NA_BLOCK_ROWS = 2
NA_UNION_ROWS = NA_WIN_ROWS + NA_BLOCK_ROWS - 1

F32 = jnp.float32
BF16 = jnp.bfloat16
VMEM_LIMIT = 52 * 1024 * 1024

NT_DIMS = (((1,), (1,)), ((), ()))
TN_DIMS = (((0,), (0,)), ((), ()))


def _params(n_axes):
    return pltpu.CompilerParams(dimension_semantics=("arbitrary",) * n_axes,
                                vmem_limit_bytes=VMEM_LIMIT)


def _rms(x, g):
    return x * lax.rsqrt(jnp.mean(x * x, axis=-1, keepdims=True) + EPS) * g


def _softmax_av(s_list, v_list):
    m = s_list[0].max(axis=-1, keepdims=True)
    for s in s_list[1:]:
        m = jnp.maximum(m, s.max(axis=-1, keepdims=True))
    acc = None
    den = None
    for s, v in zip(s_list, v_list):
        p = jnp.exp(s - m)
        l = p.sum(axis=-1, keepdims=True)
        o = jnp.dot(p.astype(BF16), v, preferred_element_type=F32)
        acc = o if acc is None else acc + o
        den = l if den is None else den + l
    return acc / den


def _adaln_kernel(c_ref, w_ref, b_ref, o_ref):
    c = c_ref[...]
    a = c * jax.nn.sigmoid(c)
    o_ref[...] = jnp.dot(a.astype(BF16), w_ref[...].astype(BF16),
                         preferred_element_type=F32) + b_ref[...]


def _adaln(cvec, w_ada, b_ada):
    tn = 1536
    return pl.pallas_call(
        _adaln_kernel,
        grid=(DEPTH, 6 * D_MODEL // tn),
        in_specs=[
            pl.BlockSpec((N_MOD_ROWS, D_MODEL), lambda l, j: (0, 0)),
            pl.BlockSpec((None, D_MODEL, tn), lambda l, j: (l, 0, j)),
            pl.BlockSpec((None, 1, tn), lambda l, j: (l, 0, j)),
        ],
        out_specs=pl.BlockSpec((None, N_MOD_ROWS, tn), lambda l, j: (l, 0, j)),
        out_shape=jax.ShapeDtypeStruct((DEPTH, N_MOD_ROWS, 6 * D_MODEL), F32),
        compiler_params=_params(2),
        name="adaln",
    )(cvec, w_ada, b_ada.reshape(DEPTH, 1, 6 * D_MODEL))


def _na_r0(r, rows):
    return min(max(r - NA_WIN_ROWS // 2, 0), rows - NA_WIN_ROWS)


def _na_union_start(p, rows):
    return min(_na_r0(NA_BLOCK_ROWS * p, rows), rows - NA_UNION_ROWS)


def _na_bias_kernel(tbl_ref, o_ref, *, rows):
    l = pl.program_id(0)
    h = pl.program_id(1)
    qi = lax.broadcasted_iota(jnp.int32, (GRID_W, GRID_W), 0)
    ki = lax.broadcasted_iota(jnp.int32, (GRID_W, GRID_W), 1)
    dc = jnp.clip(ki - qi, -(NA_WIN_COLS - 1), NA_WIN_COLS - 1) + NA_WIN_COLS - 1
    c0 = jnp.clip(qi - NA_WIN_COLS // 2, 0, GRID_W - NA_WIN_COLS)
    col_ok = (ki >= c0) & (ki < c0 + NA_WIN_COLS)
    n_dr = 2 * NA_WIN_ROWS - 1
    n_dc = 2 * NA_WIN_COLS - 1
    tiles = []
    for dr in range(n_dr):
        t = jnp.zeros((GRID_W, GRID_W), F32)
        for d in range(n_dc):
            t = jnp.where(dc == d, tbl_ref[((l * NA_HEADS + h) * n_dr + dr) * n_dc + d], t)
        tiles.append(jnp.where(col_ok, t, NEG))
    neg_tile = jnp.full((GRID_W, GRID_W), NEG, F32)
    for p in range(rows // NA_BLOCK_ROWS):
        start = _na_union_start(p, rows)
        for a in range(NA_BLOCK_ROWS):
            r = NA_BLOCK_ROWS * p + a
            r0 = _na_r0(r, rows)
            for j in range(NA_UNION_ROWS):
                kr = start + j
                inside = r0 <= kr < r0 + NA_WIN_ROWS
                tile = tiles[kr - r + NA_WIN_ROWS - 1] if inside else neg_tile
                o_ref[p, a * GRID_W:(a + 1) * GRID_W, j * GRID_W:(j + 1) * GRID_W] = tile


def _na_bias_expand(na_bias, rows):
    n_blocks = rows // NA_BLOCK_ROWS
    qn = NA_BLOCK_ROWS * GRID_W
    kn = NA_UNION_ROWS * GRID_W
    return pl.pallas_call(
        functools.partial(_na_bias_kernel, rows=rows),
        grid=(DEPTH, NA_HEADS),
        in_specs=[pl.BlockSpec(memory_space=pltpu.SMEM)],
        out_specs=pl.BlockSpec((None, None, n_blocks, qn, kn), lambda l, h: (l, h, 0, 0, 0)),
        out_shape=jax.ShapeDtypeStruct((DEPTH, NA_HEADS, n_blocks, qn, kn), F32),
        compiler_params=_params(2),
        name="na_bias_expand",
    )(na_bias.reshape(-1))


def _mod_row_map(layer, tiles_per_row, first_row):
    if tiles_per_row is None:
        return lambda i: (layer, first_row, 0, 0)
    return lambda i: (layer, first_row + i // tiles_per_row, 0, 0)


def _inproj_kernel(x_ref, mod_ref, g_ref, wna_ref, wgq_ref, wml_ref, wgc_ref, wgr_ref, bgc_ref, bgr_ref,
                   zna_ref, zgq_ref, zml_ref, gcol_ref, grow_ref):
    x = x_ref[...]
    h = _rms(x, g_ref[0:1, :]) * (1.0 + mod_ref[1:2, :]) + mod_ref[0:1, :]
    hb = h.astype(BF16)
    zna_ref[...] = jnp.dot(hb, wna_ref[...], preferred_element_type=F32)
    zgq_ref[...] = jnp.dot(hb, wgq_ref[...], preferred_element_type=F32)
    zml_ref[...] = jnp.dot(hb, wml_ref[...], preferred_element_type=F32)
    gc = jnp.dot(hb, wgc_ref[...], preferred_element_type=F32)
    gcol_ref[...] = gc[:, :N_GATES] + bgc_ref[...]
    grow_ref[...] = lax.dot_general(wgr_ref[...], hb, NT_DIMS, preferred_element_type=F32) + bgr_ref[...]


def _inproj(x, mods, g_norm, w, layer, tiles_per_row, first_row, tm):
    n = x.shape[0]
    wspec = lambda width: pl.BlockSpec((None, D_MODEL, width), lambda i: (layer, 0, 0))
    return pl.pallas_call(
        _inproj_kernel,
        grid=(n // tm,),
        in_specs=[
            pl.BlockSpec((tm, D_MODEL), lambda i: (i, 0)),
            pl.BlockSpec((None, None, 6, D_MODEL), _mod_row_map(layer, tiles_per_row, first_row)),
            pl.BlockSpec((None, 4, D_MODEL), lambda i: (layer, 0, 0)),
            wspec(ZNA_W), wspec(ZGQ_W), wspec(ZML_W), wspec(LANES),
            pl.BlockSpec((None, N_GATES, D_MODEL), lambda i: (layer, 0, 0)),
            pl.BlockSpec((None, 1, N_GATES), lambda i: (layer, 0, 0)),
            pl.BlockSpec((None, N_GATES, 1), lambda i: (layer, 0, 0)),
        ],
        out_specs=[
            pl.BlockSpec((tm, ZNA_W), lambda i: (i, 0)),
            pl.BlockSpec((tm, ZGQ_W), lambda i: (i, 0)),
            pl.BlockSpec((tm, ZML_W), lambda i: (i, 0)),
            pl.BlockSpec((tm, N_GATES), lambda i: (i, 0)),
            pl.BlockSpec((N_GATES, tm), lambda i: (0, i)),
        ],
        out_shape=[
            jax.ShapeDtypeStruct((n, ZNA_W), F32),
            jax.ShapeDtypeStruct((n, ZGQ_W), F32),
            jax.ShapeDtypeStruct((n, ZML_W), F32),
            jax.ShapeDtypeStruct((n, N_GATES), F32),
            jax.ShapeDtypeStruct((N_GATES, n), F32),
        ],
        compiler_params=_params(1),
        name="inproj",
    )(x, mods, g_norm, w["na"], w["gq"], w["ml"], w["gate_col"], w["gate_row"], w["b_col"], w["b_row"])


def _outproj_kernel(x_ref, mna_ref, mgq_ref, mml_ref, mod_ref, g_ref, w_ref, o_ref):
    acc = jnp.dot(mna_ref[...], w_ref[0:NA_W, :], preferred_element_type=F32)
    acc += jnp.dot(mgq_ref[...], w_ref[NA_W:NA_W + GQA_QW, :], preferred_element_type=F32)
    acc += jnp.dot(mml_ref[...], w_ref[NA_W + GQA_QW:, :], preferred_element_type=F32)
    o_ref[...] = x_ref[...] + mod_ref[2:3, :] * _rms(acc, g_ref[1:2, :])


def _outproj(x, mna, mgq, mml, mods, g_norm, w_out, layer, tiles_per_row, first_row, tm):
    n = x.shape[0]
    return pl.pallas_call(
        _outproj_kernel,
        grid=(n // tm,),
        in_specs=[
            pl.BlockSpec((tm, D_MODEL), lambda i: (i, 0)),
            pl.BlockSpec((tm, NA_W), lambda i: (i, 0)),
            pl.BlockSpec((tm, GQA_QW), lambda i: (i, 0)),
            pl.BlockSpec((tm, ML_W), lambda i: (i, 0)),
            pl.BlockSpec((None, None, 6, D_MODEL), _mod_row_map(layer, tiles_per_row, first_row)),
            pl.BlockSpec((None, 4, D_MODEL), lambda i: (layer, 0, 0)),
            pl.BlockSpec((None, D_MODEL, D_MODEL), lambda i: (layer, 0, 0)),
        ],
        out_specs=pl.BlockSpec((tm, D_MODEL), lambda i: (i, 0)),
        out_shape=jax.ShapeDtypeStruct((n, D_MODEL), F32),
        compiler_params=_params(1),
        name="outproj",
    )(x, mna, mgq, mml, mods, g_norm, w_out)


def _ffn_kernel(x_ref, mod_ref, g_ref, wg_ref, wu_ref, wd_ref, o_ref, hn_ref, acc_ref, *, n_chunks):
    j = pl.program_id(1)

    @pl.when(j == 0)
    def _():
        h = _rms(x_ref[...], g_ref[2:3, :]) * (1.0 + mod_ref[4:5, :]) + mod_ref[3:4, :]
        hn_ref[...] = h.astype(BF16)

    hb = hn_ref[...]
    gate = jnp.dot(hb, wg_ref[...], preferred_element_type=F32)
    up = jnp.dot(hb, wu_ref[...], preferred_element_type=F32)
    act = (gate * jax.nn.sigmoid(gate) * up).astype(BF16)
    part = jnp.dot(act, wd_ref[...], preferred_element_type=F32)

    @pl.when(j == 0)
    def _():
        acc_ref[...] = part

    @pl.when(j > 0)
    def _():
        acc_ref[...] += part

    @pl.when(j == n_chunks - 1)
    def _():
        o_ref[...] = x_ref[...] + mod_ref[5:6, :] * _rms(acc_ref[...], g_ref[3:4, :])


def _ffn(x, mods, g_norm, w_gu, w_down, layer, tiles_per_row, first_row, tm, ck):
    n = x.shape[0]
    n_chunks = FF_HIDDEN // ck
    mod_map = _mod_row_map(layer, tiles_per_row, first_row)
    return pl.pallas_call(
        functools.partial(_ffn_kernel, n_chunks=n_chunks),
        grid=(n // tm, n_chunks),
        in_specs=[
            pl.BlockSpec((tm, D_MODEL), lambda i, j: (i, 0)),
            pl.BlockSpec((None, None, 6, D_MODEL), lambda i, j: mod_map(i)),
            pl.BlockSpec((None, 4, D_MODEL), lambda i, j: (layer, 0, 0)),
            pl.BlockSpec((None, D_MODEL, ck), lambda i, j: (layer, 0, j)),
            pl.BlockSpec((None, D_MODEL, ck), lambda i, j: (layer, 0, n_chunks + j)),
            pl.BlockSpec((None, ck, D_MODEL), lambda i, j: (layer, j, 0)),
        ],
        out_specs=pl.BlockSpec((tm, D_MODEL), lambda i, j: (i, 0)),
        out_shape=jax.ShapeDtypeStruct((n, D_MODEL), F32),
        scratch_shapes=[pltpu.VMEM((tm, D_MODEL), BF16), pltpu.VMEM((tm, D_MODEL), F32)],
        compiler_params=_params(2),
        name="ffn",
    )(x, mods, g_norm, w_gu, w_gu, w_down)


def _head(ref, base, h):
    return ref[:, base + h * HEAD_DIM:base + (h + 1) * HEAD_DIM]


def _ctx_attn_kernel(zna_ref, zgq_ref, gqk_ref, mna_ref, mgq_ref, kvna_ref, kvgq_ref):
    def attend(q, k, v):
        s = lax.dot_general(q.astype(BF16), k.astype(BF16), NT_DIMS, preferred_element_type=F32) * QK_SCALE
        return _softmax_av([s], [v.astype(BF16)])

    for h in range(NA_HEADS):
        q, k, v = _head(zna_ref, 0, h), _head(zna_ref, NA_W, h), _head(zna_ref, 2 * NA_W, h)
        kvna_ref[0, h] = k
        kvna_ref[1, h] = v
        mna_ref[:, h * HEAD_DIM:(h + 1) * HEAD_DIM] = attend(q, k, v).astype(BF16)
    gq, gk = gqk_ref[0:1, :], gqk_ref[1:2, :]
    for j in range(GQA_KV_HEADS):
        k = _rms(_head(zgq_ref, GQA_QW, j), gk)
        v = _head(zgq_ref, GQA_QW + GQA_KW, j)
        kvgq_ref[0, j] = k
        kvgq_ref[1, j] = v
        for g in range(GQA_GROUP):
            h = j * GQA_GROUP + g
            q = _rms(_head(zgq_ref, 0, h), gq)
            mgq_ref[:, h * HEAD_DIM:(h + 1) * HEAD_DIM] = attend(q, k, v).astype(BF16)


def _ctx_attn(zna, zgq, g_qk, layer, batch, t):
    n = zna.shape[0]
    return pl.pallas_call(
        _ctx_attn_kernel,
        grid=(batch,),
        in_specs=[
            pl.BlockSpec((t, ZNA_W), lambda b: (b, 0)),
            pl.BlockSpec((t, ZGQ_W), lambda b: (b, 0)),
            pl.BlockSpec((None, 2, HEAD_DIM), lambda b: (layer, 0, 0)),
        ],
        out_specs=[
            pl.BlockSpec((t, NA_W), lambda b: (b, 0)),
            pl.BlockSpec((t, GQA_QW), lambda b: (b, 0)),
            pl.BlockSpec((None, 2, NA_HEADS, t, HEAD_DIM), lambda b: (b, 0, 0, 0, 0)),
            pl.BlockSpec((None, 2, GQA_KV_HEADS, t, HEAD_DIM), lambda b: (b, 0, 0, 0, 0)),
        ],
        out_shape=[
            jax.ShapeDtypeStruct((n, NA_W), BF16),
            jax.ShapeDtypeStruct((n, GQA_QW), BF16),
            jax.ShapeDtypeStruct((batch, 2, NA_HEADS, t, HEAD_DIM), F32),
            jax.ShapeDtypeStruct((batch, 2, GQA_KV_HEADS, t, HEAD_DIM), F32),
        ],
        compiler_params=_params(1),
        name="ctx_attn",
    )(zna, zgq, g_qk)


def _lat_na_kernel(zna_ref, cache_ref, bias_ref, o_ref, q_s, k_s, v_s, *, rows):
    qn = NA_BLOCK_ROWS * GRID_W
    kn = NA_UNION_ROWS * GRID_W
    for h in range(NA_HEADS):
        q_s[h] = _head(zna_ref, 0, h).astype(BF16)
        k_s[h] = _head(zna_ref, NA_W, h).astype(BF16)
        v_s[h] = _head(zna_ref, 2 * NA_W, h).astype(BF16)

    for h in range(NA_HEADS):
        kc = cache_ref[0, h].astype(BF16)
        vc = cache_ref[1, h].astype(BF16)
        for p in range(rows // NA_BLOCK_ROWS):
            k0 = _na_union_start(p, rows) * GRID_W
            q = q_s[h, p * qn:(p + 1) * qn, :]
            kw = k_s[h, k0:k0 + kn, :]
            vw = v_s[h, k0:k0 + kn, :]
            s_win = lax.dot_general(q, kw, NT_DIMS, preferred_element_type=F32) * QK_SCALE + bias_ref[h, p]
            s_ctx = lax.dot_general(q, kc, NT_DIMS, preferred_element_type=F32) * QK_SCALE
            o = _softmax_av([s_win, s_ctx], [vw, vc])
            o_ref[p * qn:(p + 1) * qn, h * HEAD_DIM:(h + 1) * HEAD_DIM] = o.astype(BF16)


def _lat_na(zna, cache_na_kv, bias, layer, batch, t):
    n = zna.shape[0]
    rows = t // GRID_W
    past = cache_na_kv.shape[-2]
    n_blocks = rows // NA_BLOCK_ROWS
    qn = NA_BLOCK_ROWS * GRID_W
    kn = NA_UNION_ROWS * GRID_W
    return pl.pallas_call(
        functools.partial(_lat_na_kernel, rows=rows),
        grid=(batch,),
        in_specs=[
            pl.BlockSpec((t, ZNA_W), lambda b: (b, 0)),
            pl.BlockSpec((None, None, 2, NA_HEADS, past, HEAD_DIM), lambda b: (b, layer, 0, 0, 0, 0)),
            pl.BlockSpec((None, NA_HEADS, n_blocks, qn, kn), lambda b: (layer, 0, 0, 0, 0)),
        ],
        out_specs=pl.BlockSpec((t, NA_W), lambda b: (b, 0)),
        out_shape=jax.ShapeDtypeStruct((n, NA_W), BF16),
        scratch_shapes=[pltpu.VMEM((NA_HEADS, t, HEAD_DIM), BF16)] * 3,
        compiler_params=_params(1),
        name="lat_na",
    )(zna, cache_na_kv, bias)


def _rope_tables(t):
    half = HEAD_DIM // 2
    quarter = half // 2
    inv = 1.0 / (ROPE_BASE ** (jnp.arange(quarter, dtype=F32) / quarter))
    tt = jnp.arange(t)
    row = (tt // GRID_W).astype(F32)
    col = (tt % GRID_W).astype(F32)
    ang_r = row[:, None] * inv[None, :]
    ang_c = col[:, None] * inv[None, :]
    cos = jnp.concatenate([jnp.cos(ang_r)] * 2 + [jnp.cos(ang_c)] * 2, axis=-1)
    sin = jnp.concatenate([-jnp.sin(ang_r), jnp.sin(ang_r), -jnp.sin(ang_c), jnp.sin(ang_c)], axis=-1)
    reps = LANES // HEAD_DIM
    return jnp.tile(cos, (1, reps)), jnp.tile(sin, (1, reps))


def _norm_rope_pair(x, gain, cos, sin):
    quarter = HEAD_DIM // 4
    lane = lax.broadcasted_iota(jnp.int32, x.shape, 1)
    first = lane < HEAD_DIM
    xsq = x * x
    ms0 = jnp.sum(jnp.where(first, xsq, 0.0), axis=-1, keepdims=True) * (1.0 / HEAD_DIM)
    ms1 = jnp.sum(jnp.where(first, 0.0, xsq), axis=-1, keepdims=True) * (1.0 / HEAD_DIM)
    r = jnp.where(first, lax.rsqrt(ms0 + EPS), lax.rsqrt(ms1 + EPS))
    xn = x * r * gain
    lower = (lane & (2 * quarter - 1)) < quarter
    partner = jnp.where(lower, pltpu.roll(xn, LANES - quarter, 1), pltpu.roll(xn, quarter, 1))
    return xn * cos + partner * sin


def _lat_gqa_kernel(zgq_ref, cache_ref, gqk_ref, cos_ref, sin_ref, o_ref, q_s, k_s, v_s, *, t, past, rb):
    cos = cos_ref[...]
    sin = sin_ref[...]
    gq = jnp.concatenate([gqk_ref[0:1, :]] * (LANES // HEAD_DIM), axis=-1)
    gk = jnp.concatenate([gqk_ref[1:2, :]] * (LANES // HEAD_DIM), axis=-1)
    heads_per_vreg = LANES // HEAD_DIM
    for i in range(GQA_Q_HEADS // heads_per_vreg):
        x = _norm_rope_pair(zgq_ref[:, i * LANES:(i + 1) * LANES], gq, cos, sin).astype(BF16)
        for a in range(heads_per_vreg):
            q_s[i * heads_per_vreg + a] = x[:, a * HEAD_DIM:(a + 1) * HEAD_DIM]
    kx = _norm_rope_pair(zgq_ref[:, GQA_QW:GQA_QW + GQA_KW], gk, cos, sin).astype(BF16)
    for j in range(GQA_KV_HEADS):
        k_s[j, 0:t, :] = kx[:, j * HEAD_DIM:(j + 1) * HEAD_DIM]
        k_s[j, t:t + past, :] = cache_ref[0, j].astype(BF16)
        v_s[j, 0:t, :] = _head(zgq_ref, GQA_QW + GQA_KW, j).astype(BF16)
        v_s[j, t:t + past, :] = cache_ref[1, j].astype(BF16)

    for h in range(GQA_Q_HEADS):
        j = h // GQA_GROUP

        def body(i, carry, h=h, j=j):
            r0 = pl.multiple_of(i * rb, rb)
            q = q_s[h, pl.ds(r0, rb), :]
            s = lax.dot_general(q, k_s[j], NT_DIMS, preferred_element_type=F32) * QK_SCALE
            o = _softmax_av([s], [v_s[j]])
            o_ref[pl.ds(r0, rb), h * HEAD_DIM:(h + 1) * HEAD_DIM] = o.astype(BF16)
            return carry

        lax.fori_loop(0, t // rb, body, 0)


def _lat_gqa(zgq, cache_gqa_kv, g_qk, cos, sin, layer, batch, t):
    n = zgq.shape[0]
    past = cache_gqa_kv.shape[-2]
    rb = 256
    return pl.pallas_call(
        functools.partial(_lat_gqa_kernel, t=t, past=past, rb=rb),
        grid=(batch,),
        in_specs=[
            pl.BlockSpec((t, ZGQ_W), lambda b: (b, 0)),
            pl.BlockSpec((None, None, 2, GQA_KV_HEADS, past, HEAD_DIM), lambda b: (b, layer, 0, 0, 0, 0)),
            pl.BlockSpec((None, 2, HEAD_DIM), lambda b: (layer, 0, 0)),
            pl.BlockSpec((t, LANES), lambda b: (0, 0)),
            pl.BlockSpec((t, LANES), lambda b: (0, 0)),
        ],
        out_specs=pl.BlockSpec((t, GQA_QW), lambda b: (b, 0)),
        out_shape=jax.ShapeDtypeStruct((n, GQA_QW), BF16),
        scratch_shapes=[
            pltpu.VMEM((GQA_Q_HEADS, t, HEAD_DIM), BF16),
            pltpu.VMEM((GQA_KV_HEADS, t + past, HEAD_DIM), BF16),
            pltpu.VMEM((GQA_KV_HEADS, t + past, HEAD_DIM), BF16),
        ],
        compiler_params=_params(1),
        name="lat_gqa",
    )(zgq, cache_gqa_kv, g_qk, cos, sin)


def _split3(x):
    x1 = x.astype(BF16)
    r1 = x - x1.astype(F32)
    x2 = r1.astype(BF16)
    x3 = (r1 - x2.astype(F32)).astype(BF16)
    return x1, x2, x3


def _log_sigmoid(x):
    return jnp.minimum(x, 0.0) - jnp.log1p(jnp.exp(-jnp.abs(x)))


def _mlstm_kernel(zml_ref, gcol_ref, grow_ref, c0_ref, n0_ref, m0_ref, gml_ref, *rest, t, emit_state):
    if emit_state:
        o_ref, cf_ref, nf_ref, mf_ref, hs_ref, c_s, n_s, m_s, col_s, row_s = rest
    else:
        o_ref, hs_ref, c_s, n_s, m_s, col_s, row_s = rest
    L = MLSTM_CHUNK
    H = MLSTM_HEADS
    nc = t // L

    ti = lax.broadcasted_iota(jnp.int32, (t, t), 0)
    ui = lax.broadcasted_iota(jnp.int32, (t, t), 1)
    same = (ti & -L) == (ui & -L)
    lower = jnp.where(same & (ui <= ti), 1.0, 0.0).astype(BF16)
    upper = jnp.where(same & (ui >= ti), 1.0, 0.0).astype(BF16)

    gc = gcol_ref[...]
    lf_c = _split3(_log_sigmoid(gc))
    pre_c = sum(jnp.dot(lower, p, preferred_element_type=F32) for p in lf_c)
    suf_c = sum(jnp.dot(upper, p, preferred_element_type=F32) for p in lf_c)
    cidx = lax.broadcasted_iota(jnp.int32, gc.shape, 1)
    col_s[...] = jnp.where((cidx >= H) & (cidx < 2 * H), pre_c, jnp.where(cidx >= 3 * H, suf_c, gc))

    gr = grow_ref[...]
    lf_r = _split3(_log_sigmoid(gr))
    pre_r = sum(jnp.dot(p, upper, preferred_element_type=F32) for p in lf_r)
    suf_r = sum(jnp.dot(p, lower, preferred_element_type=F32) for p in lf_r)
    ridx = lax.broadcasted_iota(jnp.int32, gr.shape, 0)
    rows_all = jnp.where((ridx >= H) & (ridx < 2 * H), pre_r, jnp.where(ridx >= 3 * H, suf_r, gr))
    for c in range(nc):
        row_s[c] = rows_all[:, c * L:(c + 1) * L]

    for d in range(2):
        for h in range(H):
            c_s[d * H + h] = c0_ref[d, h]
            n_s[d * H + h] = jnp.broadcast_to(n0_ref[d, h:h + 1, :], (8, HEAD_DIM))
            m_s[d * H + h] = jnp.broadcast_to(m0_ref[d:d + 1, h:h + 1], (8, LANES))

    si = lax.broadcasted_iota(jnp.int32, (L, L), 1)
    tj = lax.broadcasted_iota(jnp.int32, (L, L), 0)

    for d in range(2):
        mask = (si <= tj) if d == 0 else (si >= tj)

        def chunk(c, carry, d=d, mask=mask):
            cc = c if d == 0 else nc - 1 - c
            t0 = pl.multiple_of(cc * L, L)
            rows = row_s[cc]
            for h in range(H):
                idx = d * H + h
                gi = 2 * d * H + h
                gf = gi + H
                qc = zml_ref[pl.ds(t0, L), h * HEAD_DIM:(h + 1) * HEAD_DIM]
                kc = zml_ref[pl.ds(t0, L), ML_W + h * HEAD_DIM:ML_W + (h + 1) * HEAD_DIM] * QK_SCALE
                vc = zml_ref[pl.ds(t0, L), 2 * ML_W + h * HEAD_DIM:2 * ML_W + (h + 1) * HEAD_DIM]
                i_col = col_s[pl.ds(t0, L), gi:gi + 1]
                b_col = col_s[pl.ds(t0, L), gf:gf + 1]
                i_row = rows[gi:gi + 1, :]
                b_row = rows[gf:gf + 1, :]
                m_prev = m_s[idx][0:1, 0:1]
                c_prev = c_s[idx]
                n_prev = n_s[idx][0:1, :]
                qb, kb, vb = qc.astype(BF16), kc.astype(BF16), vc.astype(BF16)

                logw = jnp.where(mask, b_col + (i_row - b_row), -jnp.inf)
                m_inter = b_col + m_prev
                m_t = jnp.maximum(jnp.max(logw, axis=-1, keepdims=True), m_inter)
                w = jnp.exp(logw - m_t)
                s = lax.dot_general(qb, kb, NT_DIMS, preferred_element_type=F32) * w
                decay = jnp.exp(m_inter - m_t)
                inter = lax.dot_general(qb, c_prev.astype(BF16), NT_DIMS, preferred_element_type=F32)
                num = jnp.dot(s.astype(BF16), vb, preferred_element_type=F32) + decay * inter
                den = (jnp.sum(s, axis=-1, keepdims=True)
                       + decay * jnp.sum(qc * n_prev, axis=-1, keepdims=True))
                hh = num / jnp.maximum(jnp.abs(den), jnp.exp(-m_t))
                lanes = slice(h * HEAD_DIM, (h + 1) * HEAD_DIM)
                if d == 0:
                    hs_ref[pl.ds(t0, L), lanes] = hh
                else:
                    hs_ref[pl.ds(t0, L), lanes] += hh

                b_end = b_col[L - 1:L, :] if d == 0 else b_col[0:1, :]
                lw_end = b_end - b_col + i_col
                m_new = jnp.maximum(b_end + m_prev, jnp.max(lw_end, axis=0, keepdims=True))
                w_end = jnp.exp(lw_end - m_new)
                carry_decay = jnp.exp(b_end + m_prev - m_new)
                vw = (vc * w_end).astype(BF16)
                c_s[idx] = carry_decay * c_prev + lax.dot_general(vw, kb, TN_DIMS, preferred_element_type=F32)
                n_new = carry_decay * n_prev + jnp.sum(w_end * kc, axis=0, keepdims=True)
                n_s[idx] = jnp.broadcast_to(n_new, (8, HEAD_DIM))
                m_s[idx] = jnp.broadcast_to(m_new, (8, LANES))
            return carry

        lax.fori_loop(0, nc, chunk, 0)

    for h in range(H):
        lanes = slice(h * HEAD_DIM, (h + 1) * HEAD_DIM)
        hv = hs_ref[:, lanes]
        hn = hv * lax.rsqrt(jnp.mean(hv * hv, axis=-1, keepdims=True) + EPS) * gml_ref[:, lanes]
        og = jax.nn.sigmoid(zml_ref[:, 3 * ML_W + h * HEAD_DIM:3 * ML_W + (h + 1) * HEAD_DIM])
        o_ref[:, lanes] = (hn * og).astype(BF16)

    if emit_state:
        for d in range(2):
            for h in range(H):
                idx = d * H + h
                cf_ref[d, h] = c_s[idx]
                nf_ref[d, h:h + 1, :] = n_s[idx][0:1, :]
                mf_ref[d:d + 1, h:h + 1] = m_s[idx][0:1, 0:1]


def _mlstm(zml, gcol, grow, c0, n0, m0, g_ml, layer, batch, t, emit_state, state_layer):
    n = zml.shape[0]
    H = MLSTM_HEADS
    nc = t // MLSTM_CHUNK
    if state_layer is None:
        c_spec = pl.BlockSpec((None, 2, H, HEAD_DIM, HEAD_DIM), lambda b: (0, 0, 0, 0, 0))
        n_spec = pl.BlockSpec((None, 2, H, HEAD_DIM), lambda b: (0, 0, 0, 0))
        m_spec = pl.BlockSpec((None, 2, H), lambda b: (0, 0, 0))
    else:
        c_spec = pl.BlockSpec((None, None, 2, H, HEAD_DIM, HEAD_DIM), lambda b: (b, state_layer, 0, 0, 0, 0))
        n_spec = pl.BlockSpec((None, None, 2, H, HEAD_DIM), lambda b: (b, state_layer, 0, 0, 0))
        m_spec = pl.BlockSpec((None, None, 2, H), lambda b: (b, state_layer, 0, 0))
    out_specs = [pl.BlockSpec((t, ML_W), lambda b: (b, 0))]
    out_shape = [jax.ShapeDtypeStruct((n, ML_W), BF16)]
    if emit_state:
        out_specs += [
            pl.BlockSpec((None, 2, H, HEAD_DIM, HEAD_DIM), lambda b: (b, 0, 0, 0, 0)),
            pl.BlockSpec((None, 2, H, HEAD_DIM), lambda b: (b, 0, 0, 0)),
            pl.BlockSpec((None, 2, H), lambda b: (b, 0, 0)),
        ]
        out_shape += [
            jax.ShapeDtypeStruct((batch, 2, H, HEAD_DIM, HEAD_DIM), F32),
            jax.ShapeDtypeStruct((batch, 2, H, HEAD_DIM), F32),
            jax.ShapeDtypeStruct((batch, 2, H), F32),
        ]
    return pl.pallas_call(
        functools.partial(_mlstm_kernel, t=t, emit_state=emit_state),
        grid=(batch,),
        in_specs=[
            pl.BlockSpec((t, ZML_W), lambda b: (b, 0)),
            pl.BlockSpec((t, N_GATES), lambda b: (b, 0)),
            pl.BlockSpec((N_GATES, t), lambda b: (0, b)),
            c_spec, n_spec, m_spec,
            pl.BlockSpec((None, 1, ML_W), lambda b: (layer, 0, 0)),
        ],
        out_specs=out_specs,
        out_shape=out_shape,
        scratch_shapes=[
            pltpu.VMEM((t, ML_W), F32),
            pltpu.VMEM((2 * H, HEAD_DIM, HEAD_DIM), F32),
            pltpu.VMEM((2 * H, 8, HEAD_DIM), F32),
            pltpu.VMEM((2 * H, 8, LANES), F32),
            pltpu.VMEM((t, N_GATES), F32),
            pltpu.VMEM((nc, N_GATES, MLSTM_CHUNK), F32),
        ],
        compiler_params=_params(1),
        name="mlstm",
    )(zml, gcol, grow, c0, n0, m0, g_ml)


def _layer_path(x, mods, layer, first_row, tiles_row_tokens, weights, mixers, tm_in, tm_out, tm_ffn):
    per_row = lambda tm: None if tiles_row_tokens is None else tiles_row_tokens // tm
    g_norm = weights["g_norm"]
    zna, zgq, zml, gcol, grow = _inproj(x, mods, g_norm, weights["w_in"], layer, per_row(tm_in), first_row, tm_in)
    mna, mgq, mml, extra = mixers(zna, zgq, zml, gcol, grow)
    x = _outproj(x, mna, mgq, mml, mods, g_norm, weights["w_out"], layer, per_row(tm_out), first_row, tm_out)
    x = _ffn(x, mods, g_norm, weights["w_gu"], weights["w_down"], layer, per_row(tm_ffn), first_row,
             tm_ffn, FF_HIDDEN // 2)
    return x, extra


def kernel(x_prompt, x_sample, cache_na_kv, cache_gqa_kv, state_mlstm_C, state_mlstm_n, state_mlstm_m,
           c, c_ctx, w_in, b_gates, w_out, g_norm, g_qk, g_mlstm, na_bias, w_ada, b_ada, w_gu, w_down):
    batch, seq, _ = x_prompt.shape
    dec_batch, dec_seq, _ = x_sample.shape
    assert dec_batch + 1 <= N_MOD_ROWS and dec_seq % GRID_W == 0

    cvec = jnp.concatenate([c_ctx[None, :], c, jnp.zeros((N_MOD_ROWS - 1 - dec_batch, D_MODEL), F32)], axis=0)
    mods = _adaln(cvec, w_ada, b_ada).reshape(DEPTH, N_MOD_ROWS, 6, D_MODEL)
    bias = _na_bias_expand(na_bias, dec_seq // GRID_W)
    cos, sin = _rope_tables(dec_seq)

    gate_cols = w_in[:, :, OFF_GATES:]
    weights = {
        "g_norm": g_norm,
        "w_in": {
            "na": w_in[:, :, :OFF_GQ].astype(BF16),
            "gq": w_in[:, :, OFF_GQ:OFF_ML].astype(BF16),
            "ml": w_in[:, :, OFF_ML:OFF_GATES].astype(BF16),
            "gate_col": jnp.pad(gate_cols, ((0, 0), (0, 0), (0, LANES - N_GATES))).astype(BF16),
            "gate_row": jnp.swapaxes(gate_cols, 1, 2).astype(BF16),
            "b_col": b_gates.reshape(DEPTH, 1, N_GATES),
            "b_row": b_gates.reshape(DEPTH, N_GATES, 1),
        },
        "w_out": w_out.astype(BF16),
        "w_gu": w_gu.astype(BF16),
        "w_down": w_down.astype(BF16),
    }
    g_ml = g_mlstm.reshape(DEPTH, 1, ML_W)
    zero_c = jnp.zeros((1, 2, MLSTM_HEADS, HEAD_DIM, HEAD_DIM), F32)
    zero_n = jnp.zeros((1, 2, MLSTM_HEADS, HEAD_DIM), F32)
    zero_m = jnp.zeros((1, 2, MLSTM_HEADS), F32)

    xp = x_prompt.reshape(batch * seq, D_MODEL)
    xs = x_sample.reshape(dec_batch * dec_seq, D_MODEL)
    na_l, gqa_l, c_l, n_l, m_l = [], [], [], [], []
    for layer in range(DEPTH):
        def ctx_mixers(zna, zgq, zml, gcol, grow, layer=layer):
            mna, mgq, kv_na, kv_gq = _ctx_attn(zna, zgq, g_qk, layer, batch, seq)
            mml, cf, nf, mf = _mlstm(zml, gcol, grow, zero_c, zero_n, zero_m, g_ml, layer, batch, seq,
                                     True, None)
            return mna, mgq, mml, (kv_na, kv_gq, cf, nf, mf)

        def lat_mixers(zna, zgq, zml, gcol, grow, layer=layer):
            mna = _lat_na(zna, cache_na_kv, bias, layer, dec_batch, dec_seq)
            mgq = _lat_gqa(zgq, cache_gqa_kv, g_qk, cos, sin, layer, dec_batch, dec_seq)
            (mml,) = _mlstm(zml, gcol, grow, state_mlstm_C, state_mlstm_n, state_mlstm_m, g_ml, layer,
                            dec_batch, dec_seq, False, layer)
            return mna, mgq, mml, None

        xp, (kv_na, kv_gq, cf, nf, mf) = _layer_path(xp, mods, layer, 0, None, weights, ctx_mixers,
                                                       512, 512, 512)
        na_l.append(kv_na)
        gqa_l.append(kv_gq)
        c_l.append(cf)
        n_l.append(nf)
        m_l.append(mf)
        xs, _ = _layer_path(xs, mods, layer, 1, dec_seq, weights, lat_mixers, 512, 512, 512)

    return (xp.reshape(batch, seq, D_MODEL), xs.reshape(dec_batch, dec_seq, D_MODEL),
            jnp.stack(na_l, axis=1), jnp.stack(gqa_l, axis=1),
            jnp.stack(c_l, axis=1), jnp.stack(n_l, axis=1), jnp.stack(m_l, axis=1))
```

```python
import functools

import jax
import jax.numpy as jnp
import numpy as np
from jax import lax
from jax.experimental import pallas as pl
from jax.experimental.pallas import tpu as pltpu

D_MODEL = 1024
DEPTH = 4
GRID_W = 64
HEAD_DIM = 64
NA_HEADS = 4
GQA_Q_HEADS = 8
GQA_KV_HEADS = 2
GQA_GROUP = GQA_Q_HEADS // GQA_KV_HEADS
MLSTM_HEADS = 4
NA_WIN_ROWS = 8
NA_WIN_COLS = 16
MLSTM_CHUNK = 64
ROPE_BASE = 10000.0
EPS = 1e-6
NEG = -1e30
NA_W = NA_HEADS * HEAD_DIM
GQA_QW = GQA_Q_HEADS * HEAD_DIM
GQA_KW = GQA_KV_HEADS * HEAD_DIM
ML_W = MLSTM_HEADS * HEAD_DIM
N_GATES = 4 * MLSTM_HEADS
N_SCANS = 2 * MLSTM_HEADS
MLSTM_TILE_KINDS = 3
MLSTM_SEL_ROWS = 32
FF_HIDDEN = ((8 * D_MODEL + 3 * 256 - 1) // (3 * 256)) * 256
QK_SCALE = HEAD_DIM ** -0.5

ZNA_W = 3 * NA_W
ZGQ_W = GQA_QW + 2 * GQA_KW
ZML_W = 4 * ML_W
OFF_GQ = ZNA_W
OFF_ML = ZNA_W + ZGQ_W
OFF_GATES = OFF_ML + ZML_W

LANES = 128
N_MOD_ROWS = 16
NA_BLOCK_ROWS = 2
NA_UNION_ROWS = NA_WIN_ROWS + NA_BLOCK_ROWS - 1

F32 = jnp.float32
BF16 = jnp.bfloat16
VMEM_LIMIT = 52 * 1024 * 1024

NT_DIMS = (((1,), (1,)), ((), ()))


def _params(n_axes):
    return pltpu.CompilerParams(dimension_semantics=("arbitrary",) * n_axes,
                                vmem_limit_bytes=VMEM_LIMIT)


def _rms(x, g):
    return x * lax.rsqrt(jnp.mean(x * x, axis=-1, keepdims=True) + EPS) * g


def _softmax_av(s_list, v_list):
    m = s_list[0].max(axis=-1, keepdims=True)
    for s in s_list[1:]:
        m = jnp.maximum(m, s.max(axis=-1, keepdims=True))
    acc = None
    den = None
    for s, v in zip(s_list, v_list):
        p = jnp.exp(s - m)
        l = p.sum(axis=-1, keepdims=True)
        o = jnp.dot(p.astype(BF16), v, preferred_element_type=F32)
        acc = o if acc is None else acc + o
        den = l if den is None else den + l
    return acc / den


def _adaln_kernel(c_ref, w_ref, b_ref, o_ref):
    c = c_ref[...]
    a = c * jax.nn.sigmoid(c)
    o_ref[...] = jnp.dot(a.astype(BF16), w_ref[...].astype(BF16),
                         preferred_element_type=F32) + b_ref[...]


def _adaln(cvec, w_ada, b_ada):
    tn = 1536
    return pl.pallas_call(
        _adaln_kernel,
        grid=(DEPTH, 6 * D_MODEL // tn),
        in_specs=[
            pl.BlockSpec((N_MOD_ROWS, D_MODEL), lambda l, j: (0, 0)),
            pl.BlockSpec((None, D_MODEL, tn), lambda l, j: (l, 0, j)),
            pl.BlockSpec((None, 1, tn), lambda l, j: (l, 0, j)),
        ],
        out_specs=pl.BlockSpec((None, N_MOD_ROWS, tn), lambda l, j: (l, 0, j)),
        out_shape=jax.ShapeDtypeStruct((DEPTH, N_MOD_ROWS, 6 * D_MODEL), F32),
        compiler_params=_params(2),
        name="adaln",
    )(cvec, w_ada, b_ada.reshape(DEPTH, 1, 6 * D_MODEL))


def _na_r0(r, rows):
    return min(max(r - NA_WIN_ROWS // 2, 0), rows - NA_WIN_ROWS)


def _na_union_start(p, rows):
    return min(_na_r0(NA_BLOCK_ROWS * p, rows), rows - NA_UNION_ROWS)


def _na_bias_kernel(tbl_ref, o_ref, *, rows):
    l = pl.program_id(0)
    h = pl.program_id(1)
    qi = lax.broadcasted_iota(jnp.int32, (GRID_W, GRID_W), 0)
    ki = lax.broadcasted_iota(jnp.int32, (GRID_W, GRID_W), 1)
    dc = jnp.clip(ki - qi, -(NA_WIN_COLS - 1), NA_WIN_COLS - 1) + NA_WIN_COLS - 1
    c0 = jnp.clip(qi - NA_WIN_COLS // 2, 0, GRID_W - NA_WIN_COLS)
    col_ok = (ki >= c0) & (ki < c0 + NA_WIN_COLS)
    n_dr = 2 * NA_WIN_ROWS - 1
    n_dc = 2 * NA_WIN_COLS - 1
    tiles = []
    for dr in range(n_dr):
        t = jnp.zeros((GRID_W, GRID_W), F32)
        for d in range(n_dc):
            t = jnp.where(dc == d, tbl_ref[((l * NA_HEADS + h) * n_dr + dr) * n_dc + d], t)
        tiles.append(jnp.where(col_ok, t, NEG))
    neg_tile = jnp.full((GRID_W, GRID_W), NEG, F32)
    for p in range(rows // NA_BLOCK_ROWS):
        start = _na_union_start(p, rows)
        for a in range(NA_BLOCK_ROWS):
            r = NA_BLOCK_ROWS * p + a
            r0 = _na_r0(r, rows)
            for j in range(NA_UNION_ROWS):
                kr = start + j
                inside = r0 <= kr < r0 + NA_WIN_ROWS
                tile = tiles[kr - r + NA_WIN_ROWS - 1] if inside else neg_tile
                o_ref[p, a * GRID_W:(a + 1) * GRID_W, j * GRID_W:(j + 1) * GRID_W] = tile


def _na_bias_expand(na_bias, rows):
    n_blocks = rows // NA_BLOCK_ROWS
    qn = NA_BLOCK_ROWS * GRID_W
    kn = NA_UNION_ROWS * GRID_W
    return pl.pallas_call(
        functools.partial(_na_bias_kernel, rows=rows),
        grid=(DEPTH, NA_HEADS),
        in_specs=[pl.BlockSpec(memory_space=pltpu.SMEM)],
        out_specs=pl.BlockSpec((None, None, n_blocks, qn, kn), lambda l, h: (l, h, 0, 0, 0)),
        out_shape=jax.ShapeDtypeStruct((DEPTH, NA_HEADS, n_blocks, qn, kn), F32),
        compiler_params=_params(2),
        name="na_bias_expand",
    )(na_bias.reshape(-1))


def _mod_row_map(layer, tiles_per_row, first_row):
    if tiles_per_row is None:
        return lambda i: (layer, first_row, 0, 0)
    return lambda i: (layer, first_row + i // tiles_per_row, 0, 0)


def _inproj_kernel(x_ref, mod_ref, g_ref, wna_ref, wgq_ref, wml_ref, wgc_ref, wgr_ref, bgc_ref, bgr_ref,
                   zna_ref, zgq_ref, zml_ref, gi_ref, gf_ref, grow_ref):
    x = x_ref[...]
    h = _rms(x, g_ref[0:1, :]) * (1.0 + mod_ref[1:2, :]) + mod_ref[0:1, :]
    hb = h.astype(BF16)
    zna_ref[...] = jnp.dot(hb, wna_ref[...], preferred_element_type=F32)
    zgq_ref[...] = jnp.dot(hb, wgq_ref[...], preferred_element_type=F32)
    zml_ref[...] = jnp.dot(hb, wml_ref[...], preferred_element_type=F32)
    gc = jnp.dot(hb, wgc_ref[...], preferred_element_type=F32)
    gi_ref[...] = gc[:, 0:N_SCANS] + bgc_ref[0:1, :]
    gf_ref[...] = gc[:, LANES:LANES + N_SCANS] + bgc_ref[1:2, :]
    grow_ref[...] = lax.dot_general(wgr_ref[...], hb, NT_DIMS, preferred_element_type=F32) + bgr_ref[...]


def _inproj(x, mods, g_norm, w, layer, tiles_per_row, first_row, tm):
    n = x.shape[0]
    wspec = lambda width: pl.BlockSpec((None, D_MODEL, width), lambda i: (layer, 0, 0))
    return pl.pallas_call(
        _inproj_kernel,
        grid=(n // tm,),
        in_specs=[
            pl.BlockSpec((tm, D_MODEL), lambda i: (i, 0)),
            pl.BlockSpec((None, None, 6, D_MODEL), _mod_row_map(layer, tiles_per_row, first_row)),
            pl.BlockSpec((None, 4, D_MODEL), lambda i: (layer, 0, 0)),
            wspec(ZNA_W), wspec(ZGQ_W), wspec(ZML_W), wspec(2 * LANES),
            pl.BlockSpec((None, N_GATES, D_MODEL), lambda i: (layer, 0, 0)),
            pl.BlockSpec((None, 2, N_SCANS), lambda i: (layer, 0, 0)),
            pl.BlockSpec((None, N_GATES, 1), lambda i: (layer, 0, 0)),
        ],
        out_specs=[
            pl.BlockSpec((tm, ZNA_W), lambda i: (i, 0)),
            pl.BlockSpec((tm, ZGQ_W), lambda i: (i, 0)),
            pl.BlockSpec((tm, ZML_W), lambda i: (i, 0)),
            pl.BlockSpec((tm, N_SCANS), lambda i: (i, 0)),
            pl.BlockSpec((tm, N_SCANS), lambda i: (i, 0)),
            pl.BlockSpec((N_GATES, tm), lambda i: (0, i)),
        ],
        out_shape=[
            jax.ShapeDtypeStruct((n, ZNA_W), F32),
            jax.ShapeDtypeStruct((n, ZGQ_W), F32),
            jax.ShapeDtypeStruct((n, ZML_W), F32),
            jax.ShapeDtypeStruct((n, N_SCANS), F32),
            jax.ShapeDtypeStruct((n, N_SCANS), F32),
            jax.ShapeDtypeStruct((N_GATES, n), F32),
        ],
        compiler_params=_params(1),
        name="inproj",
    )(x, mods, g_norm, w["na"], w["gq"], w["ml"], w["gate_col"], w["gate_row"], w["b_col"], w["b_row"])


def _outproj_kernel(x_ref, mna_ref, mgq_ref, mml_ref, mod_ref, g_ref, w_ref, o_ref):
    acc = jnp.dot(mna_ref[...], w_ref[0:NA_W, :], preferred_element_type=F32)
    acc += jnp.dot(mgq_ref[...], w_ref[NA_W:NA_W + GQA_QW, :], preferred_element_type=F32)
    acc += jnp.dot(mml_ref[...], w_ref[NA_W + GQA_QW:, :], preferred_element_type=F32)
    o_ref[...] = x_ref[...] + mod_ref[2:3, :] * _rms(acc, g_ref[1:2, :])


def _outproj(x, mna, mgq, mml, mods, g_norm, w_out, layer, tiles_per_row, first_row, tm):
    n = x.shape[0]
    return pl.pallas_call(
        _outproj_kernel,
        grid=(n // tm,),
        in_specs=[
            pl.BlockSpec((tm, D_MODEL), lambda i: (i, 0)),
            pl.BlockSpec((tm, NA_W), lambda i: (i, 0)),
            pl.BlockSpec((tm, GQA_QW), lambda i: (i, 0)),
            pl.BlockSpec((tm, ML_W), lambda i: (i, 0)),
            pl.BlockSpec((None, None, 6, D_MODEL), _mod_row_map(layer, tiles_per_row, first_row)),
            pl.BlockSpec((None, 4, D_MODEL), lambda i: (layer, 0, 0)),
            pl.BlockSpec((None, D_MODEL, D_MODEL), lambda i: (layer, 0, 0)),
        ],
        out_specs=pl.BlockSpec((tm, D_MODEL), lambda i: (i, 0)),
        out_shape=jax.ShapeDtypeStruct((n, D_MODEL), F32),
        compiler_params=_params(1),
        name="outproj",
    )(x, mna, mgq, mml, mods, g_norm, w_out)


def _ffn_kernel(x_ref, mod_ref, g_ref, wg_ref, wu_ref, wd_ref, o_ref, hn_ref, acc_ref, *, n_chunks):
    j = pl.program_id(1)

    @pl.when(j == 0)
    def _():
        h = _rms(x_ref[...], g_ref[2:3, :]) * (1.0 + mod_ref[4:5, :]) + mod_ref[3:4, :]
        hn_ref[...] = h.astype(BF16)

    hb = hn_ref[...]
    gate = jnp.dot(hb, wg_ref[...], preferred_element_type=F32)
    up = jnp.dot(hb, wu_ref[...], preferred_element_type=F32)
    act = (gate * jax.nn.sigmoid(gate) * up).astype(BF16)
    part = jnp.dot(act, wd_ref[...], preferred_element_type=F32)

    @pl.when(j == 0)
    def _():
        acc_ref[...] = part

    @pl.when(j > 0)
    def _():
        acc_ref[...] += part

    @pl.when(j == n_chunks - 1)
    def _():
        o_ref[...] = x_ref[...] + mod_ref[5:6, :] * _rms(acc_ref[...], g_ref[3:4, :])


def _ffn(x, mods, g_norm, w_gu, w_down, layer, tiles_per_row, first_row, tm, ck):
    n = x.shape[0]
    n_chunks = FF_HIDDEN // ck
    mod_map = _mod_row_map(layer, tiles_per_row, first_row)
    return pl.pallas_call(
        functools.partial(_ffn_kernel, n_chunks=n_chunks),
        grid=(n // tm, n_chunks),
        in_specs=[
            pl.BlockSpec((tm, D_MODEL), lambda i, j: (i, 0)),
            pl.BlockSpec((None, None, 6, D_MODEL), lambda i, j: mod_map(i)),
            pl.BlockSpec((None, 4, D_MODEL), lambda i, j: (layer, 0, 0)),
            pl.BlockSpec((None, D_MODEL, ck), lambda i, j: (layer, 0, j)),
            pl.BlockSpec((None, D_MODEL, ck), lambda i, j: (layer, 0, n_chunks + j)),
            pl.BlockSpec((None, ck, D_MODEL), lambda i, j: (layer, j, 0)),
        ],
        out_specs=pl.BlockSpec((tm, D_MODEL), lambda i, j: (i, 0)),
        out_shape=jax.ShapeDtypeStruct((n, D_MODEL), F32),
        scratch_shapes=[pltpu.VMEM((tm, D_MODEL), BF16), pltpu.VMEM((tm, D_MODEL), F32)],
        compiler_params=_params(2),
        name="ffn",
    )(x, mods, g_norm, w_gu, w_gu, w_down)


def _head(ref, base, h):
    return ref[:, base + h * HEAD_DIM:base + (h + 1) * HEAD_DIM]


def _ctx_attn_kernel(zna_ref, zgq_ref, gqk_ref, mna_ref, mgq_ref, kvna_ref, kvgq_ref):
    def attend(q, k, v):
        s = lax.dot_general(q.astype(BF16), k.astype(BF16), NT_DIMS, preferred_element_type=F32) * QK_SCALE
        return _softmax_av([s], [v.astype(BF16)])

    for h in range(NA_HEADS):
        q, k, v = _head(zna_ref, 0, h), _head(zna_ref, NA_W, h), _head(zna_ref, 2 * NA_W, h)
        kvna_ref[0, h] = k
        kvna_ref[1, h] = v
        mna_ref[:, h * HEAD_DIM:(h + 1) * HEAD_DIM] = attend(q, k, v).astype(BF16)
    gq, gk = gqk_ref[0:1, :], gqk_ref[1:2, :]
    for j in range(GQA_KV_HEADS):
        k = _rms(_head(zgq_ref, GQA_QW, j), gk)
        v = _head(zgq_ref, GQA_QW + GQA_KW, j)
        kvgq_ref[0, j] = k
        kvgq_ref[1, j] = v
        for g in range(GQA_GROUP):
            h = j * GQA_GROUP + g
            q = _rms(_head(zgq_ref, 0, h), gq)
            mgq_ref[:, h * HEAD_DIM:(h + 1) * HEAD_DIM] = attend(q, k, v).astype(BF16)


def _ctx_attn(zna, zgq, g_qk, layer, batch, t):
    n = zna.shape[0]
    return pl.pallas_call(
        _ctx_attn_kernel,
        grid=(batch,),
        in_specs=[
            pl.BlockSpec((t, ZNA_W), lambda b: (b, 0)),
            pl.BlockSpec((t, ZGQ_W), lambda b: (b, 0)),
            pl.BlockSpec((None, 2, HEAD_DIM), lambda b: (layer, 0, 0)),
        ],
        out_specs=[
            pl.BlockSpec((t, NA_W), lambda b: (b, 0)),
            pl.BlockSpec((t, GQA_QW), lambda b: (b, 0)),
            pl.BlockSpec((None, 2, NA_HEADS, t, HEAD_DIM), lambda b: (b, 0, 0, 0, 0)),
            pl.BlockSpec((None, 2, GQA_KV_HEADS, t, HEAD_DIM), lambda b: (b, 0, 0, 0, 0)),
        ],
        out_shape=[
            jax.ShapeDtypeStruct((n, NA_W), BF16),
            jax.ShapeDtypeStruct((n, GQA_QW), BF16),
            jax.ShapeDtypeStruct((batch, 2, NA_HEADS, t, HEAD_DIM), F32),
            jax.ShapeDtypeStruct((batch, 2, GQA_KV_HEADS, t, HEAD_DIM), F32),
        ],
        compiler_params=_params(1),
        name="ctx_attn",
    )(zna, zgq, g_qk)


def _lat_na_kernel(zna_ref, cache_ref, bias_ref, o_ref, q_s, k_s, v_s, *, rows):
    qn = NA_BLOCK_ROWS * GRID_W
    kn = NA_UNION_ROWS * GRID_W
    for h in range(NA_HEADS):
        q_s[h] = _head(zna_ref, 0, h).astype(BF16)
        k_s[h] = _head(zna_ref, NA_W, h).astype(BF16)
        v_s[h] = _head(zna_ref, 2 * NA_W, h).astype(BF16)

    for h in range(NA_HEADS):
        kc = cache_ref[0, h].astype(BF16)
        vc = cache_ref[1, h].astype(BF16)
        for p in range(rows // NA_BLOCK_ROWS):
            k0 = _na_union_start(p, rows) * GRID_W
            q = q_s[h, p * qn:(p + 1) * qn, :]
            kw = k_s[h, k0:k0 + kn, :]
            vw = v_s[h, k0:k0 + kn, :]
            s_win = lax.dot_general(q, kw, NT_DIMS, preferred_element_type=F32) * QK_SCALE + bias_ref[h, p]
            s_ctx = lax.dot_general(q, kc, NT_DIMS, preferred_element_type=F32) * QK_SCALE
            o = _softmax_av([s_win, s_ctx], [vw, vc])
            o_ref[p * qn:(p + 1) * qn, h * HEAD_DIM:(h + 1) * HEAD_DIM] = o.astype(BF16)


def _lat_na(zna, cache_na_kv, bias, layer, batch, t):
    n = zna.shape[0]
    rows = t // GRID_W
    past = cache_na_kv.shape[-2]
    n_blocks = rows // NA_BLOCK_ROWS
    qn = NA_BLOCK_ROWS * GRID_W
    kn = NA_UNION_ROWS * GRID_W
    return pl.pallas_call(
        functools.partial(_lat_na_kernel, rows=rows),
        grid=(batch,),
        in_specs=[
            pl.BlockSpec((t, ZNA_W), lambda b: (b, 0)),
            pl.BlockSpec((None, None, 2, NA_HEADS, past, HEAD_DIM), lambda b: (b, layer, 0, 0, 0, 0)),
            pl.BlockSpec((None, NA_HEADS, n_blocks, qn, kn), lambda b: (layer, 0, 0, 0, 0)),
        ],
        out_specs=pl.BlockSpec((t, NA_W), lambda b: (b, 0)),
        out_shape=jax.ShapeDtypeStruct((n, NA_W), BF16),
        scratch_shapes=[pltpu.VMEM((NA_HEADS, t, HEAD_DIM), BF16)] * 3,
        compiler_params=_params(1),
        name="lat_na",
    )(zna, cache_na_kv, bias)


def _rope_tables(t):
    half = HEAD_DIM // 2
    quarter = half // 2
    inv = 1.0 / (ROPE_BASE ** (jnp.arange(quarter, dtype=F32) / quarter))
    tt = jnp.arange(t)
    row = (tt // GRID_W).astype(F32)
    col = (tt % GRID_W).astype(F32)
    ang_r = row[:, None] * inv[None, :]
    ang_c = col[:, None] * inv[None, :]
    cos = jnp.concatenate([jnp.cos(ang_r)] * 2 + [jnp.cos(ang_c)] * 2, axis=-1)
    sin = jnp.concatenate([-jnp.sin(ang_r), jnp.sin(ang_r), -jnp.sin(ang_c), jnp.sin(ang_c)], axis=-1)
    reps = LANES // HEAD_DIM
    return jnp.tile(cos, (1, reps)), jnp.tile(sin, (1, reps))


def _norm_rope_pair(x, gain, cos, sin):
    quarter = HEAD_DIM // 4
    lane = lax.broadcasted_iota(jnp.int32, x.shape, 1)
    first = lane < HEAD_DIM
    xsq = x * x
    ms0 = jnp.sum(jnp.where(first, xsq, 0.0), axis=-1, keepdims=True) * (1.0 / HEAD_DIM)
    ms1 = jnp.sum(jnp.where(first, 0.0, xsq), axis=-1, keepdims=True) * (1.0 / HEAD_DIM)
    r = jnp.where(first, lax.rsqrt(ms0 + EPS), lax.rsqrt(ms1 + EPS))
    xn = x * r * gain
    lower = (lane & (2 * quarter - 1)) < quarter
    partner = jnp.where(lower, pltpu.roll(xn, LANES - quarter, 1), pltpu.roll(xn, quarter, 1))
    return xn * cos + partner * sin


def _lat_gqa_kernel(zgq_ref, cache_ref, gqk_ref, cos_ref, sin_ref, o_ref, q_s, k_s, v_s, *, t, past, rb):
    cos = cos_ref[...]
    sin = sin_ref[...]
    gq = jnp.concatenate([gqk_ref[0:1, :]] * (LANES // HEAD_DIM), axis=-1)
    gk = jnp.concatenate([gqk_ref[1:2, :]] * (LANES // HEAD_DIM), axis=-1)
    heads_per_vreg = LANES // HEAD_DIM
    for i in range(GQA_Q_HEADS // heads_per_vreg):
        x = _norm_rope_pair(zgq_ref[:, i * LANES:(i + 1) * LANES], gq, cos, sin).astype(BF16)
        for a in range(heads_per_vreg):
            q_s[i * heads_per_vreg + a] = x[:, a * HEAD_DIM:(a + 1) * HEAD_DIM]
    kx = _norm_rope_pair(zgq_ref[:, GQA_QW:GQA_QW + GQA_KW], gk, cos, sin).astype(BF16)
    for j in range(GQA_KV_HEADS):
        k_s[j, 0:t, :] = kx[:, j * HEAD_DIM:(j + 1) * HEAD_DIM]
        k_s[j, t:t + past, :] = cache_ref[0, j].astype(BF16)
        v_s[j, 0:t, :] = _head(zgq_ref, GQA_QW + GQA_KW, j).astype(BF16)
        v_s[j, t:t + past, :] = cache_ref[1, j].astype(BF16)

    for h in range(GQA_Q_HEADS):
        j = h // GQA_GROUP

        def body(i, carry, h=h, j=j):
            r0 = pl.multiple_of(i * rb, rb)
            q = q_s[h, pl.ds(r0, rb), :]
            s = lax.dot_general(q, k_s[j], NT_DIMS, preferred_element_type=F32) * QK_SCALE
            o = _softmax_av([s], [v_s[j]])
            o_ref[pl.ds(r0, rb), h * HEAD_DIM:(h + 1) * HEAD_DIM] = o.astype(BF16)
            return carry

        lax.fori_loop(0, t // rb, body, 0)


def _lat_gqa(zgq, cache_gqa_kv, g_qk, cos, sin, layer, batch, t):
    n = zgq.shape[0]
    past = cache_gqa_kv.shape[-2]
    rb = 256
    return pl.pallas_call(
        functools.partial(_lat_gqa_kernel, t=t, past=past, rb=rb),
        grid=(batch,),
        in_specs=[
            pl.BlockSpec((t, ZGQ_W), lambda b: (b, 0)),
            pl.BlockSpec((None, None, 2, GQA_KV_HEADS, past, HEAD_DIM), lambda b: (b, layer, 0, 0, 0, 0)),
            pl.BlockSpec((None, 2, HEAD_DIM), lambda b: (layer, 0, 0)),
            pl.BlockSpec((t, LANES), lambda b: (0, 0)),
            pl.BlockSpec((t, LANES), lambda b: (0, 0)),
        ],
        out_specs=pl.BlockSpec((t, GQA_QW), lambda b: (b, 0)),
        out_shape=jax.ShapeDtypeStruct((n, GQA_QW), BF16),
        scratch_shapes=[
            pltpu.VMEM((GQA_Q_HEADS, t, HEAD_DIM), BF16),
            pltpu.VMEM((GQA_KV_HEADS, t + past, HEAD_DIM), BF16),
            pltpu.VMEM((GQA_KV_HEADS, t + past, HEAD_DIM), BF16),
        ],
        compiler_params=_params(1),
        name="lat_gqa",
    )(zgq, cache_gqa_kv, g_qk, cos, sin)


def _split3(x):
    x1 = x.astype(BF16)
    r1 = x - x1.astype(F32)
    x2 = r1.astype(BF16)
    x3 = (r1 - x2.astype(F32)).astype(BF16)
    return x1, x2, x3


def _log_sigmoid(x):
    return jnp.minimum(x, 0.0) - jnp.log1p(jnp.exp(-jnp.abs(x)))


def _mlstm_select_matrix():
    H = MLSTM_HEADS
    sel = np.zeros((MLSTM_SEL_ROWS, (H // 2) * 2 * MLSTM_TILE_KINDS * LANES), np.float32)
    for j in range(H // 2):
        for d in range(2):
            for q in range(MLSTM_TILE_KINDS):
                for a in range(2):
                    col0 = ((j * 2 + d) * MLSTM_TILE_KINDS + q) * LANES + a * HEAD_DIM
                    sel[q * N_SCANS + d * H + 2 * j + a, col0:col0 + HEAD_DIM] = 1.0
    return sel


def _mlstm_kernel(zml_ref, gi_ref, gf_ref, grow_ref, c0_ref, n0_ref, m0_ref, gml_ref, sel_ref, *rest,
                  t, emit_state):
    if emit_state:
        o_ref, cf_ref, nf_ref, mf_ref = rest[:4]
    else:
        o_ref = rest[0]
    hs_ref, tri_s, st_s, dst_s, cst_s, row_s = rest[-6:]
    L = MLSTM_CHUNK
    H = MLSTM_HEADS
    HD = HEAD_DIM
    NP = H // 2
    nc = t // L
    tb = tri_s.shape[-1]

    @pl.when(pl.program_id(0) == 0)
    def _():
        ti = lax.broadcasted_iota(jnp.int32, (tb, tb), 0)
        ui = lax.broadcasted_iota(jnp.int32, (tb, tb), 1)
        same = (ti & -L) == (ui & -L)
        tri_s[0] = jnp.where(same & (ui <= ti), 1.0, 0.0).astype(BF16)
        tri_s[1] = jnp.where(same & (ui >= ti), 1.0, 0.0).astype(BF16)

    lower, upper = tri_s[0], tri_s[1]

    def chunk_sums_cols(x):
        parts = _split3(x)
        pre, suf = [], []
        for i in range(t // tb):
            blk = [p[i * tb:(i + 1) * tb] for p in parts]
            pre.append(sum(jnp.dot(lower, p, preferred_element_type=F32) for p in blk))
            suf.append(sum(jnp.dot(upper, p, preferred_element_type=F32) for p in blk))
        return jnp.concatenate(pre, axis=0), jnp.concatenate(suf, axis=0)

    def chunk_sums_rows(x):
        parts = _split3(x)
        pre, suf = [], []
        for i in range(t // tb):
            blk = [p[:, i * tb:(i + 1) * tb] for p in parts]
            pre.append(sum(jnp.dot(p, upper, preferred_element_type=F32) for p in blk))
            suf.append(sum(jnp.dot(p, lower, preferred_element_type=F32) for p in blk))
        return jnp.concatenate(pre, axis=1), jnp.concatenate(suf, axis=1)

    pre_c, suf_c = chunk_sums_cols(_log_sigmoid(gf_ref[...]))
    lane_c = lax.broadcasted_iota(jnp.int32, (t, N_SCANS), 1)
    b3 = jnp.where(lane_c < H, pre_c, suf_c).reshape(nc, L, N_SCANS)
    i3 = gi_ref[...].reshape(nc, L, N_SCANS)
    fwd3 = lax.broadcasted_iota(jnp.int32, (nc, 1, N_SCANS), 2) < H
    b_end3 = jnp.where(fwd3, b3[:, L - 1:L, :], b3[:, 0:1, :])
    lw_end3 = b_end3 - b3 + i3
    a3 = jnp.max(lw_end3, axis=1, keepdims=True)
    wloc3 = jnp.exp(lw_end3 - a3)

    fwd1 = lax.broadcasted_iota(jnp.int32, (1, N_SCANS), 1) < H
    m = m0_ref[...]
    m_start, carry_decay, contrib_scale = [], [], []
    for j in range(nc):
        a_j = jnp.where(fwd1, a3[j], a3[nc - 1 - j])
        g_j = jnp.where(fwd1, b_end3[j], b_end3[nc - 1 - j])
        m_start.append(m)
        m_next = jnp.maximum(g_j + m, a_j)
        carry_decay.append(jnp.exp(g_j + m - m_next))
        contrib_scale.append(jnp.exp(a_j - m_next))
        m = m_next
    mst3 = jnp.concatenate([jnp.where(fwd1, m_start[c], m_start[nc - 1 - c])[None] for c in range(nc)], axis=0)

    pad = jnp.zeros((t, MLSTM_SEL_ROWS - MLSTM_TILE_KINDS * N_SCANS), F32)
    cols = jnp.concatenate([b3.reshape(t, N_SCANS), (b3 + mst3).reshape(t, N_SCANS),
                            wloc3.reshape(t, N_SCANS), pad], axis=1)
    col_parts = _split3(cols)
    tile_w = MLSTM_TILE_KINDS * LANES

    def tiles(j, d):
        sel = sel_ref[:, (2 * j + d) * tile_w:(2 * j + d + 1) * tile_w]
        x = sum(jnp.dot(p, sel, preferred_element_type=F32) for p in col_parts)
        return [x[:, q * LANES:(q + 1) * LANES].reshape(nc, L, LANES) for q in range(MLSTM_TILE_KINDS)]

    gr = grow_ref[...]
    pre_r, suf_r = chunk_sums_rows(_log_sigmoid(gr))
    sub_r = lax.broadcasted_iota(jnp.int32, (N_SCANS, t), 0)
    rowv = gr[0:N_SCANS] - jnp.where(sub_r < H, pre_r[N_SCANS:], suf_r[N_SCANS:])
    for j in range(NP):
        for d in range(2):
            e = d * H + 2 * j
            for c in range(nc):
                row_s[2 * j + d, c] = jnp.concatenate(
                    [rowv[e:e + 1, c * L:(c + 1) * L], rowv[e + 1:e + 2, c * L:(c + 1) * L]], axis=1)

    lane_a = lax.broadcasted_iota(jnp.int32, (1, 1, LANES), 2) < HD
    sub_a = lax.broadcasted_iota(jnp.int32, (1, 2 * HD, 1), 1) < HD
    diag = sub_a == lane_a
    diag4 = jnp.concatenate([diag] * 4, axis=2)

    def stack_heads(x3):
        return jnp.concatenate([jnp.where(lane_a, x3, 0.0), jnp.where(lane_a, 0.0, x3)], axis=1)

    def pair_cols(base, j):
        return slice(base + j * LANES, base + (j + 1) * LANES)

    zero_blk = jnp.zeros((HD, HD), F32)
    for j in range(NP):
        cols_d = []
        for d in range(2):
            ca, cb = c0_ref[d, 2 * j].T, c0_ref[d, 2 * j + 1].T
            na = jnp.broadcast_to(n0_ref[d, 2 * j:2 * j + 1, :], (HD, HD)).T
            nb = jnp.broadcast_to(n0_ref[d, 2 * j + 1:2 * j + 2, :], (HD, HD)).T
            top = jnp.concatenate([ca, zero_blk, na, zero_blk], axis=1)
            bot = jnp.concatenate([zero_blk, cb, zero_blk, nb], axis=1)
            cols_d.append(jnp.concatenate([top, bot], axis=0))
        st_s[j] = jnp.concatenate(cols_d, axis=1)

    for j in range(NP):
        k3 = (zml_ref[:, pair_cols(ML_W, j)] * QK_SCALE).reshape(nc, L, LANES).astype(BF16)
        v3 = zml_ref[:, pair_cols(2 * ML_W, j)].reshape(nc, L, LANES)
        rhs = []
        for d in range(2):
            wl = tiles(j, d)[2]
            rhs += [v3 * wl, wl]
        rhs = jnp.concatenate(rhs, axis=2).astype(BF16)
        contrib = jnp.einsum("csk,csn->ckn", k3, rhs, preferred_element_type=F32)
        dst_s[j] = jnp.where(diag4, contrib, 0.0)

    def lane_scale(v, j):
        pieces = []
        for d in range(2):
            sa = jnp.broadcast_to(v[:, d * H + 2 * j:d * H + 2 * j + 1], (1, HD))
            sb = jnp.broadcast_to(v[:, d * H + 2 * j + 1:d * H + 2 * j + 2], (1, HD))
            pieces += [sa, sb, sa, sb]
        return jnp.concatenate(pieces, axis=1)

    for j in range(NP):
        st = st_s[j]
        for step in range(nc):
            cb = nc - 1 - step
            stb = st.astype(BF16)
            cst_s[j, step, :, 0:2 * LANES] = stb[:, 0:2 * LANES]
            cst_s[j, cb, :, 2 * LANES:] = stb[:, 2 * LANES:]
            delta = jnp.concatenate([dst_s[j, step, :, 0:2 * LANES], dst_s[j, cb, :, 2 * LANES:]], axis=1)
            st = lane_scale(carry_decay[step], j) * st + lane_scale(contrib_scale[step], j) * delta
        st_s[j] = st

    sidx = lax.broadcasted_iota(jnp.int32, (1, L, LANES), 2) & (HD - 1)
    tidx = lax.broadcasted_iota(jnp.int32, (1, L, LANES), 1)
    masks = (sidx <= tidx, sidx >= tidx)
    ones_blk = jnp.broadcast_to(jnp.where(diag, 1.0, 0.0).astype(BF16), (nc, 2 * HD, LANES))
    neg_inf = -jnp.inf
    for j in range(NP):
        q3 = zml_ref[:, pair_cols(0, j)].reshape(nc, L, LANES).astype(BF16)
        k3 = (zml_ref[:, pair_cols(ML_W, j)] * QK_SCALE).reshape(nc, L, LANES)
        v3 = zml_ref[:, pair_cols(2 * ML_W, j)].reshape(nc, L, LANES)
        qk = jnp.einsum("ctd,cnd->ctn", q3, stack_heads(k3).astype(BF16), preferred_element_type=F32)
        v_aug = jnp.concatenate([stack_heads(v3).astype(BF16), ones_blk], axis=2)
        out = None
        for d in range(2):
            b_t, bm_t, _ = tiles(j, d)
            logw = jnp.where(masks[d], b_t + row_s[2 * j + d], neg_inf)
            rmax_a = jnp.max(jnp.where(lane_a, logw, neg_inf), axis=-1, keepdims=True)
            rmax_b = jnp.max(jnp.where(lane_a, neg_inf, logw), axis=-1, keepdims=True)
            m_t = jnp.maximum(jnp.where(lane_a, rmax_a, rmax_b), bm_t)
            s = qk * jnp.exp(logw - m_t)
            decay = jnp.exp(bm_t - m_t)
            sv = jnp.einsum("cts,csn->ctn", s.astype(BF16), v_aug, preferred_element_type=F32)
            state = cst_s[j, :, :, 2 * d * LANES:2 * (d + 1) * LANES]
            inter = jnp.einsum("ctk,ckn->ctn", q3, state, preferred_element_type=F32)
            num = sv[:, :, 0:LANES] + decay * inter[:, :, 0:LANES]
            den = sv[:, :, LANES:] + decay * inter[:, :, LANES:]
            h_d = num / jnp.maximum(jnp.abs(den), jnp.exp(-m_t))
            out = h_d if out is None else out + h_d
        hs_ref[:, pair_cols(0, j)] = out.reshape(t, LANES)

    for h in range(H):
        lanes = slice(h * HD, (h + 1) * HD)
        hv = hs_ref[:, lanes]
        hn = hv * lax.rsqrt(jnp.mean(hv * hv, axis=-1, keepdims=True) + EPS) * gml_ref[:, lanes]
        og = jax.nn.sigmoid(zml_ref[:, 3 * ML_W + h * HD:3 * ML_W + (h + 1) * HD])
        o_ref[:, lanes] = (hn * og).astype(BF16)

    if emit_state:
        for j in range(NP):
            st = st_s[j]
            for d in range(2):
                for a in range(2):
                    rows = slice(a * HD, (a + 1) * HD)
                    c0 = 2 * d * LANES + a * HD
                    cf_ref[d, 2 * j + a] = st[rows, c0:c0 + HD].T
                    nf_ref[d, 2 * j + a:2 * j + a + 1, :] = st[rows, c0 + LANES:c0 + LANES + HD].T[0:1, :]
        mf_ref[...] = m


def _mlstm(zml, gi, gf, grow, c0, n0, m0, g_ml, layer, batch, t, emit_state, state_layer):
    n = zml.shape[0]
    H = MLSTM_HEADS
    L = MLSTM_CHUNK
    nc = t // L
    sel = jnp.asarray(_mlstm_select_matrix(), BF16)
    tri_block = min(t, 256)
    assert L & (L - 1) == 0 and tri_block % L == 0 and t % tri_block == 0
    if state_layer is None:
        c_spec = pl.BlockSpec((None, 2, H, HEAD_DIM, HEAD_DIM), lambda b: (0, 0, 0, 0, 0))
        n_spec = pl.BlockSpec((None, 2, H, HEAD_DIM), lambda b: (0, 0, 0, 0))
        m_spec = pl.BlockSpec((None, 1, N_SCANS), lambda b: (0, 0, 0))
    else:
        c_spec = pl.BlockSpec((None, None, 2, H, HEAD_DIM, HEAD_DIM), lambda b: (b, state_layer, 0, 0, 0, 0))
        n_spec = pl.BlockSpec((None, None, 2, H, HEAD_DIM), lambda b: (b, state_layer, 0, 0, 0))
        m_spec = pl.BlockSpec((None, None, 1, N_SCANS), lambda b: (b, state_layer, 0, 0))
    out_specs = [pl.BlockSpec((t, ML_W), lambda b: (b, 0))]
    out_shape = [jax.ShapeDtypeStruct((n, ML_W), BF16)]
    if emit_state:
        out_specs += [
            pl.BlockSpec((None, 2, H, HEAD_DIM, HEAD_DIM), lambda b: (b, 0, 0, 0, 0)),
            pl.BlockSpec((None, 2, H, HEAD_DIM), lambda b: (b, 0, 0, 0)),
            pl.BlockSpec((None, 1, N_SCANS), lambda b: (b, 0, 0)),
        ]
        out_shape += [
            jax.ShapeDtypeStruct((batch, 2, H, HEAD_DIM, HEAD_DIM), F32),
            jax.ShapeDtypeStruct((batch, 2, H, HEAD_DIM), F32),
            jax.ShapeDtypeStruct((batch, 1, N_SCANS), F32),
        ]
    return pl.pallas_call(
        functools.partial(_mlstm_kernel, t=t, emit_state=emit_state),
        grid=(batch,),
        in_specs=[
            pl.BlockSpec((t, ZML_W), lambda b: (b, 0)),
            pl.BlockSpec((t, N_SCANS), lambda b: (b, 0)),
            pl.BlockSpec((t, N_SCANS), lambda b: (b, 0)),
            pl.BlockSpec((N_GATES, t), lambda b: (0, b)),
            c_spec, n_spec, m_spec,
            pl.BlockSpec((None, 1, ML_W), lambda b: (layer, 0, 0)),
            pl.BlockSpec(sel.shape, lambda b: (0, 0)),
        ],
        out_specs=out_specs,
        out_shape=out_shape,
        scratch_shapes=[
            pltpu.VMEM((t, ML_W), F32),
            pltpu.VMEM((2, tri_block, tri_block), BF16),
            pltpu.VMEM((H // 2, 2 * HEAD_DIM, 4 * LANES), F32),
            pltpu.VMEM((H // 2, nc, 2 * HEAD_DIM, 4 * LANES), F32),
            pltpu.VMEM((H // 2, nc, 2 * HEAD_DIM, 4 * LANES), BF16),
            pltpu.VMEM((H, nc, 1, LANES), F32),
        ],
        compiler_params=_params(1),
        name="mlstm",
    )(zml, gi, gf, grow, c0, n0, m0, g_ml, sel)


def _layer_path(x, mods, layer, first_row, tiles_row_tokens, weights, mixers, tm_in, tm_out, tm_ffn):
    per_row = lambda tm: None if tiles_row_tokens is None else tiles_row_tokens // tm
    g_norm = weights["g_norm"]
    zna, zgq, zml, gi, gf, grow = _inproj(x, mods, g_norm, weights["w_in"], layer, per_row(tm_in), first_row,
                                          tm_in)
    mna, mgq, mml, extra = mixers(zna, zgq, zml, gi, gf, grow)
    x = _outproj(x, mna, mgq, mml, mods, g_norm, weights["w_out"], layer, per_row(tm_out), first_row, tm_out)
    x = _ffn(x, mods, g_norm, weights["w_gu"], weights["w_down"], layer, per_row(tm_ffn), first_row,
             tm_ffn, FF_HIDDEN // 2)
    return x, extra


def kernel(x_prompt, x_sample, cache_na_kv, cache_gqa_kv, state_mlstm_C, state_mlstm_n, state_mlstm_m,
           c, c_ctx, w_in, b_gates, w_out, g_norm, g_qk, g_mlstm, na_bias, w_ada, b_ada, w_gu, w_down):
    batch, seq, _ = x_prompt.shape
    dec_batch, dec_seq, _ = x_sample.shape
    assert dec_batch + 1 <= N_MOD_ROWS and dec_seq % GRID_W == 0

    cvec = jnp.concatenate([c_ctx[None, :], c, jnp.zeros((N_MOD_ROWS - 1 - dec_batch, D_MODEL), F32)], axis=0)
    mods = _adaln(cvec, w_ada, b_ada).reshape(DEPTH, N_MOD_ROWS, 6, D_MODEL)
    bias = _na_bias_expand(na_bias, dec_seq // GRID_W)
    cos, sin = _rope_tables(dec_seq)

    scan_i = np.array([2 * MLSTM_HEADS * d + h for d in range(2) for h in range(MLSTM_HEADS)])
    scan_f = scan_i + MLSTM_HEADS
    w_gi, w_gf = w_in[:, :, OFF_GATES + scan_i], w_in[:, :, OFF_GATES + scan_f]
    lane_pad = lambda a: jnp.pad(a, ((0, 0), (0, 0), (0, LANES - N_SCANS)))
    weights = {
        "g_norm": g_norm,
        "w_in": {
            "na": w_in[:, :, :OFF_GQ].astype(BF16),
            "gq": w_in[:, :, OFF_GQ:OFF_ML].astype(BF16),
            "ml": w_in[:, :, OFF_ML:OFF_GATES].astype(BF16),
            "gate_col": jnp.concatenate([lane_pad(w_gi), lane_pad(w_gf)], axis=-1).astype(BF16),
            "gate_row": jnp.swapaxes(jnp.concatenate([w_gi, w_gf], axis=-1), 1, 2).astype(BF16),
            "b_col": jnp.stack([b_gates[:, scan_i], b_gates[:, scan_f]], axis=1),
            "b_row": jnp.concatenate([b_gates[:, scan_i], b_gates[:, scan_f]], axis=-1)[:, :, None],
        },
        "w_out": w_out.astype(BF16),
        "w_gu": w_gu.astype(BF16),
        "w_down": w_down.astype(BF16),
    }
    g_ml = g_mlstm.reshape(DEPTH, 1, ML_W)
    zero_c = jnp.zeros((1, 2, MLSTM_HEADS, HEAD_DIM, HEAD_DIM), F32)
    zero_n = jnp.zeros((1, 2, MLSTM_HEADS, HEAD_DIM), F32)
    zero_m = jnp.zeros((1, 1, N_SCANS), F32)
    m0_lat = state_mlstm_m.reshape(dec_batch, DEPTH, 1, N_SCANS)

    xp = x_prompt.reshape(batch * seq, D_MODEL)
    xs = x_sample.reshape(dec_batch * dec_seq, D_MODEL)
    na_l, gqa_l, c_l, n_l, m_l = [], [], [], [], []
    for layer in range(DEPTH):
        def ctx_mixers(zna, zgq, zml, gi, gf, grow, layer=layer):
            mna, mgq, kv_na, kv_gq = _ctx_attn(zna, zgq, g_qk, layer, batch, seq)
            mml, cf, nf, mf = _mlstm(zml, gi, gf, grow, zero_c, zero_n, zero_m, g_ml, layer, batch, seq,
                                     True, None)
            return mna, mgq, mml, (kv_na, kv_gq, cf, nf, mf.reshape(batch, 2, MLSTM_HEADS))

        def lat_mixers(zna, zgq, zml, gi, gf, grow, layer=layer):
            mna = _lat_na(zna, cache_na_kv, bias, layer, dec_batch, dec_seq)
            mgq = _lat_gqa(zgq, cache_gqa_kv, g_qk, cos, sin, layer, dec_batch, dec_seq)
            (mml,) = _mlstm(zml, gi, gf, grow, state_mlstm_C, state_mlstm_n, m0_lat, g_ml, layer,
                            dec_batch, dec_seq, False, layer)
            return mna, mgq, mml, None

        xp, (kv_na, kv_gq, cf, nf, mf) = _layer_path(xp, mods, layer, 0, None, weights, ctx_mixers,
                                                       512, 512, 512)
        na_l.append(kv_na)
        gqa_l.append(kv_gq)
        c_l.append(cf)
        n_l.append(nf)
        m_l.append(mf)
        xs, _ = _layer_path(xs, mods, layer, 1, dec_seq, weights, lat_mixers, 512, 512, 512)

    return (xp.reshape(batch, seq, D_MODEL), xs.reshape(dec_batch, dec_seq, D_MODEL),
            jnp.stack(na_l, axis=1), jnp.stack(gqa_l, axis=1),
            jnp.stack(c_l, axis=1), jnp.stack(n_l, axis=1), jnp.stack(m_l, axis=1))
```

```python
import functools

import jax
import jax.numpy as jnp
import numpy as np
from jax import lax
from jax.experimental import pallas as pl
from jax.experimental.pallas import tpu as pltpu

D_MODEL = 1024
DEPTH = 4
GRID_W = 64
HEAD_DIM = 64
NA_HEADS = 4
GQA_Q_HEADS = 8
GQA_KV_HEADS = 2
GQA_GROUP = GQA_Q_HEADS // GQA_KV_HEADS
MLSTM_HEADS = 4
NA_WIN_ROWS = 8
NA_WIN_COLS = 16
MLSTM_CHUNK = 64
ROPE_BASE = 10000.0
EPS = 1e-6
NEG = -1e30
NA_W = NA_HEADS * HEAD_DIM
GQA_QW = GQA_Q_HEADS * HEAD_DIM
GQA_KW = GQA_KV_HEADS * HEAD_DIM
ML_W = MLSTM_HEADS * HEAD_DIM
N_GATES = 4 * MLSTM_HEADS
N_SCANS = 2 * MLSTM_HEADS
MLSTM_TILE_KINDS = 3
MLSTM_SEL_ROWS = 32
FF_HIDDEN = ((8 * D_MODEL + 3 * 256 - 1) // (3 * 256)) * 256
QK_SCALE = HEAD_DIM ** -0.5

ZNA_W = 3 * NA_W
ZGQ_W = GQA_QW + 2 * GQA_KW
ZML_W = 4 * ML_W
OFF_GQ = ZNA_W
OFF_ML = ZNA_W + ZGQ_W
OFF_GATES = OFF_ML + ZML_W

LANES = 128
N_MOD_ROWS = 16
NA_BLOCK_ROWS = 2
NA_UNION_ROWS = NA_WIN_ROWS + NA_BLOCK_ROWS - 1

F32 = jnp.float32
BF16 = jnp.bfloat16
VMEM_LIMIT = 52 * 1024 * 1024

NT_DIMS = (((1,), (1,)), ((), ()))


def _params(n_axes):
    return pltpu.CompilerParams(dimension_semantics=("arbitrary",) * n_axes,
                                vmem_limit_bytes=VMEM_LIMIT)


def _rms(x, g):
    return x * lax.rsqrt(jnp.mean(x * x, axis=-1, keepdims=True) + EPS) * g


def _lane_first(shape):
    return lax.broadcasted_iota(jnp.int32, shape, len(shape) - 1) < HEAD_DIM


def _pair_queries(x):
    first = _lane_first(x.shape)
    return jnp.where(first, x, 0.0).astype(BF16), jnp.where(first, 0.0, x).astype(BF16)


def _pair_values(v2):
    first = _lane_first(v2.shape)
    ones, zeros = jnp.ones_like(v2), jnp.zeros_like(v2)
    r0 = jnp.concatenate([jnp.where(first, v2, 0.0), jnp.where(first, ones, zeros)], axis=1)
    r1 = jnp.concatenate([jnp.where(first, 0.0, v2), jnp.where(first, zeros, ones)], axis=1)
    return r0.astype(BF16), r1.astype(BF16)


def _pair_values_t(vt2):
    first = lax.broadcasted_iota(jnp.int32, vt2.shape, 0) < HEAD_DIM
    ones, zeros = jnp.ones_like(vt2), jnp.zeros_like(vt2)
    r0 = jnp.concatenate([jnp.where(first, vt2, 0.0), jnp.where(first, ones, zeros)], axis=0)
    r1 = jnp.concatenate([jnp.where(first, 0.0, vt2), jnp.where(first, zeros, ones)], axis=0)
    return r0.astype(BF16), r1.astype(BF16)


def _pair_attend(q_pair, blocks):
    acc = None
    for a in range(2):
        scores = []
        for blk in blocks:
            if blk["k_nt"]:
                s = lax.dot_general(q_pair[a], blk["k"], NT_DIMS, preferred_element_type=F32)
            else:
                s = jnp.dot(q_pair[a], blk["k"], preferred_element_type=F32)
            if blk.get("bias") is not None:
                s = s + blk["bias"][a]
            scores.append(s)
        m = scores[0].max(axis=-1, keepdims=True)
        for s in scores[1:]:
            m = jnp.maximum(m, s.max(axis=-1, keepdims=True))
        for s, blk in zip(scores, blocks):
            p = jnp.exp(s - m).astype(BF16)
            if blk["r_nt"]:
                term = lax.dot_general(p, blk["r"][a], NT_DIMS, preferred_element_type=F32)
            else:
                term = jnp.dot(p, blk["r"][a], preferred_element_type=F32)
            acc = term if acc is None else acc + term
    return acc[:, :LANES] / acc[:, LANES:]


def _adaln_kernel(c_ref, w_ref, b_ref, o_ref):
    c = c_ref[...]
    a = c * jax.nn.sigmoid(c)
    o_ref[...] = jnp.dot(a.astype(BF16), w_ref[...].astype(BF16),
                         preferred_element_type=F32) + b_ref[...]


def _adaln(cvec, w_ada, b_ada):
    tn = 1536
    return pl.pallas_call(
        _adaln_kernel,
        grid=(DEPTH, 6 * D_MODEL // tn),
        in_specs=[
            pl.BlockSpec((N_MOD_ROWS, D_MODEL), lambda l, j: (0, 0)),
            pl.BlockSpec((None, D_MODEL, tn), lambda l, j: (l, 0, j)),
            pl.BlockSpec((None, 1, tn), lambda l, j: (l, 0, j)),
        ],
        out_specs=pl.BlockSpec((None, N_MOD_ROWS, tn), lambda l, j: (l, 0, j)),
        out_shape=jax.ShapeDtypeStruct((DEPTH, N_MOD_ROWS, 6 * D_MODEL), F32),
        compiler_params=_params(2),
        name="adaln",
    )(cvec, w_ada, b_ada.reshape(DEPTH, 1, 6 * D_MODEL))


def _na_r0(r, rows):
    return min(max(r - NA_WIN_ROWS // 2, 0), rows - NA_WIN_ROWS)


def _na_union_start(p, rows):
    return min(_na_r0(NA_BLOCK_ROWS * p, rows), rows - NA_UNION_ROWS)


def _na_bias_kernel(tbl_ref, o_ref, *, rows):
    l = pl.program_id(0)
    h = pl.program_id(1)
    qi = lax.broadcasted_iota(jnp.int32, (GRID_W, GRID_W), 0)
    ki = lax.broadcasted_iota(jnp.int32, (GRID_W, GRID_W), 1)
    dc = jnp.clip(ki - qi, -(NA_WIN_COLS - 1), NA_WIN_COLS - 1) + NA_WIN_COLS - 1
    c0 = jnp.clip(qi - NA_WIN_COLS // 2, 0, GRID_W - NA_WIN_COLS)
    col_ok = (ki >= c0) & (ki < c0 + NA_WIN_COLS)
    n_dr = 2 * NA_WIN_ROWS - 1
    n_dc = 2 * NA_WIN_COLS - 1
    tiles = []
    for dr in range(n_dr):
        t = jnp.zeros((GRID_W, GRID_W), F32)
        for d in range(n_dc):
            t = jnp.where(dc == d, tbl_ref[((l * NA_HEADS + h) * n_dr + dr) * n_dc + d], t)
        tiles.append(jnp.where(col_ok, t, NEG))
    neg_tile = jnp.full((GRID_W, GRID_W), NEG, F32)
    for p in range(rows // NA_BLOCK_ROWS):
        start = _na_union_start(p, rows)
        for a in range(NA_BLOCK_ROWS):
            r = NA_BLOCK_ROWS * p + a
            r0 = _na_r0(r, rows)
            for j in range(NA_UNION_ROWS):
                kr = start + j
                inside = r0 <= kr < r0 + NA_WIN_ROWS
                tile = tiles[kr - r + NA_WIN_ROWS - 1] if inside else neg_tile
                o_ref[p, a * GRID_W:(a + 1) * GRID_W, j * GRID_W:(j + 1) * GRID_W] = tile


def _na_bias_expand(na_bias, rows):
    n_blocks = rows // NA_BLOCK_ROWS
    qn = NA_BLOCK_ROWS * GRID_W
    kn = NA_UNION_ROWS * GRID_W
    return pl.pallas_call(
        functools.partial(_na_bias_kernel, rows=rows),
        grid=(DEPTH, NA_HEADS),
        in_specs=[pl.BlockSpec(memory_space=pltpu.SMEM)],
        out_specs=pl.BlockSpec((None, None, n_blocks, qn, kn), lambda l, h: (l, h, 0, 0, 0)),
        out_shape=jax.ShapeDtypeStruct((DEPTH, NA_HEADS, n_blocks, qn, kn), F32),
        compiler_params=_params(2),
        name="na_bias_expand",
    )(na_bias.reshape(-1))


def _mod_row_map(layer, tiles_per_row, first_row):
    if tiles_per_row is None:
        return lambda i: (layer, first_row, 0, 0)
    return lambda i: (layer, first_row + i // tiles_per_row, 0, 0)


def _inproj_kernel(x_ref, mod_ref, g_ref, wna_ref, wgq_ref, wml_ref, wgc_ref, wgr_ref, bgc_ref, bgr_ref,
                   zna_ref, zgq_ref, zml_ref, gi_ref, gf_ref, grow_ref):
    x = x_ref[...]
    h = _rms(x, g_ref[0:1, :]) * (1.0 + mod_ref[1:2, :]) + mod_ref[0:1, :]
    hb = h.astype(BF16)
    zna_ref[...] = jnp.dot(hb, wna_ref[...], preferred_element_type=F32)
    zgq_ref[...] = jnp.dot(hb, wgq_ref[...], preferred_element_type=F32)
    zml_ref[...] = jnp.dot(hb, wml_ref[...], preferred_element_type=F32)
    gc = jnp.dot(hb, wgc_ref[...], preferred_element_type=F32)
    gi_ref[...] = gc[:, 0:N_SCANS] + bgc_ref[0:1, :]
    gf_ref[...] = gc[:, LANES:LANES + N_SCANS] + bgc_ref[1:2, :]
    grow_ref[...] = lax.dot_general(wgr_ref[...], hb, NT_DIMS, preferred_element_type=F32) + bgr_ref[...]


def _inproj(x, mods, g_norm, w, layer, tiles_per_row, first_row, tm):
    n = x.shape[0]
    wspec = lambda width: pl.BlockSpec((None, D_MODEL, width), lambda i: (layer, 0, 0))
    return pl.pallas_call(
        _inproj_kernel,
        grid=(n // tm,),
        in_specs=[
            pl.BlockSpec((tm, D_MODEL), lambda i: (i, 0)),
            pl.BlockSpec((None, None, 6, D_MODEL), _mod_row_map(layer, tiles_per_row, first_row)),
            pl.BlockSpec((None, 4, D_MODEL), lambda i: (layer, 0, 0)),
            wspec(ZNA_W), wspec(ZGQ_W), wspec(ZML_W), wspec(2 * LANES),
            pl.BlockSpec((None, N_GATES, D_MODEL), lambda i: (layer, 0, 0)),
            pl.BlockSpec((None, 2, N_SCANS), lambda i: (layer, 0, 0)),
            pl.BlockSpec((None, N_GATES, 1), lambda i: (layer, 0, 0)),
        ],
        out_specs=[
            pl.BlockSpec((tm, ZNA_W), lambda i: (i, 0)),
            pl.BlockSpec((tm, ZGQ_W), lambda i: (i, 0)),
            pl.BlockSpec((tm, ZML_W), lambda i: (i, 0)),
            pl.BlockSpec((tm, N_SCANS), lambda i: (i, 0)),
            pl.BlockSpec((tm, N_SCANS), lambda i: (i, 0)),
            pl.BlockSpec((N_GATES, tm), lambda i: (0, i)),
        ],
        out_shape=[
            jax.ShapeDtypeStruct((n, ZNA_W), F32),
            jax.ShapeDtypeStruct((n, ZGQ_W), F32),
            jax.ShapeDtypeStruct((n, ZML_W), F32),
            jax.ShapeDtypeStruct((n, N_SCANS), F32),
            jax.ShapeDtypeStruct((n, N_SCANS), F32),
            jax.ShapeDtypeStruct((N_GATES, n), F32),
        ],
        compiler_params=_params(1),
        name="inproj",
    )(x, mods, g_norm, w["na"], w["gq"], w["ml"], w["gate_col"], w["gate_row"], w["b_col"], w["b_row"])


def _outproj_kernel(x_ref, mna_ref, mgq_ref, mml_ref, mod_ref, g_ref, w_ref, o_ref):
    acc = jnp.dot(mna_ref[...], w_ref[0:NA_W, :], preferred_element_type=F32)
    acc += jnp.dot(mgq_ref[...], w_ref[NA_W:NA_W + GQA_QW, :], preferred_element_type=F32)
    acc += jnp.dot(mml_ref[...], w_ref[NA_W + GQA_QW:, :], preferred_element_type=F32)
    o_ref[...] = x_ref[...] + mod_ref[2:3, :] * _rms(acc, g_ref[1:2, :])


def _outproj(x, mna, mgq, mml, mods, g_norm, w_out, layer, tiles_per_row, first_row, tm):
    n = x.shape[0]
    return pl.pallas_call(
        _outproj_kernel,
        grid=(n // tm,),
        in_specs=[
            pl.BlockSpec((tm, D_MODEL), lambda i: (i, 0)),
            pl.BlockSpec((tm, NA_W), lambda i: (i, 0)),
            pl.BlockSpec((tm, GQA_QW), lambda i: (i, 0)),
            pl.BlockSpec((tm, ML_W), lambda i: (i, 0)),
            pl.BlockSpec((None, None, 6, D_MODEL), _mod_row_map(layer, tiles_per_row, first_row)),
            pl.BlockSpec((None, 4, D_MODEL), lambda i: (layer, 0, 0)),
            pl.BlockSpec((None, D_MODEL, D_MODEL), lambda i: (layer, 0, 0)),
        ],
        out_specs=pl.BlockSpec((tm, D_MODEL), lambda i: (i, 0)),
        out_shape=jax.ShapeDtypeStruct((n, D_MODEL), F32),
        compiler_params=_params(1),
        name="outproj",
    )(x, mna, mgq, mml, mods, g_norm, w_out)


def _ffn_kernel(x_ref, mod_ref, g_ref, wg_ref, wu_ref, wd_ref, o_ref, hn_ref, acc_ref, *, n_chunks):
    j = pl.program_id(1)

    @pl.when(j == 0)
    def _():
        h = _rms(x_ref[...], g_ref[2:3, :]) * (1.0 + mod_ref[4:5, :]) + mod_ref[3:4, :]
        hn_ref[...] = h.astype(BF16)

    hb = hn_ref[...]
    gate = jnp.dot(hb, wg_ref[...], preferred_element_type=F32)
    up = jnp.dot(hb, wu_ref[...], preferred_element_type=F32)
    act = (gate * jax.nn.sigmoid(gate) * up).astype(BF16)
    part = jnp.dot(act, wd_ref[...], preferred_element_type=F32)

    @pl.when(j == 0)
    def _():
        acc_ref[...] = part

    @pl.when(j > 0)
    def _():
        acc_ref[...] += part

    @pl.when(j == n_chunks - 1)
    def _():
        o_ref[...] = x_ref[...] + mod_ref[5:6, :] * _rms(acc_ref[...], g_ref[3:4, :])


def _ffn(x, mods, g_norm, w_gu, w_down, layer, tiles_per_row, first_row, tm, ck):
    n = x.shape[0]
    n_chunks = FF_HIDDEN // ck
    mod_map = _mod_row_map(layer, tiles_per_row, first_row)
    return pl.pallas_call(
        functools.partial(_ffn_kernel, n_chunks=n_chunks),
        grid=(n // tm, n_chunks),
        in_specs=[
            pl.BlockSpec((tm, D_MODEL), lambda i, j: (i, 0)),
            pl.BlockSpec((None, None, 6, D_MODEL), lambda i, j: mod_map(i)),
            pl.BlockSpec((None, 4, D_MODEL), lambda i, j: (layer, 0, 0)),
            pl.BlockSpec((None, D_MODEL, ck), lambda i, j: (layer, 0, j)),
            pl.BlockSpec((None, D_MODEL, ck), lambda i, j: (layer, 0, n_chunks + j)),
            pl.BlockSpec((None, ck, D_MODEL), lambda i, j: (layer, j, 0)),
        ],
        out_specs=pl.BlockSpec((tm, D_MODEL), lambda i, j: (i, 0)),
        out_shape=jax.ShapeDtypeStruct((n, D_MODEL), F32),
        scratch_shapes=[pltpu.VMEM((tm, D_MODEL), BF16), pltpu.VMEM((tm, D_MODEL), F32)],
        compiler_params=_params(2),
        name="ffn",
    )(x, mods, g_norm, w_gu, w_gu, w_down)


def _pair_rms(x, gain):
    first = _lane_first(x.shape)
    xsq = x * x
    ms0 = jnp.sum(jnp.where(first, xsq, 0.0), axis=-1, keepdims=True) * (1.0 / HEAD_DIM)
    ms1 = jnp.sum(jnp.where(first, 0.0, xsq), axis=-1, keepdims=True) * (1.0 / HEAD_DIM)
    return x * jnp.where(first, lax.rsqrt(ms0 + EPS), lax.rsqrt(ms1 + EPS)) * gain


def _pair_gain(gqk_ref, row):
    return jnp.concatenate([gqk_ref[row:row + 1, :]] * (LANES // HEAD_DIM), axis=-1)


def _ctx_attn_kernel(zna_ref, zgq_ref, gqk_ref, *rest):
    mna_ref, mgq_ref, kvna_ref, kvgq_ref = rest[-4:]

    def store_t(ref, which, pair, x2):
        xt = x2.T
        for a in range(2):
            ref[which, 2 * pair + a] = xt[a * HEAD_DIM:(a + 1) * HEAD_DIM]

    for i in range(NA_HEADS // 2):
        cols = slice(i * LANES, (i + 1) * LANES)
        q2 = zna_ref[:, cols] * QK_SCALE
        k2 = zna_ref[:, NA_W + i * LANES:NA_W + (i + 1) * LANES]
        v2 = zna_ref[:, 2 * NA_W + i * LANES:2 * NA_W + (i + 1) * LANES]
        store_t(kvna_ref, 0, i, k2)
        store_t(kvna_ref, 1, i, v2)
        blocks = [{"k": k2.astype(BF16), "k_nt": True, "r": _pair_values(v2), "r_nt": False}]
        mna_ref[:, cols] = _pair_attend(_pair_queries(q2), blocks).astype(BF16)

    gq, gk = _pair_gain(gqk_ref, 0), _pair_gain(gqk_ref, 1)
    k2 = _pair_rms(zgq_ref[:, GQA_QW:GQA_QW + GQA_KW], gk)
    v2 = zgq_ref[:, GQA_QW + GQA_KW:]
    store_t(kvgq_ref, 0, 0, k2)
    store_t(kvgq_ref, 1, 0, v2)
    blocks = [{"k": k2.astype(BF16), "k_nt": True, "r": _pair_values(v2), "r_nt": False}]
    for p in range(GQA_GROUP):
        cols = slice(p * LANES, (p + 1) * LANES)
        q2 = _pair_rms(zgq_ref[:, cols], gq) * QK_SCALE
        mgq_ref[:, cols] = _pair_attend(_pair_queries(q2), blocks).astype(BF16)


def _ctx_attn(zna, zgq, g_qk, kv_na_prev, kv_gq_prev, layer, batch, t):
    n = zna.shape[0]
    in_specs = [
        pl.BlockSpec((t, ZNA_W), lambda b: (b, 0)),
        pl.BlockSpec((t, ZGQ_W), lambda b: (b, 0)),
        pl.BlockSpec((None, 2, HEAD_DIM), lambda b: (layer, 0, 0)),
    ]
    args = [zna, zgq, g_qk]
    aliases = {}
    if kv_na_prev is not None:
        in_specs += [pl.BlockSpec(memory_space=pl.ANY)] * 2
        args += [kv_na_prev, kv_gq_prev]
        aliases = {3: 2, 4: 3}
    return pl.pallas_call(
        _ctx_attn_kernel,
        grid=(batch,),
        in_specs=in_specs,
        out_specs=[
            pl.BlockSpec((t, NA_W), lambda b: (b, 0)),
            pl.BlockSpec((t, GQA_QW), lambda b: (b, 0)),
            pl.BlockSpec((None, None, 2, NA_HEADS, HEAD_DIM, t), lambda b: (b, layer, 0, 0, 0, 0)),
            pl.BlockSpec((None, None, 2, GQA_KV_HEADS, HEAD_DIM, t), lambda b: (b, layer, 0, 0, 0, 0)),
        ],
        out_shape=[
            jax.ShapeDtypeStruct((n, NA_W), BF16),
            jax.ShapeDtypeStruct((n, GQA_QW), BF16),
            jax.ShapeDtypeStruct((batch, DEPTH, 2, NA_HEADS, HEAD_DIM, t), F32),
            jax.ShapeDtypeStruct((batch, DEPTH, 2, GQA_KV_HEADS, HEAD_DIM, t), F32),
        ],
        input_output_aliases=aliases,
        compiler_params=_params(1),
        name="ctx_attn",
    )(*args)


def _lat_na_kernel(zna_ref, cache_ref, bias_ref, o_ref, *, rows):
    qn = NA_BLOCK_ROWS * GRID_W
    kn = NA_UNION_ROWS * GRID_W
    n_blocks = rows // NA_BLOCK_ROWS
    for i in range(NA_HEADS // 2):
        cols = slice(i * LANES, (i + 1) * LANES)
        q_pair = _pair_queries(zna_ref[:, cols] * QK_SCALE)
        k2 = zna_ref[:, NA_W + i * LANES:NA_W + (i + 1) * LANES].astype(BF16)
        r = _pair_values(zna_ref[:, 2 * NA_W + i * LANES:2 * NA_W + (i + 1) * LANES])
        kc = cache_ref[0, i].astype(BF16)
        rc = _pair_values_t(cache_ref[1, i])
        acc = None
        for a in range(2):
            s_ctx = jnp.dot(q_pair[a], kc, preferred_element_type=F32)
            s_win, m_blocks = [], []
            for p in range(n_blocks):
                k0 = _na_union_start(p, rows) * GRID_W
                s = lax.dot_general(q_pair[a][p * qn:(p + 1) * qn], k2[k0:k0 + kn], NT_DIMS,
                                    preferred_element_type=F32) + bias_ref[2 * i + a, p]
                s_win.append(s)
                m_blocks.append(s.max(axis=-1, keepdims=True))
            m = jnp.maximum(jnp.concatenate(m_blocks, axis=0), s_ctx.max(axis=-1, keepdims=True))
            term = lax.dot_general(jnp.exp(s_ctx - m).astype(BF16), rc[a], NT_DIMS, preferred_element_type=F32)
            wins = []
            for p in range(n_blocks):
                k0 = _na_union_start(p, rows) * GRID_W
                pw = jnp.exp(s_win[p] - m[p * qn:(p + 1) * qn]).astype(BF16)
                wins.append(jnp.dot(pw, r[a][k0:k0 + kn], preferred_element_type=F32))
            term = term + jnp.concatenate(wins, axis=0)
            acc = term if acc is None else acc + term
        o_ref[:, cols] = (acc[:, :LANES] / acc[:, LANES:]).astype(BF16)


def _lat_na(zna, cache_t, bias, layer, batch, t):
    n = zna.shape[0]
    rows = t // GRID_W
    past = cache_t.shape[-1]
    n_blocks = rows // NA_BLOCK_ROWS
    qn = NA_BLOCK_ROWS * GRID_W
    kn = NA_UNION_ROWS * GRID_W
    return pl.pallas_call(
        functools.partial(_lat_na_kernel, rows=rows),
        grid=(batch,),
        in_specs=[
            pl.BlockSpec((t, ZNA_W), lambda b: (b, 0)),
            pl.BlockSpec((None, None, 2, NA_HEADS // 2, LANES, past), lambda b: (b, layer, 0, 0, 0, 0)),
            pl.BlockSpec((None, NA_HEADS, n_blocks, qn, kn), lambda b: (layer, 0, 0, 0, 0)),
        ],
        out_specs=pl.BlockSpec((t, NA_W), lambda b: (b, 0)),
        out_shape=jax.ShapeDtypeStruct((n, NA_W), BF16),
        compiler_params=_params(1),
        name="lat_na",
    )(zna, cache_t, bias)


def _rope_tables(t):
    half = HEAD_DIM // 2
    quarter = half // 2
    inv = 1.0 / (ROPE_BASE ** (jnp.arange(quarter, dtype=F32) / quarter))
    tt = jnp.arange(t)
    row = (tt // GRID_W).astype(F32)
    col = (tt % GRID_W).astype(F32)
    ang_r = row[:, None] * inv[None, :]
    ang_c = col[:, None] * inv[None, :]
    cos = jnp.concatenate([jnp.cos(ang_r)] * 2 + [jnp.cos(ang_c)] * 2, axis=-1)
    sin = jnp.concatenate([-jnp.sin(ang_r), jnp.sin(ang_r), -jnp.sin(ang_c), jnp.sin(ang_c)], axis=-1)
    reps = LANES // HEAD_DIM
    return jnp.tile(cos, (1, reps)), jnp.tile(sin, (1, reps))


def _pair_rope(xn, cos, sin):
    quarter = HEAD_DIM // 4
    lane = lax.broadcasted_iota(jnp.int32, xn.shape, 1)
    lower = (lane & (2 * quarter - 1)) < quarter
    partner = jnp.where(lower, pltpu.roll(xn, LANES - quarter, 1), pltpu.roll(xn, quarter, 1))
    return xn * cos + partner * sin


def _lat_gqa_kernel(zgq_ref, cache_ref, gqk_ref, cos_ref, sin_ref, o_ref):
    cos = cos_ref[...]
    sin = sin_ref[...]
    gq, gk = _pair_gain(gqk_ref, 0), _pair_gain(gqk_ref, 1)
    keys = _pair_rope(_pair_rms(zgq_ref[:, GQA_QW:GQA_QW + GQA_KW], gk), cos, sin).astype(BF16)
    blocks = [
        {"k": keys, "k_nt": True, "r": _pair_values(zgq_ref[:, GQA_QW + GQA_KW:]), "r_nt": False},
        {"k": cache_ref[0].astype(BF16), "k_nt": False, "r": _pair_values_t(cache_ref[1]), "r_nt": True},
    ]
    for p in range(GQA_GROUP):
        x = _pair_rope(_pair_rms(zgq_ref[:, p * LANES:(p + 1) * LANES], gq), cos, sin) * QK_SCALE
        o_ref[:, p * LANES:(p + 1) * LANES] = _pair_attend(_pair_queries(x), blocks).astype(BF16)


def _lat_gqa(zgq, cache_t, g_qk, cos, sin, layer, batch, t):
    n = zgq.shape[0]
    past = cache_t.shape[-1]
    return pl.pallas_call(
        _lat_gqa_kernel,
        grid=(batch,),
        in_specs=[
            pl.BlockSpec((t, ZGQ_W), lambda b: (b, 0)),
            pl.BlockSpec((None, None, 2, GQA_KW, past), lambda b: (b, layer, 0, 0, 0)),
            pl.BlockSpec((None, 2, HEAD_DIM), lambda b: (layer, 0, 0)),
            pl.BlockSpec((t, LANES), lambda b: (0, 0)),
            pl.BlockSpec((t, LANES), lambda b: (0, 0)),
        ],
        out_specs=pl.BlockSpec((t, GQA_QW), lambda b: (b, 0)),
        out_shape=jax.ShapeDtypeStruct((n, GQA_QW), BF16),
        compiler_params=_params(1),
        name="lat_gqa",
    )(zgq, cache_t, g_qk, cos, sin)


def _split3(x):
    x1 = x.astype(BF16)
    r1 = x - x1.astype(F32)
    x2 = r1.astype(BF16)
    x3 = (r1 - x2.astype(F32)).astype(BF16)
    return x1, x2, x3


def _log_sigmoid(x):
    return jnp.minimum(x, 0.0) - jnp.log1p(jnp.exp(-jnp.abs(x)))


def _mlstm_select_matrix():
    H = MLSTM_HEADS
    sel = np.zeros((MLSTM_SEL_ROWS, (H // 2) * 2 * MLSTM_TILE_KINDS * LANES), np.float32)
    for j in range(H // 2):
        for d in range(2):
            for q in range(MLSTM_TILE_KINDS):
                for a in range(2):
                    col0 = ((j * 2 + d) * MLSTM_TILE_KINDS + q) * LANES + a * HEAD_DIM
                    sel[q * N_SCANS + d * H + 2 * j + a, col0:col0 + HEAD_DIM] = 1.0
    return sel


def _mlstm_kernel(zml_ref, gi_ref, gf_ref, grow_ref, c0_ref, n0_ref, m0_ref, gml_ref, sel_ref, *rest,
                  t, emit_state):
    if emit_state:
        o_ref, cf_ref, nf_ref, mf_ref = rest[:4]
    else:
        o_ref = rest[0]
    hs_ref, tri_s, st_s, dst_s, cst_s, row_s = rest[-6:]
    L = MLSTM_CHUNK
    H = MLSTM_HEADS
    HD = HEAD_DIM
    NP = H // 2
    nc = t // L
    tb = tri_s.shape[-1]

    @pl.when(pl.program_id(0) == 0)
    def _():
        ti = lax.broadcasted_iota(jnp.int32, (tb, tb), 0)
        ui = lax.broadcasted_iota(jnp.int32, (tb, tb), 1)
        same = (ti & -L) == (ui & -L)
        tri_s[0] = jnp.where(same & (ui <= ti), 1.0, 0.0).astype(BF16)
        tri_s[1] = jnp.where(same & (ui >= ti), 1.0, 0.0).astype(BF16)

    lower, upper = tri_s[0], tri_s[1]

    def chunk_sums_cols(x):
        parts = _split3(x)
        pre, suf = [], []
        for i in range(t // tb):
            blk = [p[i * tb:(i + 1) * tb] for p in parts]
            pre.append(sum(jnp.dot(lower, p, preferred_element_type=F32) for p in blk))
            suf.append(sum(jnp.dot(upper, p, preferred_element_type=F32) for p in blk))
        return jnp.concatenate(pre, axis=0), jnp.concatenate(suf, axis=0)

    def chunk_sums_rows(x):
        parts = _split3(x)
        pre, suf = [], []
        for i in range(t // tb):
            blk = [p[:, i * tb:(i + 1) * tb] for p in parts]
            pre.append(sum(jnp.dot(p, upper, preferred_element_type=F32) for p in blk))
            suf.append(sum(jnp.dot(p, lower, preferred_element_type=F32) for p in blk))
        return jnp.concatenate(pre, axis=1), jnp.concatenate(suf, axis=1)

    pre_c, suf_c = chunk_sums_cols(_log_sigmoid(gf_ref[...]))
    lane_c = lax.broadcasted_iota(jnp.int32, (t, N_SCANS), 1)
    b3 = jnp.where(lane_c < H, pre_c, suf_c).reshape(nc, L, N_SCANS)
    i3 = gi_ref[...].reshape(nc, L, N_SCANS)
    fwd3 = lax.broadcasted_iota(jnp.int32, (nc, 1, N_SCANS), 2) < H
    b_end3 = jnp.where(fwd3, b3[:, L - 1:L, :], b3[:, 0:1, :])
    lw_end3 = b_end3 - b3 + i3
    a3 = jnp.max(lw_end3, axis=1, keepdims=True)
    wloc3 = jnp.exp(lw_end3 - a3)

    fwd1 = lax.broadcasted_iota(jnp.int32, (1, N_SCANS), 1) < H
    m = m0_ref[...]
    m_start, carry_decay, contrib_scale = [], [], []
    for j in range(nc):
        a_j = jnp.where(fwd1, a3[j], a3[nc - 1 - j])
        g_j = jnp.where(fwd1, b_end3[j], b_end3[nc - 1 - j])
        m_start.append(m)
        m_next = jnp.maximum(g_j + m, a_j)
        carry_decay.append(jnp.exp(g_j + m - m_next))
        contrib_scale.append(jnp.exp(a_j - m_next))
        m = m_next
    mst3 = jnp.concatenate([jnp.where(fwd1, m_start[c], m_start[nc - 1 - c])[None] for c in range(nc)], axis=0)

    pad = jnp.zeros((t, MLSTM_SEL_ROWS - MLSTM_TILE_KINDS * N_SCANS), F32)
    cols = jnp.concatenate([b3.reshape(t, N_SCANS), (b3 + mst3).reshape(t, N_SCANS),
                            wloc3.reshape(t, N_SCANS), pad], axis=1)
    col_parts = _split3(cols)
    tile_w = MLSTM_TILE_KINDS * LANES

    def tiles(j, d):
        sel = sel_ref[:, (2 * j + d) * tile_w:(2 * j + d + 1) * tile_w]
        x = sum(jnp.dot(p, sel, preferred_element_type=F32) for p in col_parts)
        return [x[:, q * LANES:(q + 1) * LANES].reshape(nc, L, LANES) for q in range(MLSTM_TILE_KINDS)]

    gr = grow_ref[...]
    pre_r, suf_r = chunk_sums_rows(_log_sigmoid(gr))
    sub_r = lax.broadcasted_iota(jnp.int32, (N_SCANS, t), 0)
    rowv = gr[0:N_SCANS] - jnp.where(sub_r < H, pre_r[N_SCANS:], suf_r[N_SCANS:])
    for j in range(NP):
        for d in range(2):
            e = d * H + 2 * j
            for c in range(nc):
                row_s[2 * j + d, c] = jnp.concatenate(
                    [rowv[e:e + 1, c * L:(c + 1) * L], rowv[e + 1:e + 2, c * L:(c + 1) * L]], axis=1)

    lane_a = lax.broadcasted_iota(jnp.int32, (1, 1, LANES), 2) < HD
    sub_a = lax.broadcasted_iota(jnp.int32, (1, 2 * HD, 1), 1) < HD
    diag = sub_a == lane_a
    diag4 = jnp.concatenate([diag] * 4, axis=2)

    def stack_heads(x3):
        return jnp.concatenate([jnp.where(lane_a, x3, 0.0), jnp.where(lane_a, 0.0, x3)], axis=1)

    def pair_cols(base, j):
        return slice(base + j * LANES, base + (j + 1) * LANES)

    zero_blk = jnp.zeros((HD, HD), F32)
    for j in range(NP):
        cols_d = []
        for d in range(2):
            ca, cb = c0_ref[d, 2 * j].T, c0_ref[d, 2 * j + 1].T
            na = jnp.broadcast_to(n0_ref[d, 2 * j:2 * j + 1, :], (HD, HD)).T
            nb = jnp.broadcast_to(n0_ref[d, 2 * j + 1:2 * j + 2, :], (HD, HD)).T
            top = jnp.concatenate([ca, zero_blk, na, zero_blk], axis=1)
            bot = jnp.concatenate([zero_blk, cb, zero_blk, nb], axis=1)
            cols_d.append(jnp.concatenate([top, bot], axis=0))
        st_s[j] = jnp.concatenate(cols_d, axis=1)

    for j in range(NP):
        k3 = (zml_ref[:, pair_cols(ML_W, j)] * QK_SCALE).reshape(nc, L, LANES).astype(BF16)
        v3 = zml_ref[:, pair_cols(2 * ML_W, j)].reshape(nc, L, LANES)
        rhs = []
        for d in range(2):
            wl = tiles(j, d)[2]
            rhs += [v3 * wl, wl]
        rhs = jnp.concatenate(rhs, axis=2).astype(BF16)
        contrib = jnp.einsum("csk,csn->ckn", k3, rhs, preferred_element_type=F32)
        dst_s[j] = jnp.where(diag4, contrib, 0.0)

    def lane_scale(v, j):
        pieces = []
        for d in range(2):
            sa = jnp.broadcast_to(v[:, d * H + 2 * j:d * H + 2 * j + 1], (1, HD))
            sb = jnp.broadcast_to(v[:, d * H + 2 * j + 1:d * H + 2 * j + 2], (1, HD))
            pieces += [sa, sb, sa, sb]
        return jnp.concatenate(pieces, axis=1)

    for j in range(NP):
        st = st_s[j]
        for step in range(nc):
            cb = nc - 1 - step
            stb = st.astype(BF16)
            cst_s[j, step, :, 0:2 * LANES] = stb[:, 0:2 * LANES]
            cst_s[j, cb, :, 2 * LANES:] = stb[:, 2 * LANES:]
            delta = jnp.concatenate([dst_s[j, step, :, 0:2 * LANES], dst_s[j, cb, :, 2 * LANES:]], axis=1)
            st = lane_scale(carry_decay[step], j) * st + lane_scale(contrib_scale[step], j) * delta
        st_s[j] = st

    sidx = lax.broadcasted_iota(jnp.int32, (1, L, LANES), 2) & (HD - 1)
    tidx = lax.broadcasted_iota(jnp.int32, (1, L, LANES), 1)
    masks = (sidx <= tidx, sidx >= tidx)
    ones_blk = jnp.broadcast_to(jnp.where(diag, 1.0, 0.0).astype(BF16), (nc, 2 * HD, LANES))
    neg_inf = -jnp.inf
    for j in range(NP):
        q3 = zml_ref[:, pair_cols(0, j)].reshape(nc, L, LANES).astype(BF16)
        k3 = (zml_ref[:, pair_cols(ML_W, j)] * QK_SCALE).reshape(nc, L, LANES)
        v3 = zml_ref[:, pair_cols(2 * ML_W, j)].reshape(nc, L, LANES)
        qk = jnp.einsum("ctd,cnd->ctn", q3, stack_heads(k3).astype(BF16), preferred_element_type=F32)
        v_aug = jnp.concatenate([stack_heads(v3).astype(BF16), ones_blk], axis=2)
        out = None
        for d in range(2):
            b_t, bm_t, _ = tiles(j, d)
            logw = jnp.where(masks[d], b_t + row_s[2 * j + d], neg_inf)
            rmax_a = jnp.max(jnp.where(lane_a, logw, neg_inf), axis=-1, keepdims=True)
            rmax_b = jnp.max(jnp.where(lane_a, neg_inf, logw), axis=-1, keepdims=True)
            m_t = jnp.maximum(jnp.where(lane_a, rmax_a, rmax_b), bm_t)
            s = qk * jnp.exp(logw - m_t)
            decay = jnp.exp(bm_t - m_t)
            sv = jnp.einsum("cts,csn->ctn", s.astype(BF16), v_aug, preferred_element_type=F32)
            state = cst_s[j, :, :, 2 * d * LANES:2 * (d + 1) * LANES]
            inter = jnp.einsum("ctk,ckn->ctn", q3, state, preferred_element_type=F32)
            num = sv[:, :, 0:LANES] + decay * inter[:, :, 0:LANES]
            den = sv[:, :, LANES:] + decay * inter[:, :, LANES:]
            h_d = num / jnp.maximum(jnp.abs(den), jnp.exp(-m_t))
            out = h_d if out is None else out + h_d
        hs_ref[:, pair_cols(0, j)] = out.reshape(t, LANES)

    for h in range(H):
        lanes = slice(h * HD, (h + 1) * HD)
        hv = hs_ref[:, lanes]
        hn = hv * lax.rsqrt(jnp.mean(hv * hv, axis=-1, keepdims=True) + EPS) * gml_ref[:, lanes]
        og = jax.nn.sigmoid(zml_ref[:, 3 * ML_W + h * HD:3 * ML_W + (h + 1) * HD])
        o_ref[:, lanes] = (hn * og).astype(BF16)

    if emit_state:
        for j in range(NP):
            st = st_s[j]
            for d in range(2):
                for a in range(2):
                    rows = slice(a * HD, (a + 1) * HD)
                    c0 = 2 * d * LANES + a * HD
                    cf_ref[d, 2 * j + a] = st[rows, c0:c0 + HD].T
                    nf_ref[d, 2 * j + a:2 * j + a + 1, :] = st[rows, c0 + LANES:c0 + LANES + HD].T[0:1, :]
        mf_ref[...] = m


def _mlstm(zml, gi, gf, grow, c0, n0, m0, g_ml, layer, batch, t, emit_state, state_layer):
    n = zml.shape[0]
    H = MLSTM_HEADS
    L = MLSTM_CHUNK
    nc = t // L
    sel = jnp.asarray(_mlstm_select_matrix(), BF16)
    tri_block = min(t, 256)
    assert L & (L - 1) == 0 and tri_block % L == 0 and t % tri_block == 0
    if state_layer is None:
        c_spec = pl.BlockSpec((None, 2, H, HEAD_DIM, HEAD_DIM), lambda b: (0, 0, 0, 0, 0))
        n_spec = pl.BlockSpec((None, 2, H, HEAD_DIM), lambda b: (0, 0, 0, 0))
        m_spec = pl.BlockSpec((None, 1, N_SCANS), lambda b: (0, 0, 0))
    else:
        c_spec = pl.BlockSpec((None, None, 2, H, HEAD_DIM, HEAD_DIM), lambda b: (b, state_layer, 0, 0, 0, 0))
        n_spec = pl.BlockSpec((None, None, 2, H, HEAD_DIM), lambda b: (b, state_layer, 0, 0, 0))
        m_spec = pl.BlockSpec((None, None, 1, N_SCANS), lambda b: (b, state_layer, 0, 0))
    out_specs = [pl.BlockSpec((t, ML_W), lambda b: (b, 0))]
    out_shape = [jax.ShapeDtypeStruct((n, ML_W), BF16)]
    if emit_state:
        out_specs += [
            pl.BlockSpec((None, 2, H, HEAD_DIM, HEAD_DIM), lambda b: (b, 0, 0, 0, 0)),
            pl.BlockSpec((None, 2, H, HEAD_DIM), lambda b: (b, 0, 0, 0)),
            pl.BlockSpec((None, 1, N_SCANS), lambda b: (b, 0, 0)),
        ]
        out_shape += [
            jax.ShapeDtypeStruct((batch, 2, H, HEAD_DIM, HEAD_DIM), F32),
            jax.ShapeDtypeStruct((batch, 2, H, HEAD_DIM), F32),
            jax.ShapeDtypeStruct((batch, 1, N_SCANS), F32),
        ]
    return pl.pallas_call(
        functools.partial(_mlstm_kernel, t=t, emit_state=emit_state),
        grid=(batch,),
        in_specs=[
            pl.BlockSpec((t, ZML_W), lambda b: (b, 0)),
            pl.BlockSpec((t, N_SCANS), lambda b: (b, 0)),
            pl.BlockSpec((t, N_SCANS), lambda b: (b, 0)),
            pl.BlockSpec((N_GATES, t), lambda b: (0, b)),
            c_spec, n_spec, m_spec,
            pl.BlockSpec((None, 1, ML_W), lambda b: (layer, 0, 0)),
            pl.BlockSpec(sel.shape, lambda b: (0, 0)),
        ],
        out_specs=out_specs,
        out_shape=out_shape,
        scratch_shapes=[
            pltpu.VMEM((t, ML_W), F32),
            pltpu.VMEM((2, tri_block, tri_block), BF16),
            pltpu.VMEM((H // 2, 2 * HEAD_DIM, 4 * LANES), F32),
            pltpu.VMEM((H // 2, nc, 2 * HEAD_DIM, 4 * LANES), F32),
            pltpu.VMEM((H // 2, nc, 2 * HEAD_DIM, 4 * LANES), BF16),
            pltpu.VMEM((H, nc, 1, LANES), F32),
        ],
        compiler_params=_params(1),
        name="mlstm",
    )(zml, gi, gf, grow, c0, n0, m0, g_ml, sel)


def _layer_path(x, mods, layer, first_row, tiles_row_tokens, weights, mixers, tm_in, tm_out, tm_ffn):
    per_row = lambda tm: None if tiles_row_tokens is None else tiles_row_tokens // tm
    g_norm = weights["g_norm"]
    zna, zgq, zml, gi, gf, grow = _inproj(x, mods, g_norm, weights["w_in"], layer, per_row(tm_in), first_row,
                                          tm_in)
    mna, mgq, mml, extra = mixers(zna, zgq, zml, gi, gf, grow)
    x = _outproj(x, mna, mgq, mml, mods, g_norm, weights["w_out"], layer, per_row(tm_out), first_row, tm_out)
    x = _ffn(x, mods, g_norm, weights["w_gu"], weights["w_down"], layer, per_row(tm_ffn), first_row,
             tm_ffn, FF_HIDDEN // 2)
    return x, extra


def _gqa_pair_order():
    return [a * GQA_GROUP + p for p in range(GQA_GROUP) for a in range(GQA_KV_HEADS)]


def _take_blocks(x, axis, base, width, order):
    return jnp.concatenate([lax.slice_in_dim(x, base + width * o, base + width * (o + 1), axis=axis)
                            for o in order], axis=axis)


def kernel(x_prompt, x_sample, cache_na_kv, cache_gqa_kv, state_mlstm_C, state_mlstm_n, state_mlstm_m,
           c, c_ctx, w_in, b_gates, w_out, g_norm, g_qk, g_mlstm, na_bias, w_ada, b_ada, w_gu, w_down):
    batch, seq, _ = x_prompt.shape
    dec_batch, dec_seq, _ = x_sample.shape
    past = cache_na_kv.shape[-2]
    assert dec_batch + 1 <= N_MOD_ROWS and dec_seq % GRID_W == 0 and GQA_KV_HEADS == 2

    cvec = jnp.concatenate([c_ctx[None, :], c, jnp.zeros((N_MOD_ROWS - 1 - dec_batch, D_MODEL), F32)], axis=0)
    mods = _adaln(cvec, w_ada, b_ada).reshape(DEPTH, N_MOD_ROWS, 6, D_MODEL)
    bias = _na_bias_expand(na_bias, dec_seq // GRID_W)
    cos, sin = _rope_tables(dec_seq)

    pair_order = _gqa_pair_order()
    scan_order = [2 * d for d in range(2)]
    H = MLSTM_HEADS
    w_gi = _take_blocks(w_in, 2, OFF_GATES, H, scan_order)
    w_gf = _take_blocks(w_in, 2, OFF_GATES + H, H, scan_order)
    b_gi = _take_blocks(b_gates, 1, 0, H, scan_order)
    b_gf = _take_blocks(b_gates, 1, H, H, scan_order)
    lane_pad = lambda a: jnp.pad(a, ((0, 0), (0, 0), (0, LANES - N_SCANS)))
    w_gq = jnp.concatenate([_take_blocks(w_in, 2, OFF_GQ, HEAD_DIM, pair_order),
                            w_in[:, :, OFF_GQ + GQA_QW:OFF_ML]], axis=2)
    w_out_rows = jnp.concatenate([w_out[:, :NA_W], _take_blocks(w_out, 1, NA_W, HEAD_DIM, pair_order),
                                  w_out[:, NA_W + GQA_QW:]], axis=1)
    weights = {
        "g_norm": g_norm,
        "w_in": {
            "na": w_in[:, :, :OFF_GQ].astype(BF16),
            "gq": w_gq.astype(BF16),
            "ml": w_in[:, :, OFF_ML:OFF_GATES].astype(BF16),
            "gate_col": jnp.concatenate([lane_pad(w_gi), lane_pad(w_gf)], axis=-1).astype(BF16),
            "gate_row": jnp.swapaxes(jnp.concatenate([w_gi, w_gf], axis=-1), 1, 2).astype(BF16),
            "b_col": jnp.stack([b_gi, b_gf], axis=1),
            "b_row": jnp.concatenate([b_gi, b_gf], axis=-1)[:, :, None],
        },
        "w_out": w_out_rows.astype(BF16),
        "w_gu": w_gu.astype(BF16),
        "w_down": w_down.astype(BF16),
    }
    g_ml = g_mlstm.reshape(DEPTH, 1, ML_W)
    zero_c = jnp.zeros((1, 2, MLSTM_HEADS, HEAD_DIM, HEAD_DIM), F32)
    zero_n = jnp.zeros((1, 2, MLSTM_HEADS, HEAD_DIM), F32)
    zero_m = jnp.zeros((1, 1, N_SCANS), F32)
    m0_lat = state_mlstm_m.reshape(dec_batch, DEPTH, 1, N_SCANS)
    cache_na_t = jnp.swapaxes(cache_na_kv, -1, -2).reshape(dec_batch, DEPTH, 2, NA_HEADS // 2, LANES, past)
    cache_gq_t = jnp.swapaxes(cache_gqa_kv, -1, -2).reshape(dec_batch, DEPTH, 2, GQA_KW, past)

    xp = x_prompt.reshape(batch * seq, D_MODEL)
    xs = x_sample.reshape(dec_batch * dec_seq, D_MODEL)
    kv_na = kv_gq = None
    c_l, n_l, m_l = [], [], []
    for layer in range(DEPTH):
        def ctx_mixers(zna, zgq, zml, gi, gf, grow, layer=layer, kv_na=kv_na, kv_gq=kv_gq):
            mna, mgq, kv_na, kv_gq = _ctx_attn(zna, zgq, g_qk, kv_na, kv_gq, layer, batch, seq)
            mml, cf, nf, mf = _mlstm(zml, gi, gf, grow, zero_c, zero_n, zero_m, g_ml, layer, batch, seq,
                                     True, None)
            return mna, mgq, mml, (kv_na, kv_gq, cf, nf, mf.reshape(batch, 2, MLSTM_HEADS))

        def lat_mixers(zna, zgq, zml, gi, gf, grow, layer=layer):
            mna = _lat_na(zna, cache_na_t, bias, layer, dec_batch, dec_seq)
            mgq = _lat_gqa(zgq, cache_gq_t, g_qk, cos, sin, layer, dec_batch, dec_seq)
            (mml,) = _mlstm(zml, gi, gf, grow, state_mlstm_C, state_mlstm_n, m0_lat, g_ml, layer,
                            dec_batch, dec_seq, False, layer)
            return mna, mgq, mml, None

        xp, (kv_na, kv_gq, cf, nf, mf) = _layer_path(xp, mods, layer, 0, None, weights, ctx_mixers,
                                                       512, 512, 512)
        c_l.append(cf)
        n_l.append(nf)
        m_l.append(mf)
        xs, _ = _layer_path(xs, mods, layer, 1, dec_seq, weights, lat_mixers, 512, 512, 512)

    return (xp.reshape(batch, seq, D_MODEL), xs.reshape(dec_batch, dec_seq, D_MODEL),
            jnp.swapaxes(kv_na, -1, -2), jnp.swapaxes(kv_gq, -1, -2),
            jnp.stack(c_l, axis=1), jnp.stack(n_l, axis=1), jnp.stack(m_l, axis=1))
```

```python
import functools

import jax
import jax.numpy as jnp
import numpy as np
from jax import lax
from jax.experimental import pallas as pl
from jax.experimental.pallas import tpu as pltpu

D_MODEL = 1024
DEPTH = 4
GRID_W = 64
HEAD_DIM = 64
NA_HEADS = 4
GQA_Q_HEADS = 8
GQA_KV_HEADS = 2
GQA_GROUP = GQA_Q_HEADS // GQA_KV_HEADS
MLSTM_HEADS = 4
NA_WIN_ROWS = 8
NA_WIN_COLS = 16
MLSTM_CHUNK = 64
ROPE_BASE = 10000.0
EPS = 1e-6
NEG = -1e30
NA_W = NA_HEADS * HEAD_DIM
GQA_QW = GQA_Q_HEADS * HEAD_DIM
GQA_KW = GQA_KV_HEADS * HEAD_DIM
ML_W = MLSTM_HEADS * HEAD_DIM
N_GATES = 4 * MLSTM_HEADS
N_SCANS = 2 * MLSTM_HEADS
MLSTM_TILE_KINDS = 2
MLSTM_SEL_ROWS = MLSTM_TILE_KINDS * N_SCANS
FF_HIDDEN = ((8 * D_MODEL + 3 * 256 - 1) // (3 * 256)) * 256
QK_SCALE = HEAD_DIM ** -0.5

ZNA_W = 3 * NA_W
ZGQ_W = GQA_QW + 2 * GQA_KW
ZML_W = 4 * ML_W
OFF_GQ = ZNA_W
OFF_ML = ZNA_W + ZGQ_W
OFF_GATES = OFF_ML + ZML_W

LANES = 128
N_MOD_ROWS = 16
NA_BLOCK_ROWS = 2
NA_UNION_ROWS = NA_WIN_ROWS + NA_BLOCK_ROWS - 1

F32 = jnp.float32
BF16 = jnp.bfloat16
VMEM_LIMIT = 52 * 1024 * 1024

NT_DIMS = (((1,), (1,)), ((), ()))


def _params(n_axes):
    return pltpu.CompilerParams(dimension_semantics=("arbitrary",) * n_axes,
                                vmem_limit_bytes=VMEM_LIMIT)


def _rms(x, g):
    return x * lax.rsqrt(jnp.mean(x * x, axis=-1, keepdims=True) + EPS) * g


def _lane_first(shape):
    return lax.broadcasted_iota(jnp.int32, shape, len(shape) - 1) < HEAD_DIM


def _pair_queries(x):
    first = _lane_first(x.shape)
    return jnp.where(first, x, 0.0).astype(BF16), jnp.where(first, 0.0, x).astype(BF16)


def _pair_values(v2):
    first = _lane_first(v2.shape)
    ones, zeros = jnp.ones_like(v2), jnp.zeros_like(v2)
    r0 = jnp.concatenate([jnp.where(first, v2, 0.0), jnp.where(first, ones, zeros)], axis=1)
    r1 = jnp.concatenate([jnp.where(first, 0.0, v2), jnp.where(first, zeros, ones)], axis=1)
    return r0.astype(BF16), r1.astype(BF16)


def _pair_values_t(vt2):
    first = lax.broadcasted_iota(jnp.int32, vt2.shape, 0) < HEAD_DIM
    ones, zeros = jnp.ones_like(vt2), jnp.zeros_like(vt2)
    r0 = jnp.concatenate([jnp.where(first, vt2, 0.0), jnp.where(first, ones, zeros)], axis=0)
    r1 = jnp.concatenate([jnp.where(first, 0.0, vt2), jnp.where(first, zeros, ones)], axis=0)
    return r0.astype(BF16), r1.astype(BF16)


def _pair_attend(q_pair, blocks):
    acc = None
    for a in range(2):
        scores = []
        for blk in blocks:
            if blk["k_nt"]:
                s = lax.dot_general(q_pair[a], blk["k"], NT_DIMS, preferred_element_type=F32)
            else:
                s = jnp.dot(q_pair[a], blk["k"], preferred_element_type=F32)
            if blk.get("bias") is not None:
                s = s + blk["bias"][a]
            scores.append(s)
        m = scores[0].max(axis=-1, keepdims=True)
        for s in scores[1:]:
            m = jnp.maximum(m, s.max(axis=-1, keepdims=True))
        for s, blk in zip(scores, blocks):
            p = jnp.exp(s - m).astype(BF16)
            if blk["r_nt"]:
                term = lax.dot_general(p, blk["r"][a], NT_DIMS, preferred_element_type=F32)
            else:
                term = jnp.dot(p, blk["r"][a], preferred_element_type=F32)
            acc = term if acc is None else acc + term
    return acc[:, :LANES] / acc[:, LANES:]


def _adaln_kernel(c_ref, w_ref, b_ref, o_ref):
    c = c_ref[...]
    a = c * jax.nn.sigmoid(c)
    o_ref[...] = jnp.dot(a.astype(BF16), w_ref[...].astype(BF16),
                         preferred_element_type=F32) + b_ref[...]


def _adaln(cvec, w_ada, b_ada):
    tn = 1536
    return pl.pallas_call(
        _adaln_kernel,
        grid=(DEPTH, 6 * D_MODEL // tn),
        in_specs=[
            pl.BlockSpec((N_MOD_ROWS, D_MODEL), lambda l, j: (0, 0)),
            pl.BlockSpec((None, D_MODEL, tn), lambda l, j: (l, 0, j)),
            pl.BlockSpec((None, 1, tn), lambda l, j: (l, 0, j)),
        ],
        out_specs=pl.BlockSpec((None, N_MOD_ROWS, tn), lambda l, j: (l, 0, j)),
        out_shape=jax.ShapeDtypeStruct((DEPTH, N_MOD_ROWS, 6 * D_MODEL), F32),
        compiler_params=_params(2),
        name="adaln",
    )(cvec, w_ada, b_ada.reshape(DEPTH, 1, 6 * D_MODEL))


def _na_r0(r, rows):
    return min(max(r - NA_WIN_ROWS // 2, 0), rows - NA_WIN_ROWS)


def _na_union_start(p, rows):
    return min(_na_r0(NA_BLOCK_ROWS * p, rows), rows - NA_UNION_ROWS)


def _na_bias_kernel(tbl_ref, o_ref, *, rows):
    l = pl.program_id(0)
    h = pl.program_id(1)
    qi = lax.broadcasted_iota(jnp.int32, (GRID_W, GRID_W), 0)
    ki = lax.broadcasted_iota(jnp.int32, (GRID_W, GRID_W), 1)
    dc = jnp.clip(ki - qi, -(NA_WIN_COLS - 1), NA_WIN_COLS - 1) + NA_WIN_COLS - 1
    c0 = jnp.clip(qi - NA_WIN_COLS // 2, 0, GRID_W - NA_WIN_COLS)
    col_ok = (ki >= c0) & (ki < c0 + NA_WIN_COLS)
    n_dr = 2 * NA_WIN_ROWS - 1
    n_dc = 2 * NA_WIN_COLS - 1
    tiles = []
    for dr in range(n_dr):
        t = jnp.zeros((GRID_W, GRID_W), F32)
        for d in range(n_dc):
            t = jnp.where(dc == d, tbl_ref[((l * NA_HEADS + h) * n_dr + dr) * n_dc + d], t)
        tiles.append(jnp.where(col_ok, t, NEG))
    neg_tile = jnp.full((GRID_W, GRID_W), NEG, F32)
    for p in range(rows // NA_BLOCK_ROWS):
        start = _na_union_start(p, rows)
        for a in range(NA_BLOCK_ROWS):
            r = NA_BLOCK_ROWS * p + a
            r0 = _na_r0(r, rows)
            for j in range(NA_UNION_ROWS):
                kr = start + j
                inside = r0 <= kr < r0 + NA_WIN_ROWS
                tile = tiles[kr - r + NA_WIN_ROWS - 1] if inside else neg_tile
                o_ref[p, a * GRID_W:(a + 1) * GRID_W, j * GRID_W:(j + 1) * GRID_W] = tile


def _na_bias_expand(na_bias, rows):
    n_blocks = rows // NA_BLOCK_ROWS
    qn = NA_BLOCK_ROWS * GRID_W
    kn = NA_UNION_ROWS * GRID_W
    return pl.pallas_call(
        functools.partial(_na_bias_kernel, rows=rows),
        grid=(DEPTH, NA_HEADS),
        in_specs=[pl.BlockSpec(memory_space=pltpu.SMEM)],
        out_specs=pl.BlockSpec((None, None, n_blocks, qn, kn), lambda l, h: (l, h, 0, 0, 0)),
        out_shape=jax.ShapeDtypeStruct((DEPTH, NA_HEADS, n_blocks, qn, kn), F32),
        compiler_params=_params(2),
        name="na_bias_expand",
    )(na_bias.reshape(-1))


def _mod_row_map(layer, tiles_per_row, first_row):
    if tiles_per_row is None:
        return lambda i: (layer, first_row, 0, 0)
    return lambda i: (layer, first_row + i // tiles_per_row, 0, 0)


def _inproj_kernel(x_ref, mod_ref, g_ref, wna_ref, wgq_ref, wml_ref, wgc_ref, wgr_ref, bgc_ref, bgr_ref,
                   zna_ref, zgq_ref, zml_ref, gi_ref, gf_ref, grow_ref):
    x = x_ref[...]
    h = _rms(x, g_ref[0:1, :]) * (1.0 + mod_ref[1:2, :]) + mod_ref[0:1, :]
    hb = h.astype(BF16)
    zna_ref[...] = jnp.dot(hb, wna_ref[...], preferred_element_type=F32)
    zgq_ref[...] = jnp.dot(hb, wgq_ref[...], preferred_element_type=F32)
    zml_ref[...] = jnp.dot(hb, wml_ref[...], preferred_element_type=F32)
    gc = jnp.dot(hb, wgc_ref[...], preferred_element_type=F32)
    gi_ref[...] = gc[:, 0:N_SCANS] + bgc_ref[0:1, :]
    gf_ref[...] = gc[:, LANES:LANES + N_SCANS] + bgc_ref[1:2, :]
    grow_ref[...] = lax.dot_general(wgr_ref[...], hb, NT_DIMS, preferred_element_type=F32) + bgr_ref[...]


def _inproj(x, mods, g_norm, w, layer, tiles_per_row, first_row, tm):
    n = x.shape[0]
    wspec = lambda width: pl.BlockSpec((None, D_MODEL, width), lambda i: (layer, 0, 0))
    return pl.pallas_call(
        _inproj_kernel,
        grid=(n // tm,),
        in_specs=[
            pl.BlockSpec((tm, D_MODEL), lambda i: (i, 0)),
            pl.BlockSpec((None, None, 6, D_MODEL), _mod_row_map(layer, tiles_per_row, first_row)),
            pl.BlockSpec((None, 4, D_MODEL), lambda i: (layer, 0, 0)),
            wspec(ZNA_W), wspec(ZGQ_W), wspec(ZML_W), wspec(2 * LANES),
            pl.BlockSpec((None, N_GATES, D_MODEL), lambda i: (layer, 0, 0)),
            pl.BlockSpec((None, 2, N_SCANS), lambda i: (layer, 0, 0)),
            pl.BlockSpec((None, N_GATES, 1), lambda i: (layer, 0, 0)),
        ],
        out_specs=[
            pl.BlockSpec((tm, ZNA_W), lambda i: (i, 0)),
            pl.BlockSpec((tm, ZGQ_W), lambda i: (i, 0)),
            pl.BlockSpec((tm, ZML_W), lambda i: (i, 0)),
            pl.BlockSpec((tm, N_SCANS), lambda i: (i, 0)),
            pl.BlockSpec((tm, N_SCANS), lambda i: (i, 0)),
            pl.BlockSpec((N_GATES, tm), lambda i: (0, i)),
        ],
        out_shape=[
            jax.ShapeDtypeStruct((n, ZNA_W), F32),
            jax.ShapeDtypeStruct((n, ZGQ_W), F32),
            jax.ShapeDtypeStruct((n, ZML_W), F32),
            jax.ShapeDtypeStruct((n, N_SCANS), F32),
            jax.ShapeDtypeStruct((n, N_SCANS), F32),
            jax.ShapeDtypeStruct((N_GATES, n), F32),
        ],
        compiler_params=_params(1),
        name="inproj",
    )(x, mods, g_norm, w["na"], w["gq"], w["ml"], w["gate_col"], w["gate_row"], w["b_col"], w["b_row"])


def _post_kernel(x_ref, mna_ref, mgq_ref, mml_ref, mod_ref, g_ref, wo_ref, wgu_ref, wd_ref, o_ref):
    acc = jnp.dot(mna_ref[...], wo_ref[0:NA_W, :], preferred_element_type=F32)
    acc += jnp.dot(mgq_ref[...], wo_ref[NA_W:NA_W + GQA_QW, :], preferred_element_type=F32)
    acc += jnp.dot(mml_ref[...], wo_ref[NA_W + GQA_QW:, :], preferred_element_type=F32)
    x1 = x_ref[...] + mod_ref[2:3, :] * _rms(acc, g_ref[1:2, :])
    hb = (_rms(x1, g_ref[2:3, :]) * (1.0 + mod_ref[4:5, :]) + mod_ref[3:4, :]).astype(BF16)
    gate = jnp.dot(hb, wgu_ref[:, 0:FF_HIDDEN], preferred_element_type=F32)
    up = jnp.dot(hb, wgu_ref[:, FF_HIDDEN:], preferred_element_type=F32)
    act = (gate * jax.nn.sigmoid(gate) * up).astype(BF16)
    f = jnp.dot(act, wd_ref[...], preferred_element_type=F32)
    o_ref[...] = x1 + mod_ref[5:6, :] * _rms(f, g_ref[3:4, :])


def _post(x, mna, mgq, mml, mods, g_norm, w_out, w_gu, w_down, layer, tiles_per_row, first_row, tm):
    n = x.shape[0]
    resident = lambda shape: pl.BlockSpec((None,) + shape, lambda i: (layer, 0, 0), pipeline_mode=pl.Buffered(1))
    return pl.pallas_call(
        _post_kernel,
        grid=(n // tm,),
        in_specs=[
            pl.BlockSpec((tm, D_MODEL), lambda i: (i, 0)),
            pl.BlockSpec((tm, NA_W), lambda i: (i, 0)),
            pl.BlockSpec((tm, GQA_QW), lambda i: (i, 0)),
            pl.BlockSpec((tm, ML_W), lambda i: (i, 0)),
            pl.BlockSpec((None, None, 6, D_MODEL), _mod_row_map(layer, tiles_per_row, first_row)),
            pl.BlockSpec((None, 4, D_MODEL), lambda i: (layer, 0, 0)),
            resident((D_MODEL, D_MODEL)),
            resident((D_MODEL, 2 * FF_HIDDEN)),
            resident((FF_HIDDEN, D_MODEL)),
        ],
        out_specs=pl.BlockSpec((tm, D_MODEL), lambda i: (i, 0)),
        out_shape=jax.ShapeDtypeStruct((n, D_MODEL), F32),
        compiler_params=_params(1),
        name="post",
    )(x, mna, mgq, mml, mods, g_norm, w_out, w_gu, w_down)


def _pair_rms(x, gain, sums_on_mxu=False):
    xsq = x * x
    if sums_on_mxu:
        row = lax.broadcasted_iota(jnp.int32, (LANES, LANES), 0) < HEAD_DIM
        col = lax.broadcasted_iota(jnp.int32, (LANES, LANES), 1) < HEAD_DIM
        ones_blk = jnp.where(row == col, 1.0, 0.0).astype(BF16)
        sums = sum(jnp.dot(p, ones_blk, preferred_element_type=F32) for p in _split3(xsq))
    else:
        first = _lane_first(x.shape)
        sums = jnp.where(first, jnp.sum(jnp.where(first, xsq, 0.0), axis=-1, keepdims=True),
                         jnp.sum(jnp.where(first, 0.0, xsq), axis=-1, keepdims=True))
    return x * lax.rsqrt(sums * (1.0 / HEAD_DIM) + EPS) * gain


def _pair_gain(gqk_ref, row):
    return jnp.concatenate([gqk_ref[row:row + 1, :]] * (LANES // HEAD_DIM), axis=-1)


def _ctx_attn_kernel(zna_ref, zgq_ref, gqk_ref, *rest):
    mna_ref, mgq_ref, kvna_ref, kvgq_ref = rest[-4:]

    def store_t(ref, which, pair, x2):
        xt = x2.T
        for a in range(2):
            ref[which, 2 * pair + a] = xt[a * HEAD_DIM:(a + 1) * HEAD_DIM]

    for i in range(NA_HEADS // 2):
        cols = slice(i * LANES, (i + 1) * LANES)
        q2 = zna_ref[:, cols] * QK_SCALE
        k2 = zna_ref[:, NA_W + i * LANES:NA_W + (i + 1) * LANES]
        v2 = zna_ref[:, 2 * NA_W + i * LANES:2 * NA_W + (i + 1) * LANES]
        store_t(kvna_ref, 0, i, k2)
        store_t(kvna_ref, 1, i, v2)
        blocks = [{"k": k2.astype(BF16), "k_nt": True, "r": _pair_values(v2), "r_nt": False}]
        mna_ref[:, cols] = _pair_attend(_pair_queries(q2), blocks).astype(BF16)

    gq, gk = _pair_gain(gqk_ref, 0), _pair_gain(gqk_ref, 1)
    k2 = _pair_rms(zgq_ref[:, GQA_QW:GQA_QW + GQA_KW], gk)
    v2 = zgq_ref[:, GQA_QW + GQA_KW:]
    store_t(kvgq_ref, 0, 0, k2)
    store_t(kvgq_ref, 1, 0, v2)
    blocks = [{"k": k2.astype(BF16), "k_nt": True, "r": _pair_values(v2), "r_nt": False}]
    for p in range(GQA_GROUP):
        cols = slice(p * LANES, (p + 1) * LANES)
        q2 = _pair_rms(zgq_ref[:, cols], gq) * QK_SCALE
        mgq_ref[:, cols] = _pair_attend(_pair_queries(q2), blocks).astype(BF16)


def _ctx_attn(zna, zgq, g_qk, kv_na_prev, kv_gq_prev, layer, batch, t):
    n = zna.shape[0]
    in_specs = [
        pl.BlockSpec((t, ZNA_W), lambda b: (b, 0)),
        pl.BlockSpec((t, ZGQ_W), lambda b: (b, 0)),
        pl.BlockSpec((None, 2, HEAD_DIM), lambda b: (layer, 0, 0)),
    ]
    args = [zna, zgq, g_qk]
    aliases = {}
    if kv_na_prev is not None:
        in_specs += [pl.BlockSpec(memory_space=pl.ANY)] * 2
        args += [kv_na_prev, kv_gq_prev]
        aliases = {3: 2, 4: 3}
    return pl.pallas_call(
        _ctx_attn_kernel,
        grid=(batch,),
        in_specs=in_specs,
        out_specs=[
            pl.BlockSpec((t, NA_W), lambda b: (b, 0)),
            pl.BlockSpec((t, GQA_QW), lambda b: (b, 0)),
            pl.BlockSpec((None, None, 2, NA_HEADS, HEAD_DIM, t), lambda b: (b, layer, 0, 0, 0, 0)),
            pl.BlockSpec((None, None, 2, GQA_KV_HEADS, HEAD_DIM, t), lambda b: (b, layer, 0, 0, 0, 0)),
        ],
        out_shape=[
            jax.ShapeDtypeStruct((n, NA_W), BF16),
            jax.ShapeDtypeStruct((n, GQA_QW), BF16),
            jax.ShapeDtypeStruct((batch, DEPTH, 2, NA_HEADS, HEAD_DIM, t), F32),
            jax.ShapeDtypeStruct((batch, DEPTH, 2, GQA_KV_HEADS, HEAD_DIM, t), F32),
        ],
        input_output_aliases=aliases,
        compiler_params=_params(1),
        name="ctx_attn",
    )(*args)


def _lat_na_kernel(zna_ref, cache_ref, bias_ref, o_ref, *, rows):
    qn = NA_BLOCK_ROWS * GRID_W
    kn = NA_UNION_ROWS * GRID_W
    n_blocks = rows // NA_BLOCK_ROWS
    for i in range(NA_HEADS // 2):
        cols = slice(i * LANES, (i + 1) * LANES)
        q_pair = _pair_queries(zna_ref[:, cols] * QK_SCALE)
        k2 = zna_ref[:, NA_W + i * LANES:NA_W + (i + 1) * LANES].astype(BF16)
        r = _pair_values(zna_ref[:, 2 * NA_W + i * LANES:2 * NA_W + (i + 1) * LANES])
        kc = cache_ref[0, i].astype(BF16)
        rc = _pair_values_t(cache_ref[1, i])
        acc = None
        for a in range(2):
            s_ctx = jnp.dot(q_pair[a], kc, preferred_element_type=F32)
            s_win, m_blocks = [], []
            for p in range(n_blocks):
                k0 = _na_union_start(p, rows) * GRID_W
                s = lax.dot_general(q_pair[a][p * qn:(p + 1) * qn], k2[k0:k0 + kn], NT_DIMS,
                                    preferred_element_type=F32) + bias_ref[2 * i + a, p]
                s_win.append(s)
                m_blocks.append(s.max(axis=-1, keepdims=True))
            m = jnp.maximum(jnp.concatenate(m_blocks, axis=0), s_ctx.max(axis=-1, keepdims=True))
            term = lax.dot_general(jnp.exp(s_ctx - m).astype(BF16), rc[a], NT_DIMS, preferred_element_type=F32)
            wins = []
            for p in range(n_blocks):
                k0 = _na_union_start(p, rows) * GRID_W
                pw = jnp.exp(s_win[p] - m[p * qn:(p + 1) * qn]).astype(BF16)
                wins.append(jnp.dot(pw, r[a][k0:k0 + kn], preferred_element_type=F32))
            term = term + jnp.concatenate(wins, axis=0)
            acc = term if acc is None else acc + term
        o_ref[:, cols] = (acc[:, :LANES] / acc[:, LANES:]).astype(BF16)


def _lat_na(zna, cache_t, bias, layer, batch, t):
    n = zna.shape[0]
    rows = t // GRID_W
    past = cache_t.shape[-1]
    n_blocks = rows // NA_BLOCK_ROWS
    qn = NA_BLOCK_ROWS * GRID_W
    kn = NA_UNION_ROWS * GRID_W
    return pl.pallas_call(
        functools.partial(_lat_na_kernel, rows=rows),
        grid=(batch,),
        in_specs=[
            pl.BlockSpec((t, ZNA_W), lambda b: (b, 0)),
            pl.BlockSpec((None, None, 2, NA_HEADS // 2, LANES, past), lambda b: (b, layer, 0, 0, 0, 0)),
            pl.BlockSpec((None, NA_HEADS, n_blocks, qn, kn), lambda b: (layer, 0, 0, 0, 0)),
        ],
        out_specs=pl.BlockSpec((t, NA_W), lambda b: (b, 0)),
        out_shape=jax.ShapeDtypeStruct((n, NA_W), BF16),
        compiler_params=_params(1),
        name="lat_na",
    )(zna, cache_t, bias)


def _rope_tables(t):
    half = HEAD_DIM // 2
    quarter = half // 2
    inv = 1.0 / (ROPE_BASE ** (jnp.arange(quarter, dtype=F32) / quarter))
    tt = jnp.arange(t)
    row = (tt // GRID_W).astype(F32)
    col = (tt % GRID_W).astype(F32)
    ang_r = row[:, None] * inv[None, :]
    ang_c = col[:, None] * inv[None, :]
    cos = jnp.concatenate([jnp.cos(ang_r)] * 2 + [jnp.cos(ang_c)] * 2, axis=-1)
    sin = jnp.concatenate([-jnp.sin(ang_r), jnp.sin(ang_r), -jnp.sin(ang_c), jnp.sin(ang_c)], axis=-1)
    reps = LANES // HEAD_DIM
    return jnp.tile(cos, (1, reps)), jnp.tile(sin, (1, reps))


def _pair_rope(xn, cos, sin):
    quarter = HEAD_DIM // 4
    lane = lax.broadcasted_iota(jnp.int32, xn.shape, 1)
    lower = (lane & (2 * quarter - 1)) < quarter
    partner = jnp.where(lower, pltpu.roll(xn, LANES - quarter, 1), pltpu.roll(xn, quarter, 1))
    return xn * cos + partner * sin


def _lat_gqa_kernel(zgq_ref, cache_ref, gqk_ref, cos_ref, sin_ref, o_ref):
    cos = cos_ref[...]
    sin = sin_ref[...]
    gq, gk = _pair_gain(gqk_ref, 0), _pair_gain(gqk_ref, 1)
    keys = _pair_rope(_pair_rms(zgq_ref[:, GQA_QW:GQA_QW + GQA_KW], gk, True), cos, sin).astype(BF16)
    blocks = [
        {"k": keys, "k_nt": True, "r": _pair_values(zgq_ref[:, GQA_QW + GQA_KW:]), "r_nt": False},
        {"k": cache_ref[0].astype(BF16), "k_nt": False, "r": _pair_values_t(cache_ref[1]), "r_nt": True},
    ]
    for p in range(GQA_GROUP):
        x = _pair_rope(_pair_rms(zgq_ref[:, p * LANES:(p + 1) * LANES], gq, True), cos, sin) * QK_SCALE
        o_ref[:, p * LANES:(p + 1) * LANES] = _pair_attend(_pair_queries(x), blocks).astype(BF16)


def _lat_gqa(zgq, cache_t, g_qk, cos, sin, layer, batch, t):
    n = zgq.shape[0]
    past = cache_t.shape[-1]
    return pl.pallas_call(
        _lat_gqa_kernel,
        grid=(batch,),
        in_specs=[
            pl.BlockSpec((t, ZGQ_W), lambda b: (b, 0)),
            pl.BlockSpec((None, None, 2, GQA_KW, past), lambda b: (b, layer, 0, 0, 0)),
            pl.BlockSpec((None, 2, HEAD_DIM), lambda b: (layer, 0, 0)),
            pl.BlockSpec((t, LANES), lambda b: (0, 0)),
            pl.BlockSpec((t, LANES), lambda b: (0, 0)),
        ],
        out_specs=pl.BlockSpec((t, GQA_QW), lambda b: (b, 0)),
        out_shape=jax.ShapeDtypeStruct((n, GQA_QW), BF16),
        compiler_params=_params(1),
        name="lat_gqa",
    )(zgq, cache_t, g_qk, cos, sin)


def _split3(x):
    x1 = x.astype(BF16)
    r1 = x - x1.astype(F32)
    x2 = r1.astype(BF16)
    x3 = (r1 - x2.astype(F32)).astype(BF16)
    return x1, x2, x3


def _log_sigmoid(x):
    return jnp.minimum(x, 0.0) - jnp.log1p(jnp.exp(-jnp.abs(x)))


def _mlstm_select_matrix():
    H = MLSTM_HEADS
    sel = np.zeros((MLSTM_SEL_ROWS, (H // 2) * 2 * MLSTM_TILE_KINDS * LANES), np.float32)
    for j in range(H // 2):
        for d in range(2):
            for q in range(MLSTM_TILE_KINDS):
                for a in range(2):
                    col0 = ((j * 2 + d) * MLSTM_TILE_KINDS + q) * LANES + a * HEAD_DIM
                    sel[q * N_SCANS + d * H + 2 * j + a, col0:col0 + HEAD_DIM] = 1.0
    return sel


def _mlstm_kernel(zml_ref, gi_ref, gf_ref, grow_ref, c0_ref, n0_ref, m0_ref, gml_ref, sel_ref, *rest,
                  t, emit_state):
    if emit_state:
        o_ref, cf_ref, nf_ref, mf_ref = rest[:4]
    else:
        o_ref = rest[0]
    tri_s, st_s, dst_s, cst_s, row_s = rest[-5:]
    L = MLSTM_CHUNK
    H = MLSTM_HEADS
    HD = HEAD_DIM
    NP = H // 2
    nc = t // L
    tb = tri_s.shape[-1]

    @pl.when(pl.program_id(0) == 0)
    def _():
        ti = lax.broadcasted_iota(jnp.int32, (tb, tb), 0)
        ui = lax.broadcasted_iota(jnp.int32, (tb, tb), 1)
        same = (ti & -L) == (ui & -L)
        tri_s[0] = jnp.where(same & (ui <= ti), 1.0, 0.0).astype(BF16)
        tri_s[1] = jnp.where(same & (ui >= ti), 1.0, 0.0).astype(BF16)

    lower, upper = tri_s[0], tri_s[1]

    def chunk_sums_cols(x):
        parts = _split3(x)
        pre, suf = [], []
        for i in range(t // tb):
            blk = [p[i * tb:(i + 1) * tb] for p in parts]
            pre.append(sum(jnp.dot(lower, p, preferred_element_type=F32) for p in blk))
            suf.append(sum(jnp.dot(upper, p, preferred_element_type=F32) for p in blk))
        return jnp.concatenate(pre, axis=0), jnp.concatenate(suf, axis=0)

    def chunk_sums_rows(x):
        parts = _split3(x)
        pre, suf = [], []
        for i in range(t // tb):
            blk = [p[:, i * tb:(i + 1) * tb] for p in parts]
            pre.append(sum(jnp.dot(p, upper, preferred_element_type=F32) for p in blk))
            suf.append(sum(jnp.dot(p, lower, preferred_element_type=F32) for p in blk))
        return jnp.concatenate(pre, axis=1), jnp.concatenate(suf, axis=1)

    pre_c, suf_c = chunk_sums_cols(_log_sigmoid(gf_ref[...]))
    lane_c = lax.broadcasted_iota(jnp.int32, (t, N_SCANS), 1)
    b3 = jnp.where(lane_c < H, pre_c, suf_c).reshape(nc, L, N_SCANS)
    i3 = gi_ref[...].reshape(nc, L, N_SCANS)
    fwd3 = lax.broadcasted_iota(jnp.int32, (nc, 1, N_SCANS), 2) < H
    b_end3 = jnp.where(fwd3, b3[:, L - 1:L, :], b3[:, 0:1, :])
    lw_end3 = b_end3 - b3 + i3
    a3 = jnp.max(lw_end3, axis=1, keepdims=True)
    wloc3 = jnp.exp(lw_end3 - a3)

    fwd1 = lax.broadcasted_iota(jnp.int32, (1, N_SCANS), 1) < H
    m = m0_ref[...]
    m_start, carry_decay, contrib_scale = [], [], []
    for j in range(nc):
        a_j = jnp.where(fwd1, a3[j], a3[nc - 1 - j])
        g_j = jnp.where(fwd1, b_end3[j], b_end3[nc - 1 - j])
        m_start.append(m)
        m_next = jnp.maximum(g_j + m, a_j)
        carry_decay.append(jnp.exp(g_j + m - m_next))
        contrib_scale.append(jnp.exp(a_j - m_next))
        m = m_next
    mst3 = jnp.concatenate([jnp.where(fwd1, m_start[c], m_start[nc - 1 - c])[None] for c in range(nc)], axis=0)

    cols = jnp.concatenate([b3.reshape(t, N_SCANS), wloc3.reshape(t, N_SCANS)], axis=1)
    tiles_all = sum(jnp.dot(p, sel_ref[...], preferred_element_type=F32) for p in _split3(cols))
    tile_w = MLSTM_TILE_KINDS * LANES

    def tiles(j, d):
        x = tiles_all[:, (2 * j + d) * tile_w:(2 * j + d + 1) * tile_w]
        return [x[:, q * LANES:(q + 1) * LANES].reshape(nc, L, LANES) for q in range(MLSTM_TILE_KINDS)]

    gr = grow_ref[...]
    pre_r, suf_r = chunk_sums_rows(_log_sigmoid(gr))
    sub_r = lax.broadcasted_iota(jnp.int32, (N_SCANS, t), 0)
    rowv = gr[0:N_SCANS] - jnp.where(sub_r < H, pre_r[N_SCANS:], suf_r[N_SCANS:])
    for j in range(NP):
        for d in range(2):
            e = d * H + 2 * j
            for c in range(nc):
                row_s[2 * j + d, c] = jnp.concatenate(
                    [rowv[e:e + 1, c * L:(c + 1) * L], rowv[e + 1:e + 2, c * L:(c + 1) * L]], axis=1)

    lane_a = lax.broadcasted_iota(jnp.int32, (1, 1, LANES), 2) < HD
    sub_a = lax.broadcasted_iota(jnp.int32, (1, 2 * HD, 1), 1) < HD
    diag = sub_a == lane_a
    diag4 = jnp.concatenate([diag] * 4, axis=2)

    def stack_heads(x3):
        return jnp.concatenate([jnp.where(lane_a, x3, 0.0), jnp.where(lane_a, 0.0, x3)], axis=1)

    def pair_cols(base, j):
        return slice(base + j * LANES, base + (j + 1) * LANES)

    zero_blk = jnp.zeros((HD, HD), F32)
    for j in range(NP):
        cols_d = []
        for d in range(2):
            ca, cb = c0_ref[d, 2 * j].T, c0_ref[d, 2 * j + 1].T
            na = jnp.broadcast_to(n0_ref[d, 2 * j:2 * j + 1, :], (HD, HD)).T
            nb = jnp.broadcast_to(n0_ref[d, 2 * j + 1:2 * j + 2, :], (HD, HD)).T
            top = jnp.concatenate([ca, zero_blk, na, zero_blk], axis=1)
            bot = jnp.concatenate([zero_blk, cb, zero_blk, nb], axis=1)
            cols_d.append(jnp.concatenate([top, bot], axis=0))
        st_s[j] = jnp.concatenate(cols_d, axis=1)

    for j in range(NP):
        k3 = (zml_ref[:, pair_cols(ML_W, j)] * QK_SCALE).reshape(nc, L, LANES).astype(BF16)
        v3 = zml_ref[:, pair_cols(2 * ML_W, j)].reshape(nc, L, LANES)
        rhs = []
        for d in range(2):
            wl = tiles(j, d)[1]
            rhs += [v3 * wl, wl]
        rhs = jnp.concatenate(rhs, axis=2).astype(BF16)
        contrib = jnp.einsum("csk,csn->ckn", k3, rhs, preferred_element_type=F32)
        dst_s[j] = jnp.where(diag4, contrib, 0.0)

    def lane_scale(v, j):
        pieces = []
        for d in range(2):
            sa = jnp.broadcast_to(v[:, d * H + 2 * j:d * H + 2 * j + 1], (1, HD))
            sb = jnp.broadcast_to(v[:, d * H + 2 * j + 1:d * H + 2 * j + 2], (1, HD))
            pieces += [sa, sb, sa, sb]
        return jnp.concatenate(pieces, axis=1)

    for j in range(NP):
        st = st_s[j]
        for step in range(nc):
            cb = nc - 1 - step
            stb = st.astype(BF16)
            cst_s[j, step, :, 0:2 * LANES] = stb[:, 0:2 * LANES]
            cst_s[j, cb, :, 2 * LANES:] = stb[:, 2 * LANES:]
            delta = jnp.concatenate([dst_s[j, step, :, 0:2 * LANES], dst_s[j, cb, :, 2 * LANES:]], axis=1)
            st = lane_scale(carry_decay[step], j) * st + lane_scale(contrib_scale[step], j) * delta
        st_s[j] = st

    sidx = lax.broadcasted_iota(jnp.int32, (1, L, LANES), 2) & (HD - 1)
    tidx = lax.broadcasted_iota(jnp.int32, (1, L, LANES), 1)
    masks = (sidx <= tidx, sidx >= tidx)
    ones_blk = jnp.broadcast_to(jnp.where(diag, 1.0, 0.0).astype(BF16), (nc, 2 * HD, LANES))
    neg_inf = -jnp.inf
    for j in range(NP):
        q3 = zml_ref[:, pair_cols(0, j)].reshape(nc, L, LANES).astype(BF16)
        k3 = (zml_ref[:, pair_cols(ML_W, j)] * QK_SCALE).reshape(nc, L, LANES)
        v3 = zml_ref[:, pair_cols(2 * ML_W, j)].reshape(nc, L, LANES)
        qk = jnp.einsum("ctd,cnd->ctn", q3, stack_heads(k3).astype(BF16), preferred_element_type=F32)
        v_aug = jnp.concatenate([stack_heads(v3).astype(BF16), ones_blk], axis=2)
        out = None
        for d in range(2):
            b_t = tiles(j, d)[0]
            e = d * H + 2 * j
            bm_t = b_t + jnp.where(lane_a, mst3[:, :, e:e + 1], mst3[:, :, e + 1:e + 2])
            logw = jnp.where(masks[d], b_t + row_s[2 * j + d], neg_inf)
            rmax_a = jnp.max(jnp.where(lane_a, logw, neg_inf), axis=-1, keepdims=True)
            rmax_b = jnp.max(jnp.where(lane_a, neg_inf, logw), axis=-1, keepdims=True)
            m_t = jnp.maximum(jnp.where(lane_a, rmax_a, rmax_b), bm_t)
            s = qk * jnp.exp(logw - m_t)
            decay = jnp.exp(bm_t - m_t)
            sv = jnp.einsum("cts,csn->ctn", s.astype(BF16), v_aug, preferred_element_type=F32)
            state = cst_s[j, :, :, 2 * d * LANES:2 * (d + 1) * LANES]
            inter = jnp.einsum("ctk,ckn->ctn", q3, state, preferred_element_type=F32)
            num = sv[:, :, 0:LANES] + decay * inter[:, :, 0:LANES]
            den = sv[:, :, LANES:] + decay * inter[:, :, LANES:]
            h_d = num / jnp.maximum(jnp.abs(den), jnp.exp(-m_t))
            out = h_d if out is None else out + h_d
        cols = pair_cols(0, j)
        og = jax.nn.sigmoid(zml_ref[:, pair_cols(3 * ML_W, j)])
        o_ref[:, cols] = (_pair_rms(out.reshape(t, LANES), gml_ref[:, cols]) * og).astype(BF16)

    if emit_state:
        for j in range(NP):
            st = st_s[j]
            for d in range(2):
                for a in range(2):
                    rows = slice(a * HD, (a + 1) * HD)
                    c0 = 2 * d * LANES + a * HD
                    cf_ref[d, 2 * j + a] = st[rows, c0:c0 + HD].T
                    nf_ref[d, 2 * j + a:2 * j + a + 1, :] = st[rows, c0 + LANES:c0 + LANES + HD].T[0:1, :]
        mf_ref[...] = m


def _mlstm(zml, gi, gf, grow, c0, n0, m0, g_ml, layer, batch, t, emit_state, state_layer):
    n = zml.shape[0]
    H = MLSTM_HEADS
    L = MLSTM_CHUNK
    nc = t // L
    sel = jnp.asarray(_mlstm_select_matrix(), BF16)
    tri_block = min(t, 256)
    assert L & (L - 1) == 0 and tri_block % L == 0 and t % tri_block == 0
    if state_layer is None:
        c_spec = pl.BlockSpec((None, 2, H, HEAD_DIM, HEAD_DIM), lambda b: (0, 0, 0, 0, 0))
        n_spec = pl.BlockSpec((None, 2, H, HEAD_DIM), lambda b: (0, 0, 0, 0))
        m_spec = pl.BlockSpec((None, 1, N_SCANS), lambda b: (0, 0, 0))
    else:
        c_spec = pl.BlockSpec((None, None, 2, H, HEAD_DIM, HEAD_DIM), lambda b: (b, state_layer, 0, 0, 0, 0))
        n_spec = pl.BlockSpec((None, None, 2, H, HEAD_DIM), lambda b: (b, state_layer, 0, 0, 0))
        m_spec = pl.BlockSpec((None, None, 1, N_SCANS), lambda b: (b, state_layer, 0, 0))
    out_specs = [pl.BlockSpec((t, ML_W), lambda b: (b, 0))]
    out_shape = [jax.ShapeDtypeStruct((n, ML_W), BF16)]
    if emit_state:
        out_specs += [
            pl.BlockSpec((None, 2, H, HEAD_DIM, HEAD_DIM), lambda b: (b, 0, 0, 0, 0)),
            pl.BlockSpec((None, 2, H, HEAD_DIM), lambda b: (b, 0, 0, 0)),
            pl.BlockSpec((None, 1, N_SCANS), lambda b: (b, 0, 0)),
        ]
        out_shape += [
            jax.ShapeDtypeStruct((batch, 2, H, HEAD_DIM, HEAD_DIM), F32),
            jax.ShapeDtypeStruct((batch, 2, H, HEAD_DIM), F32),
            jax.ShapeDtypeStruct((batch, 1, N_SCANS), F32),
        ]
    return pl.pallas_call(
        functools.partial(_mlstm_kernel, t=t, emit_state=emit_state),
        grid=(batch,),
        in_specs=[
            pl.BlockSpec((t, ZML_W), lambda b: (b, 0)),
            pl.BlockSpec((t, N_SCANS), lambda b: (b, 0)),
            pl.BlockSpec((t, N_SCANS), lambda b: (b, 0)),
            pl.BlockSpec((N_GATES, t), lambda b: (0, b)),
            c_spec, n_spec, m_spec,
            pl.BlockSpec((None, 1, ML_W), lambda b: (layer, 0, 0)),
            pl.BlockSpec(sel.shape, lambda b: (0, 0)),
        ],
        out_specs=out_specs,
        out_shape=out_shape,
        scratch_shapes=[
            pltpu.VMEM((2, tri_block, tri_block), BF16),
            pltpu.VMEM((H // 2, 2 * HEAD_DIM, 4 * LANES), F32),
            pltpu.VMEM((H // 2, nc, 2 * HEAD_DIM, 4 * LANES), F32),
            pltpu.VMEM((H // 2, nc, 2 * HEAD_DIM, 4 * LANES), BF16),
            pltpu.VMEM((H, nc, 1, LANES), F32),
        ],
        compiler_params=_params(1),
        name="mlstm",
    )(zml, gi, gf, grow, c0, n0, m0, g_ml, sel)


TOKEN_TILE = 512


def _layer_path(x, mods, layer, first_row, tiles_row_tokens, weights, mixers):
    per_row = None if tiles_row_tokens is None else tiles_row_tokens // TOKEN_TILE
    g_norm = weights["g_norm"]
    zna, zgq, zml, gi, gf, grow = _inproj(x, mods, g_norm, weights["w_in"], layer, per_row, first_row, TOKEN_TILE)
    mna, mgq, mml, extra = mixers(zna, zgq, zml, gi, gf, grow)
    x = _post(x, mna, mgq, mml, mods, g_norm, weights["w_out"], weights["w_gu"], weights["w_down"], layer,
              per_row, first_row, TOKEN_TILE)
    return x, extra


def _gqa_pair_order():
    return [a * GQA_GROUP + p for p in range(GQA_GROUP) for a in range(GQA_KV_HEADS)]


def _take_blocks(x, axis, base, width, order):
    return jnp.concatenate([lax.slice_in_dim(x, base + width * o, base + width * (o + 1), axis=axis)
                            for o in order], axis=axis)


def kernel(x_prompt, x_sample, cache_na_kv, cache_gqa_kv, state_mlstm_C, state_mlstm_n, state_mlstm_m,
           c, c_ctx, w_in, b_gates, w_out, g_norm, g_qk, g_mlstm, na_bias, w_ada, b_ada, w_gu, w_down):
    batch, seq, _ = x_prompt.shape
    dec_batch, dec_seq, _ = x_sample.shape
    past = cache_na_kv.shape[-2]
    assert dec_batch + 1 <= N_MOD_ROWS and dec_seq % GRID_W == 0 and GQA_KV_HEADS == 2

    cvec = jnp.concatenate([c_ctx[None, :], c, jnp.zeros((N_MOD_ROWS - 1 - dec_batch, D_MODEL), F32)], axis=0)
    mods = _adaln(cvec, w_ada, b_ada).reshape(DEPTH, N_MOD_ROWS, 6, D_MODEL)
    bias = _na_bias_expand(na_bias, dec_seq // GRID_W)
    cos, sin = _rope_tables(dec_seq)

    pair_order = _gqa_pair_order()
    scan_order = [2 * d for d in range(2)]
    H = MLSTM_HEADS
    w_gi = _take_blocks(w_in, 2, OFF_GATES, H, scan_order)
    w_gf = _take_blocks(w_in, 2, OFF_GATES + H, H, scan_order)
    b_gi = _take_blocks(b_gates, 1, 0, H, scan_order)
    b_gf = _take_blocks(b_gates, 1, H, H, scan_order)
    lane_pad = lambda a: jnp.pad(a, ((0, 0), (0, 0), (0, LANES - N_SCANS)))
    w_gq = jnp.concatenate([_take_blocks(w_in, 2, OFF_GQ, HEAD_DIM, pair_order),
                            w_in[:, :, OFF_GQ + GQA_QW:OFF_ML]], axis=2)
    w_out_rows = jnp.concatenate([w_out[:, :NA_W], _take_blocks(w_out, 1, NA_W, HEAD_DIM, pair_order),
                                  w_out[:, NA_W + GQA_QW:]], axis=1)
    weights = {
        "g_norm": g_norm,
        "w_in": {
            "na": w_in[:, :, :OFF_GQ].astype(BF16),
            "gq": w_gq.astype(BF16),
            "ml": w_in[:, :, OFF_ML:OFF_GATES].astype(BF16),
            "gate_col": jnp.concatenate([lane_pad(w_gi), lane_pad(w_gf)], axis=-1).astype(BF16),
            "gate_row": jnp.swapaxes(jnp.concatenate([w_gi, w_gf], axis=-1), 1, 2).astype(BF16),
            "b_col": jnp.stack([b_gi, b_gf], axis=1),
            "b_row": jnp.concatenate([b_gi, b_gf], axis=-1)[:, :, None],
        },
        "w_out": w_out_rows.astype(BF16),
        "w_gu": w_gu.astype(BF16),
        "w_down": w_down.astype(BF16),
    }
    g_ml = g_mlstm.reshape(DEPTH, 1, ML_W)
    zero_c = jnp.zeros((1, 2, MLSTM_HEADS, HEAD_DIM, HEAD_DIM), F32)
    zero_n = jnp.zeros((1, 2, MLSTM_HEADS, HEAD_DIM), F32)
    zero_m = jnp.zeros((1, 1, N_SCANS), F32)
    m0_lat = state_mlstm_m.reshape(dec_batch, DEPTH, 1, N_SCANS)
    cache_na_t = jnp.swapaxes(cache_na_kv, -1, -2).reshape(dec_batch, DEPTH, 2, NA_HEADS // 2, LANES, past)
    cache_gq_t = jnp.swapaxes(cache_gqa_kv, -1, -2).reshape(dec_batch, DEPTH, 2, GQA_KW, past)

    xp = x_prompt.reshape(batch * seq, D_MODEL)
    xs = x_sample.reshape(dec_batch * dec_seq, D_MODEL)
    kv_na = kv_gq = None
    c_l, n_l, m_l = [], [], []
    for layer in range(DEPTH):
        def ctx_mixers(zna, zgq, zml, gi, gf, grow, layer=layer, kv_na=kv_na, kv_gq=kv_gq):
            mna, mgq, kv_na, kv_gq = _ctx_attn(zna, zgq, g_qk, kv_na, kv_gq, layer, batch, seq)
            mml, cf, nf, mf = _mlstm(zml, gi, gf, grow, zero_c, zero_n, zero_m, g_ml, layer, batch, seq,
                                     True, None)
            return mna, mgq, mml, (kv_na, kv_gq, cf, nf, mf.reshape(batch, 2, MLSTM_HEADS))

        def lat_mixers(zna, zgq, zml, gi, gf, grow, layer=layer):
            mna = _lat_na(zna, cache_na_t, bias, layer, dec_batch, dec_seq)
            mgq = _lat_gqa(zgq, cache_gq_t, g_qk, cos, sin, layer, dec_batch, dec_seq)
            (mml,) = _mlstm(zml, gi, gf, grow, state_mlstm_C, state_mlstm_n, m0_lat, g_ml, layer,
                            dec_batch, dec_seq, False, layer)
            return mna, mgq, mml, None

        xp, (kv_na, kv_gq, cf, nf, mf) = _layer_path(xp, mods, layer, 0, None, weights, ctx_mixers)
        c_l.append(cf)
        n_l.append(nf)
        m_l.append(mf)
        xs, _ = _layer_path(xs, mods, layer, 1, dec_seq, weights, lat_mixers)

    return (xp.reshape(batch, seq, D_MODEL), xs.reshape(dec_batch, dec_seq, D_MODEL),
            jnp.swapaxes(kv_na, -1, -2), jnp.swapaxes(kv_gq, -1, -2),
            jnp.stack(c_l, axis=1), jnp.stack(n_l, axis=1), jnp.stack(m_l, axis=1))
```

```python
import functools

import jax
import jax.numpy as jnp
import numpy as np
from jax import lax
from jax.experimental import pallas as pl
from jax.experimental.pallas import tpu as pltpu

D_MODEL = 1024
DEPTH = 4
GRID_W = 64
HEAD_DIM = 64
NA_HEADS = 4
GQA_Q_HEADS = 8
GQA_KV_HEADS = 2
GQA_GROUP = GQA_Q_HEADS // GQA_KV_HEADS
MLSTM_HEADS = 4
NA_WIN_ROWS = 8
NA_WIN_COLS = 16
MLSTM_CHUNK = 64
ROPE_BASE = 10000.0
EPS = 1e-6
NEG = -1e30
NA_W = NA_HEADS * HEAD_DIM
GQA_QW = GQA_Q_HEADS * HEAD_DIM
GQA_KW = GQA_KV_HEADS * HEAD_DIM
ML_W = MLSTM_HEADS * HEAD_DIM
N_GATES = 4 * MLSTM_HEADS
N_SCANS = 2 * MLSTM_HEADS
MLSTM_TILE_KINDS = 2
MLSTM_SEL_ROWS = MLSTM_TILE_KINDS * N_SCANS
FF_HIDDEN = ((8 * D_MODEL + 3 * 256 - 1) // (3 * 256)) * 256
QK_SCALE = HEAD_DIM ** -0.5

ZNA_W = 3 * NA_W
ZGQ_W = GQA_QW + 2 * GQA_KW
ZML_W = 4 * ML_W
OFF_GQ = ZNA_W
OFF_ML = ZNA_W + ZGQ_W
OFF_GATES = OFF_ML + ZML_W

LANES = 128
N_MOD_ROWS = 16
NA_BLOCK_ROWS = 2
NA_UNION_ROWS = NA_WIN_ROWS + NA_BLOCK_ROWS - 1

F32 = jnp.float32
BF16 = jnp.bfloat16
VMEM_LIMIT = 52 * 1024 * 1024

NT_DIMS = (((1,), (1,)), ((), ()))


def _params(n_axes):
    return pltpu.CompilerParams(dimension_semantics=("arbitrary",) * n_axes,
                                vmem_limit_bytes=VMEM_LIMIT)


def _rms(x, g):
    return x * lax.rsqrt(jnp.mean(x * x, axis=-1, keepdims=True) + EPS) * g


def _lane_first(shape):
    return lax.broadcasted_iota(jnp.int32, shape, len(shape) - 1) < HEAD_DIM


def _pair_queries(x):
    first = _lane_first(x.shape)
    return jnp.where(first, x, 0.0).astype(BF16), jnp.where(first, 0.0, x).astype(BF16)


def _pair_values(v2):
    first = _lane_first(v2.shape)
    ones, zeros = jnp.ones_like(v2), jnp.zeros_like(v2)
    r0 = jnp.concatenate([jnp.where(first, v2, 0.0), jnp.where(first, ones, zeros)], axis=1)
    r1 = jnp.concatenate([jnp.where(first, 0.0, v2), jnp.where(first, zeros, ones)], axis=1)
    return r0.astype(BF16), r1.astype(BF16)


def _pair_values_t(vt2):
    first = lax.broadcasted_iota(jnp.int32, vt2.shape, 0) < HEAD_DIM
    ones, zeros = jnp.ones_like(vt2), jnp.zeros_like(vt2)
    r0 = jnp.concatenate([jnp.where(first, vt2, 0.0), jnp.where(first, ones, zeros)], axis=0)
    r1 = jnp.concatenate([jnp.where(first, 0.0, vt2), jnp.where(first, zeros, ones)], axis=0)
    return r0.astype(BF16), r1.astype(BF16)


def _pair_scores(q_pair, blocks):
    out = []
    for a in range(2):
        scores = []
        for blk in blocks:
            if blk["k_nt"]:
                s = lax.dot_general(q_pair[a], blk["k"], NT_DIMS, preferred_element_type=F32)
            else:
                s = jnp.dot(q_pair[a], blk["k"], preferred_element_type=F32)
            if blk.get("bias") is not None:
                s = s + blk["bias"][a]
            scores.append(s)
        out.append(scores)
    return out


def _pair_finish(all_scores, blocks):
    acc = None
    for a in range(2):
        scores = all_scores[a]
        m = scores[0].max(axis=-1, keepdims=True)
        for s in scores[1:]:
            m = jnp.maximum(m, s.max(axis=-1, keepdims=True))
        for s, blk in zip(scores, blocks):
            p = jnp.exp(s - m).astype(BF16)
            if blk["r_nt"]:
                term = lax.dot_general(p, blk["r"][a], NT_DIMS, preferred_element_type=F32)
            else:
                term = jnp.dot(p, blk["r"][a], preferred_element_type=F32)
            acc = term if acc is None else acc + term
    return acc[:, :LANES] / acc[:, LANES:]


def _pair_attend(q_pair, blocks):
    return _pair_finish(_pair_scores(q_pair, blocks), blocks)


def _adaln_kernel(c_ref, w_ref, b_ref, o_ref):
    c = c_ref[...]
    a = c * jax.nn.sigmoid(c)
    o_ref[...] = jnp.dot(a.astype(BF16), w_ref[...].astype(BF16),
                         preferred_element_type=F32) + b_ref[...]


def _adaln(cvec, w_ada, b_ada):
    tn = 1536
    return pl.pallas_call(
        _adaln_kernel,
        grid=(DEPTH, 6 * D_MODEL // tn),
        in_specs=[
            pl.BlockSpec((N_MOD_ROWS, D_MODEL), lambda l, j: (0, 0)),
            pl.BlockSpec((None, D_MODEL, tn), lambda l, j: (l, 0, j)),
            pl.BlockSpec((None, 1, tn), lambda l, j: (l, 0, j)),
        ],
        out_specs=pl.BlockSpec((None, N_MOD_ROWS, tn), lambda l, j: (l, 0, j)),
        out_shape=jax.ShapeDtypeStruct((DEPTH, N_MOD_ROWS, 6 * D_MODEL), F32),
        compiler_params=_params(2),
        name="adaln",
    )(cvec, w_ada, b_ada.reshape(DEPTH, 1, 6 * D_MODEL))


def _na_r0(r, rows):
    return min(max(r - NA_WIN_ROWS // 2, 0), rows - NA_WIN_ROWS)


def _na_union_start(p, rows):
    return min(_na_r0(NA_BLOCK_ROWS * p, rows), rows - NA_UNION_ROWS)


def _na_bias_kernel(tbl_ref, o_ref, *, rows):
    l = pl.program_id(0)
    h = pl.program_id(1)
    qi = lax.broadcasted_iota(jnp.int32, (GRID_W, GRID_W), 0)
    ki = lax.broadcasted_iota(jnp.int32, (GRID_W, GRID_W), 1)
    dc = jnp.clip(ki - qi, -(NA_WIN_COLS - 1), NA_WIN_COLS - 1) + NA_WIN_COLS - 1
    c0 = jnp.clip(qi - NA_WIN_COLS // 2, 0, GRID_W - NA_WIN_COLS)
    col_ok = (ki >= c0) & (ki < c0 + NA_WIN_COLS)
    n_dr = 2 * NA_WIN_ROWS - 1
    n_dc = 2 * NA_WIN_COLS - 1
    tiles = []
    for dr in range(n_dr):
        t = jnp.zeros((GRID_W, GRID_W), F32)
        for d in range(n_dc):
            t = jnp.where(dc == d, tbl_ref[((l * NA_HEADS + h) * n_dr + dr) * n_dc + d], t)
        tiles.append(jnp.where(col_ok, t, NEG))
    neg_tile = jnp.full((GRID_W, GRID_W), NEG, F32)
    for p in range(rows // NA_BLOCK_ROWS):
        start = _na_union_start(p, rows)
        for a in range(NA_BLOCK_ROWS):
            r = NA_BLOCK_ROWS * p + a
            r0 = _na_r0(r, rows)
            for j in range(NA_UNION_ROWS):
                kr = start + j
                inside = r0 <= kr < r0 + NA_WIN_ROWS
                tile = tiles[kr - r + NA_WIN_ROWS - 1] if inside else neg_tile
                o_ref[p, a * GRID_W:(a + 1) * GRID_W, j * GRID_W:(j + 1) * GRID_W] = tile


def _na_bias_expand(na_bias, rows):
    n_blocks = rows // NA_BLOCK_ROWS
    qn = NA_BLOCK_ROWS * GRID_W
    kn = NA_UNION_ROWS * GRID_W
    return pl.pallas_call(
        functools.partial(_na_bias_kernel, rows=rows),
        grid=(DEPTH, NA_HEADS),
        in_specs=[pl.BlockSpec(memory_space=pltpu.SMEM)],
        out_specs=pl.BlockSpec((None, None, n_blocks, qn, kn), lambda l, h: (l, h, 0, 0, 0)),
        out_shape=jax.ShapeDtypeStruct((DEPTH, NA_HEADS, n_blocks, qn, kn), F32),
        compiler_params=_params(2),
        name="na_bias_expand",
    )(na_bias.reshape(-1))


def _mod_row_map(layer, tiles_per_row, first_row):
    if tiles_per_row is None:
        return lambda i: (layer, first_row, 0, 0)
    return lambda i: (layer, first_row + i // tiles_per_row, 0, 0)


def _inproj_kernel(x_ref, mod_ref, g_ref, wna_ref, wgq_ref, wml_ref, wgc_ref, wgr_ref, bgc_ref, bgr_ref,
                   zna_ref, zgq_ref, zml_ref, gi_ref, gf_ref, grow_ref):
    x = x_ref[...]
    h = _rms(x, g_ref[0:1, :]) * (1.0 + mod_ref[1:2, :]) + mod_ref[0:1, :]
    hb = h.astype(BF16)
    zna_ref[...] = lax.dot_general(hb, wna_ref[...], NT_DIMS, preferred_element_type=F32)
    zgq_ref[...] = lax.dot_general(hb, wgq_ref[...], NT_DIMS, preferred_element_type=F32)
    zml_ref[...] = lax.dot_general(hb, wml_ref[...], NT_DIMS, preferred_element_type=F32)
    gc = jnp.dot(hb, wgc_ref[...], preferred_element_type=F32)
    gi_ref[...] = gc[:, 0:N_SCANS] + bgc_ref[0:1, :]
    gf_ref[...] = gc[:, LANES:LANES + N_SCANS] + bgc_ref[1:2, :]
    grow_ref[...] = lax.dot_general(wgr_ref[...], hb, NT_DIMS, preferred_element_type=F32) + bgr_ref[...]


def _inproj(x, mods, g_norm, w, layer, tiles_per_row, first_row, tm):
    n = x.shape[0]
    wspec = lambda width: pl.BlockSpec((None, width, D_MODEL), lambda i: (layer, 0, 0))
    return pl.pallas_call(
        _inproj_kernel,
        grid=(n // tm,),
        in_specs=[
            pl.BlockSpec((tm, D_MODEL), lambda i: (i, 0)),
            pl.BlockSpec((None, None, 6, D_MODEL), _mod_row_map(layer, tiles_per_row, first_row)),
            pl.BlockSpec((None, 4, D_MODEL), lambda i: (layer, 0, 0)),
            wspec(ZNA_W), wspec(ZGQ_W), wspec(ZML_W),
            pl.BlockSpec((None, D_MODEL, 2 * LANES), lambda i: (layer, 0, 0)),
            pl.BlockSpec((None, N_GATES, D_MODEL), lambda i: (layer, 0, 0)),
            pl.BlockSpec((None, 2, N_SCANS), lambda i: (layer, 0, 0)),
            pl.BlockSpec((None, N_GATES, 1), lambda i: (layer, 0, 0)),
        ],
        out_specs=[
            pl.BlockSpec((tm, ZNA_W), lambda i: (i, 0)),
            pl.BlockSpec((tm, ZGQ_W), lambda i: (i, 0)),
            pl.BlockSpec((tm, ZML_W), lambda i: (i, 0)),
            pl.BlockSpec((tm, N_SCANS), lambda i: (i, 0)),
            pl.BlockSpec((tm, N_SCANS), lambda i: (i, 0)),
            pl.BlockSpec((N_GATES, tm), lambda i: (0, i)),
        ],
        out_shape=[
            jax.ShapeDtypeStruct((n, ZNA_W), F32),
            jax.ShapeDtypeStruct((n, ZGQ_W), F32),
            jax.ShapeDtypeStruct((n, ZML_W), F32),
            jax.ShapeDtypeStruct((n, N_SCANS), F32),
            jax.ShapeDtypeStruct((n, N_SCANS), F32),
            jax.ShapeDtypeStruct((N_GATES, n), F32),
        ],
        compiler_params=_params(1),
        name="inproj",
    )(x, mods, g_norm, w["na"], w["gq"], w["ml"], w["gate_col"], w["gate_row"], w["b_col"], w["b_row"])


def _post_kernel(x_ref, mna_ref, mgq_ref, mml_ref, mod_ref, g_ref, wo_ref, wgu_ref, wd_ref, o_ref):
    acc = jnp.dot(mna_ref[...], wo_ref[0:NA_W, :], preferred_element_type=F32)
    acc += jnp.dot(mgq_ref[...], wo_ref[NA_W:NA_W + GQA_QW, :], preferred_element_type=F32)
    acc += jnp.dot(mml_ref[...], wo_ref[NA_W + GQA_QW:, :], preferred_element_type=F32)
    x1 = x_ref[...] + mod_ref[2:3, :] * _rms(acc, g_ref[1:2, :])
    hb = (_rms(x1, g_ref[2:3, :]) * (1.0 + mod_ref[4:5, :]) + mod_ref[3:4, :]).astype(BF16)
    gate = jnp.dot(hb, wgu_ref[:, 0:FF_HIDDEN], preferred_element_type=F32)
    up = jnp.dot(hb, wgu_ref[:, FF_HIDDEN:], preferred_element_type=F32)
    act = (gate * jax.nn.sigmoid(gate) * up).astype(BF16)
    f = jnp.dot(act, wd_ref[...], preferred_element_type=F32)
    o_ref[...] = x1 + mod_ref[5:6, :] * _rms(f, g_ref[3:4, :])


def _post(x, mna, mgq, mml, mods, g_norm, w_out, w_gu, w_down, layer, tiles_per_row, first_row, tm):
    n = x.shape[0]
    resident = lambda shape: pl.BlockSpec((None,) + shape, lambda i: (layer, 0, 0), pipeline_mode=pl.Buffered(1))
    return pl.pallas_call(
        _post_kernel,
        grid=(n // tm,),
        in_specs=[
            pl.BlockSpec((tm, D_MODEL), lambda i: (i, 0)),
            pl.BlockSpec((tm, NA_W), lambda i: (i, 0)),
            pl.BlockSpec((tm, GQA_QW), lambda i: (i, 0)),
            pl.BlockSpec((tm, ML_W), lambda i: (i, 0)),
            pl.BlockSpec((None, None, 6, D_MODEL), _mod_row_map(layer, tiles_per_row, first_row)),
            pl.BlockSpec((None, 4, D_MODEL), lambda i: (layer, 0, 0)),
            resident((D_MODEL, D_MODEL)),
            resident((D_MODEL, 2 * FF_HIDDEN)),
            resident((FF_HIDDEN, D_MODEL)),
        ],
        out_specs=pl.BlockSpec((tm, D_MODEL), lambda i: (i, 0)),
        out_shape=jax.ShapeDtypeStruct((n, D_MODEL), F32),
        compiler_params=_params(1),
        name="post",
    )(x, mna, mgq, mml, mods, g_norm, w_out, w_gu, w_down)


def _pair_rms(x, gain, sums_on_mxu=False):
    xsq = x * x
    if sums_on_mxu:
        row = lax.broadcasted_iota(jnp.int32, (LANES, LANES), 0) < HEAD_DIM
        col = lax.broadcasted_iota(jnp.int32, (LANES, LANES), 1) < HEAD_DIM
        ones_blk = jnp.where(row == col, 1.0, 0.0).astype(BF16)
        sums = sum(jnp.dot(p, ones_blk, preferred_element_type=F32) for p in _split3(xsq))
    else:
        first = _lane_first(x.shape)
        sums = jnp.where(first, jnp.sum(jnp.where(first, xsq, 0.0), axis=-1, keepdims=True),
                         jnp.sum(jnp.where(first, 0.0, xsq), axis=-1, keepdims=True))
    return x * lax.rsqrt(sums * (1.0 / HEAD_DIM) + EPS) * gain


def _pair_gain(gqk_ref, row):
    return jnp.concatenate([gqk_ref[row:row + 1, :]] * (LANES // HEAD_DIM), axis=-1)


def _ctx_attn_kernel(zna_ref, zgq_ref, gqk_ref, *rest):
    mna_ref, mgq_ref, kvna_ref, kvgq_ref = rest[-4:]

    def store_t(ref, which, pair, x2):
        xt = x2.T
        for a in range(2):
            ref[which, 2 * pair + a] = xt[a * HEAD_DIM:(a + 1) * HEAD_DIM]

    for i in range(NA_HEADS // 2):
        cols = slice(i * LANES, (i + 1) * LANES)
        q2 = zna_ref[:, cols] * QK_SCALE
        k2 = zna_ref[:, NA_W + i * LANES:NA_W + (i + 1) * LANES]
        v2 = zna_ref[:, 2 * NA_W + i * LANES:2 * NA_W + (i + 1) * LANES]
        store_t(kvna_ref, 0, i, k2)
        store_t(kvna_ref, 1, i, v2)
        blocks = [{"k": k2.astype(BF16), "k_nt": True, "r": _pair_values(v2), "r_nt": False}]
        mna_ref[:, cols] = _pair_attend(_pair_queries(q2), blocks).astype(BF16)

    gq, gk = _pair_gain(gqk_ref, 0), _pair_gain(gqk_ref, 1)
    k2 = _pair_rms(zgq_ref[:, GQA_QW:GQA_QW + GQA_KW], gk)
    v2 = zgq_ref[:, GQA_QW + GQA_KW:]
    store_t(kvgq_ref, 0, 0, k2)
    store_t(kvgq_ref, 1, 0, v2)
    blocks = [{"k": k2.astype(BF16), "k_nt": True, "r": _pair_values(v2), "r_nt": False}]
    for p in range(GQA_GROUP):
        cols = slice(p * LANES, (p + 1) * LANES)
        q2 = _pair_rms(zgq_ref[:, cols], gq) * QK_SCALE
        mgq_ref[:, cols] = _pair_attend(_pair_queries(q2), blocks).astype(BF16)


def _ctx_attn(zna, zgq, g_qk, kv_na_buf, kv_gq_buf, layer, batch, t):
    n = zna.shape[0]
    return pl.pallas_call(
        _ctx_attn_kernel,
        grid=(batch,),
        in_specs=[
            pl.BlockSpec((t, ZNA_W), lambda b: (b, 0)),
            pl.BlockSpec((t, ZGQ_W), lambda b: (b, 0)),
            pl.BlockSpec((None, 2, HEAD_DIM), lambda b: (layer, 0, 0)),
            pl.BlockSpec(memory_space=pl.ANY),
            pl.BlockSpec(memory_space=pl.ANY),
        ],
        out_specs=[
            pl.BlockSpec((t, NA_W), lambda b: (b, 0)),
            pl.BlockSpec((t, GQA_QW), lambda b: (b, 0)),
            pl.BlockSpec((None, None, 2, NA_HEADS, HEAD_DIM, t), lambda b: (b, layer, 0, 0, 0, 0)),
            pl.BlockSpec((None, None, 2, GQA_KV_HEADS, HEAD_DIM, t), lambda b: (b, layer, 0, 0, 0, 0)),
        ],
        out_shape=[
            jax.ShapeDtypeStruct((n, NA_W), BF16),
            jax.ShapeDtypeStruct((n, GQA_QW), BF16),
            jax.ShapeDtypeStruct((batch, DEPTH, 2, NA_HEADS, HEAD_DIM, t), F32),
            jax.ShapeDtypeStruct((batch, DEPTH, 2, GQA_KV_HEADS, HEAD_DIM, t), F32),
        ],
        input_output_aliases={3: 2, 4: 3},
        compiler_params=_params(1),
        name="ctx_attn",
    )(zna, zgq, g_qk, kv_na_buf, kv_gq_buf)


def _lat_na_kernel(zna_ref, cache_ref, bias_ref, o_ref, *, rows):
    qn = NA_BLOCK_ROWS * GRID_W
    kn = NA_UNION_ROWS * GRID_W
    n_blocks = rows // NA_BLOCK_ROWS
    for i in range(NA_HEADS // 2):
        cols = slice(i * LANES, (i + 1) * LANES)
        q_pair = _pair_queries(zna_ref[:, cols] * QK_SCALE)
        k2 = zna_ref[:, NA_W + i * LANES:NA_W + (i + 1) * LANES].astype(BF16)
        r = _pair_values(zna_ref[:, 2 * NA_W + i * LANES:2 * NA_W + (i + 1) * LANES])
        kc = cache_ref[0, i].astype(BF16)
        rc = _pair_values_t(cache_ref[1, i])
        acc = None
        for a in range(2):
            s_ctx = jnp.dot(q_pair[a], kc, preferred_element_type=F32)
            s_win, m_blocks = [], []
            for p in range(n_blocks):
                k0 = _na_union_start(p, rows) * GRID_W
                s = lax.dot_general(q_pair[a][p * qn:(p + 1) * qn], k2[k0:k0 + kn], NT_DIMS,
                                    preferred_element_type=F32) + bias_ref[2 * i + a, p]
                s_win.append(s)
                m_blocks.append(s.max(axis=-1, keepdims=True))
            m = jnp.maximum(jnp.concatenate(m_blocks, axis=0), s_ctx.max(axis=-1, keepdims=True))
            term = lax.dot_general(jnp.exp(s_ctx - m).astype(BF16), rc[a], NT_DIMS, preferred_element_type=F32)
            wins = []
            for p in range(n_blocks):
                k0 = _na_union_start(p, rows) * GRID_W
                pw = jnp.exp(s_win[p] - m[p * qn:(p + 1) * qn]).astype(BF16)
                wins.append(jnp.dot(pw, r[a][k0:k0 + kn], preferred_element_type=F32))
            term = term + jnp.concatenate(wins, axis=0)
            acc = term if acc is None else acc + term
        o_ref[:, cols] = (acc[:, :LANES] / acc[:, LANES:]).astype(BF16)


def _lat_na(zna, cache_t, bias, layer, batch, t):
    n = zna.shape[0]
    rows = t // GRID_W
    past = cache_t.shape[-1]
    n_blocks = rows // NA_BLOCK_ROWS
    qn = NA_BLOCK_ROWS * GRID_W
    kn = NA_UNION_ROWS * GRID_W
    return pl.pallas_call(
        functools.partial(_lat_na_kernel, rows=rows),
        grid=(batch,),
        in_specs=[
            pl.BlockSpec((t, ZNA_W), lambda b: (b, 0)),
            pl.BlockSpec((None, None, 2, NA_HEADS // 2, LANES, past), lambda b: (b, layer, 0, 0, 0, 0)),
            pl.BlockSpec((None, NA_HEADS, n_blocks, qn, kn), lambda b: (layer, 0, 0, 0, 0)),
        ],
        out_specs=pl.BlockSpec((t, NA_W), lambda b: (b, 0)),
        out_shape=jax.ShapeDtypeStruct((n, NA_W), BF16),
        compiler_params=_params(1),
        name="lat_na",
    )(zna, cache_t, bias)


def _rope_tables(t):
    half = HEAD_DIM // 2
    quarter = half // 2
    inv = 1.0 / (ROPE_BASE ** (jnp.arange(quarter, dtype=F32) / quarter))
    tt = jnp.arange(t)
    row = (tt // GRID_W).astype(F32)
    col = (tt % GRID_W).astype(F32)
    ang_r = row[:, None] * inv[None, :]
    ang_c = col[:, None] * inv[None, :]
    cos = jnp.concatenate([jnp.cos(ang_r)] * 2 + [jnp.cos(ang_c)] * 2, axis=-1)
    sin = jnp.concatenate([-jnp.sin(ang_r), jnp.sin(ang_r), -jnp.sin(ang_c), jnp.sin(ang_c)], axis=-1)
    reps = LANES // HEAD_DIM
    return jnp.tile(cos, (1, reps)), jnp.tile(sin, (1, reps))


def _pair_rope(xn, cos, sin):
    quarter = HEAD_DIM // 4
    lane = lax.broadcasted_iota(jnp.int32, xn.shape, 1)
    lower = (lane & (2 * quarter - 1)) < quarter
    partner = jnp.where(lower, pltpu.roll(xn, LANES - quarter, 1), pltpu.roll(xn, quarter, 1))
    return xn * cos + partner * sin


def _lat_gqa_kernel(zgq_ref, cache_ref, gqk_ref, cos_ref, sin_ref, o_ref):
    cos = cos_ref[...]
    sin = sin_ref[...]
    gq, gk = _pair_gain(gqk_ref, 0), _pair_gain(gqk_ref, 1)
    keys = _pair_rope(_pair_rms(zgq_ref[:, GQA_QW:GQA_QW + GQA_KW], gk, True), cos, sin).astype(BF16)
    blocks = [
        {"k": keys, "k_nt": True, "r": _pair_values(zgq_ref[:, GQA_QW + GQA_KW:]), "r_nt": False},
        {"k": cache_ref[0].astype(BF16), "k_nt": False, "r": _pair_values_t(cache_ref[1]), "r_nt": True},
    ]
    def scores(p):
        x = _pair_rope(_pair_rms(zgq_ref[:, p * LANES:(p + 1) * LANES], gq, True), cos, sin) * QK_SCALE
        return _pair_scores(_pair_queries(x), blocks)

    nxt = scores(0)
    for p in range(GQA_GROUP):
        cur = nxt
        if p + 1 < GQA_GROUP:
            nxt = scores(p + 1)
        o_ref[:, p * LANES:(p + 1) * LANES] = _pair_finish(cur, blocks).astype(BF16)


def _lat_gqa(zgq, cache_t, g_qk, cos, sin, layer, batch, t):
    n = zgq.shape[0]
    past = cache_t.shape[-1]
    return pl.pallas_call(
        _lat_gqa_kernel,
        grid=(batch,),
        in_specs=[
            pl.BlockSpec((t, ZGQ_W), lambda b: (b, 0)),
            pl.BlockSpec((None, None, 2, GQA_KW, past), lambda b: (b, layer, 0, 0, 0)),
            pl.BlockSpec((None, 2, HEAD_DIM), lambda b: (layer, 0, 0)),
            pl.BlockSpec((t, LANES), lambda b: (0, 0)),
            pl.BlockSpec((t, LANES), lambda b: (0, 0)),
        ],
        out_specs=pl.BlockSpec((t, GQA_QW), lambda b: (b, 0)),
        out_shape=jax.ShapeDtypeStruct((n, GQA_QW), BF16),
        compiler_params=_params(1),
        name="lat_gqa",
    )(zgq, cache_t, g_qk, cos, sin)


def _split3(x):
    x1 = x.astype(BF16)
    r1 = x - x1.astype(F32)
    x2 = r1.astype(BF16)
    x3 = (r1 - x2.astype(F32)).astype(BF16)
    return x1, x2, x3


def _log_sigmoid(x):
    return jnp.minimum(x, 0.0) - jnp.log1p(jnp.exp(-jnp.abs(x)))


def _mlstm_select_matrix():
    H = MLSTM_HEADS
    sel = np.zeros((MLSTM_SEL_ROWS, (H // 2) * 2 * MLSTM_TILE_KINDS * LANES), np.float32)
    for j in range(H // 2):
        for d in range(2):
            for q in range(MLSTM_TILE_KINDS):
                for a in range(2):
                    col0 = ((j * 2 + d) * MLSTM_TILE_KINDS + q) * LANES + a * HEAD_DIM
                    sel[q * N_SCANS + d * H + 2 * j + a, col0:col0 + HEAD_DIM] = 1.0
    return sel


def _mlstm_kernel(zml_ref, gi_ref, gf_ref, grow_ref, c0_ref, n0_ref, m0_ref, gml_ref, sel_ref, *rest,
                  t, emit_state):
    if emit_state:
        o_ref, cf_ref, nf_ref, mf_ref = rest[:4]
    else:
        o_ref = rest[0]
    tri_s, st_s, dst_s, cst_s, row_s = rest[-5:]
    L = MLSTM_CHUNK
    H = MLSTM_HEADS
    HD = HEAD_DIM
    NP = H // 2
    nc = t // L
    tb = tri_s.shape[-1]

    @pl.when(pl.program_id(0) == 0)
    def _():
        ti = lax.broadcasted_iota(jnp.int32, (tb, tb), 0)
        ui = lax.broadcasted_iota(jnp.int32, (tb, tb), 1)
        same = (ti & -L) == (ui & -L)
        tri_s[0] = jnp.where(same & (ui <= ti), 1.0, 0.0).astype(BF16)
        tri_s[1] = jnp.where(same & (ui >= ti), 1.0, 0.0).astype(BF16)

    lower, upper = tri_s[0], tri_s[1]

    def chunk_sums_cols(x):
        parts = _split3(x)
        pre, suf = [], []
        for i in range(t // tb):
            blk = [p[i * tb:(i + 1) * tb] for p in parts]
            pre.append(sum(jnp.dot(lower, p, preferred_element_type=F32) for p in blk))
            suf.append(sum(jnp.dot(upper, p, preferred_element_type=F32) for p in blk))
        return jnp.concatenate(pre, axis=0), jnp.concatenate(suf, axis=0)

    def chunk_sums_rows(x):
        parts = _split3(x)
        pre, suf = [], []
        for i in range(t // tb):
            blk = [p[:, i * tb:(i + 1) * tb] for p in parts]
            pre.append(sum(jnp.dot(p, upper, preferred_element_type=F32) for p in blk))
            suf.append(sum(jnp.dot(p, lower, preferred_element_type=F32) for p in blk))
        return jnp.concatenate(pre, axis=1), jnp.concatenate(suf, axis=1)

    pre_c, suf_c = chunk_sums_cols(_log_sigmoid(gf_ref[...]))
    lane_c = lax.broadcasted_iota(jnp.int32, (t, N_SCANS), 1)
    b3 = jnp.where(lane_c < H, pre_c, suf_c).reshape(nc, L, N_SCANS)
    i3 = gi_ref[...].reshape(nc, L, N_SCANS)
    fwd3 = lax.broadcasted_iota(jnp.int32, (nc, 1, N_SCANS), 2) < H
    b_end3 = jnp.where(fwd3, b3[:, L - 1:L, :], b3[:, 0:1, :])
    lw_end3 = b_end3 - b3 + i3
    a3 = jnp.max(lw_end3, axis=1, keepdims=True)
    wloc3 = jnp.exp(lw_end3 - a3)

    fwd1 = lax.broadcasted_iota(jnp.int32, (1, N_SCANS), 1) < H
    m = m0_ref[...]
    m_start, carry_decay, contrib_scale = [], [], []
    for j in range(nc):
        a_j = jnp.where(fwd1, a3[j], a3[nc - 1 - j])
        g_j = jnp.where(fwd1, b_end3[j], b_end3[nc - 1 - j])
        m_start.append(m)
        m_next = jnp.maximum(g_j + m, a_j)
        carry_decay.append(jnp.exp(g_j + m - m_next))
        contrib_scale.append(jnp.exp(a_j - m_next))
        m = m_next
    mst3 = jnp.concatenate([jnp.where(fwd1, m_start[c], m_start[nc - 1 - c])[None] for c in range(nc)], axis=0)

    cols = jnp.concatenate([b3.reshape(t, N_SCANS), wloc3.reshape(t, N_SCANS)], axis=1)
    tiles_all = sum(jnp.dot(p, sel_ref[...], preferred_element_type=F32) for p in _split3(cols))
    tile_w = MLSTM_TILE_KINDS * LANES

    def tiles(j, d):
        x = tiles_all[:, (2 * j + d) * tile_w:(2 * j + d + 1) * tile_w]
        return [x[:, q * LANES:(q + 1) * LANES].reshape(nc, L, LANES) for q in range(MLSTM_TILE_KINDS)]

    gr = grow_ref[...]
    pre_r, suf_r = chunk_sums_rows(_log_sigmoid(gr))
    sub_r = lax.broadcasted_iota(jnp.int32, (N_SCANS, t), 0)
    rowv = gr[0:N_SCANS] - jnp.where(sub_r < H, pre_r[N_SCANS:], suf_r[N_SCANS:])
    for j in range(NP):
        for d in range(2):
            e = d * H + 2 * j
            for c in range(nc):
                row_s[2 * j + d, c] = jnp.concatenate(
                    [rowv[e:e + 1, c * L:(c + 1) * L], rowv[e + 1:e + 2, c * L:(c + 1) * L]], axis=1)

    lane_a = lax.broadcasted_iota(jnp.int32, (1, 1, LANES), 2) < HD
    sub_a = lax.broadcasted_iota(jnp.int32, (1, 2 * HD, 1), 1) < HD
    diag = sub_a == lane_a
    diag4 = jnp.concatenate([diag] * 4, axis=2)

    def stack_heads(x3):
        return jnp.concatenate([jnp.where(lane_a, x3, 0.0), jnp.where(lane_a, 0.0, x3)], axis=1)

    def pair_cols(base, j):
        return slice(base + j * LANES, base + (j + 1) * LANES)

    zero_blk = jnp.zeros((HD, HD), F32)
    for j in range(NP):
        cols_d = []
        for d in range(2):
            ca, cb = c0_ref[d, 2 * j].T, c0_ref[d, 2 * j + 1].T
            na = jnp.broadcast_to(n0_ref[d, 2 * j:2 * j + 1, :], (HD, HD)).T
            nb = jnp.broadcast_to(n0_ref[d, 2 * j + 1:2 * j + 2, :], (HD, HD)).T
            top = jnp.concatenate([ca, zero_blk, na, zero_blk], axis=1)
            bot = jnp.concatenate([zero_blk, cb, zero_blk, nb], axis=1)
            cols_d.append(jnp.concatenate([top, bot], axis=0))
        st_s[j] = jnp.concatenate(cols_d, axis=1)

    for j in range(NP):
        k3 = (zml_ref[:, pair_cols(ML_W, j)] * QK_SCALE).reshape(nc, L, LANES).astype(BF16)
        v3 = zml_ref[:, pair_cols(2 * ML_W, j)].reshape(nc, L, LANES)
        rhs = []
        for d in range(2):
            wl = tiles(j, d)[1]
            rhs += [v3 * wl, wl]
        rhs = jnp.concatenate(rhs, axis=2).astype(BF16)
        contrib = jnp.einsum("csk,csn->ckn", k3, rhs, preferred_element_type=F32)
        dst_s[j] = jnp.where(diag4, contrib, 0.0)

    def lane_scale(v, j):
        pieces = []
        for d in range(2):
            sa = jnp.broadcast_to(v[:, d * H + 2 * j:d * H + 2 * j + 1], (1, HD))
            sb = jnp.broadcast_to(v[:, d * H + 2 * j + 1:d * H + 2 * j + 2], (1, HD))
            pieces += [sa, sb, sa, sb]
        return jnp.concatenate(pieces, axis=1)

    for j in range(NP):
        st = st_s[j]
        for step in range(nc):
            cb = nc - 1 - step
            stb = st.astype(BF16)
            cst_s[j, step, :, 0:2 * LANES] = stb[:, 0:2 * LANES]
            cst_s[j, cb, :, 2 * LANES:] = stb[:, 2 * LANES:]
            delta = jnp.concatenate([dst_s[j, step, :, 0:2 * LANES], dst_s[j, cb, :, 2 * LANES:]], axis=1)
            st = lane_scale(carry_decay[step], j) * st + lane_scale(contrib_scale[step], j) * delta
        st_s[j] = st

    sidx = lax.broadcasted_iota(jnp.int32, (1, L, LANES), 2) & (HD - 1)
    tidx = lax.broadcasted_iota(jnp.int32, (1, L, LANES), 1)
    masks = (sidx <= tidx, sidx >= tidx)
    ones_blk = jnp.broadcast_to(jnp.where(diag, 1.0, 0.0).astype(BF16), (nc, 2 * HD, LANES))
    neg_inf = -jnp.inf
    for j in range(NP):
        q3 = zml_ref[:, pair_cols(0, j)].reshape(nc, L, LANES).astype(BF16)
        k3 = (zml_ref[:, pair_cols(ML_W, j)] * QK_SCALE).reshape(nc, L, LANES)
        v3 = zml_ref[:, pair_cols(2 * ML_W, j)].reshape(nc, L, LANES)
        qk = jnp.einsum("ctd,cnd->ctn", q3, stack_heads(k3).astype(BF16), preferred_element_type=F32)
        v_aug = jnp.concatenate([stack_heads(v3).astype(BF16), ones_blk], axis=2)
        out = None
        for d in range(2):
            b_t = tiles(j, d)[0]
            e = d * H + 2 * j
            bm_t = b_t + jnp.where(lane_a, mst3[:, :, e:e + 1], mst3[:, :, e + 1:e + 2])
            logw = jnp.where(masks[d], b_t + row_s[2 * j + d], neg_inf)
            rmax_a = jnp.max(jnp.where(lane_a, logw, neg_inf), axis=-1, keepdims=True)
            rmax_b = jnp.max(jnp.where(lane_a, neg_inf, logw), axis=-1, keepdims=True)
            m_t = jnp.maximum(jnp.where(lane_a, rmax_a, rmax_b), bm_t)
            s = qk * jnp.exp(logw - m_t)
            decay = jnp.exp(bm_t - m_t)
            sv = jnp.einsum("cts,csn->ctn", s.astype(BF16), v_aug, preferred_element_type=F32)
            state = cst_s[j, :, :, 2 * d * LANES:2 * (d + 1) * LANES]
            inter = jnp.einsum("ctk,ckn->ctn", q3, state, preferred_element_type=F32)
            num = sv[:, :, 0:LANES] + decay * inter[:, :, 0:LANES]
            den = sv[:, :, LANES:] + decay * inter[:, :, LANES:]
            h_d = num / jnp.maximum(jnp.abs(den), jnp.exp(-m_t))
            out = h_d if out is None else out + h_d
        cols = pair_cols(0, j)
        og = jax.nn.sigmoid(zml_ref[:, pair_cols(3 * ML_W, j)])
        o_ref[:, cols] = (_pair_rms(out.reshape(t, LANES), gml_ref[:, cols]) * og).astype(BF16)

    if emit_state:
        for j in range(NP):
            st = st_s[j]
            for d in range(2):
                for a in range(2):
                    rows = slice(a * HD, (a + 1) * HD)
                    c0 = 2 * d * LANES + a * HD
                    cf_ref[d, 2 * j + a] = st[rows, c0:c0 + HD].T
                    nf_ref[d, 2 * j + a:2 * j + a + 1, :] = st[rows, c0 + LANES:c0 + LANES + HD].T[0:1, :]
        mf_ref[...] = m


def _mlstm(zml, gi, gf, grow, c0, n0, m0, g_ml, layer, batch, t, emit_state, state_layer):
    n = zml.shape[0]
    H = MLSTM_HEADS
    L = MLSTM_CHUNK
    nc = t // L
    sel = jnp.asarray(_mlstm_select_matrix(), BF16)
    tri_block = min(t, 256)
    assert L & (L - 1) == 0 and tri_block % L == 0 and t % tri_block == 0
    if state_layer is None:
        c_spec = pl.BlockSpec((None, 2, H, HEAD_DIM, HEAD_DIM), lambda b: (0, 0, 0, 0, 0))
        n_spec = pl.BlockSpec((None, 2, H, HEAD_DIM), lambda b: (0, 0, 0, 0))
        m_spec = pl.BlockSpec((None, 1, N_SCANS), lambda b: (0, 0, 0))
    else:
        c_spec = pl.BlockSpec((None, None, 2, H, HEAD_DIM, HEAD_DIM), lambda b: (b, state_layer, 0, 0, 0, 0))
        n_spec = pl.BlockSpec((None, None, 2, H, HEAD_DIM), lambda b: (b, state_layer, 0, 0, 0))
        m_spec = pl.BlockSpec((None, None, 1, N_SCANS), lambda b: (b, state_layer, 0, 0))
    out_specs = [pl.BlockSpec((t, ML_W), lambda b: (b, 0))]
    out_shape = [jax.ShapeDtypeStruct((n, ML_W), BF16)]
    if emit_state:
        out_specs += [
            pl.BlockSpec((None, 2, H, HEAD_DIM, HEAD_DIM), lambda b: (b, 0, 0, 0, 0)),
            pl.BlockSpec((None, 2, H, HEAD_DIM), lambda b: (b, 0, 0, 0)),
            pl.BlockSpec((None, 1, N_SCANS), lambda b: (b, 0, 0)),
        ]
        out_shape += [
            jax.ShapeDtypeStruct((batch, 2, H, HEAD_DIM, HEAD_DIM), F32),
            jax.ShapeDtypeStruct((batch, 2, H, HEAD_DIM), F32),
            jax.ShapeDtypeStruct((batch, 1, N_SCANS), F32),
        ]
    return pl.pallas_call(
        functools.partial(_mlstm_kernel, t=t, emit_state=emit_state),
        grid=(batch,),
        in_specs=[
            pl.BlockSpec((t, ZML_W), lambda b: (b, 0)),
            pl.BlockSpec((t, N_SCANS), lambda b: (b, 0)),
            pl.BlockSpec((t, N_SCANS), lambda b: (b, 0)),
            pl.BlockSpec((N_GATES, t), lambda b: (0, b)),
            c_spec, n_spec, m_spec,
            pl.BlockSpec((None, 1, ML_W), lambda b: (layer, 0, 0)),
            pl.BlockSpec(sel.shape, lambda b: (0, 0)),
        ],
        out_specs=out_specs,
        out_shape=out_shape,
        scratch_shapes=[
            pltpu.VMEM((2, tri_block, tri_block), BF16),
            pltpu.VMEM((H // 2, 2 * HEAD_DIM, 4 * LANES), F32),
            pltpu.VMEM((H // 2, nc, 2 * HEAD_DIM, 4 * LANES), F32),
            pltpu.VMEM((H // 2, nc, 2 * HEAD_DIM, 4 * LANES), BF16),
            pltpu.VMEM((H, nc, 1, LANES), F32),
        ],
        compiler_params=_params(1),
        name="mlstm",
    )(zml, gi, gf, grow, c0, n0, m0, g_ml, sel)


TOKEN_TILE = 512


def _layer_path(x, mods, layer, first_row, tiles_row_tokens, weights, mixers):
    per_row = None if tiles_row_tokens is None else tiles_row_tokens // TOKEN_TILE
    g_norm = weights["g_norm"]
    zna, zgq, zml, gi, gf, grow = _inproj(x, mods, g_norm, weights["w_in"], layer, per_row, first_row, TOKEN_TILE)
    mna, mgq, mml, extra = mixers(zna, zgq, zml, gi, gf, grow)
    x = _post(x, mna, mgq, mml, mods, g_norm, weights["w_out"], weights["w_gu"], weights["w_down"], layer,
              per_row, first_row, TOKEN_TILE)
    return x, extra


def _gqa_pair_order():
    return [a * GQA_GROUP + p for p in range(GQA_GROUP) for a in range(GQA_KV_HEADS)]


def _take_blocks(x, axis, base, width, order):
    return jnp.concatenate([lax.slice_in_dim(x, base + width * o, base + width * (o + 1), axis=axis)
                            for o in order], axis=axis)


def kernel(x_prompt, x_sample, cache_na_kv, cache_gqa_kv, state_mlstm_C, state_mlstm_n, state_mlstm_m,
           c, c_ctx, w_in, b_gates, w_out, g_norm, g_qk, g_mlstm, na_bias, w_ada, b_ada, w_gu, w_down):
    batch, seq, _ = x_prompt.shape
    dec_batch, dec_seq, _ = x_sample.shape
    past = cache_na_kv.shape[-2]
    assert dec_batch + 1 <= N_MOD_ROWS and dec_seq % GRID_W == 0 and GQA_KV_HEADS == 2

    cvec = jnp.concatenate([c_ctx[None, :], c, jnp.zeros((N_MOD_ROWS - 1 - dec_batch, D_MODEL), F32)], axis=0)
    mods = _adaln(cvec, w_ada, b_ada).reshape(DEPTH, N_MOD_ROWS, 6, D_MODEL)
    bias = _na_bias_expand(na_bias, dec_seq // GRID_W)
    cos, sin = _rope_tables(dec_seq)

    pair_order = _gqa_pair_order()
    scan_order = [2 * d for d in range(2)]
    H = MLSTM_HEADS
    w_in_t = jnp.swapaxes(w_in, 1, 2)
    w_gi = _take_blocks(w_in_t, 1, OFF_GATES, H, scan_order)
    w_gf = _take_blocks(w_in_t, 1, OFF_GATES + H, H, scan_order)
    b_gi = _take_blocks(b_gates, 1, 0, H, scan_order)
    b_gf = _take_blocks(b_gates, 1, H, H, scan_order)
    lane_pad = lambda a: jnp.pad(jnp.swapaxes(a, 1, 2), ((0, 0), (0, 0), (0, LANES - N_SCANS)))
    w_gq = jnp.concatenate([_take_blocks(w_in_t, 1, OFF_GQ, HEAD_DIM, pair_order),
                            w_in_t[:, OFF_GQ + GQA_QW:OFF_ML]], axis=1)
    w_out_rows = jnp.concatenate([w_out[:, :NA_W], _take_blocks(w_out, 1, NA_W, HEAD_DIM, pair_order),
                                  w_out[:, NA_W + GQA_QW:]], axis=1)
    weights = {
        "g_norm": g_norm,
        "w_in": {
            "na": w_in_t[:, :OFF_GQ].astype(BF16),
            "gq": w_gq.astype(BF16),
            "ml": w_in_t[:, OFF_ML:OFF_GATES].astype(BF16),
            "gate_col": jnp.concatenate([lane_pad(w_gi), lane_pad(w_gf)], axis=-1).astype(BF16),
            "gate_row": jnp.concatenate([w_gi, w_gf], axis=1).astype(BF16),
            "b_col": jnp.stack([b_gi, b_gf], axis=1),
            "b_row": jnp.concatenate([b_gi, b_gf], axis=-1)[:, :, None],
        },
        "w_out": w_out_rows.astype(BF16),
        "w_gu": w_gu.astype(BF16),
        "w_down": w_down.astype(BF16),
    }
    g_ml = g_mlstm.reshape(DEPTH, 1, ML_W)
    zero_c = jnp.zeros((1, 2, MLSTM_HEADS, HEAD_DIM, HEAD_DIM), F32)
    zero_n = jnp.zeros((1, 2, MLSTM_HEADS, HEAD_DIM), F32)
    zero_m = jnp.zeros((1, 1, N_SCANS), F32)
    m0_lat = state_mlstm_m.reshape(dec_batch, DEPTH, 1, N_SCANS)
    cache_na_t = jnp.swapaxes(cache_na_kv, -1, -2).reshape(dec_batch, DEPTH, 2, NA_HEADS // 2, LANES, past)
    cache_gq_t = jnp.swapaxes(cache_gqa_kv, -1, -2).reshape(dec_batch, DEPTH, 2, GQA_KW, past)

    xp = x_prompt.reshape(batch * seq, D_MODEL)
    xs = x_sample.reshape(dec_batch * dec_seq, D_MODEL)
    kv_na = jnp.zeros((batch, DEPTH, 2, NA_HEADS, HEAD_DIM, seq), F32)
    kv_gq = jnp.zeros((batch, DEPTH, 2, GQA_KV_HEADS, HEAD_DIM, seq), F32)
    c_l, n_l, m_l = [], [], []
    for layer in range(DEPTH):
        def ctx_mixers(zna, zgq, zml, gi, gf, grow, layer=layer, kv_na=kv_na, kv_gq=kv_gq):
            mna, mgq, kv_na, kv_gq = _ctx_attn(zna, zgq, g_qk, kv_na, kv_gq, layer, batch, seq)
            mml, cf, nf, mf = _mlstm(zml, gi, gf, grow, zero_c, zero_n, zero_m, g_ml, layer, batch, seq,
                                     True, None)
            return mna, mgq, mml, (kv_na, kv_gq, cf, nf, mf.reshape(batch, 2, MLSTM_HEADS))

        def lat_mixers(zna, zgq, zml, gi, gf, grow, layer=layer):
            mna = _lat_na(zna, cache_na_t, bias, layer, dec_batch, dec_seq)
            mgq = _lat_gqa(zgq, cache_gq_t, g_qk, cos, sin, layer, dec_batch, dec_seq)
            (mml,) = _mlstm(zml, gi, gf, grow, state_mlstm_C, state_mlstm_n, m0_lat, g_ml, layer,
                            dec_batch, dec_seq, False, layer)
            return mna, mgq, mml, None

        xp, (kv_na, kv_gq, cf, nf, mf) = _layer_path(xp, mods, layer, 0, None, weights, ctx_mixers)
        c_l.append(cf)
        n_l.append(nf)
        m_l.append(mf)
        xs, _ = _layer_path(xs, mods, layer, 1, dec_seq, weights, lat_mixers)

    return (xp.reshape(batch, seq, D_MODEL), xs.reshape(dec_batch, dec_seq, D_MODEL),
            jnp.swapaxes(kv_na, -1, -2), jnp.swapaxes(kv_gq, -1, -2),
            jnp.stack(c_l, axis=1), jnp.stack(n_l, axis=1), jnp.stack(m_l, axis=1))
```

```python
import functools

import jax
import jax.numpy as jnp
import numpy as np
from jax import lax
from jax.experimental import pallas as pl
from jax.experimental.pallas import tpu as pltpu

D_MODEL = 1024
DEPTH = 4
GRID_W = 64
HEAD_DIM = 64
NA_HEADS = 4
GQA_Q_HEADS = 8
GQA_KV_HEADS = 2
GQA_GROUP = GQA_Q_HEADS // GQA_KV_HEADS
MLSTM_HEADS = 4
NA_WIN_ROWS = 8
NA_WIN_COLS = 16
MLSTM_CHUNK = 64
ROPE_BASE = 10000.0
EPS = 1e-6
NEG = -1e30
NA_W = NA_HEADS * HEAD_DIM
GQA_QW = GQA_Q_HEADS * HEAD_DIM
GQA_KW = GQA_KV_HEADS * HEAD_DIM
ML_W = MLSTM_HEADS * HEAD_DIM
N_GATES = 4 * MLSTM_HEADS
N_SCANS = 2 * MLSTM_HEADS
MLSTM_TILE_KINDS = 2
MLSTM_SEL_ROWS = MLSTM_TILE_KINDS * N_SCANS
FF_HIDDEN = ((8 * D_MODEL + 3 * 256 - 1) // (3 * 256)) * 256
QK_SCALE = HEAD_DIM ** -0.5

ZNA_W = 3 * NA_W
ZGQ_W = GQA_QW + 2 * GQA_KW
ZML_W = 4 * ML_W
OFF_GQ = ZNA_W
OFF_ML = ZNA_W + ZGQ_W
OFF_GATES = OFF_ML + ZML_W

LANES = 128
MXU_DIM = 256
N_MOD_ROWS = 16
NA_BLOCK_ROWS = 2
NA_UNION_ROWS = NA_WIN_ROWS + NA_BLOCK_ROWS - 1

F32 = jnp.float32
BF16 = jnp.bfloat16
VMEM_LIMIT = 52 * 1024 * 1024
TOKEN_TILE = 512
ADALN_TILE = 1536

NT_DIMS = (((1,), (1,)), ((), ()))


def _params(n_axes):
    return pltpu.CompilerParams(dimension_semantics=("arbitrary",) * n_axes,
                                vmem_limit_bytes=VMEM_LIMIT)


def _rms(x, g):
    return x * lax.rsqrt(jnp.mean(x * x, axis=-1, keepdims=True) + EPS) * g


def _lane_first(shape):
    return lax.broadcasted_iota(jnp.int32, shape, len(shape) - 1) < HEAD_DIM


def _pair_queries(x):
    first = _lane_first(x.shape)
    return jnp.where(first, x, 0.0).astype(BF16), jnp.where(first, 0.0, x).astype(BF16)


def _pair_values(v2):
    first = _lane_first(v2.shape)
    ones, zeros = jnp.ones_like(v2), jnp.zeros_like(v2)
    r0 = jnp.concatenate([jnp.where(first, v2, 0.0), jnp.where(first, ones, zeros)], axis=1)
    r1 = jnp.concatenate([jnp.where(first, 0.0, v2), jnp.where(first, zeros, ones)], axis=1)
    return r0.astype(BF16), r1.astype(BF16)


def _pair_values_t(vt2):
    first = lax.broadcasted_iota(jnp.int32, vt2.shape, 0) < HEAD_DIM
    ones, zeros = jnp.ones_like(vt2), jnp.zeros_like(vt2)
    r0 = jnp.concatenate([jnp.where(first, vt2, 0.0), jnp.where(first, ones, zeros)], axis=0)
    r1 = jnp.concatenate([jnp.where(first, 0.0, vt2), jnp.where(first, zeros, ones)], axis=0)
    return r0.astype(BF16), r1.astype(BF16)


def _pair_scores(q_pair, blocks):
    out = []
    for a in range(2):
        scores = []
        for blk in blocks:
            if blk["k_nt"]:
                s = lax.dot_general(q_pair[a], blk["k"], NT_DIMS, preferred_element_type=F32)
            else:
                s = jnp.dot(q_pair[a], blk["k"], preferred_element_type=F32)
            if blk.get("bias") is not None:
                s = s + blk["bias"][a]
            scores.append(s)
        out.append(scores)
    return out


def _pair_finish(all_scores, blocks):
    acc = None
    for a in range(2):
        scores = all_scores[a]
        m = scores[0].max(axis=-1, keepdims=True)
        for s in scores[1:]:
            m = jnp.maximum(m, s.max(axis=-1, keepdims=True))
        for s, blk in zip(scores, blocks):
            p = jnp.exp(s - m).astype(BF16)
            if blk["r_nt"]:
                term = lax.dot_general(p, blk["r"][a], NT_DIMS, preferred_element_type=F32)
            else:
                term = jnp.dot(p, blk["r"][a], preferred_element_type=F32)
            acc = term if acc is None else acc + term
    return acc[:, :LANES] / acc[:, LANES:]


def _adaln_kernel(c_ref, w_ref, b_ref, o_ref):
    c = c_ref[...]
    a = c * jax.nn.sigmoid(c)
    o_ref[...] = jnp.dot(a.astype(BF16), w_ref[...].astype(BF16),
                         preferred_element_type=F32) + b_ref[...]


def _adaln(cvec, w_ada, b_ada):
    tn = ADALN_TILE
    return pl.pallas_call(
        _adaln_kernel,
        grid=(DEPTH, 6 * D_MODEL // tn),
        in_specs=[
            pl.BlockSpec((N_MOD_ROWS, D_MODEL), lambda l, j: (0, 0)),
            pl.BlockSpec((None, D_MODEL, tn), lambda l, j: (l, 0, j)),
            pl.BlockSpec((None, 1, tn), lambda l, j: (l, 0, j)),
        ],
        out_specs=pl.BlockSpec((None, N_MOD_ROWS, tn), lambda l, j: (l, 0, j)),
        out_shape=jax.ShapeDtypeStruct((DEPTH, N_MOD_ROWS, 6 * D_MODEL), F32),
        compiler_params=_params(2),
        name="adaln",
    )(cvec, w_ada, b_ada.reshape(DEPTH, 1, 6 * D_MODEL))


def _na_r0(r, rows):
    return min(max(r - NA_WIN_ROWS // 2, 0), rows - NA_WIN_ROWS)


def _na_union_start(p, rows):
    return min(_na_r0(NA_BLOCK_ROWS * p, rows), rows - NA_UNION_ROWS)


def _na_bias_kernel(tbl_ref, o_ref, *, rows):
    l = pl.program_id(0)
    h = pl.program_id(1)
    qi = lax.broadcasted_iota(jnp.int32, (GRID_W, GRID_W), 0)
    ki = lax.broadcasted_iota(jnp.int32, (GRID_W, GRID_W), 1)
    dc = jnp.clip(ki - qi, -(NA_WIN_COLS - 1), NA_WIN_COLS - 1) + NA_WIN_COLS - 1
    c0 = jnp.clip(qi - NA_WIN_COLS // 2, 0, GRID_W - NA_WIN_COLS)
    col_ok = (ki >= c0) & (ki < c0 + NA_WIN_COLS)
    n_dr = 2 * NA_WIN_ROWS - 1
    n_dc = 2 * NA_WIN_COLS - 1
    tiles = []
    for dr in range(n_dr):
        t = jnp.zeros((GRID_W, GRID_W), F32)
        for d in range(n_dc):
            t = jnp.where(dc == d, tbl_ref[((l * NA_HEADS + h) * n_dr + dr) * n_dc + d], t)
        tiles.append(jnp.where(col_ok, t, NEG))
    neg_tile = jnp.full((GRID_W, GRID_W), NEG, F32)
    for p in range(rows // NA_BLOCK_ROWS):
        start = _na_union_start(p, rows)
        for a in range(NA_BLOCK_ROWS):
            r = NA_BLOCK_ROWS * p + a
            r0 = _na_r0(r, rows)
            for j in range(NA_UNION_ROWS):
                kr = start + j
                inside = r0 <= kr < r0 + NA_WIN_ROWS
                tile = tiles[kr - r + NA_WIN_ROWS - 1] if inside else neg_tile
                o_ref[p, a * GRID_W:(a + 1) * GRID_W, j * GRID_W:(j + 1) * GRID_W] = tile


def _na_bias_expand(na_bias, rows):
    n_blocks = rows // NA_BLOCK_ROWS
    qn = NA_BLOCK_ROWS * GRID_W
    kn = NA_UNION_ROWS * GRID_W
    return pl.pallas_call(
        functools.partial(_na_bias_kernel, rows=rows),
        grid=(DEPTH, NA_HEADS),
        in_specs=[pl.BlockSpec(memory_space=pltpu.SMEM)],
        out_specs=pl.BlockSpec((None, None, n_blocks, qn, kn), lambda l, h: (l, h, 0, 0, 0)),
        out_shape=jax.ShapeDtypeStruct((DEPTH, NA_HEADS, n_blocks, qn, kn), F32),
        compiler_params=_params(2),
        name="na_bias_expand",
    )(na_bias.reshape(-1))


def _mod_row_map(layer, tiles_per_row, first_row):
    if tiles_per_row is None:
        return lambda i: (layer, first_row, 0, 0)
    return lambda i: (layer, first_row + i // tiles_per_row, 0, 0)


def _inproj_kernel(x_ref, mod_ref, g_ref, wna_ref, wgq_ref, wml_ref, wgc_ref, wgr_ref, bgc_ref, bgr_ref,
                   zna_ref, zgq_ref, zml_ref, gi_ref, gf_ref, grow_ref):
    x = x_ref[...]
    h = _rms(x, g_ref[0:1, :]) * (1.0 + mod_ref[1:2, :]) + mod_ref[0:1, :]
    hb = h.astype(BF16)
    zna_ref[...] = lax.dot_general(hb, wna_ref[...], NT_DIMS, preferred_element_type=F32)
    zgq_ref[...] = lax.dot_general(hb, wgq_ref[...], NT_DIMS, preferred_element_type=F32)
    zml_ref[...] = lax.dot_general(hb, wml_ref[...], NT_DIMS, preferred_element_type=F32)
    gc = jnp.dot(hb, wgc_ref[...], preferred_element_type=F32)
    gi_ref[...] = gc[:, 0:N_SCANS] + bgc_ref[0:1, :]
    gf_ref[...] = gc[:, LANES:LANES + N_SCANS] + bgc_ref[1:2, :]
    grow_ref[...] = lax.dot_general(wgr_ref[...], hb, NT_DIMS, preferred_element_type=F32) + bgr_ref[...]


def _inproj(x, mods, g_norm, w, layer, tiles_per_row, first_row, tm):
    n = x.shape[0]
    wspec = lambda width: pl.BlockSpec((None, width, D_MODEL), lambda i: (layer, 0, 0))
    return pl.pallas_call(
        _inproj_kernel,
        grid=(n // tm,),
        in_specs=[
            pl.BlockSpec((tm, D_MODEL), lambda i: (i, 0)),
            pl.BlockSpec((None, None, 6, D_MODEL), _mod_row_map(layer, tiles_per_row, first_row)),
            pl.BlockSpec((None, 4, D_MODEL), lambda i: (layer, 0, 0)),
            wspec(ZNA_W), wspec(ZGQ_W), wspec(ZML_W),
            pl.BlockSpec((None, D_MODEL, 2 * LANES), lambda i: (layer, 0, 0)),
            pl.BlockSpec((None, N_GATES, D_MODEL), lambda i: (layer, 0, 0)),
            pl.BlockSpec((None, 2, N_SCANS), lambda i: (layer, 0, 0)),
            pl.BlockSpec((None, N_GATES, 1), lambda i: (layer, 0, 0)),
        ],
        out_specs=[
            pl.BlockSpec((tm, ZNA_W), lambda i: (i, 0)),
            pl.BlockSpec((tm, ZGQ_W), lambda i: (i, 0)),
            pl.BlockSpec((tm, ZML_W), lambda i: (i, 0)),
            pl.BlockSpec((tm, N_SCANS), lambda i: (i, 0)),
            pl.BlockSpec((tm, N_SCANS), lambda i: (i, 0)),
            pl.BlockSpec((N_GATES, tm), lambda i: (0, i)),
        ],
        out_shape=[
            jax.ShapeDtypeStruct((n, ZNA_W), F32),
            jax.ShapeDtypeStruct((n, ZGQ_W), F32),
            jax.ShapeDtypeStruct((n, ZML_W), F32),
            jax.ShapeDtypeStruct((n, N_SCANS), F32),
            jax.ShapeDtypeStruct((n, N_SCANS), F32),
            jax.ShapeDtypeStruct((N_GATES, n), F32),
        ],
        compiler_params=_params(1),
        name="inproj",
    )(x, mods, g_norm, w["na"], w["gq"], w["ml"], w["gate_col"], w["gate_row"], w["b_col"], w["b_row"])


def _post_kernel(x_ref, mna_ref, mgq_ref, mml_ref, mod_ref, g_ref, wo_ref, wgu_ref, wd_ref, o_ref):
    acc = jnp.dot(mna_ref[...], wo_ref[0:NA_W, :], preferred_element_type=F32)
    acc += jnp.dot(mgq_ref[...], wo_ref[NA_W:NA_W + GQA_QW, :], preferred_element_type=F32)
    acc += jnp.dot(mml_ref[...], wo_ref[NA_W + GQA_QW:, :], preferred_element_type=F32)
    x1 = x_ref[...] + mod_ref[2:3, :] * _rms(acc, g_ref[1:2, :])
    hb = (_rms(x1, g_ref[2:3, :]) * (1.0 + mod_ref[4:5, :]) + mod_ref[3:4, :]).astype(BF16)
    gate = jnp.dot(hb, wgu_ref[:, 0:FF_HIDDEN], preferred_element_type=F32)
    up = jnp.dot(hb, wgu_ref[:, FF_HIDDEN:], preferred_element_type=F32)
    act = (gate * jax.nn.sigmoid(gate) * up).astype(BF16)
    f = jnp.dot(act, wd_ref[...], preferred_element_type=F32)
    o_ref[...] = x1 + mod_ref[5:6, :] * _rms(f, g_ref[3:4, :])


def _post(x, mna, mgq, mml, mods, g_norm, w_out, w_gu, w_down, layer, tiles_per_row, first_row, tm):
    n = x.shape[0]
    resident = lambda shape: pl.BlockSpec((None,) + shape, lambda i: (layer, 0, 0), pipeline_mode=pl.Buffered(1))
    return pl.pallas_call(
        _post_kernel,
        grid=(n // tm,),
        in_specs=[
            pl.BlockSpec((tm, D_MODEL), lambda i: (i, 0)),
            pl.BlockSpec((tm, NA_W), lambda i: (i, 0)),
            pl.BlockSpec((tm, GQA_QW), lambda i: (i, 0)),
            pl.BlockSpec((tm, ML_W), lambda i: (i, 0)),
            pl.BlockSpec((None, None, 6, D_MODEL), _mod_row_map(layer, tiles_per_row, first_row)),
            pl.BlockSpec((None, 4, D_MODEL), lambda i: (layer, 0, 0)),
            resident((D_MODEL, D_MODEL)),
            resident((D_MODEL, 2 * FF_HIDDEN)),
            resident((FF_HIDDEN, D_MODEL)),
        ],
        out_specs=pl.BlockSpec((tm, D_MODEL), lambda i: (i, 0)),
        out_shape=jax.ShapeDtypeStruct((n, D_MODEL), F32),
        compiler_params=_params(1),
        name="post",
    )(x, mna, mgq, mml, mods, g_norm, w_out, w_gu, w_down)


def _pair_rms(x, gain, sums_on_mxu=False):
    xsq = x * x
    if sums_on_mxu:
        row = lax.broadcasted_iota(jnp.int32, (LANES, LANES), 0) < HEAD_DIM
        col = lax.broadcasted_iota(jnp.int32, (LANES, LANES), 1) < HEAD_DIM
        ones_blk = jnp.where(row == col, 1.0, 0.0).astype(BF16)
        sums = sum(jnp.dot(p, ones_blk, preferred_element_type=F32) for p in _split3(xsq))
    else:
        first = _lane_first(x.shape)
        sums = jnp.where(first, jnp.sum(jnp.where(first, xsq, 0.0), axis=-1, keepdims=True),
                         jnp.sum(jnp.where(first, 0.0, xsq), axis=-1, keepdims=True))
    return x * lax.rsqrt(sums * (1.0 / HEAD_DIM) + EPS) * gain


def _pair_gain(gqk_ref, row):
    return jnp.concatenate([gqk_ref[row:row + 1, :]] * (LANES // HEAD_DIM), axis=-1)


def _ctx_attn_kernel(zna_ref, zgq_ref, gqk_ref, *rest):
    mna_ref, mgq_ref, kvna_ref, kvgq_ref = rest[-4:]

    def store_t(ref, which, pair, x2):
        xt = x2.T
        for a in range(2):
            ref[which, 2 * pair + a] = xt[a * HEAD_DIM:(a + 1) * HEAD_DIM]

    pending = []
    for i in range(NA_HEADS // 2):
        cols = slice(i * LANES, (i + 1) * LANES)
        q2 = zna_ref[:, cols] * QK_SCALE
        k2 = zna_ref[:, NA_W + i * LANES:NA_W + (i + 1) * LANES]
        v2 = zna_ref[:, 2 * NA_W + i * LANES:2 * NA_W + (i + 1) * LANES]
        store_t(kvna_ref, 0, i, k2)
        store_t(kvna_ref, 1, i, v2)
        blocks = [{"k": k2.astype(BF16), "k_nt": True, "r": _pair_values(v2), "r_nt": False}]
        pending.append((mna_ref, cols, _pair_scores(_pair_queries(q2), blocks), blocks))

    gq, gk = _pair_gain(gqk_ref, 0), _pair_gain(gqk_ref, 1)
    k2 = _pair_rms(zgq_ref[:, GQA_QW:GQA_QW + GQA_KW], gk)
    v2 = zgq_ref[:, GQA_QW + GQA_KW:]
    store_t(kvgq_ref, 0, 0, k2)
    store_t(kvgq_ref, 1, 0, v2)
    blocks = [{"k": k2.astype(BF16), "k_nt": True, "r": _pair_values(v2), "r_nt": False}]
    for p in range(GQA_GROUP):
        cols = slice(p * LANES, (p + 1) * LANES)
        q2 = _pair_rms(zgq_ref[:, cols], gq) * QK_SCALE
        pending.append((mgq_ref, cols, _pair_scores(_pair_queries(q2), blocks), blocks))

    for ref, cols, scores, blocks in pending:
        ref[:, cols] = _pair_finish(scores, blocks).astype(BF16)


def _ctx_attn(zna, zgq, g_qk, kv_na_buf, kv_gq_buf, layer, batch, t):
    n = zna.shape[0]
    return pl.pallas_call(
        _ctx_attn_kernel,
        grid=(batch,),
        in_specs=[
            pl.BlockSpec((t, ZNA_W), lambda b: (b, 0)),
            pl.BlockSpec((t, ZGQ_W), lambda b: (b, 0)),
            pl.BlockSpec((None, 2, HEAD_DIM), lambda b: (layer, 0, 0)),
            pl.BlockSpec(memory_space=pl.ANY),
            pl.BlockSpec(memory_space=pl.ANY),
        ],
        out_specs=[
            pl.BlockSpec((t, NA_W), lambda b: (b, 0)),
            pl.BlockSpec((t, GQA_QW), lambda b: (b, 0)),
            pl.BlockSpec((None, None, 2, NA_HEADS, HEAD_DIM, t), lambda b: (b, layer, 0, 0, 0, 0)),
            pl.BlockSpec((None, None, 2, GQA_KV_HEADS, HEAD_DIM, t), lambda b: (b, layer, 0, 0, 0, 0)),
        ],
        out_shape=[
            jax.ShapeDtypeStruct((n, NA_W), BF16),
            jax.ShapeDtypeStruct((n, GQA_QW), BF16),
            jax.ShapeDtypeStruct((batch, DEPTH, 2, NA_HEADS, HEAD_DIM, t), F32),
            jax.ShapeDtypeStruct((batch, DEPTH, 2, GQA_KV_HEADS, HEAD_DIM, t), F32),
        ],
        input_output_aliases={3: 2, 4: 3},
        compiler_params=_params(1),
        name="ctx_attn",
    )(zna, zgq, g_qk, kv_na_buf, kv_gq_buf)


def _lat_na_kernel(zna_ref, cache_ref, bias_ref, o_ref, *, rows):
    qn = NA_BLOCK_ROWS * GRID_W
    kn = NA_UNION_ROWS * GRID_W
    n_blocks = rows // NA_BLOCK_ROWS
    prepared = []
    for i in range(NA_HEADS // 2):
        cols = slice(i * LANES, (i + 1) * LANES)
        q_pair = _pair_queries(zna_ref[:, cols] * QK_SCALE)
        k2 = zna_ref[:, NA_W + i * LANES:NA_W + (i + 1) * LANES].astype(BF16)
        kc = cache_ref[0, i].astype(BF16)
        scores = []
        for a in range(2):
            s_ctx = jnp.dot(q_pair[a], kc, preferred_element_type=F32)
            s_win = []
            for p in range(n_blocks):
                k0 = _na_union_start(p, rows) * GRID_W
                s_win.append(lax.dot_general(q_pair[a][p * qn:(p + 1) * qn], k2[k0:k0 + kn], NT_DIMS,
                                             preferred_element_type=F32) + bias_ref[2 * i + a, p])
            scores.append((s_ctx, s_win))
        prepared.append(scores)
    for i in range(NA_HEADS // 2):
        cols = slice(i * LANES, (i + 1) * LANES)
        r = _pair_values(zna_ref[:, 2 * NA_W + i * LANES:2 * NA_W + (i + 1) * LANES])
        rc = _pair_values_t(cache_ref[1, i])
        acc = None
        for a in range(2):
            s_ctx, s_win = prepared[i][a]
            m_win = jnp.concatenate([s.max(axis=-1, keepdims=True) for s in s_win], axis=0)
            m = jnp.maximum(m_win, s_ctx.max(axis=-1, keepdims=True))
            term = lax.dot_general(jnp.exp(s_ctx - m).astype(BF16), rc[a], NT_DIMS, preferred_element_type=F32)
            wins = []
            for p in range(n_blocks):
                k0 = _na_union_start(p, rows) * GRID_W
                pw = jnp.exp(s_win[p] - m[p * qn:(p + 1) * qn]).astype(BF16)
                wins.append(jnp.dot(pw, r[a][k0:k0 + kn], preferred_element_type=F32))
            term = term + jnp.concatenate(wins, axis=0)
            acc = term if acc is None else acc + term
        o_ref[:, cols] = (acc[:, :LANES] / acc[:, LANES:]).astype(BF16)


def _lat_na(zna, cache_t, bias, layer, batch, t):
    n = zna.shape[0]
    rows = t // GRID_W
    past = cache_t.shape[-1]
    n_blocks = rows // NA_BLOCK_ROWS
    qn = NA_BLOCK_ROWS * GRID_W
    kn = NA_UNION_ROWS * GRID_W
    return pl.pallas_call(
        functools.partial(_lat_na_kernel, rows=rows),
        grid=(batch,),
        in_specs=[
            pl.BlockSpec((t, ZNA_W), lambda b: (b, 0)),
            pl.BlockSpec((None, None, 2, NA_HEADS // 2, LANES, past), lambda b: (b, layer, 0, 0, 0, 0)),
            pl.BlockSpec((None, NA_HEADS, n_blocks, qn, kn), lambda b: (layer, 0, 0, 0, 0)),
        ],
        out_specs=pl.BlockSpec((t, NA_W), lambda b: (b, 0)),
        out_shape=jax.ShapeDtypeStruct((n, NA_W), BF16),
        compiler_params=_params(1),
        name="lat_na",
    )(zna, cache_t, bias)


def _rope_tables(t):
    half = HEAD_DIM // 2
    quarter = half // 2
    inv = 1.0 / (ROPE_BASE ** (jnp.arange(quarter, dtype=F32) / quarter))
    tt = jnp.arange(t)
    row = (tt // GRID_W).astype(F32)
    col = (tt % GRID_W).astype(F32)
    ang_r = row[:, None] * inv[None, :]
    ang_c = col[:, None] * inv[None, :]
    cos = jnp.concatenate([jnp.cos(ang_r)] * 2 + [jnp.cos(ang_c)] * 2, axis=-1)
    sin = jnp.concatenate([-jnp.sin(ang_r), jnp.sin(ang_r), -jnp.sin(ang_c), jnp.sin(ang_c)], axis=-1)
    reps = LANES // HEAD_DIM
    return jnp.tile(cos, (1, reps)), jnp.tile(sin, (1, reps))


def _pair_rope(xn, cos, sin):
    quarter = HEAD_DIM // 4
    lane = lax.broadcasted_iota(jnp.int32, xn.shape, 1)
    lower = (lane & (2 * quarter - 1)) < quarter
    partner = jnp.where(lower, pltpu.roll(xn, LANES - quarter, 1), pltpu.roll(xn, quarter, 1))
    return xn * cos + partner * sin


def _lat_gqa_kernel(zgq_ref, cache_ref, gqk_ref, cos_ref, sin_ref, o_ref):
    cos = cos_ref[...]
    sin = sin_ref[...]
    gq, gk = _pair_gain(gqk_ref, 0), _pair_gain(gqk_ref, 1)
    keys = _pair_rope(_pair_rms(zgq_ref[:, GQA_QW:GQA_QW + GQA_KW], gk, True), cos, sin).astype(BF16)
    blocks = [
        {"k": keys, "k_nt": True, "r": _pair_values(zgq_ref[:, GQA_QW + GQA_KW:]), "r_nt": False},
        {"k": cache_ref[0].astype(BF16), "k_nt": False, "r": _pair_values_t(cache_ref[1]), "r_nt": True},
    ]
    def scores(p):
        x = _pair_rope(_pair_rms(zgq_ref[:, p * LANES:(p + 1) * LANES], gq, True), cos, sin) * QK_SCALE
        return _pair_scores(_pair_queries(x), blocks)

    nxt = scores(0)
    for p in range(GQA_GROUP):
        cur = nxt
        if p + 1 < GQA_GROUP:
            nxt = scores(p + 1)
        o_ref[:, p * LANES:(p + 1) * LANES] = _pair_finish(cur, blocks).astype(BF16)


def _lat_gqa(zgq, cache_t, g_qk, cos, sin, layer, batch, t):
    n = zgq.shape[0]
    past = cache_t.shape[-1]
    return pl.pallas_call(
        _lat_gqa_kernel,
        grid=(batch,),
        in_specs=[
            pl.BlockSpec((t, ZGQ_W), lambda b: (b, 0)),
            pl.BlockSpec((None, None, 2, GQA_KW, past), lambda b: (b, layer, 0, 0, 0)),
            pl.BlockSpec((None, 2, HEAD_DIM), lambda b: (layer, 0, 0)),
            pl.BlockSpec((t, LANES), lambda b: (0, 0)),
            pl.BlockSpec((t, LANES), lambda b: (0, 0)),
        ],
        out_specs=pl.BlockSpec((t, GQA_QW), lambda b: (b, 0)),
        out_shape=jax.ShapeDtypeStruct((n, GQA_QW), BF16),
        compiler_params=_params(1),
        name="lat_gqa",
    )(zgq, cache_t, g_qk, cos, sin)


def _split3(x):
    x1 = x.astype(BF16)
    r1 = x - x1.astype(F32)
    x2 = r1.astype(BF16)
    x3 = (r1 - x2.astype(F32)).astype(BF16)
    return x1, x2, x3


def _log_sigmoid(x):
    return jnp.minimum(x, 0.0) - jnp.log1p(jnp.exp(-jnp.abs(x)))


def _mlstm_select_matrix():
    H = MLSTM_HEADS
    sel = np.zeros((MLSTM_SEL_ROWS, (H // 2) * 2 * MLSTM_TILE_KINDS * LANES), np.float32)
    for j in range(H // 2):
        for d in range(2):
            for q in range(MLSTM_TILE_KINDS):
                for a in range(2):
                    col0 = ((j * 2 + d) * MLSTM_TILE_KINDS + q) * LANES + a * HEAD_DIM
                    sel[q * N_SCANS + d * H + 2 * j + a, col0:col0 + HEAD_DIM] = 1.0
    return sel


def _mlstm_kernel(zml_ref, gi_ref, gf_ref, grow_ref, c0_ref, n0_ref, m0_ref, gml_ref, sel_ref, *rest,
                  t, emit_state):
    if emit_state:
        o_ref, cf_ref, nf_ref, mf_ref = rest[:4]
    else:
        o_ref = rest[0]
    tri_s, st_s, dst_s, cst_s, row_s = rest[-5:]
    L = MLSTM_CHUNK
    H = MLSTM_HEADS
    HD = HEAD_DIM
    NP = H // 2
    nc = t // L
    tb = tri_s.shape[-1]

    @pl.when(pl.program_id(0) == 0)
    def _():
        ti = lax.broadcasted_iota(jnp.int32, (tb, tb), 0)
        ui = lax.broadcasted_iota(jnp.int32, (tb, tb), 1)
        same = (ti & -L) == (ui & -L)
        tri_s[0] = jnp.where(same & (ui <= ti), 1.0, 0.0).astype(BF16)
        tri_s[1] = jnp.where(same & (ui >= ti), 1.0, 0.0).astype(BF16)

    lower, upper = tri_s[0], tri_s[1]

    def chunk_sums_cols(x):
        parts = _split3(x)
        pre, suf = [], []
        for i in range(t // tb):
            blk = [p[i * tb:(i + 1) * tb] for p in parts]
            pre.append(sum(jnp.dot(lower, p, preferred_element_type=F32) for p in blk))
            suf.append(sum(jnp.dot(upper, p, preferred_element_type=F32) for p in blk))
        return jnp.concatenate(pre, axis=0), jnp.concatenate(suf, axis=0)

    def chunk_sums_rows(x):
        parts = _split3(x)
        pre, suf = [], []
        for i in range(t // tb):
            blk = [p[:, i * tb:(i + 1) * tb] for p in parts]
            pre.append(sum(jnp.dot(p, upper, preferred_element_type=F32) for p in blk))
            suf.append(sum(jnp.dot(p, lower, preferred_element_type=F32) for p in blk))
        return jnp.concatenate(pre, axis=1), jnp.concatenate(suf, axis=1)

    pre_c, suf_c = chunk_sums_cols(_log_sigmoid(gf_ref[...]))
    lane_c = lax.broadcasted_iota(jnp.int32, (t, N_SCANS), 1)
    b3 = jnp.where(lane_c < H, pre_c, suf_c).reshape(nc, L, N_SCANS)
    i3 = gi_ref[...].reshape(nc, L, N_SCANS)
    fwd3 = lax.broadcasted_iota(jnp.int32, (nc, 1, N_SCANS), 2) < H
    b_end3 = jnp.where(fwd3, b3[:, L - 1:L, :], b3[:, 0:1, :])
    lw_end3 = b_end3 - b3 + i3
    a3 = jnp.max(lw_end3, axis=1, keepdims=True)
    wloc3 = jnp.exp(lw_end3 - a3)

    fwd1 = lax.broadcasted_iota(jnp.int32, (1, N_SCANS), 1) < H
    m = m0_ref[...]
    m_start, carry_decay, contrib_scale = [], [], []
    for j in range(nc):
        a_j = jnp.where(fwd1, a3[j], a3[nc - 1 - j])
        g_j = jnp.where(fwd1, b_end3[j], b_end3[nc - 1 - j])
        m_start.append(m)
        m_next = jnp.maximum(g_j + m, a_j)
        carry_decay.append(jnp.exp(g_j + m - m_next))
        contrib_scale.append(jnp.exp(a_j - m_next))
        m = m_next
    mst3 = jnp.concatenate([jnp.where(fwd1, m_start[c], m_start[nc - 1 - c])[None] for c in range(nc)], axis=0)

    cols = jnp.concatenate([b3.reshape(t, N_SCANS), wloc3.reshape(t, N_SCANS)], axis=1)
    tiles_all = sum(jnp.dot(p, sel_ref[...], preferred_element_type=F32) for p in _split3(cols))
    tile_w = MLSTM_TILE_KINDS * LANES

    def tiles(j, d):
        x = tiles_all[:, (2 * j + d) * tile_w:(2 * j + d + 1) * tile_w]
        return [x[:, q * LANES:(q + 1) * LANES].reshape(nc, L, LANES) for q in range(MLSTM_TILE_KINDS)]

    gr = grow_ref[...]
    pre_r, suf_r = chunk_sums_rows(_log_sigmoid(gr))
    sub_r = lax.broadcasted_iota(jnp.int32, (N_SCANS, t), 0)
    rowv = gr[0:N_SCANS] - jnp.where(sub_r < H, pre_r[N_SCANS:], suf_r[N_SCANS:])
    for j in range(NP):
        for d in range(2):
            e = d * H + 2 * j
            for c in range(nc):
                row_s[2 * j + d, c] = jnp.concatenate(
                    [rowv[e:e + 1, c * L:(c + 1) * L], rowv[e + 1:e + 2, c * L:(c + 1) * L]], axis=1)

    lane_a = lax.broadcasted_iota(jnp.int32, (1, 1, LANES), 2) < HD
    sub_a = lax.broadcasted_iota(jnp.int32, (1, 2 * HD, 1), 1) < HD
    diag = sub_a == lane_a
    diag4 = jnp.concatenate([diag] * 4, axis=2)

    def stack_heads(x3):
        return jnp.concatenate([jnp.where(lane_a, x3, 0.0), jnp.where(lane_a, 0.0, x3)], axis=1)

    def pair_cols(base, j):
        return slice(base + j * LANES, base + (j + 1) * LANES)

    zero_blk = jnp.zeros((HD, HD), F32)
    for j in range(NP):
        cols_d = []
        for d in range(2):
            ca, cb = c0_ref[d, 2 * j].T, c0_ref[d, 2 * j + 1].T
            na = jnp.broadcast_to(n0_ref[d, 2 * j:2 * j + 1, :], (HD, HD)).T
            nb = jnp.broadcast_to(n0_ref[d, 2 * j + 1:2 * j + 2, :], (HD, HD)).T
            top = jnp.concatenate([ca, zero_blk, na, zero_blk], axis=1)
            bot = jnp.concatenate([zero_blk, cb, zero_blk, nb], axis=1)
            cols_d.append(jnp.concatenate([top, bot], axis=0))
        st_s[j] = jnp.concatenate(cols_d, axis=1)

    for j in range(NP):
        k3 = (zml_ref[:, pair_cols(ML_W, j)] * QK_SCALE).reshape(nc, L, LANES).astype(BF16)
        v3 = zml_ref[:, pair_cols(2 * ML_W, j)].reshape(nc, L, LANES)
        rhs = []
        for d in range(2):
            wl = tiles(j, d)[1]
            rhs += [v3 * wl, wl]
        rhs = jnp.concatenate(rhs, axis=2).astype(BF16)
        contrib = jnp.einsum("csk,csn->ckn", k3, rhs, preferred_element_type=F32)
        dst_s[j] = jnp.where(diag4, contrib, 0.0)

    def lane_scale(v, j):
        pieces = []
        for d in range(2):
            sa = jnp.broadcast_to(v[:, d * H + 2 * j:d * H + 2 * j + 1], (1, HD))
            sb = jnp.broadcast_to(v[:, d * H + 2 * j + 1:d * H + 2 * j + 2], (1, HD))
            pieces += [sa, sb, sa, sb]
        return jnp.concatenate(pieces, axis=1)

    for j in range(NP):
        st = st_s[j]
        for step in range(nc):
            cb = nc - 1 - step
            stb = st.astype(BF16)
            cst_s[j, step, :, 0:2 * LANES] = stb[:, 0:2 * LANES]
            cst_s[j, cb, :, 2 * LANES:] = stb[:, 2 * LANES:]
            delta = jnp.concatenate([dst_s[j, step, :, 0:2 * LANES], dst_s[j, cb, :, 2 * LANES:]], axis=1)
            st = lane_scale(carry_decay[step], j) * st + lane_scale(contrib_scale[step], j) * delta
        st_s[j] = st

    sidx = lax.broadcasted_iota(jnp.int32, (1, L, LANES), 2) & (HD - 1)
    tidx = lax.broadcasted_iota(jnp.int32, (1, L, LANES), 1)
    masks = (sidx <= tidx, sidx >= tidx)
    ones_blk = jnp.broadcast_to(jnp.where(diag, 1.0, 0.0).astype(BF16), (nc, 2 * HD, LANES))
    neg_inf = -jnp.inf
    for j in range(NP):
        q3 = zml_ref[:, pair_cols(0, j)].reshape(nc, L, LANES).astype(BF16)
        k3 = (zml_ref[:, pair_cols(ML_W, j)] * QK_SCALE).reshape(nc, L, LANES)
        v3 = zml_ref[:, pair_cols(2 * ML_W, j)].reshape(nc, L, LANES)
        qk = jnp.einsum("ctd,cnd->ctn", q3, stack_heads(k3).astype(BF16), preferred_element_type=F32)
        v_aug = jnp.concatenate([stack_heads(v3).astype(BF16), ones_blk], axis=2)
        out = None
        for d in range(2):
            b_t = tiles(j, d)[0]
            e = d * H + 2 * j
            bm_t = b_t + jnp.where(lane_a, mst3[:, :, e:e + 1], mst3[:, :, e + 1:e + 2])
            logw = jnp.where(masks[d], b_t + row_s[2 * j + d], neg_inf)
            rmax_a = jnp.max(jnp.where(lane_a, logw, neg_inf), axis=-1, keepdims=True)
            rmax_b = jnp.max(jnp.where(lane_a, neg_inf, logw), axis=-1, keepdims=True)
            m_t = jnp.maximum(jnp.where(lane_a, rmax_a, rmax_b), bm_t)
            s = qk * jnp.exp(logw - m_t)
            decay = jnp.exp(bm_t - m_t)
            sv = jnp.einsum("cts,csn->ctn", s.astype(BF16), v_aug, preferred_element_type=F32)
            state = cst_s[j, :, :, 2 * d * LANES:2 * (d + 1) * LANES]
            inter = jnp.einsum("ctk,ckn->ctn", q3, state, preferred_element_type=F32)
            num = sv[:, :, 0:LANES] + decay * inter[:, :, 0:LANES]
            den = sv[:, :, LANES:] + decay * inter[:, :, LANES:]
            h_d = num / jnp.maximum(jnp.abs(den), jnp.exp(-m_t))
            out = h_d if out is None else out + h_d
        cols = pair_cols(0, j)
        og = jax.nn.sigmoid(zml_ref[:, pair_cols(3 * ML_W, j)])
        o_ref[:, cols] = (_pair_rms(out.reshape(t, LANES), gml_ref[:, cols]) * og).astype(BF16)

    if emit_state:
        for j in range(NP):
            st = st_s[j]
            for d in range(2):
                for a in range(2):
                    rows = slice(a * HD, (a + 1) * HD)
                    c0 = 2 * d * LANES + a * HD
                    cf_ref[d, 2 * j + a] = st[rows, c0:c0 + HD].T
                    nf_ref[d, 2 * j + a:2 * j + a + 1, :] = st[rows, c0 + LANES:c0 + LANES + HD].T[0:1, :]
        mf_ref[...] = m


def _mlstm(zml, gi, gf, grow, c0, n0, m0, g_ml, layer, batch, t, emit_state, state_layer):
    n = zml.shape[0]
    H = MLSTM_HEADS
    L = MLSTM_CHUNK
    nc = t // L
    sel = jnp.asarray(_mlstm_select_matrix(), BF16)
    tri_block = min(t, MXU_DIM)
    assert L & (L - 1) == 0 and tri_block % L == 0 and t % tri_block == 0
    if state_layer is None:
        c_spec = pl.BlockSpec((None, 2, H, HEAD_DIM, HEAD_DIM), lambda b: (0, 0, 0, 0, 0))
        n_spec = pl.BlockSpec((None, 2, H, HEAD_DIM), lambda b: (0, 0, 0, 0))
        m_spec = pl.BlockSpec((None, 1, N_SCANS), lambda b: (0, 0, 0))
    else:
        c_spec = pl.BlockSpec((None, None, 2, H, HEAD_DIM, HEAD_DIM), lambda b: (b, state_layer, 0, 0, 0, 0))
        n_spec = pl.BlockSpec((None, None, 2, H, HEAD_DIM), lambda b: (b, state_layer, 0, 0, 0))
        m_spec = pl.BlockSpec((None, None, 1, N_SCANS), lambda b: (b, state_layer, 0, 0))
    out_specs = [pl.BlockSpec((t, ML_W), lambda b: (b, 0))]
    out_shape = [jax.ShapeDtypeStruct((n, ML_W), BF16)]
    if emit_state:
        out_specs += [
            pl.BlockSpec((None, 2, H, HEAD_DIM, HEAD_DIM), lambda b: (b, 0, 0, 0, 0)),
            pl.BlockSpec((None, 2, H, HEAD_DIM), lambda b: (b, 0, 0, 0)),
            pl.BlockSpec((None, 1, N_SCANS), lambda b: (b, 0, 0)),
        ]
        out_shape += [
            jax.ShapeDtypeStruct((batch, 2, H, HEAD_DIM, HEAD_DIM), F32),
            jax.ShapeDtypeStruct((batch, 2, H, HEAD_DIM), F32),
            jax.ShapeDtypeStruct((batch, 1, N_SCANS), F32),
        ]
    return pl.pallas_call(
        functools.partial(_mlstm_kernel, t=t, emit_state=emit_state),
        grid=(batch,),
        in_specs=[
            pl.BlockSpec((t, ZML_W), lambda b: (b, 0)),
            pl.BlockSpec((t, N_SCANS), lambda b: (b, 0)),
            pl.BlockSpec((t, N_SCANS), lambda b: (b, 0)),
            pl.BlockSpec((N_GATES, t), lambda b: (0, b)),
            c_spec, n_spec, m_spec,
            pl.BlockSpec((None, 1, ML_W), lambda b: (layer, 0, 0)),
            pl.BlockSpec(sel.shape, lambda b: (0, 0)),
        ],
        out_specs=out_specs,
        out_shape=out_shape,
        scratch_shapes=[
            pltpu.VMEM((2, tri_block, tri_block), BF16),
            pltpu.VMEM((H // 2, 2 * HEAD_DIM, 4 * LANES), F32),
            pltpu.VMEM((H // 2, nc, 2 * HEAD_DIM, 4 * LANES), F32),
            pltpu.VMEM((H // 2, nc, 2 * HEAD_DIM, 4 * LANES), BF16),
            pltpu.VMEM((H, nc, 1, LANES), F32),
        ],
        compiler_params=_params(1),
        name="mlstm",
    )(zml, gi, gf, grow, c0, n0, m0, g_ml, sel)


def _layer_path(x, mods, layer, first_row, tiles_row_tokens, weights, mixers):
    per_row = None if tiles_row_tokens is None else tiles_row_tokens // TOKEN_TILE
    g_norm = weights["g_norm"]
    zna, zgq, zml, gi, gf, grow = _inproj(x, mods, g_norm, weights["w_in"], layer, per_row, first_row, TOKEN_TILE)
    mna, mgq, mml, extra = mixers(zna, zgq, zml, gi, gf, grow)
    x = _post(x, mna, mgq, mml, mods, g_norm, weights["w_out"], weights["w_gu"], weights["w_down"], layer,
              per_row, first_row, TOKEN_TILE)
    return x, extra


def _gqa_pair_order():
    return [a * GQA_GROUP + p for p in range(GQA_GROUP) for a in range(GQA_KV_HEADS)]


def _take_blocks(x, axis, base, width, order):
    return jnp.concatenate([lax.slice_in_dim(x, base + width * o, base + width * (o + 1), axis=axis)
                            for o in order], axis=axis)


def kernel(x_prompt, x_sample, cache_na_kv, cache_gqa_kv, state_mlstm_C, state_mlstm_n, state_mlstm_m,
           c, c_ctx, w_in, b_gates, w_out, g_norm, g_qk, g_mlstm, na_bias, w_ada, b_ada, w_gu, w_down):
    batch, seq, _ = x_prompt.shape
    dec_batch, dec_seq, _ = x_sample.shape
    past = cache_na_kv.shape[-2]
    assert dec_batch + 1 <= N_MOD_ROWS and dec_seq % GRID_W == 0 and GQA_KV_HEADS == 2

    cvec = jnp.concatenate([c_ctx[None, :], c, jnp.zeros((N_MOD_ROWS - 1 - dec_batch, D_MODEL), F32)], axis=0)
    mods = _adaln(cvec, w_ada, b_ada).reshape(DEPTH, N_MOD_ROWS, 6, D_MODEL)
    bias = _na_bias_expand(na_bias, dec_seq // GRID_W)
    cos, sin = _rope_tables(dec_seq)

    pair_order = _gqa_pair_order()
    scan_order = [2 * d for d in range(2)]
    H = MLSTM_HEADS
    w_in_t = jnp.swapaxes(w_in, 1, 2)
    w_gi = _take_blocks(w_in_t, 1, OFF_GATES, H, scan_order)
    w_gf = _take_blocks(w_in_t, 1, OFF_GATES + H, H, scan_order)
    b_gi = _take_blocks(b_gates, 1, 0, H, scan_order)
    b_gf = _take_blocks(b_gates, 1, H, H, scan_order)
    lane_pad = lambda a: jnp.pad(jnp.swapaxes(a, 1, 2), ((0, 0), (0, 0), (0, LANES - N_SCANS)))
    w_gq = jnp.concatenate([_take_blocks(w_in_t, 1, OFF_GQ, HEAD_DIM, pair_order),
                            w_in_t[:, OFF_GQ + GQA_QW:OFF_ML]], axis=1)
    w_out_rows = jnp.concatenate([w_out[:, :NA_W], _take_blocks(w_out, 1, NA_W, HEAD_DIM, pair_order),
                                  w_out[:, NA_W + GQA_QW:]], axis=1)
    weights = {
        "g_norm": g_norm,
        "w_in": {
            "na": w_in_t[:, :OFF_GQ].astype(BF16),
            "gq": w_gq.astype(BF16),
            "ml": w_in_t[:, OFF_ML:OFF_GATES].astype(BF16),
            "gate_col": jnp.concatenate([lane_pad(w_gi), lane_pad(w_gf)], axis=-1).astype(BF16),
            "gate_row": jnp.concatenate([w_gi, w_gf], axis=1).astype(BF16),
            "b_col": jnp.stack([b_gi, b_gf], axis=1),
            "b_row": jnp.concatenate([b_gi, b_gf], axis=-1)[:, :, None],
        },
        "w_out": w_out_rows.astype(BF16),
        "w_gu": w_gu.astype(BF16),
        "w_down": w_down.astype(BF16),
    }
    g_ml = g_mlstm.reshape(DEPTH, 1, ML_W)
    zero_c = jnp.zeros((1, 2, MLSTM_HEADS, HEAD_DIM, HEAD_DIM), F32)
    zero_n = jnp.zeros((1, 2, MLSTM_HEADS, HEAD_DIM), F32)
    zero_m = jnp.zeros((1, 1, N_SCANS), F32)
    m0_lat = state_mlstm_m.reshape(dec_batch, DEPTH, 1, N_SCANS)
    cache_na_t = jnp.swapaxes(cache_na_kv, -1, -2).reshape(dec_batch, DEPTH, 2, NA_HEADS // 2, LANES, past)
    cache_gq_t = jnp.swapaxes(cache_gqa_kv, -1, -2).reshape(dec_batch, DEPTH, 2, GQA_KW, past)

    xp = x_prompt.reshape(batch * seq, D_MODEL)
    xs = x_sample.reshape(dec_batch * dec_seq, D_MODEL)
    kv_na = jnp.zeros((batch, DEPTH, 2, NA_HEADS, HEAD_DIM, seq), F32)
    kv_gq = jnp.zeros((batch, DEPTH, 2, GQA_KV_HEADS, HEAD_DIM, seq), F32)
    c_l, n_l, m_l = [], [], []
    for layer in range(DEPTH):
        def ctx_mixers(zna, zgq, zml, gi, gf, grow, layer=layer, kv_na=kv_na, kv_gq=kv_gq):
            mna, mgq, kv_na, kv_gq = _ctx_attn(zna, zgq, g_qk, kv_na, kv_gq, layer, batch, seq)
            mml, cf, nf, mf = _mlstm(zml, gi, gf, grow, zero_c, zero_n, zero_m, g_ml, layer, batch, seq,
                                     True, None)
            return mna, mgq, mml, (kv_na, kv_gq, cf, nf, mf.reshape(batch, 2, MLSTM_HEADS))

        def lat_mixers(zna, zgq, zml, gi, gf, grow, layer=layer):
            mna = _lat_na(zna, cache_na_t, bias, layer, dec_batch, dec_seq)
            mgq = _lat_gqa(zgq, cache_gq_t, g_qk, cos, sin, layer, dec_batch, dec_seq)
            (mml,) = _mlstm(zml, gi, gf, grow, state_mlstm_C, state_mlstm_n, m0_lat, g_ml, layer,
                            dec_batch, dec_seq, False, layer)
            return mna, mgq, mml, None

        xp, (kv_na, kv_gq, cf, nf, mf) = _layer_path(xp, mods, layer, 0, None, weights, ctx_mixers)
        c_l.append(cf)
        n_l.append(nf)
        m_l.append(mf)
        xs, _ = _layer_path(xs, mods, layer, 1, dec_seq, weights, lat_mixers)

    return (xp.reshape(batch, seq, D_MODEL), xs.reshape(dec_batch, dec_seq, D_MODEL),
            jnp.swapaxes(kv_na, -1, -2), jnp.swapaxes(kv_gq, -1, -2),
            jnp.stack(c_l, axis=1), jnp.stack(n_l, axis=1), jnp.stack(m_l, axis=1))
```

```python
import functools

import jax
import jax.numpy as jnp
import numpy as np
from jax import lax
from jax.experimental import pallas as pl
from jax.experimental.pallas import tpu as pltpu

D_MODEL = 1024
DEPTH = 4
GRID_W = 64
HEAD_DIM = 64
NA_HEADS = 4
GQA_Q_HEADS = 8
GQA_KV_HEADS = 2
GQA_GROUP = GQA_Q_HEADS // GQA_KV_HEADS
MLSTM_HEADS = 4
NA_WIN_ROWS = 8
NA_WIN_COLS = 16
MLSTM_CHUNK = 64
ROPE_BASE = 10000.0
EPS = 1e-6
NEG = -1e30
NA_W = NA_HEADS * HEAD_DIM
GQA_QW = GQA_Q_HEADS * HEAD_DIM
GQA_KW = GQA_KV_HEADS * HEAD_DIM
ML_W = MLSTM_HEADS * HEAD_DIM
N_GATES = 4 * MLSTM_HEADS
N_SCANS = 2 * MLSTM_HEADS
MLSTM_TILE_KINDS = 2
MLSTM_SEL_ROWS = MLSTM_TILE_KINDS * N_SCANS
FF_HIDDEN = ((8 * D_MODEL + 3 * 256 - 1) // (3 * 256)) * 256
QK_SCALE = HEAD_DIM ** -0.5

ZNA_W = 3 * NA_W
ZGQ_W = GQA_QW + 2 * GQA_KW
ZML_W = 4 * ML_W
OFF_GQ = ZNA_W
OFF_ML = ZNA_W + ZGQ_W
OFF_GATES = OFF_ML + ZML_W

LANES = 128
MXU_DIM = 256
N_MOD_ROWS = 16
NA_BLOCK_ROWS = 2
NA_UNION_ROWS = NA_WIN_ROWS + NA_BLOCK_ROWS - 1

F32 = jnp.float32
BF16 = jnp.bfloat16
VMEM_LIMIT = 52 * 1024 * 1024
TOKEN_TILE = 512
ADALN_TILE = 1536
POST_ROW_GROUPS = 2

NT_DIMS = (((1,), (1,)), ((), ()))


def _params(n_axes):
    return pltpu.CompilerParams(dimension_semantics=("arbitrary",) * n_axes,
                                vmem_limit_bytes=VMEM_LIMIT)


def _rms(x, g):
    return x * lax.rsqrt(jnp.mean(x * x, axis=-1, keepdims=True) + EPS) * g


def _lane_first(shape):
    return lax.broadcasted_iota(jnp.int32, shape, len(shape) - 1) < HEAD_DIM


def _pair_queries(x):
    first = _lane_first(x.shape)
    return jnp.where(first, x, 0.0).astype(BF16), jnp.where(first, 0.0, x).astype(BF16)


def _pair_values(v2):
    first = _lane_first(v2.shape)
    ones, zeros = jnp.ones_like(v2), jnp.zeros_like(v2)
    r0 = jnp.concatenate([jnp.where(first, v2, 0.0), jnp.where(first, ones, zeros)], axis=1)
    r1 = jnp.concatenate([jnp.where(first, 0.0, v2), jnp.where(first, zeros, ones)], axis=1)
    return r0.astype(BF16), r1.astype(BF16)


def _pair_values_t(vt2):
    first = lax.broadcasted_iota(jnp.int32, vt2.shape, 0) < HEAD_DIM
    ones, zeros = jnp.ones_like(vt2), jnp.zeros_like(vt2)
    r0 = jnp.concatenate([jnp.where(first, vt2, 0.0), jnp.where(first, ones, zeros)], axis=0)
    r1 = jnp.concatenate([jnp.where(first, 0.0, vt2), jnp.where(first, zeros, ones)], axis=0)
    return r0.astype(BF16), r1.astype(BF16)


def _pair_scores(q_pair, blocks):
    out = []
    for a in range(2):
        scores = []
        for blk in blocks:
            if blk["k_nt"]:
                s = lax.dot_general(q_pair[a], blk["k"], NT_DIMS, preferred_element_type=F32)
            else:
                s = jnp.dot(q_pair[a], blk["k"], preferred_element_type=F32)
            if blk.get("bias") is not None:
                s = s + blk["bias"][a]
            scores.append(s)
        out.append(scores)
    return out


def _pair_finish(all_scores, blocks):
    acc = None
    for a in range(2):
        scores = all_scores[a]
        m = scores[0].max(axis=-1, keepdims=True)
        for s in scores[1:]:
            m = jnp.maximum(m, s.max(axis=-1, keepdims=True))
        for s, blk in zip(scores, blocks):
            p = jnp.exp(s - m).astype(BF16)
            if blk["r_nt"]:
                term = lax.dot_general(p, blk["r"][a], NT_DIMS, preferred_element_type=F32)
            else:
                term = jnp.dot(p, blk["r"][a], preferred_element_type=F32)
            acc = term if acc is None else acc + term
    return acc[:, :LANES] / acc[:, LANES:]


def _adaln_kernel(c_ref, w_ref, b_ref, o_ref):
    c = c_ref[...]
    a = c * jax.nn.sigmoid(c)
    o_ref[...] = jnp.dot(a.astype(BF16), w_ref[...].astype(BF16),
                         preferred_element_type=F32) + b_ref[...]


def _adaln(cvec, w_ada, b_ada):
    tn = ADALN_TILE
    return pl.pallas_call(
        _adaln_kernel,
        grid=(DEPTH, 6 * D_MODEL // tn),
        in_specs=[
            pl.BlockSpec((N_MOD_ROWS, D_MODEL), lambda l, j: (0, 0)),
            pl.BlockSpec((None, D_MODEL, tn), lambda l, j: (l, 0, j)),
            pl.BlockSpec((None, 1, tn), lambda l, j: (l, 0, j)),
        ],
        out_specs=pl.BlockSpec((None, N_MOD_ROWS, tn), lambda l, j: (l, 0, j)),
        out_shape=jax.ShapeDtypeStruct((DEPTH, N_MOD_ROWS, 6 * D_MODEL), F32),
        compiler_params=_params(2),
        name="adaln",
    )(cvec, w_ada, b_ada.reshape(DEPTH, 1, 6 * D_MODEL))


def _na_r0(r, rows):
    return min(max(r - NA_WIN_ROWS // 2, 0), rows - NA_WIN_ROWS)


def _na_union_start(p, rows):
    return min(_na_r0(NA_BLOCK_ROWS * p, rows), rows - NA_UNION_ROWS)


def _na_bias_kernel(tbl_ref, o_ref, *, rows):
    l = pl.program_id(0)
    h = pl.program_id(1)
    qi = lax.broadcasted_iota(jnp.int32, (GRID_W, GRID_W), 0)
    ki = lax.broadcasted_iota(jnp.int32, (GRID_W, GRID_W), 1)
    dc = jnp.clip(ki - qi, -(NA_WIN_COLS - 1), NA_WIN_COLS - 1) + NA_WIN_COLS - 1
    c0 = jnp.clip(qi - NA_WIN_COLS // 2, 0, GRID_W - NA_WIN_COLS)
    col_ok = (ki >= c0) & (ki < c0 + NA_WIN_COLS)
    n_dr = 2 * NA_WIN_ROWS - 1
    n_dc = 2 * NA_WIN_COLS - 1
    tiles = []
    for dr in range(n_dr):
        t = jnp.zeros((GRID_W, GRID_W), F32)
        for d in range(n_dc):
            t = jnp.where(dc == d, tbl_ref[((l * NA_HEADS + h) * n_dr + dr) * n_dc + d], t)
        tiles.append(jnp.where(col_ok, t, NEG))
    neg_tile = jnp.full((GRID_W, GRID_W), NEG, F32)
    for p in range(rows // NA_BLOCK_ROWS):
        start = _na_union_start(p, rows)
        for a in range(NA_BLOCK_ROWS):
            r = NA_BLOCK_ROWS * p + a
            r0 = _na_r0(r, rows)
            for j in range(NA_UNION_ROWS):
                kr = start + j
                inside = r0 <= kr < r0 + NA_WIN_ROWS
                tile = tiles[kr - r + NA_WIN_ROWS - 1] if inside else neg_tile
                o_ref[p, a * GRID_W:(a + 1) * GRID_W, j * GRID_W:(j + 1) * GRID_W] = tile


def _na_bias_expand(na_bias, rows):
    n_blocks = rows // NA_BLOCK_ROWS
    qn = NA_BLOCK_ROWS * GRID_W
    kn = NA_UNION_ROWS * GRID_W
    return pl.pallas_call(
        functools.partial(_na_bias_kernel, rows=rows),
        grid=(DEPTH, NA_HEADS),
        in_specs=[pl.BlockSpec(memory_space=pltpu.SMEM)],
        out_specs=pl.BlockSpec((None, None, n_blocks, qn, kn), lambda l, h: (l, h, 0, 0, 0)),
        out_shape=jax.ShapeDtypeStruct((DEPTH, NA_HEADS, n_blocks, qn, kn), F32),
        compiler_params=_params(2),
        name="na_bias_expand",
    )(na_bias.reshape(-1))


def _mod_row_map(layer, tiles_per_row, first_row):
    if tiles_per_row is None:
        return lambda i: (layer, first_row, 0, 0)
    return lambda i: (layer, first_row + i // tiles_per_row, 0, 0)


def _inproj_kernel(x_ref, mod_ref, g_ref, wna_ref, wgq_ref, wml_ref, wgc_ref, wgr_ref, bgc_ref, bgr_ref,
                   zna_ref, zgq_ref, zml_ref, gi_ref, gf_ref, grow_ref):
    x = x_ref[...]
    h = _rms(x, g_ref[0:1, :]) * (1.0 + mod_ref[1:2, :]) + mod_ref[0:1, :]
    hb = h.astype(BF16)
    zna_ref[...] = lax.dot_general(hb, wna_ref[...], NT_DIMS, preferred_element_type=F32)
    zgq_ref[...] = lax.dot_general(hb, wgq_ref[...], NT_DIMS, preferred_element_type=F32)
    zml_ref[...] = lax.dot_general(hb, wml_ref[...], NT_DIMS, preferred_element_type=F32)
    gc = jnp.dot(hb, wgc_ref[...], preferred_element_type=F32)
    gi_ref[...] = gc[:, 0:N_SCANS] + bgc_ref[0:1, :]
    gf_ref[...] = gc[:, LANES:LANES + N_SCANS] + bgc_ref[1:2, :]
    grow_ref[...] = lax.dot_general(wgr_ref[...], hb, NT_DIMS, preferred_element_type=F32) + bgr_ref[...]


def _inproj(x, mods, g_norm, w, layer, tiles_per_row, first_row, tm):
    n = x.shape[0]
    wspec = lambda width: pl.BlockSpec((None, width, D_MODEL), lambda i: (layer, 0, 0))
    return pl.pallas_call(
        _inproj_kernel,
        grid=(n // tm,),
        in_specs=[
            pl.BlockSpec((tm, D_MODEL), lambda i: (i, 0)),
            pl.BlockSpec((None, None, 6, D_MODEL), _mod_row_map(layer, tiles_per_row, first_row)),
            pl.BlockSpec((None, 4, D_MODEL), lambda i: (layer, 0, 0)),
            wspec(ZNA_W), wspec(ZGQ_W), wspec(ZML_W),
            pl.BlockSpec((None, D_MODEL, 2 * LANES), lambda i: (layer, 0, 0)),
            pl.BlockSpec((None, N_GATES, D_MODEL), lambda i: (layer, 0, 0)),
            pl.BlockSpec((None, 2, N_SCANS), lambda i: (layer, 0, 0)),
            pl.BlockSpec((None, N_GATES, 1), lambda i: (layer, 0, 0)),
        ],
        out_specs=[
            pl.BlockSpec((tm, ZNA_W), lambda i: (i, 0)),
            pl.BlockSpec((tm, ZGQ_W), lambda i: (i, 0)),
            pl.BlockSpec((tm, ZML_W), lambda i: (i, 0)),
            pl.BlockSpec((tm, N_SCANS), lambda i: (i, 0)),
            pl.BlockSpec((tm, N_SCANS), lambda i: (i, 0)),
            pl.BlockSpec((N_GATES, tm), lambda i: (0, i)),
        ],
        out_shape=[
            jax.ShapeDtypeStruct((n, ZNA_W), F32),
            jax.ShapeDtypeStruct((n, ZGQ_W), F32),
            jax.ShapeDtypeStruct((n, ZML_W), F32),
            jax.ShapeDtypeStruct((n, N_SCANS), F32),
            jax.ShapeDtypeStruct((n, N_SCANS), F32),
            jax.ShapeDtypeStruct((N_GATES, n), F32),
        ],
        compiler_params=_params(1),
        name="inproj",
    )(x, mods, g_norm, w["na"], w["gq"], w["ml"], w["gate_col"], w["gate_row"], w["b_col"], w["b_row"])


def _post_kernel(x_ref, mna_ref, mgq_ref, mml_ref, mod_ref, g_ref, wo_ref, wgu_ref, wd_ref, o_ref):
    rows = x_ref.shape[0] // POST_ROW_GROUPS
    rs = [slice(i * rows, (i + 1) * rows) for i in range(POST_ROW_GROUPS)]
    acc = []
    for r in rs:
        a = jnp.dot(mna_ref[r, :], wo_ref[0:NA_W, :], preferred_element_type=F32)
        a += jnp.dot(mgq_ref[r, :], wo_ref[NA_W:NA_W + GQA_QW, :], preferred_element_type=F32)
        a += jnp.dot(mml_ref[r, :], wo_ref[NA_W + GQA_QW:, :], preferred_element_type=F32)
        acc.append(a)
    x1, gate_up = [], []
    for r, a in zip(rs, acc):
        xr = x_ref[r, :] + mod_ref[2:3, :] * _rms(a, g_ref[1:2, :])
        hb = (_rms(xr, g_ref[2:3, :]) * (1.0 + mod_ref[4:5, :]) + mod_ref[3:4, :]).astype(BF16)
        x1.append(xr)
        gate_up.append((jnp.dot(hb, wgu_ref[:, 0:FF_HIDDEN], preferred_element_type=F32),
                        jnp.dot(hb, wgu_ref[:, FF_HIDDEN:], preferred_element_type=F32)))
    f = []
    for gate, up in gate_up:
        act = (gate * jax.nn.sigmoid(gate) * up).astype(BF16)
        f.append(jnp.dot(act, wd_ref[...], preferred_element_type=F32))
    for r, xr, fr in zip(rs, x1, f):
        o_ref[r, :] = xr + mod_ref[5:6, :] * _rms(fr, g_ref[3:4, :])


def _post(x, mna, mgq, mml, mods, g_norm, w_out, w_gu, w_down, layer, tiles_per_row, first_row, tm):
    n = x.shape[0]
    resident = lambda shape: pl.BlockSpec((None,) + shape, lambda i: (layer, 0, 0), pipeline_mode=pl.Buffered(1))
    return pl.pallas_call(
        _post_kernel,
        grid=(n // tm,),
        in_specs=[
            pl.BlockSpec((tm, D_MODEL), lambda i: (i, 0)),
            pl.BlockSpec((tm, NA_W), lambda i: (i, 0)),
            pl.BlockSpec((tm, GQA_QW), lambda i: (i, 0)),
            pl.BlockSpec((tm, ML_W), lambda i: (i, 0)),
            pl.BlockSpec((None, None, 6, D_MODEL), _mod_row_map(layer, tiles_per_row, first_row)),
            pl.BlockSpec((None, 4, D_MODEL), lambda i: (layer, 0, 0)),
            resident((D_MODEL, D_MODEL)),
            resident((D_MODEL, 2 * FF_HIDDEN)),
            resident((FF_HIDDEN, D_MODEL)),
        ],
        out_specs=pl.BlockSpec((tm, D_MODEL), lambda i: (i, 0)),
        out_shape=jax.ShapeDtypeStruct((n, D_MODEL), F32),
        compiler_params=_params(1),
        name="post",
    )(x, mna, mgq, mml, mods, g_norm, w_out, w_gu, w_down)


def _pair_rms(x, gain, sums_on_mxu=False):
    xsq = x * x
    if sums_on_mxu:
        row = lax.broadcasted_iota(jnp.int32, (LANES, LANES), 0) < HEAD_DIM
        col = lax.broadcasted_iota(jnp.int32, (LANES, LANES), 1) < HEAD_DIM
        ones_blk = jnp.where(row == col, 1.0, 0.0).astype(BF16)
        sums = sum(jnp.dot(p, ones_blk, preferred_element_type=F32) for p in _split3(xsq))
    else:
        first = _lane_first(x.shape)
        sums = jnp.where(first, jnp.sum(jnp.where(first, xsq, 0.0), axis=-1, keepdims=True),
                         jnp.sum(jnp.where(first, 0.0, xsq), axis=-1, keepdims=True))
    return x * lax.rsqrt(sums * (1.0 / HEAD_DIM) + EPS) * gain


def _pair_gain(gqk_ref, row):
    return jnp.concatenate([gqk_ref[row:row + 1, :]] * (LANES // HEAD_DIM), axis=-1)


def _ctx_attn_kernel(zna_ref, zgq_ref, gqk_ref, *rest):
    mna_ref, mgq_ref, kvna_ref, kvgq_ref = rest[-4:]

    def store_t(ref, which, pair, x2):
        xt = x2.T
        for a in range(2):
            ref[which, 2 * pair + a] = xt[a * HEAD_DIM:(a + 1) * HEAD_DIM]

    pending = []
    for i in range(NA_HEADS // 2):
        cols = slice(i * LANES, (i + 1) * LANES)
        q2 = zna_ref[:, cols] * QK_SCALE
        k2 = zna_ref[:, NA_W + i * LANES:NA_W + (i + 1) * LANES]
        v2 = zna_ref[:, 2 * NA_W + i * LANES:2 * NA_W + (i + 1) * LANES]
        store_t(kvna_ref, 0, i, k2)
        store_t(kvna_ref, 1, i, v2)
        blocks = [{"k": k2.astype(BF16), "k_nt": True, "r": _pair_values(v2), "r_nt": False}]
        pending.append((mna_ref, cols, _pair_scores(_pair_queries(q2), blocks), blocks))

    gq, gk = _pair_gain(gqk_ref, 0), _pair_gain(gqk_ref, 1)
    k2 = _pair_rms(zgq_ref[:, GQA_QW:GQA_QW + GQA_KW], gk)
    v2 = zgq_ref[:, GQA_QW + GQA_KW:]
    store_t(kvgq_ref, 0, 0, k2)
    store_t(kvgq_ref, 1, 0, v2)
    blocks = [{"k": k2.astype(BF16), "k_nt": True, "r": _pair_values(v2), "r_nt": False}]
    for p in range(GQA_GROUP):
        cols = slice(p * LANES, (p + 1) * LANES)
        q2 = _pair_rms(zgq_ref[:, cols], gq) * QK_SCALE
        pending.append((mgq_ref, cols, _pair_scores(_pair_queries(q2), blocks), blocks))

    for ref, cols, scores, blocks in pending:
        ref[:, cols] = _pair_finish(scores, blocks).astype(BF16)


def _ctx_attn(zna, zgq, g_qk, kv_na_buf, kv_gq_buf, layer, batch, t):
    n = zna.shape[0]
    return pl.pallas_call(
        _ctx_attn_kernel,
        grid=(batch,),
        in_specs=[
            pl.BlockSpec((t, ZNA_W), lambda b: (b, 0)),
            pl.BlockSpec((t, ZGQ_W), lambda b: (b, 0)),
            pl.BlockSpec((None, 2, HEAD_DIM), lambda b: (layer, 0, 0)),
            pl.BlockSpec(memory_space=pl.ANY),
            pl.BlockSpec(memory_space=pl.ANY),
        ],
        out_specs=[
            pl.BlockSpec((t, NA_W), lambda b: (b, 0)),
            pl.BlockSpec((t, GQA_QW), lambda b: (b, 0)),
            pl.BlockSpec((None, None, 2, NA_HEADS, HEAD_DIM, t), lambda b: (b, layer, 0, 0, 0, 0)),
            pl.BlockSpec((None, None, 2, GQA_KV_HEADS, HEAD_DIM, t), lambda b: (b, layer, 0, 0, 0, 0)),
        ],
        out_shape=[
            jax.ShapeDtypeStruct((n, NA_W), BF16),
            jax.ShapeDtypeStruct((n, GQA_QW), BF16),
            jax.ShapeDtypeStruct((batch, DEPTH, 2, NA_HEADS, HEAD_DIM, t), F32),
            jax.ShapeDtypeStruct((batch, DEPTH, 2, GQA_KV_HEADS, HEAD_DIM, t), F32),
        ],
        input_output_aliases={3: 2, 4: 3},
        compiler_params=_params(1),
        name="ctx_attn",
    )(zna, zgq, g_qk, kv_na_buf, kv_gq_buf)


def _lat_na_kernel(zna_ref, cache_ref, bias_ref, o_ref, *, rows):
    qn = NA_BLOCK_ROWS * GRID_W
    kn = NA_UNION_ROWS * GRID_W
    n_blocks = rows // NA_BLOCK_ROWS
    prepared = []
    for i in range(NA_HEADS // 2):
        cols = slice(i * LANES, (i + 1) * LANES)
        q_pair = _pair_queries(zna_ref[:, cols] * QK_SCALE)
        k2 = zna_ref[:, NA_W + i * LANES:NA_W + (i + 1) * LANES].astype(BF16)
        kc = cache_ref[0, i].astype(BF16)
        scores = []
        for a in range(2):
            s_ctx = jnp.dot(q_pair[a], kc, preferred_element_type=F32)
            s_win = []
            for p in range(n_blocks):
                k0 = _na_union_start(p, rows) * GRID_W
                s_win.append(lax.dot_general(q_pair[a][p * qn:(p + 1) * qn], k2[k0:k0 + kn], NT_DIMS,
                                             preferred_element_type=F32) + bias_ref[2 * i + a, p])
            scores.append((s_ctx, s_win))
        prepared.append(scores)
    for i in range(NA_HEADS // 2):
        cols = slice(i * LANES, (i + 1) * LANES)
        r = _pair_values(zna_ref[:, 2 * NA_W + i * LANES:2 * NA_W + (i + 1) * LANES])
        rc = _pair_values_t(cache_ref[1, i])
        acc = None
        for a in range(2):
            s_ctx, s_win = prepared[i][a]
            m_win = jnp.concatenate([s.max(axis=-1, keepdims=True) for s in s_win], axis=0)
            m = jnp.maximum(m_win, s_ctx.max(axis=-1, keepdims=True))
            term = lax.dot_general(jnp.exp(s_ctx - m).astype(BF16), rc[a], NT_DIMS, preferred_element_type=F32)
            wins = []
            for p in range(n_blocks):
                k0 = _na_union_start(p, rows) * GRID_W
                pw = jnp.exp(s_win[p] - m[p * qn:(p + 1) * qn]).astype(BF16)
                wins.append(jnp.dot(pw, r[a][k0:k0 + kn], preferred_element_type=F32))
            term = term + jnp.concatenate(wins, axis=0)
            acc = term if acc is None else acc + term
        o_ref[:, cols] = (acc[:, :LANES] / acc[:, LANES:]).astype(BF16)


def _lat_na(zna, cache_t, bias, layer, batch, t):
    n = zna.shape[0]
    rows = t // GRID_W
    past = cache_t.shape[-1]
    n_blocks = rows // NA_BLOCK_ROWS
    qn = NA_BLOCK_ROWS * GRID_W
    kn = NA_UNION_ROWS * GRID_W
    return pl.pallas_call(
        functools.partial(_lat_na_kernel, rows=rows),
        grid=(batch,),
        in_specs=[
            pl.BlockSpec((t, ZNA_W), lambda b: (b, 0)),
            pl.BlockSpec((None, None, 2, NA_HEADS // 2, LANES, past), lambda b: (b, layer, 0, 0, 0, 0)),
            pl.BlockSpec((None, NA_HEADS, n_blocks, qn, kn), lambda b: (layer, 0, 0, 0, 0)),
        ],
        out_specs=pl.BlockSpec((t, NA_W), lambda b: (b, 0)),
        out_shape=jax.ShapeDtypeStruct((n, NA_W), BF16),
        compiler_params=_params(1),
        name="lat_na",
    )(zna, cache_t, bias)


def _rope_tables(t):
    half = HEAD_DIM // 2
    quarter = half // 2
    inv = 1.0 / (ROPE_BASE ** (jnp.arange(quarter, dtype=F32) / quarter))
    tt = jnp.arange(t)
    row = (tt // GRID_W).astype(F32)
    col = (tt % GRID_W).astype(F32)
    ang_r = row[:, None] * inv[None, :]
    ang_c = col[:, None] * inv[None, :]
    cos = jnp.concatenate([jnp.cos(ang_r)] * 2 + [jnp.cos(ang_c)] * 2, axis=-1)
    sin = jnp.concatenate([-jnp.sin(ang_r), jnp.sin(ang_r), -jnp.sin(ang_c), jnp.sin(ang_c)], axis=-1)
    reps = LANES // HEAD_DIM
    return jnp.tile(cos, (1, reps)), jnp.tile(sin, (1, reps))


def _pair_rope(xn, cos, sin):
    quarter = HEAD_DIM // 4
    lane = lax.broadcasted_iota(jnp.int32, xn.shape, 1)
    lower = (lane & (2 * quarter - 1)) < quarter
    partner = jnp.where(lower, pltpu.roll(xn, LANES - quarter, 1), pltpu.roll(xn, quarter, 1))
    return xn * cos + partner * sin


def _lat_gqa_kernel(zgq_ref, cache_ref, gqk_ref, cos_ref, sin_ref, o_ref):
    cos = cos_ref[...]
    sin = sin_ref[...]
    gq, gk = _pair_gain(gqk_ref, 0), _pair_gain(gqk_ref, 1)
    keys = _pair_rope(_pair_rms(zgq_ref[:, GQA_QW:GQA_QW + GQA_KW], gk, True), cos, sin).astype(BF16)
    blocks = [
        {"k": keys, "k_nt": True, "r": _pair_values(zgq_ref[:, GQA_QW + GQA_KW:]), "r_nt": False},
        {"k": cache_ref[0].astype(BF16), "k_nt": False, "r": _pair_values_t(cache_ref[1]), "r_nt": True},
    ]
    def scores(p):
        x = _pair_rope(_pair_rms(zgq_ref[:, p * LANES:(p + 1) * LANES], gq, True), cos, sin) * QK_SCALE
        return _pair_scores(_pair_queries(x), blocks)

    nxt = scores(0)
    for p in range(GQA_GROUP):
        cur = nxt
        if p + 1 < GQA_GROUP:
            nxt = scores(p + 1)
        o_ref[:, p * LANES:(p + 1) * LANES] = _pair_finish(cur, blocks).astype(BF16)


def _lat_gqa(zgq, cache_t, g_qk, cos, sin, layer, batch, t):
    n = zgq.shape[0]
    past = cache_t.shape[-1]
    return pl.pallas_call(
        _lat_gqa_kernel,
        grid=(batch,),
        in_specs=[
            pl.BlockSpec((t, ZGQ_W), lambda b: (b, 0)),
            pl.BlockSpec((None, None, 2, GQA_KW, past), lambda b: (b, layer, 0, 0, 0)),
            pl.BlockSpec((None, 2, HEAD_DIM), lambda b: (layer, 0, 0)),
            pl.BlockSpec((t, LANES), lambda b: (0, 0)),
            pl.BlockSpec((t, LANES), lambda b: (0, 0)),
        ],
        out_specs=pl.BlockSpec((t, GQA_QW), lambda b: (b, 0)),
        out_shape=jax.ShapeDtypeStruct((n, GQA_QW), BF16),
        compiler_params=_params(1),
        name="lat_gqa",
    )(zgq, cache_t, g_qk, cos, sin)


def _split3(x):
    x1 = x.astype(BF16)
    r1 = x - x1.astype(F32)
    x2 = r1.astype(BF16)
    x3 = (r1 - x2.astype(F32)).astype(BF16)
    return x1, x2, x3


def _log_sigmoid(x):
    return jnp.minimum(x, 0.0) - jnp.log1p(jnp.exp(-jnp.abs(x)))


def _mlstm_select_matrix():
    H = MLSTM_HEADS
    sel = np.zeros((MLSTM_SEL_ROWS, (H // 2) * 2 * MLSTM_TILE_KINDS * LANES), np.float32)
    for j in range(H // 2):
        for d in range(2):
            for q in range(MLSTM_TILE_KINDS):
                for a in range(2):
                    col0 = ((j * 2 + d) * MLSTM_TILE_KINDS + q) * LANES + a * HEAD_DIM
                    sel[q * N_SCANS + d * H + 2 * j + a, col0:col0 + HEAD_DIM] = 1.0
    return sel


def _mlstm_kernel(zml_ref, gi_ref, gf_ref, grow_ref, c0_ref, n0_ref, m0_ref, gml_ref, sel_ref, *rest,
                  t, emit_state):
    if emit_state:
        o_ref, cf_ref, nf_ref, mf_ref = rest[:4]
    else:
        o_ref = rest[0]
    tri_s, st_s, dst_s, cst_s, row_s = rest[-5:]
    L = MLSTM_CHUNK
    H = MLSTM_HEADS
    HD = HEAD_DIM
    NP = H // 2
    nc = t // L
    tb = tri_s.shape[-1]

    @pl.when(pl.program_id(0) == 0)
    def _():
        ti = lax.broadcasted_iota(jnp.int32, (tb, tb), 0)
        ui = lax.broadcasted_iota(jnp.int32, (tb, tb), 1)
        same = (ti & -L) == (ui & -L)
        tri_s[0] = jnp.where(same & (ui <= ti), 1.0, 0.0).astype(BF16)
        tri_s[1] = jnp.where(same & (ui >= ti), 1.0, 0.0).astype(BF16)

    lower, upper = tri_s[0], tri_s[1]

    def chunk_sums_cols(x):
        parts = _split3(x)
        pre, suf = [], []
        for i in range(t // tb):
            blk = [p[i * tb:(i + 1) * tb] for p in parts]
            pre.append(sum(jnp.dot(lower, p, preferred_element_type=F32) for p in blk))
            suf.append(sum(jnp.dot(upper, p, preferred_element_type=F32) for p in blk))
        return jnp.concatenate(pre, axis=0), jnp.concatenate(suf, axis=0)

    def chunk_sums_rows(x):
        parts = _split3(x)
        pre, suf = [], []
        for i in range(t // tb):
            blk = [p[:, i * tb:(i + 1) * tb] for p in parts]
            pre.append(sum(jnp.dot(p, upper, preferred_element_type=F32) for p in blk))
            suf.append(sum(jnp.dot(p, lower, preferred_element_type=F32) for p in blk))
        return jnp.concatenate(pre, axis=1), jnp.concatenate(suf, axis=1)

    pre_c, suf_c = chunk_sums_cols(_log_sigmoid(gf_ref[...]))
    lane_c = lax.broadcasted_iota(jnp.int32, (t, N_SCANS), 1)
    b3 = jnp.where(lane_c < H, pre_c, suf_c).reshape(nc, L, N_SCANS)
    i3 = gi_ref[...].reshape(nc, L, N_SCANS)
    fwd3 = lax.broadcasted_iota(jnp.int32, (nc, 1, N_SCANS), 2) < H
    b_end3 = jnp.where(fwd3, b3[:, L - 1:L, :], b3[:, 0:1, :])
    lw_end3 = b_end3 - b3 + i3
    a3 = jnp.max(lw_end3, axis=1, keepdims=True)
    wloc3 = jnp.exp(lw_end3 - a3)

    fwd1 = lax.broadcasted_iota(jnp.int32, (1, N_SCANS), 1) < H
    m = m0_ref[...]
    m_start, carry_decay, contrib_scale = [], [], []
    for j in range(nc):
        a_j = jnp.where(fwd1, a3[j], a3[nc - 1 - j])
        g_j = jnp.where(fwd1, b_end3[j], b_end3[nc - 1 - j])
        m_start.append(m)
        m_next = jnp.maximum(g_j + m, a_j)
        carry_decay.append(jnp.exp(g_j + m - m_next))
        contrib_scale.append(jnp.exp(a_j - m_next))
        m = m_next
    mst3 = jnp.concatenate([jnp.where(fwd1, m_start[c], m_start[nc - 1 - c])[None] for c in range(nc)], axis=0)

    cols = jnp.concatenate([b3.reshape(t, N_SCANS), wloc3.reshape(t, N_SCANS)], axis=1)
    tiles_all = sum(jnp.dot(p, sel_ref[...], preferred_element_type=F32) for p in _split3(cols))
    tile_w = MLSTM_TILE_KINDS * LANES

    def tiles(j, d):
        x = tiles_all[:, (2 * j + d) * tile_w:(2 * j + d + 1) * tile_w]
        return [x[:, q * LANES:(q + 1) * LANES].reshape(nc, L, LANES) for q in range(MLSTM_TILE_KINDS)]

    gr = grow_ref[...]
    pre_r, suf_r = chunk_sums_rows(_log_sigmoid(gr))
    sub_r = lax.broadcasted_iota(jnp.int32, (N_SCANS, t), 0)
    rowv = gr[0:N_SCANS] - jnp.where(sub_r < H, pre_r[N_SCANS:], suf_r[N_SCANS:])
    for j in range(NP):
        for d in range(2):
            e = d * H + 2 * j
            for c in range(nc):
                row_s[2 * j + d, c] = jnp.concatenate(
                    [rowv[e:e + 1, c * L:(c + 1) * L], rowv[e + 1:e + 2, c * L:(c + 1) * L]], axis=1)

    lane_a = lax.broadcasted_iota(jnp.int32, (1, 1, LANES), 2) < HD
    sub_a = lax.broadcasted_iota(jnp.int32, (1, 2 * HD, 1), 1) < HD
    diag = sub_a == lane_a
    diag4 = jnp.concatenate([diag] * 4, axis=2)

    def stack_heads(x3):
        return jnp.concatenate([jnp.where(lane_a, x3, 0.0), jnp.where(lane_a, 0.0, x3)], axis=1)

    def pair_cols(base, j):
        return slice(base + j * LANES, base + (j + 1) * LANES)

    zero_blk = jnp.zeros((HD, HD), F32)
    for j in range(NP):
        cols_d = []
        for d in range(2):
            ca, cb = c0_ref[d, 2 * j].T, c0_ref[d, 2 * j + 1].T
            na = jnp.broadcast_to(n0_ref[d, 2 * j:2 * j + 1, :], (HD, HD)).T
            nb = jnp.broadcast_to(n0_ref[d, 2 * j + 1:2 * j + 2, :], (HD, HD)).T
            top = jnp.concatenate([ca, zero_blk, na, zero_blk], axis=1)
            bot = jnp.concatenate([zero_blk, cb, zero_blk, nb], axis=1)
            cols_d.append(jnp.concatenate([top, bot], axis=0))
        st_s[j] = jnp.concatenate(cols_d, axis=1)

    for j in range(NP):
        k3 = (zml_ref[:, pair_cols(ML_W, j)] * QK_SCALE).reshape(nc, L, LANES).astype(BF16)
        v3 = zml_ref[:, pair_cols(2 * ML_W, j)].reshape(nc, L, LANES)
        rhs = []
        for d in range(2):
            wl = tiles(j, d)[1]
            rhs += [v3 * wl, wl]
        rhs = jnp.concatenate(rhs, axis=2).astype(BF16)
        contrib = jnp.einsum("csk,csn->ckn", k3, rhs, preferred_element_type=F32)
        dst_s[j] = jnp.where(diag4, contrib, 0.0)

    def lane_scale(v, j):
        pieces = []
        for d in range(2):
            sa = jnp.broadcast_to(v[:, d * H + 2 * j:d * H + 2 * j + 1], (1, HD))
            sb = jnp.broadcast_to(v[:, d * H + 2 * j + 1:d * H + 2 * j + 2], (1, HD))
            pieces += [sa, sb, sa, sb]
        return jnp.concatenate(pieces, axis=1)

    for j in range(NP):
        st = st_s[j]
        for step in range(nc):
            cb = nc - 1 - step
            stb = st.astype(BF16)
            cst_s[j, step, :, 0:2 * LANES] = stb[:, 0:2 * LANES]
            cst_s[j, cb, :, 2 * LANES:] = stb[:, 2 * LANES:]
            delta = jnp.concatenate([dst_s[j, step, :, 0:2 * LANES], dst_s[j, cb, :, 2 * LANES:]], axis=1)
            st = lane_scale(carry_decay[step], j) * st + lane_scale(contrib_scale[step], j) * delta
        st_s[j] = st

    sidx = lax.broadcasted_iota(jnp.int32, (1, L, LANES), 2) & (HD - 1)
    tidx = lax.broadcasted_iota(jnp.int32, (1, L, LANES), 1)
    masks = (sidx <= tidx, sidx >= tidx)
    ones_blk = jnp.broadcast_to(jnp.where(diag, 1.0, 0.0).astype(BF16), (nc, 2 * HD, LANES))
    neg_inf = -jnp.inf
    for j in range(NP):
        q3 = zml_ref[:, pair_cols(0, j)].reshape(nc, L, LANES).astype(BF16)
        k3 = (zml_ref[:, pair_cols(ML_W, j)] * QK_SCALE).reshape(nc, L, LANES)
        v3 = zml_ref[:, pair_cols(2 * ML_W, j)].reshape(nc, L, LANES)
        qk = jnp.einsum("ctd,cnd->ctn", q3, stack_heads(k3).astype(BF16), preferred_element_type=F32)
        v_aug = jnp.concatenate([stack_heads(v3).astype(BF16), ones_blk], axis=2)
        out = None
        for d in range(2):
            b_t = tiles(j, d)[0]
            e = d * H + 2 * j
            bm_t = b_t + jnp.where(lane_a, mst3[:, :, e:e + 1], mst3[:, :, e + 1:e + 2])
            logw = jnp.where(masks[d], b_t + row_s[2 * j + d], neg_inf)
            rmax_a = jnp.max(jnp.where(lane_a, logw, neg_inf), axis=-1, keepdims=True)
            rmax_b = jnp.max(jnp.where(lane_a, neg_inf, logw), axis=-1, keepdims=True)
            m_t = jnp.maximum(jnp.where(lane_a, rmax_a, rmax_b), bm_t)
            s = qk * jnp.exp(logw - m_t)
            decay = jnp.exp(bm_t - m_t)
            sv = jnp.einsum("cts,csn->ctn", s.astype(BF16), v_aug, preferred_element_type=F32)
            state = cst_s[j, :, :, 2 * d * LANES:2 * (d + 1) * LANES]
            inter = jnp.einsum("ctk,ckn->ctn", q3, state, preferred_element_type=F32)
            num = sv[:, :, 0:LANES] + decay * inter[:, :, 0:LANES]
            den = sv[:, :, LANES:] + decay * inter[:, :, LANES:]
            h_d = num / jnp.maximum(jnp.abs(den), jnp.exp(-m_t))
            out = h_d if out is None else out + h_d
        cols = pair_cols(0, j)
        og = jax.nn.sigmoid(zml_ref[:, pair_cols(3 * ML_W, j)])
        o_ref[:, cols] = (_pair_rms(out.reshape(t, LANES), gml_ref[:, cols]) * og).astype(BF16)

    if emit_state:
        for j in range(NP):
            st = st_s[j]
            for d in range(2):
                for a in range(2):
                    rows = slice(a * HD, (a + 1) * HD)
                    c0 = 2 * d * LANES + a * HD
                    cf_ref[d, 2 * j + a] = st[rows, c0:c0 + HD].T
                    nf_ref[d, 2 * j + a:2 * j + a + 1, :] = st[rows, c0 + LANES:c0 + LANES + HD].T[0:1, :]
        mf_ref[...] = m


def _mlstm(zml, gi, gf, grow, c0, n0, m0, g_ml, layer, batch, t, emit_state, state_layer):
    n = zml.shape[0]
    H = MLSTM_HEADS
    L = MLSTM_CHUNK
    nc = t // L
    sel = jnp.asarray(_mlstm_select_matrix(), BF16)
    tri_block = min(t, MXU_DIM)
    assert L & (L - 1) == 0 and tri_block % L == 0 and t % tri_block == 0
    if state_layer is None:
        c_spec = pl.BlockSpec((None, 2, H, HEAD_DIM, HEAD_DIM), lambda b: (0, 0, 0, 0, 0))
        n_spec = pl.BlockSpec((None, 2, H, HEAD_DIM), lambda b: (0, 0, 0, 0))
        m_spec = pl.BlockSpec((None, 1, N_SCANS), lambda b: (0, 0, 0))
    else:
        c_spec = pl.BlockSpec((None, None, 2, H, HEAD_DIM, HEAD_DIM), lambda b: (b, state_layer, 0, 0, 0, 0))
        n_spec = pl.BlockSpec((None, None, 2, H, HEAD_DIM), lambda b: (b, state_layer, 0, 0, 0))
        m_spec = pl.BlockSpec((None, None, 1, N_SCANS), lambda b: (b, state_layer, 0, 0))
    out_specs = [pl.BlockSpec((t, ML_W), lambda b: (b, 0))]
    out_shape = [jax.ShapeDtypeStruct((n, ML_W), BF16)]
    if emit_state:
        out_specs += [
            pl.BlockSpec((None, 2, H, HEAD_DIM, HEAD_DIM), lambda b: (b, 0, 0, 0, 0)),
            pl.BlockSpec((None, 2, H, HEAD_DIM), lambda b: (b, 0, 0, 0)),
            pl.BlockSpec((None, 1, N_SCANS), lambda b: (b, 0, 0)),
        ]
        out_shape += [
            jax.ShapeDtypeStruct((batch, 2, H, HEAD_DIM, HEAD_DIM), F32),
            jax.ShapeDtypeStruct((batch, 2, H, HEAD_DIM), F32),
            jax.ShapeDtypeStruct((batch, 1, N_SCANS), F32),
        ]
    return pl.pallas_call(
        functools.partial(_mlstm_kernel, t=t, emit_state=emit_state),
        grid=(batch,),
        in_specs=[
            pl.BlockSpec((t, ZML_W), lambda b: (b, 0)),
            pl.BlockSpec((t, N_SCANS), lambda b: (b, 0)),
            pl.BlockSpec((t, N_SCANS), lambda b: (b, 0)),
            pl.BlockSpec((N_GATES, t), lambda b: (0, b)),
            c_spec, n_spec, m_spec,
            pl.BlockSpec((None, 1, ML_W), lambda b: (layer, 0, 0)),
            pl.BlockSpec(sel.shape, lambda b: (0, 0)),
        ],
        out_specs=out_specs,
        out_shape=out_shape,
        scratch_shapes=[
            pltpu.VMEM((2, tri_block, tri_block), BF16),
            pltpu.VMEM((H // 2, 2 * HEAD_DIM, 4 * LANES), F32),
            pltpu.VMEM((H // 2, nc, 2 * HEAD_DIM, 4 * LANES), F32),
            pltpu.VMEM((H // 2, nc, 2 * HEAD_DIM, 4 * LANES), BF16),
            pltpu.VMEM((H, nc, 1, LANES), F32),
        ],
        compiler_params=_params(1),
        name="mlstm",
    )(zml, gi, gf, grow, c0, n0, m0, g_ml, sel)


def _layer_path(x, mods, layer, first_row, tiles_row_tokens, weights, mixers):
    per_row = None if tiles_row_tokens is None else tiles_row_tokens // TOKEN_TILE
    g_norm = weights["g_norm"]
    zna, zgq, zml, gi, gf, grow = _inproj(x, mods, g_norm, weights["w_in"], layer, per_row, first_row, TOKEN_TILE)
    mna, mgq, mml, extra = mixers(zna, zgq, zml, gi, gf, grow)
    x = _post(x, mna, mgq, mml, mods, g_norm, weights["w_out"], weights["w_gu"], weights["w_down"], layer,
              per_row, first_row, TOKEN_TILE)
    return x, extra


def _gqa_pair_order():
    return [a * GQA_GROUP + p for p in range(GQA_GROUP) for a in range(GQA_KV_HEADS)]


def _take_blocks(x, axis, base, width, order):
    return jnp.concatenate([lax.slice_in_dim(x, base + width * o, base + width * (o + 1), axis=axis)
                            for o in order], axis=axis)


def kernel(x_prompt, x_sample, cache_na_kv, cache_gqa_kv, state_mlstm_C, state_mlstm_n, state_mlstm_m,
           c, c_ctx, w_in, b_gates, w_out, g_norm, g_qk, g_mlstm, na_bias, w_ada, b_ada, w_gu, w_down):
    batch, seq, _ = x_prompt.shape
    dec_batch, dec_seq, _ = x_sample.shape
    past = cache_na_kv.shape[-2]
    assert dec_batch + 1 <= N_MOD_ROWS and dec_seq % GRID_W == 0 and GQA_KV_HEADS == 2

    cvec = jnp.concatenate([c_ctx[None, :], c, jnp.zeros((N_MOD_ROWS - 1 - dec_batch, D_MODEL), F32)], axis=0)
    mods = _adaln(cvec, w_ada, b_ada).reshape(DEPTH, N_MOD_ROWS, 6, D_MODEL)
    bias = _na_bias_expand(na_bias, dec_seq // GRID_W)
    cos, sin = _rope_tables(dec_seq)

    pair_order = _gqa_pair_order()
    scan_order = [2 * d for d in range(2)]
    H = MLSTM_HEADS
    w_in_t = jnp.swapaxes(w_in, 1, 2)
    w_gi = _take_blocks(w_in_t, 1, OFF_GATES, H, scan_order)
    w_gf = _take_blocks(w_in_t, 1, OFF_GATES + H, H, scan_order)
    b_gi = _take_blocks(b_gates, 1, 0, H, scan_order)
    b_gf = _take_blocks(b_gates, 1, H, H, scan_order)
    lane_pad = lambda a: jnp.pad(jnp.swapaxes(a, 1, 2), ((0, 0), (0, 0), (0, LANES - N_SCANS)))
    w_gq = jnp.concatenate([_take_blocks(w_in_t, 1, OFF_GQ, HEAD_DIM, pair_order),
                            w_in_t[:, OFF_GQ + GQA_QW:OFF_ML]], axis=1)
    w_out_rows = jnp.concatenate([w_out[:, :NA_W], _take_blocks(w_out, 1, NA_W, HEAD_DIM, pair_order),
                                  w_out[:, NA_W + GQA_QW:]], axis=1)
    weights = {
        "g_norm": g_norm,
        "w_in": {
            "na": w_in_t[:, :OFF_GQ].astype(BF16),
            "gq": w_gq.astype(BF16),
            "ml": w_in_t[:, OFF_ML:OFF_GATES].astype(BF16),
            "gate_col": jnp.concatenate([lane_pad(w_gi), lane_pad(w_gf)], axis=-1).astype(BF16),
            "gate_row": jnp.concatenate([w_gi, w_gf], axis=1).astype(BF16),
            "b_col": jnp.stack([b_gi, b_gf], axis=1),
            "b_row": jnp.concatenate([b_gi, b_gf], axis=-1)[:, :, None],
        },
        "w_out": w_out_rows.astype(BF16),
        "w_gu": w_gu.astype(BF16),
        "w_down": w_down.astype(BF16),
    }
    g_ml = g_mlstm.reshape(DEPTH, 1, ML_W)
    zero_c = jnp.zeros((1, 2, MLSTM_HEADS, HEAD_DIM, HEAD_DIM), F32)
    zero_n = jnp.zeros((1, 2, MLSTM_HEADS, HEAD_DIM), F32)
    zero_m = jnp.zeros((1, 1, N_SCANS), F32)
    m0_lat = state_mlstm_m.reshape(dec_batch, DEPTH, 1, N_SCANS)
    cache_na_t = jnp.swapaxes(cache_na_kv, -1, -2).reshape(dec_batch, DEPTH, 2, NA_HEADS // 2, LANES, past)
    cache_gq_t = jnp.swapaxes(cache_gqa_kv, -1, -2).reshape(dec_batch, DEPTH, 2, GQA_KW, past)

    xp = x_prompt.reshape(batch * seq, D_MODEL)
    xs = x_sample.reshape(dec_batch * dec_seq, D_MODEL)
    kv_na = jnp.zeros((batch, DEPTH, 2, NA_HEADS, HEAD_DIM, seq), F32)
    kv_gq = jnp.zeros((batch, DEPTH, 2, GQA_KV_HEADS, HEAD_DIM, seq), F32)
    c_l, n_l, m_l = [], [], []
    for layer in range(DEPTH):
        def ctx_mixers(zna, zgq, zml, gi, gf, grow, layer=layer, kv_na=kv_na, kv_gq=kv_gq):
            mna, mgq, kv_na, kv_gq = _ctx_attn(zna, zgq, g_qk, kv_na, kv_gq, layer, batch, seq)
            mml, cf, nf, mf = _mlstm(zml, gi, gf, grow, zero_c, zero_n, zero_m, g_ml, layer, batch, seq,
                                     True, None)
            return mna, mgq, mml, (kv_na, kv_gq, cf, nf, mf.reshape(batch, 2, MLSTM_HEADS))

        def lat_mixers(zna, zgq, zml, gi, gf, grow, layer=layer):
            mna = _lat_na(zna, cache_na_t, bias, layer, dec_batch, dec_seq)
            mgq = _lat_gqa(zgq, cache_gq_t, g_qk, cos, sin, layer, dec_batch, dec_seq)
            (mml,) = _mlstm(zml, gi, gf, grow, state_mlstm_C, state_mlstm_n, m0_lat, g_ml, layer,
                            dec_batch, dec_seq, False, layer)
            return mna, mgq, mml, None

        xp, (kv_na, kv_gq, cf, nf, mf) = _layer_path(xp, mods, layer, 0, None, weights, ctx_mixers)
        c_l.append(cf)
        n_l.append(nf)
        m_l.append(mf)
        xs, _ = _layer_path(xs, mods, layer, 1, dec_seq, weights, lat_mixers)

    return (xp.reshape(batch, seq, D_MODEL), xs.reshape(dec_batch, dec_seq, D_MODEL),
            jnp.swapaxes(kv_na, -1, -2), jnp.swapaxes(kv_gq, -1, -2),
            jnp.stack(c_l, axis=1), jnp.stack(n_l, axis=1), jnp.stack(m_l, axis=1))
```

```python
import functools

import jax
import jax.numpy as jnp
import numpy as np
from jax import lax
from jax.experimental import pallas as pl
from jax.experimental.pallas import tpu as pltpu

D_MODEL = 1024
DEPTH = 4
GRID_W = 64
HEAD_DIM = 64
NA_HEADS = 4
GQA_Q_HEADS = 8
GQA_KV_HEADS = 2
GQA_GROUP = GQA_Q_HEADS // GQA_KV_HEADS
MLSTM_HEADS = 4
NA_WIN_ROWS = 8
NA_WIN_COLS = 16
MLSTM_CHUNK = 64
ROPE_BASE = 10000.0
EPS = 1e-6
NEG = -1e30
NA_W = NA_HEADS * HEAD_DIM
GQA_QW = GQA_Q_HEADS * HEAD_DIM
GQA_KW = GQA_KV_HEADS * HEAD_DIM
ML_W = MLSTM_HEADS * HEAD_DIM
N_GATES = 4 * MLSTM_HEADS
N_SCANS = 2 * MLSTM_HEADS
MLSTM_TILE_KINDS = 2
MLSTM_SEL_ROWS = MLSTM_TILE_KINDS * N_SCANS
FF_HIDDEN = ((8 * D_MODEL + 3 * 256 - 1) // (3 * 256)) * 256
QK_SCALE = HEAD_DIM ** -0.5

ZNA_W = 3 * NA_W
ZGQ_W = GQA_QW + 2 * GQA_KW
ZML_W = 4 * ML_W
OFF_GQ = ZNA_W
OFF_ML = ZNA_W + ZGQ_W
OFF_GATES = OFF_ML + ZML_W

LANES = 128
MXU_DIM = 256
N_MOD_ROWS = 16
NA_BLOCK_ROWS = 2
NA_UNION_ROWS = NA_WIN_ROWS + NA_BLOCK_ROWS - 1

F32 = jnp.float32
BF16 = jnp.bfloat16
VMEM_LIMIT = 52 * 1024 * 1024
TOKEN_TILE = 512
ADALN_TILE = 1536
POST_ROW_GROUPS = 2

NT_DIMS = (((1,), (1,)), ((), ()))


def _params(n_axes):
    return pltpu.CompilerParams(dimension_semantics=("arbitrary",) * n_axes,
                                vmem_limit_bytes=VMEM_LIMIT)


def _rms(x, g):
    return x * lax.rsqrt(jnp.mean(x * x, axis=-1, keepdims=True) + EPS) * g


def _lane_first(shape):
    return lax.broadcasted_iota(jnp.int32, shape, len(shape) - 1) < HEAD_DIM


def _pair_queries(x):
    first = _lane_first(x.shape)
    return jnp.where(first, x, 0.0).astype(BF16), jnp.where(first, 0.0, x).astype(BF16)


def _pair_values(v2):
    first = _lane_first(v2.shape)
    ones, zeros = jnp.ones_like(v2), jnp.zeros_like(v2)
    r0 = jnp.concatenate([jnp.where(first, v2, 0.0), jnp.where(first, ones, zeros)], axis=1)
    r1 = jnp.concatenate([jnp.where(first, 0.0, v2), jnp.where(first, zeros, ones)], axis=1)
    return r0.astype(BF16), r1.astype(BF16)


def _pair_values_t(vt2):
    first = lax.broadcasted_iota(jnp.int32, vt2.shape, 0) < HEAD_DIM
    ones, zeros = jnp.ones_like(vt2), jnp.zeros_like(vt2)
    r0 = jnp.concatenate([jnp.where(first, vt2, 0.0), jnp.where(first, ones, zeros)], axis=0)
    r1 = jnp.concatenate([jnp.where(first, 0.0, vt2), jnp.where(first, zeros, ones)], axis=0)
    return r0.astype(BF16), r1.astype(BF16)


def _pair_scores(q_pair, blocks):
    out = []
    for a in range(2):
        scores = []
        for blk in blocks:
            if blk["k_nt"]:
                s = lax.dot_general(q_pair[a], blk["k"], NT_DIMS, preferred_element_type=F32)
            else:
                s = jnp.dot(q_pair[a], blk["k"], preferred_element_type=F32)
            if blk.get("bias") is not None:
                s = s + blk["bias"][a]
            scores.append(s)
        out.append(scores)
    return out


def _pair_finish(all_scores, blocks):
    acc = None
    for a in range(2):
        scores = all_scores[a]
        m = scores[0].max(axis=-1, keepdims=True)
        for s in scores[1:]:
            m = jnp.maximum(m, s.max(axis=-1, keepdims=True))
        for s, blk in zip(scores, blocks):
            p = jnp.exp(s - m).astype(BF16)
            if blk["r_nt"]:
                term = lax.dot_general(p, blk["r"][a], NT_DIMS, preferred_element_type=F32)
            else:
                term = jnp.dot(p, blk["r"][a], preferred_element_type=F32)
            acc = term if acc is None else acc + term
    return acc[:, :LANES] / acc[:, LANES:]


def _adaln_kernel(c_ref, w_ref, b_ref, o_ref):
    c = c_ref[...]
    a = c * jax.nn.sigmoid(c)
    o_ref[...] = jnp.dot(a.astype(BF16), w_ref[...].astype(BF16),
                         preferred_element_type=F32) + b_ref[...]


def _adaln(cvec, w_ada, b_ada):
    tn = ADALN_TILE
    return pl.pallas_call(
        _adaln_kernel,
        grid=(DEPTH, 6 * D_MODEL // tn),
        in_specs=[
            pl.BlockSpec((N_MOD_ROWS, D_MODEL), lambda l, j: (0, 0)),
            pl.BlockSpec((None, D_MODEL, tn), lambda l, j: (l, 0, j)),
            pl.BlockSpec((None, 1, tn), lambda l, j: (l, 0, j)),
        ],
        out_specs=pl.BlockSpec((None, N_MOD_ROWS, tn), lambda l, j: (l, 0, j)),
        out_shape=jax.ShapeDtypeStruct((DEPTH, N_MOD_ROWS, 6 * D_MODEL), F32),
        compiler_params=_params(2),
        name="adaln",
    )(cvec, w_ada, b_ada.reshape(DEPTH, 1, 6 * D_MODEL))


def _na_r0(r, rows):
    return min(max(r - NA_WIN_ROWS // 2, 0), rows - NA_WIN_ROWS)


def _na_union_start(p, rows):
    return min(_na_r0(NA_BLOCK_ROWS * p, rows), rows - NA_UNION_ROWS)


def _na_bias_kernel(tbl_ref, o_ref, *, rows):
    l = pl.program_id(0)
    h = pl.program_id(1)
    qi = lax.broadcasted_iota(jnp.int32, (GRID_W, GRID_W), 0)
    ki = lax.broadcasted_iota(jnp.int32, (GRID_W, GRID_W), 1)
    dc = jnp.clip(ki - qi, -(NA_WIN_COLS - 1), NA_WIN_COLS - 1) + NA_WIN_COLS - 1
    c0 = jnp.clip(qi - NA_WIN_COLS // 2, 0, GRID_W - NA_WIN_COLS)
    col_ok = (ki >= c0) & (ki < c0 + NA_WIN_COLS)
    n_dr = 2 * NA_WIN_ROWS - 1
    n_dc = 2 * NA_WIN_COLS - 1
    tiles = []
    for dr in range(n_dr):
        t = jnp.zeros((GRID_W, GRID_W), F32)
        for d in range(n_dc):
            t = jnp.where(dc == d, tbl_ref[((l * NA_HEADS + h) * n_dr + dr) * n_dc + d], t)
        tiles.append(jnp.where(col_ok, t, NEG))
    neg_tile = jnp.full((GRID_W, GRID_W), NEG, F32)
    for p in range(rows // NA_BLOCK_ROWS):
        start = _na_union_start(p, rows)
        for a in range(NA_BLOCK_ROWS):
            r = NA_BLOCK_ROWS * p + a
            r0 = _na_r0(r, rows)
            for j in range(NA_UNION_ROWS):
                kr = start + j
                inside = r0 <= kr < r0 + NA_WIN_ROWS
                tile = tiles[kr - r + NA_WIN_ROWS - 1] if inside else neg_tile
                o_ref[p, a * GRID_W:(a + 1) * GRID_W, j * GRID_W:(j + 1) * GRID_W] = tile


def _na_bias_expand(na_bias, rows):
    n_blocks = rows // NA_BLOCK_ROWS
    qn = NA_BLOCK_ROWS * GRID_W
    kn = NA_UNION_ROWS * GRID_W
    return pl.pallas_call(
        functools.partial(_na_bias_kernel, rows=rows),
        grid=(DEPTH, NA_HEADS),
        in_specs=[pl.BlockSpec(memory_space=pltpu.SMEM)],
        out_specs=pl.BlockSpec((None, None, n_blocks, qn, kn), lambda l, h: (l, h, 0, 0, 0)),
        out_shape=jax.ShapeDtypeStruct((DEPTH, NA_HEADS, n_blocks, qn, kn), F32),
        compiler_params=_params(2),
        name="na_bias_expand",
    )(na_bias.reshape(-1))


def _mod_row_map(layer, tiles_per_row, first_row):
    if tiles_per_row is None:
        return lambda i: (layer, first_row, 0, 0)
    return lambda i: (layer, first_row + i // tiles_per_row, 0, 0)


def _inproj_kernel(x_ref, mod_ref, g_ref, wna_ref, wgq_ref, wml_ref, wgr_ref, bgr_ref,
                   zna_ref, zgq_ref, zml_ref, gi_ref, gf_ref, grow_ref):
    x = x_ref[...]
    h = _rms(x, g_ref[0:1, :]) * (1.0 + mod_ref[1:2, :]) + mod_ref[0:1, :]
    hb = h.astype(BF16)
    zna_ref[...] = lax.dot_general(hb, wna_ref[...], NT_DIMS, preferred_element_type=F32)
    zgq_ref[...] = lax.dot_general(hb, wgq_ref[...], NT_DIMS, preferred_element_type=F32)
    zml_ref[...] = lax.dot_general(hb, wml_ref[...], NT_DIMS, preferred_element_type=F32)
    gr = lax.dot_general(wgr_ref[...], hb, NT_DIMS, preferred_element_type=F32) + bgr_ref[...]
    grow_ref[...] = gr
    gc = gr.T
    gi_ref[...] = gc[:, 0:N_SCANS]
    gf_ref[...] = gc[:, N_SCANS:]


def _inproj(x, mods, g_norm, w, layer, tiles_per_row, first_row, tm):
    n = x.shape[0]
    wspec = lambda width: pl.BlockSpec((None, width, D_MODEL), lambda i: (layer, 0, 0))
    return pl.pallas_call(
        _inproj_kernel,
        grid=(n // tm,),
        in_specs=[
            pl.BlockSpec((tm, D_MODEL), lambda i: (i, 0)),
            pl.BlockSpec((None, None, 6, D_MODEL), _mod_row_map(layer, tiles_per_row, first_row)),
            pl.BlockSpec((None, 4, D_MODEL), lambda i: (layer, 0, 0)),
            wspec(ZNA_W), wspec(ZGQ_W), wspec(ZML_W), wspec(N_GATES),
            pl.BlockSpec((None, N_GATES, 1), lambda i: (layer, 0, 0)),
        ],
        out_specs=[
            pl.BlockSpec((tm, ZNA_W), lambda i: (i, 0)),
            pl.BlockSpec((tm, ZGQ_W), lambda i: (i, 0)),
            pl.BlockSpec((tm, ZML_W), lambda i: (i, 0)),
            pl.BlockSpec((tm, N_SCANS), lambda i: (i, 0)),
            pl.BlockSpec((tm, N_SCANS), lambda i: (i, 0)),
            pl.BlockSpec((N_GATES, tm), lambda i: (0, i)),
        ],
        out_shape=[
            jax.ShapeDtypeStruct((n, ZNA_W), F32),
            jax.ShapeDtypeStruct((n, ZGQ_W), F32),
            jax.ShapeDtypeStruct((n, ZML_W), F32),
            jax.ShapeDtypeStruct((n, N_SCANS), F32),
            jax.ShapeDtypeStruct((n, N_SCANS), F32),
            jax.ShapeDtypeStruct((N_GATES, n), F32),
        ],
        compiler_params=_params(1),
        name="inproj",
    )(x, mods, g_norm, w["na"], w["gq"], w["ml"], w["gate_row"], w["b_row"])


def _post_kernel(x_ref, mna_ref, mgq_ref, mml_ref, mod_ref, g_ref, wo_ref, wgu_ref, wd_ref, o_ref):
    rows = x_ref.shape[0] // POST_ROW_GROUPS
    rs = [slice(i * rows, (i + 1) * rows) for i in range(POST_ROW_GROUPS)]
    acc = []
    for r in rs:
        a = jnp.dot(mna_ref[r, :], wo_ref[0:NA_W, :], preferred_element_type=F32)
        a += jnp.dot(mgq_ref[r, :], wo_ref[NA_W:NA_W + GQA_QW, :], preferred_element_type=F32)
        a += jnp.dot(mml_ref[r, :], wo_ref[NA_W + GQA_QW:, :], preferred_element_type=F32)
        acc.append(a)
    x1, gate_up = [], []
    for r, a in zip(rs, acc):
        xr = x_ref[r, :] + mod_ref[2:3, :] * _rms(a, g_ref[1:2, :])
        hb = (_rms(xr, g_ref[2:3, :]) * (1.0 + mod_ref[4:5, :]) + mod_ref[3:4, :]).astype(BF16)
        x1.append(xr)
        gate_up.append((jnp.dot(hb, wgu_ref[:, 0:FF_HIDDEN], preferred_element_type=F32),
                        jnp.dot(hb, wgu_ref[:, FF_HIDDEN:], preferred_element_type=F32)))
    f = []
    for gate, up in gate_up:
        act = (gate * jax.nn.sigmoid(gate) * up).astype(BF16)
        f.append(jnp.dot(act, wd_ref[...], preferred_element_type=F32))
    for r, xr, fr in zip(rs, x1, f):
        o_ref[r, :] = xr + mod_ref[5:6, :] * _rms(fr, g_ref[3:4, :])


def _post(x, mna, mgq, mml, mods, g_norm, w_out, w_gu, w_down, layer, tiles_per_row, first_row, tm):
    n = x.shape[0]
    resident = lambda shape: pl.BlockSpec((None,) + shape, lambda i: (layer, 0, 0), pipeline_mode=pl.Buffered(1))
    return pl.pallas_call(
        _post_kernel,
        grid=(n // tm,),
        in_specs=[
            pl.BlockSpec((tm, D_MODEL), lambda i: (i, 0)),
            pl.BlockSpec((tm, NA_W), lambda i: (i, 0)),
            pl.BlockSpec((tm, GQA_QW), lambda i: (i, 0)),
            pl.BlockSpec((tm, ML_W), lambda i: (i, 0)),
            pl.BlockSpec((None, None, 6, D_MODEL), _mod_row_map(layer, tiles_per_row, first_row)),
            pl.BlockSpec((None, 4, D_MODEL), lambda i: (layer, 0, 0)),
            resident((D_MODEL, D_MODEL)),
            resident((D_MODEL, 2 * FF_HIDDEN)),
            resident((FF_HIDDEN, D_MODEL)),
        ],
        out_specs=pl.BlockSpec((tm, D_MODEL), lambda i: (i, 0)),
        out_shape=jax.ShapeDtypeStruct((n, D_MODEL), F32),
        compiler_params=_params(1),
        name="post",
    )(x, mna, mgq, mml, mods, g_norm, w_out, w_gu, w_down)


def _pair_rms(x, gain, sums_on_mxu=False):
    xsq = x * x
    if sums_on_mxu:
        row = lax.broadcasted_iota(jnp.int32, (LANES, LANES), 0) < HEAD_DIM
        col = lax.broadcasted_iota(jnp.int32, (LANES, LANES), 1) < HEAD_DIM
        ones_blk = jnp.where(row == col, 1.0, 0.0).astype(BF16)
        sums = sum(jnp.dot(p, ones_blk, preferred_element_type=F32) for p in _split3(xsq))
    else:
        first = _lane_first(x.shape)
        sums = jnp.where(first, jnp.sum(jnp.where(first, xsq, 0.0), axis=-1, keepdims=True),
                         jnp.sum(jnp.where(first, 0.0, xsq), axis=-1, keepdims=True))
    return x * lax.rsqrt(sums * (1.0 / HEAD_DIM) + EPS) * gain


def _pair_gain(gqk_ref, row):
    return jnp.concatenate([gqk_ref[row:row + 1, :]] * (LANES // HEAD_DIM), axis=-1)


def _ctx_attn_kernel(zna_ref, zgq_ref, gqk_ref, *rest):
    mna_ref, mgq_ref, kvna_ref, kvgq_ref = rest[-4:]

    def store_t(ref, which, pair, x2):
        xt = x2.T
        for a in range(2):
            ref[which, 2 * pair + a] = xt[a * HEAD_DIM:(a + 1) * HEAD_DIM]

    pending = []
    for i in range(NA_HEADS // 2):
        cols = slice(i * LANES, (i + 1) * LANES)
        q2 = zna_ref[:, cols] * QK_SCALE
        k2 = zna_ref[:, NA_W + i * LANES:NA_W + (i + 1) * LANES]
        v2 = zna_ref[:, 2 * NA_W + i * LANES:2 * NA_W + (i + 1) * LANES]
        store_t(kvna_ref, 0, i, k2)
        store_t(kvna_ref, 1, i, v2)
        blocks = [{"k": k2.astype(BF16), "k_nt": True, "r": _pair_values(v2), "r_nt": False}]
        pending.append((mna_ref, cols, _pair_scores(_pair_queries(q2), blocks), blocks))

    gq, gk = _pair_gain(gqk_ref, 0), _pair_gain(gqk_ref, 1)
    k2 = _pair_rms(zgq_ref[:, GQA_QW:GQA_QW + GQA_KW], gk)
    v2 = zgq_ref[:, GQA_QW + GQA_KW:]
    store_t(kvgq_ref, 0, 0, k2)
    store_t(kvgq_ref, 1, 0, v2)
    blocks = [{"k": k2.astype(BF16), "k_nt": True, "r": _pair_values(v2), "r_nt": False}]
    for p in range(GQA_GROUP):
        cols = slice(p * LANES, (p + 1) * LANES)
        q2 = _pair_rms(zgq_ref[:, cols], gq) * QK_SCALE
        pending.append((mgq_ref, cols, _pair_scores(_pair_queries(q2), blocks), blocks))

    for ref, cols, scores, blocks in pending:
        ref[:, cols] = _pair_finish(scores, blocks).astype(BF16)


def _ctx_attn(zna, zgq, g_qk, kv_na_buf, kv_gq_buf, layer, batch, t):
    n = zna.shape[0]
    return pl.pallas_call(
        _ctx_attn_kernel,
        grid=(batch,),
        in_specs=[
            pl.BlockSpec((t, ZNA_W), lambda b: (b, 0)),
            pl.BlockSpec((t, ZGQ_W), lambda b: (b, 0)),
            pl.BlockSpec((None, 2, HEAD_DIM), lambda b: (layer, 0, 0)),
            pl.BlockSpec(memory_space=pl.ANY),
            pl.BlockSpec(memory_space=pl.ANY),
        ],
        out_specs=[
            pl.BlockSpec((t, NA_W), lambda b: (b, 0)),
            pl.BlockSpec((t, GQA_QW), lambda b: (b, 0)),
            pl.BlockSpec((None, None, 2, NA_HEADS, HEAD_DIM, t), lambda b: (b, layer, 0, 0, 0, 0)),
            pl.BlockSpec((None, None, 2, GQA_KV_HEADS, HEAD_DIM, t), lambda b: (b, layer, 0, 0, 0, 0)),
        ],
        out_shape=[
            jax.ShapeDtypeStruct((n, NA_W), BF16),
            jax.ShapeDtypeStruct((n, GQA_QW), BF16),
            jax.ShapeDtypeStruct((batch, DEPTH, 2, NA_HEADS, HEAD_DIM, t), F32),
            jax.ShapeDtypeStruct((batch, DEPTH, 2, GQA_KV_HEADS, HEAD_DIM, t), F32),
        ],
        input_output_aliases={3: 2, 4: 3},
        compiler_params=_params(1),
        name="ctx_attn",
    )(zna, zgq, g_qk, kv_na_buf, kv_gq_buf)


def _lat_na_kernel(zna_ref, cache_ref, bias_ref, o_ref, *, rows):
    qn = NA_BLOCK_ROWS * GRID_W
    kn = NA_UNION_ROWS * GRID_W
    n_blocks = rows // NA_BLOCK_ROWS
    prepared = []
    for i in range(NA_HEADS // 2):
        cols = slice(i * LANES, (i + 1) * LANES)
        q_pair = _pair_queries(zna_ref[:, cols] * QK_SCALE)
        k2 = zna_ref[:, NA_W + i * LANES:NA_W + (i + 1) * LANES].astype(BF16)
        kc = cache_ref[0, i].astype(BF16)
        scores = []
        for a in range(2):
            s_ctx = jnp.dot(q_pair[a], kc, preferred_element_type=F32)
            s_win = []
            for p in range(n_blocks):
                k0 = _na_union_start(p, rows) * GRID_W
                s_win.append(lax.dot_general(q_pair[a][p * qn:(p + 1) * qn], k2[k0:k0 + kn], NT_DIMS,
                                             preferred_element_type=F32) + bias_ref[2 * i + a, p])
            scores.append((s_ctx, s_win))
        prepared.append(scores)
    for i in range(NA_HEADS // 2):
        cols = slice(i * LANES, (i + 1) * LANES)
        r = _pair_values(zna_ref[:, 2 * NA_W + i * LANES:2 * NA_W + (i + 1) * LANES])
        rc = _pair_values_t(cache_ref[1, i])
        acc = None
        for a in range(2):
            s_ctx, s_win = prepared[i][a]
            m_win = jnp.concatenate([s.max(axis=-1, keepdims=True) for s in s_win], axis=0)
            m = jnp.maximum(m_win, s_ctx.max(axis=-1, keepdims=True))
            term = lax.dot_general(jnp.exp(s_ctx - m).astype(BF16), rc[a], NT_DIMS, preferred_element_type=F32)
            wins = []
            for p in range(n_blocks):
                k0 = _na_union_start(p, rows) * GRID_W
                pw = jnp.exp(s_win[p] - m[p * qn:(p + 1) * qn]).astype(BF16)
                wins.append(jnp.dot(pw, r[a][k0:k0 + kn], preferred_element_type=F32))
            term = term + jnp.concatenate(wins, axis=0)
            acc = term if acc is None else acc + term
        o_ref[:, cols] = (acc[:, :LANES] / acc[:, LANES:]).astype(BF16)


def _lat_na(zna, cache_t, bias, layer, batch, t):
    n = zna.shape[0]
    rows = t // GRID_W
    past = cache_t.shape[-1]
    n_blocks = rows // NA_BLOCK_ROWS
    qn = NA_BLOCK_ROWS * GRID_W
    kn = NA_UNION_ROWS * GRID_W
    return pl.pallas_call(
        functools.partial(_lat_na_kernel, rows=rows),
        grid=(batch,),
        in_specs=[
            pl.BlockSpec((t, ZNA_W), lambda b: (b, 0)),
            pl.BlockSpec((None, None, 2, NA_HEADS // 2, LANES, past), lambda b: (b, layer, 0, 0, 0, 0)),
            pl.BlockSpec((None, NA_HEADS, n_blocks, qn, kn), lambda b: (layer, 0, 0, 0, 0)),
        ],
        out_specs=pl.BlockSpec((t, NA_W), lambda b: (b, 0)),
        out_shape=jax.ShapeDtypeStruct((n, NA_W), BF16),
        compiler_params=_params(1),
        name="lat_na",
    )(zna, cache_t, bias)


def _rope_tables(t):
    half = HEAD_DIM // 2
    quarter = half // 2
    inv = 1.0 / (ROPE_BASE ** (jnp.arange(quarter, dtype=F32) / quarter))
    tt = jnp.arange(t)
    row = (tt // GRID_W).astype(F32)
    col = (tt % GRID_W).astype(F32)
    ang_r = row[:, None] * inv[None, :]
    ang_c = col[:, None] * inv[None, :]
    cos = jnp.concatenate([jnp.cos(ang_r)] * 2 + [jnp.cos(ang_c)] * 2, axis=-1)
    sin = jnp.concatenate([-jnp.sin(ang_r), jnp.sin(ang_r), -jnp.sin(ang_c), jnp.sin(ang_c)], axis=-1)
    reps = LANES // HEAD_DIM
    return jnp.tile(cos, (1, reps)), jnp.tile(sin, (1, reps))


def _pair_rope(xn, cos, sin):
    quarter = HEAD_DIM // 4
    lane = lax.broadcasted_iota(jnp.int32, xn.shape, 1)
    lower = (lane & (2 * quarter - 1)) < quarter
    partner = jnp.where(lower, pltpu.roll(xn, LANES - quarter, 1), pltpu.roll(xn, quarter, 1))
    return xn * cos + partner * sin


def _lat_gqa_kernel(zgq_ref, cache_ref, gqk_ref, cos_ref, sin_ref, o_ref):
    cos = cos_ref[...]
    sin = sin_ref[...]
    gq, gk = _pair_gain(gqk_ref, 0), _pair_gain(gqk_ref, 1)
    keys = _pair_rope(_pair_rms(zgq_ref[:, GQA_QW:GQA_QW + GQA_KW], gk, True), cos, sin).astype(BF16)
    blocks = [
        {"k": keys, "k_nt": True, "r": _pair_values(zgq_ref[:, GQA_QW + GQA_KW:]), "r_nt": False},
        {"k": cache_ref[0].astype(BF16), "k_nt": False, "r": _pair_values_t(cache_ref[1]), "r_nt": True},
    ]
    def scores(p):
        x = _pair_rope(_pair_rms(zgq_ref[:, p * LANES:(p + 1) * LANES], gq, True), cos, sin) * QK_SCALE
        return _pair_scores(_pair_queries(x), blocks)

    nxt = scores(0)
    for p in range(GQA_GROUP):
        cur = nxt
        if p + 1 < GQA_GROUP:
            nxt = scores(p + 1)
        o_ref[:, p * LANES:(p + 1) * LANES] = _pair_finish(cur, blocks).astype(BF16)


def _lat_gqa(zgq, cache_t, g_qk, cos, sin, layer, batch, t):
    n = zgq.shape[0]
    past = cache_t.shape[-1]
    return pl.pallas_call(
        _lat_gqa_kernel,
        grid=(batch,),
        in_specs=[
            pl.BlockSpec((t, ZGQ_W), lambda b: (b, 0)),
            pl.BlockSpec((None, None, 2, GQA_KW, past), lambda b: (b, layer, 0, 0, 0)),
            pl.BlockSpec((None, 2, HEAD_DIM), lambda b: (layer, 0, 0)),
            pl.BlockSpec((t, LANES), lambda b: (0, 0)),
            pl.BlockSpec((t, LANES), lambda b: (0, 0)),
        ],
        out_specs=pl.BlockSpec((t, GQA_QW), lambda b: (b, 0)),
        out_shape=jax.ShapeDtypeStruct((n, GQA_QW), BF16),
        compiler_params=_params(1),
        name="lat_gqa",
    )(zgq, cache_t, g_qk, cos, sin)


def _split3(x):
    x1 = x.astype(BF16)
    r1 = x - x1.astype(F32)
    x2 = r1.astype(BF16)
    x3 = (r1 - x2.astype(F32)).astype(BF16)
    return x1, x2, x3


def _log_sigmoid(x):
    return jnp.minimum(x, 0.0) - jnp.log1p(jnp.exp(-jnp.abs(x)))


def _mlstm_select_matrix():
    H = MLSTM_HEADS
    sel = np.zeros((MLSTM_SEL_ROWS, (H // 2) * 2 * MLSTM_TILE_KINDS * LANES), np.float32)
    for j in range(H // 2):
        for d in range(2):
            for q in range(MLSTM_TILE_KINDS):
                for a in range(2):
                    col0 = ((j * 2 + d) * MLSTM_TILE_KINDS + q) * LANES + a * HEAD_DIM
                    sel[q * N_SCANS + d * H + 2 * j + a, col0:col0 + HEAD_DIM] = 1.0
    return sel


def _mlstm_kernel(zml_ref, gi_ref, gf_ref, grow_ref, c0_ref, n0_ref, m0_ref, gml_ref, sel_ref, *rest,
                  t, emit_state):
    tri_s, st_s, dst_s, cst_s, row_s = rest[-5:]
    outs = rest[:-5]
    if emit_state:
        o_ref, cf_ref, nf_ref, mf_ref = outs[-4:]
    else:
        o_ref = outs[-1]
    L = MLSTM_CHUNK
    H = MLSTM_HEADS
    HD = HEAD_DIM
    NP = H // 2
    nc = t // L
    tb = tri_s.shape[-1]

    @pl.when(pl.program_id(0) == 0)
    def _():
        ti = lax.broadcasted_iota(jnp.int32, (tb, tb), 0)
        ui = lax.broadcasted_iota(jnp.int32, (tb, tb), 1)
        same = (ti & -L) == (ui & -L)
        tri_s[0] = jnp.where(same & (ui <= ti), 1.0, 0.0).astype(BF16)
        tri_s[1] = jnp.where(same & (ui >= ti), 1.0, 0.0).astype(BF16)

    lower, upper = tri_s[0], tri_s[1]

    def chunk_sums_cols(x):
        parts = _split3(x)
        pre, suf = [], []
        for i in range(t // tb):
            blk = [p[i * tb:(i + 1) * tb] for p in parts]
            pre.append(sum(jnp.dot(lower, p, preferred_element_type=F32) for p in blk))
            suf.append(sum(jnp.dot(upper, p, preferred_element_type=F32) for p in blk))
        return jnp.concatenate(pre, axis=0), jnp.concatenate(suf, axis=0)

    def chunk_sums_rows(x):
        parts = _split3(x)
        pre, suf = [], []
        for i in range(t // tb):
            blk = [p[:, i * tb:(i + 1) * tb] for p in parts]
            pre.append(sum(jnp.dot(p, upper, preferred_element_type=F32) for p in blk))
            suf.append(sum(jnp.dot(p, lower, preferred_element_type=F32) for p in blk))
        return jnp.concatenate(pre, axis=1), jnp.concatenate(suf, axis=1)

    pre_c, suf_c = chunk_sums_cols(_log_sigmoid(gf_ref[...]))
    lane_c = lax.broadcasted_iota(jnp.int32, (t, N_SCANS), 1)
    b3 = jnp.where(lane_c < H, pre_c, suf_c).reshape(nc, L, N_SCANS)
    i3 = gi_ref[...].reshape(nc, L, N_SCANS)
    fwd3 = lax.broadcasted_iota(jnp.int32, (nc, 1, N_SCANS), 2) < H
    b_end3 = jnp.where(fwd3, b3[:, L - 1:L, :], b3[:, 0:1, :])
    lw_end3 = b_end3 - b3 + i3
    a3 = jnp.max(lw_end3, axis=1, keepdims=True)
    wloc3 = jnp.exp(lw_end3 - a3)

    fwd1 = lax.broadcasted_iota(jnp.int32, (1, N_SCANS), 1) < H
    m = m0_ref[...]
    m_start, carry_decay, contrib_scale = [], [], []
    for j in range(nc):
        a_j = jnp.where(fwd1, a3[j], a3[nc - 1 - j])
        g_j = jnp.where(fwd1, b_end3[j], b_end3[nc - 1 - j])
        m_start.append(m)
        m_next = jnp.maximum(g_j + m, a_j)
        carry_decay.append(jnp.exp(g_j + m - m_next))
        contrib_scale.append(jnp.exp(a_j - m_next))
        m = m_next
    mst3 = jnp.concatenate([jnp.where(fwd1, m_start[c], m_start[nc - 1 - c])[None] for c in range(nc)], axis=0)

    cols = jnp.concatenate([b3.reshape(t, N_SCANS), wloc3.reshape(t, N_SCANS)], axis=1)
    tiles_all = sum(jnp.dot(p, sel_ref[...], preferred_element_type=F32) for p in _split3(cols))
    tile_w = MLSTM_TILE_KINDS * LANES

    def tiles(j, d):
        x = tiles_all[:, (2 * j + d) * tile_w:(2 * j + d + 1) * tile_w]
        return [x[:, q * LANES:(q + 1) * LANES].reshape(nc, L, LANES) for q in range(MLSTM_TILE_KINDS)]

    gr = grow_ref[...]
    pre_r, suf_r = chunk_sums_rows(_log_sigmoid(gr))
    sub_r = lax.broadcasted_iota(jnp.int32, (N_SCANS, t), 0)
    rowv = gr[0:N_SCANS] - jnp.where(sub_r < H, pre_r[N_SCANS:], suf_r[N_SCANS:])
    for j in range(NP):
        for d in range(2):
            e = d * H + 2 * j
            for c in range(nc):
                row_s[2 * j + d, c] = jnp.concatenate(
                    [rowv[e:e + 1, c * L:(c + 1) * L], rowv[e + 1:e + 2, c * L:(c + 1) * L]], axis=1)

    lane_a = lax.broadcasted_iota(jnp.int32, (1, 1, LANES), 2) < HD
    sub_a = lax.broadcasted_iota(jnp.int32, (1, 2 * HD, 1), 1) < HD
    diag = sub_a == lane_a
    diag4 = jnp.concatenate([diag] * 4, axis=2)

    def stack_heads(x3):
        return jnp.concatenate([jnp.where(lane_a, x3, 0.0), jnp.where(lane_a, 0.0, x3)], axis=1)

    def pair_cols(base, j):
        return slice(base + j * LANES, base + (j + 1) * LANES)

    zero_blk = jnp.zeros((HD, HD), F32)
    for j in range(NP):
        cols_d = []
        for d in range(2):
            ca, cb = c0_ref[d, 2 * j].T, c0_ref[d, 2 * j + 1].T
            na = jnp.broadcast_to(n0_ref[d, 2 * j:2 * j + 1, :], (HD, HD)).T
            nb = jnp.broadcast_to(n0_ref[d, 2 * j + 1:2 * j + 2, :], (HD, HD)).T
            top = jnp.concatenate([ca, zero_blk, na, zero_blk], axis=1)
            bot = jnp.concatenate([zero_blk, cb, zero_blk, nb], axis=1)
            cols_d.append(jnp.concatenate([top, bot], axis=0))
        st_s[j] = jnp.concatenate(cols_d, axis=1)

    for j in range(NP):
        k3 = (zml_ref[:, pair_cols(ML_W, j)] * QK_SCALE).reshape(nc, L, LANES).astype(BF16)
        v3 = zml_ref[:, pair_cols(2 * ML_W, j)].reshape(nc, L, LANES)
        rhs = []
        for d in range(2):
            wl = tiles(j, d)[1]
            rhs += [v3 * wl, wl]
        rhs = jnp.concatenate(rhs, axis=2).astype(BF16)
        contrib = jnp.einsum("csk,csn->ckn", k3, rhs, preferred_element_type=F32)
        dst_s[j] = jnp.where(diag4, contrib, 0.0)

    def lane_scale(v, j):
        pieces = []
        for d in range(2):
            sa = jnp.broadcast_to(v[:, d * H + 2 * j:d * H + 2 * j + 1], (1, HD))
            sb = jnp.broadcast_to(v[:, d * H + 2 * j + 1:d * H + 2 * j + 2], (1, HD))
            pieces += [sa, sb, sa, sb]
        return jnp.concatenate(pieces, axis=1)

    for j in range(NP):
        st = st_s[j]
        for step in range(nc):
            cb = nc - 1 - step
            stb = st.astype(BF16)
            cst_s[j, step, :, 0:2 * LANES] = stb[:, 0:2 * LANES]
            cst_s[j, cb, :, 2 * LANES:] = stb[:, 2 * LANES:]
            delta = jnp.concatenate([dst_s[j, step, :, 0:2 * LANES], dst_s[j, cb, :, 2 * LANES:]], axis=1)
            st = lane_scale(carry_decay[step], j) * st + lane_scale(contrib_scale[step], j) * delta
        st_s[j] = st

    sidx = lax.broadcasted_iota(jnp.int32, (1, L, LANES), 2) & (HD - 1)
    tidx = lax.broadcasted_iota(jnp.int32, (1, L, LANES), 1)
    masks = (sidx <= tidx, sidx >= tidx)
    ones_blk = jnp.broadcast_to(jnp.where(diag, 1.0, 0.0).astype(BF16), (nc, 2 * HD, LANES))
    neg_inf = -jnp.inf
    for j in range(NP):
        q3 = zml_ref[:, pair_cols(0, j)].reshape(nc, L, LANES).astype(BF16)
        k3 = (zml_ref[:, pair_cols(ML_W, j)] * QK_SCALE).reshape(nc, L, LANES)
        v3 = zml_ref[:, pair_cols(2 * ML_W, j)].reshape(nc, L, LANES)
        qk = jnp.einsum("ctd,cnd->ctn", q3, stack_heads(k3).astype(BF16), preferred_element_type=F32)
        v_aug = jnp.concatenate([stack_heads(v3).astype(BF16), ones_blk], axis=2)
        out = None
        for d in range(2):
            b_t = tiles(j, d)[0]
            e = d * H + 2 * j
            bm_t = b_t + jnp.where(lane_a, mst3[:, :, e:e + 1], mst3[:, :, e + 1:e + 2])
            logw = jnp.where(masks[d], b_t + row_s[2 * j + d], neg_inf)
            rmax_a = jnp.max(jnp.where(lane_a, logw, neg_inf), axis=-1, keepdims=True)
            rmax_b = jnp.max(jnp.where(lane_a, neg_inf, logw), axis=-1, keepdims=True)
            m_t = jnp.maximum(jnp.where(lane_a, rmax_a, rmax_b), bm_t)
            s = qk * jnp.exp(logw - m_t)
            decay = jnp.exp(bm_t - m_t)
            sv = jnp.einsum("cts,csn->ctn", s.astype(BF16), v_aug, preferred_element_type=F32)
            state = cst_s[j, :, :, 2 * d * LANES:2 * (d + 1) * LANES]
            inter = jnp.einsum("ctk,ckn->ctn", q3, state, preferred_element_type=F32)
            num = sv[:, :, 0:LANES] + decay * inter[:, :, 0:LANES]
            den = sv[:, :, LANES:] + decay * inter[:, :, LANES:]
            h_d = num / jnp.maximum(jnp.abs(den), jnp.exp(-m_t))
            out = h_d if out is None else out + h_d
        cols = pair_cols(0, j)
        og = jax.nn.sigmoid(zml_ref[:, pair_cols(3 * ML_W, j)])
        o_ref[:, cols] = (_pair_rms(out.reshape(t, LANES), gml_ref[:, cols]) * og).astype(BF16)

    if emit_state:
        for j in range(NP):
            st = st_s[j]
            for d in range(2):
                for a in range(2):
                    rows = slice(a * HD, (a + 1) * HD)
                    c0 = 2 * d * LANES + a * HD
                    cf_ref[d, 2 * j + a] = st[rows, c0:c0 + HD].T
                    nf_ref[d, 2 * j + a:2 * j + a + 1, :] = st[rows, c0 + LANES:c0 + LANES + HD].T[0:1, :]
        mf_ref[...] = m


def _mlstm(zml, gi, gf, grow, c0, n0, m0, g_ml, layer, batch, t, state_out, state_layer):
    emit_state = state_out is not None
    n = zml.shape[0]
    H = MLSTM_HEADS
    L = MLSTM_CHUNK
    nc = t // L
    sel = jnp.asarray(_mlstm_select_matrix(), BF16)
    tri_block = min(t, MXU_DIM)
    assert L & (L - 1) == 0 and tri_block % L == 0 and t % tri_block == 0
    if state_layer is None:
        c_spec = pl.BlockSpec((None, 2, H, HEAD_DIM, HEAD_DIM), lambda b: (0, 0, 0, 0, 0))
        n_spec = pl.BlockSpec((None, 2, H, HEAD_DIM), lambda b: (0, 0, 0, 0))
        m_spec = pl.BlockSpec((None, 1, N_SCANS), lambda b: (0, 0, 0))
    else:
        c_spec = pl.BlockSpec((None, None, 2, H, HEAD_DIM, HEAD_DIM), lambda b: (b, state_layer, 0, 0, 0, 0))
        n_spec = pl.BlockSpec((None, None, 2, H, HEAD_DIM), lambda b: (b, state_layer, 0, 0, 0))
        m_spec = pl.BlockSpec((None, None, 1, N_SCANS), lambda b: (b, state_layer, 0, 0))
    in_specs = [
        pl.BlockSpec((t, ZML_W), lambda b: (b, 0)),
        pl.BlockSpec((t, N_SCANS), lambda b: (b, 0)),
        pl.BlockSpec((t, N_SCANS), lambda b: (b, 0)),
        pl.BlockSpec((N_GATES, t), lambda b: (0, b)),
        c_spec, n_spec, m_spec,
        pl.BlockSpec((None, 1, ML_W), lambda b: (layer, 0, 0)),
        pl.BlockSpec(sel.shape, lambda b: (0, 0)),
    ]
    args = [zml, gi, gf, grow, c0, n0, m0, g_ml, sel]
    out_specs = [pl.BlockSpec((t, ML_W), lambda b: (b, 0))]
    out_shape = [jax.ShapeDtypeStruct((n, ML_W), BF16)]
    aliases = {}
    if emit_state:
        out_specs += [
            pl.BlockSpec((None, None, 2, H, HEAD_DIM, HEAD_DIM), lambda b: (b, layer, 0, 0, 0, 0)),
            pl.BlockSpec((None, None, 2, H, HEAD_DIM), lambda b: (b, layer, 0, 0, 0)),
            pl.BlockSpec((None, None, 1, N_SCANS), lambda b: (b, layer, 0, 0)),
        ]
        out_shape += [jax.ShapeDtypeStruct(buf.shape, F32) for buf in state_out]
        aliases = {len(args) + i: 1 + i for i in range(len(state_out))}
        in_specs += [pl.BlockSpec(memory_space=pl.ANY)] * len(state_out)
        args += list(state_out)
    return pl.pallas_call(
        functools.partial(_mlstm_kernel, t=t, emit_state=emit_state),
        grid=(batch,),
        in_specs=in_specs,
        out_specs=out_specs,
        out_shape=out_shape,
        input_output_aliases=aliases,
        scratch_shapes=[
            pltpu.VMEM((2, tri_block, tri_block), BF16),
            pltpu.VMEM((H // 2, 2 * HEAD_DIM, 4 * LANES), F32),
            pltpu.VMEM((H // 2, nc, 2 * HEAD_DIM, 4 * LANES), F32),
            pltpu.VMEM((H // 2, nc, 2 * HEAD_DIM, 4 * LANES), BF16),
            pltpu.VMEM((H, nc, 1, LANES), F32),
        ],
        compiler_params=_params(1),
        name="mlstm",
    )(*args)


def _layer_path(x, mods, layer, first_row, tiles_row_tokens, weights, mixers):
    per_row = None if tiles_row_tokens is None else tiles_row_tokens // TOKEN_TILE
    g_norm = weights["g_norm"]
    zna, zgq, zml, gi, gf, grow = _inproj(x, mods, g_norm, weights["w_in"], layer, per_row, first_row, TOKEN_TILE)
    mna, mgq, mml, extra = mixers(zna, zgq, zml, gi, gf, grow)
    x = _post(x, mna, mgq, mml, mods, g_norm, weights["w_out"], weights["w_gu"], weights["w_down"], layer,
              per_row, first_row, TOKEN_TILE)
    return x, extra


def _gqa_pair_order():
    return [a * GQA_GROUP + p for p in range(GQA_GROUP) for a in range(GQA_KV_HEADS)]


def _take_blocks(x, axis, base, width, order):
    return jnp.concatenate([lax.slice_in_dim(x, base + width * o, base + width * (o + 1), axis=axis)
                            for o in order], axis=axis)


def kernel(x_prompt, x_sample, cache_na_kv, cache_gqa_kv, state_mlstm_C, state_mlstm_n, state_mlstm_m,
           c, c_ctx, w_in, b_gates, w_out, g_norm, g_qk, g_mlstm, na_bias, w_ada, b_ada, w_gu, w_down):
    batch, seq, _ = x_prompt.shape
    dec_batch, dec_seq, _ = x_sample.shape
    past = cache_na_kv.shape[-2]
    assert dec_batch + 1 <= N_MOD_ROWS and dec_seq % GRID_W == 0 and GQA_KV_HEADS == 2

    cvec = jnp.concatenate([c_ctx[None, :], c, jnp.zeros((N_MOD_ROWS - 1 - dec_batch, D_MODEL), F32)], axis=0)
    mods = _adaln(cvec, w_ada, b_ada).reshape(DEPTH, N_MOD_ROWS, 6, D_MODEL)
    bias = _na_bias_expand(na_bias, dec_seq // GRID_W)
    cos, sin = _rope_tables(dec_seq)

    pair_order = _gqa_pair_order()
    scan_order = [2 * d for d in range(2)]
    H = MLSTM_HEADS
    w_in_t = jnp.swapaxes(w_in, 1, 2)
    w_gi = _take_blocks(w_in_t, 1, OFF_GATES, H, scan_order)
    w_gf = _take_blocks(w_in_t, 1, OFF_GATES + H, H, scan_order)
    b_gi = _take_blocks(b_gates, 1, 0, H, scan_order)
    b_gf = _take_blocks(b_gates, 1, H, H, scan_order)
    w_gq = jnp.concatenate([_take_blocks(w_in_t, 1, OFF_GQ, HEAD_DIM, pair_order),
                            w_in_t[:, OFF_GQ + GQA_QW:OFF_ML]], axis=1)
    w_out_rows = jnp.concatenate([w_out[:, :NA_W], _take_blocks(w_out, 1, NA_W, HEAD_DIM, pair_order),
                                  w_out[:, NA_W + GQA_QW:]], axis=1)
    weights = {
        "g_norm": g_norm,
        "w_in": {
            "na": w_in_t[:, :OFF_GQ].astype(BF16),
            "gq": w_gq.astype(BF16),
            "ml": w_in_t[:, OFF_ML:OFF_GATES].astype(BF16),
            "gate_row": jnp.concatenate([w_gi, w_gf], axis=1).astype(BF16),
            "b_row": jnp.concatenate([b_gi, b_gf], axis=-1)[:, :, None],
        },
        "w_out": w_out_rows.astype(BF16),
        "w_gu": w_gu.astype(BF16),
        "w_down": w_down.astype(BF16),
    }
    g_ml = g_mlstm.reshape(DEPTH, 1, ML_W)
    zero_c = jnp.zeros((1, 2, MLSTM_HEADS, HEAD_DIM, HEAD_DIM), F32)
    zero_n = jnp.zeros((1, 2, MLSTM_HEADS, HEAD_DIM), F32)
    zero_m = jnp.zeros((1, 1, N_SCANS), F32)
    m0_lat = state_mlstm_m.reshape(dec_batch, DEPTH, 1, N_SCANS)
    cache_na_t = jnp.swapaxes(cache_na_kv, -1, -2).reshape(dec_batch, DEPTH, 2, NA_HEADS // 2, LANES, past)
    cache_gq_t = jnp.swapaxes(cache_gqa_kv, -1, -2).reshape(dec_batch, DEPTH, 2, GQA_KW, past)

    xp = x_prompt.reshape(batch * seq, D_MODEL)
    xs = x_sample.reshape(dec_batch * dec_seq, D_MODEL)
    kv_na = jnp.zeros((batch, DEPTH, 2, NA_HEADS, HEAD_DIM, seq), F32)
    kv_gq = jnp.zeros((batch, DEPTH, 2, GQA_KV_HEADS, HEAD_DIM, seq), F32)
    states = (jnp.zeros((batch, DEPTH, 2, MLSTM_HEADS, HEAD_DIM, HEAD_DIM), F32),
              jnp.zeros((batch, DEPTH, 2, MLSTM_HEADS, HEAD_DIM), F32),
              jnp.zeros((batch, DEPTH, 1, N_SCANS), F32))
    for layer in range(DEPTH):
        def ctx_mixers(zna, zgq, zml, gi, gf, grow, layer=layer, kv_na=kv_na, kv_gq=kv_gq, states=states):
            mna, mgq, kv_na, kv_gq = _ctx_attn(zna, zgq, g_qk, kv_na, kv_gq, layer, batch, seq)
            mml, *states = _mlstm(zml, gi, gf, grow, zero_c, zero_n, zero_m, g_ml, layer, batch, seq,
                                  states, None)
            return mna, mgq, mml, (kv_na, kv_gq, tuple(states))

        def lat_mixers(zna, zgq, zml, gi, gf, grow, layer=layer):
            mna = _lat_na(zna, cache_na_t, bias, layer, dec_batch, dec_seq)
            mgq = _lat_gqa(zgq, cache_gq_t, g_qk, cos, sin, layer, dec_batch, dec_seq)
            (mml,) = _mlstm(zml, gi, gf, grow, state_mlstm_C, state_mlstm_n, m0_lat, g_ml, layer,
                            dec_batch, dec_seq, None, layer)
            return mna, mgq, mml, None

        xp, (kv_na, kv_gq, states) = _layer_path(xp, mods, layer, 0, None, weights, ctx_mixers)
        xs, _ = _layer_path(xs, mods, layer, 1, dec_seq, weights, lat_mixers)

    new_c, new_n, new_m = states
    return (xp.reshape(batch, seq, D_MODEL), xs.reshape(dec_batch, dec_seq, D_MODEL),
            jnp.swapaxes(kv_na, -1, -2), jnp.swapaxes(kv_gq, -1, -2),
            new_c, new_n, new_m.reshape(batch, DEPTH, 2, MLSTM_HEADS))
```

```python
import functools

import jax
import jax.numpy as jnp
import numpy as np
from jax import lax
from jax.experimental import pallas as pl
from jax.experimental.pallas import tpu as pltpu

D_MODEL = 1024
DEPTH = 4
GRID_W = 64
HEAD_DIM = 64
NA_HEADS = 4
GQA_Q_HEADS = 8
GQA_KV_HEADS = 2
GQA_GROUP = GQA_Q_HEADS // GQA_KV_HEADS
MLSTM_HEADS = 4
NA_WIN_ROWS = 8
NA_WIN_COLS = 16
MLSTM_CHUNK = 64
ROPE_BASE = 10000.0
EPS = 1e-6
NEG = -1e30
NA_W = NA_HEADS * HEAD_DIM
GQA_QW = GQA_Q_HEADS * HEAD_DIM
GQA_KW = GQA_KV_HEADS * HEAD_DIM
ML_W = MLSTM_HEADS * HEAD_DIM
N_GATES = 4 * MLSTM_HEADS
N_SCANS = 2 * MLSTM_HEADS
MLSTM_TILE_KINDS = 2
MLSTM_SEL_ROWS = MLSTM_TILE_KINDS * N_SCANS
FF_HIDDEN = ((8 * D_MODEL + 3 * 256 - 1) // (3 * 256)) * 256
QK_SCALE = HEAD_DIM ** -0.5

ZNA_W = 3 * NA_W
ZGQ_W = GQA_QW + 2 * GQA_KW
ZML_W = 4 * ML_W
OFF_GQ = ZNA_W
OFF_ML = ZNA_W + ZGQ_W
OFF_GATES = OFF_ML + ZML_W

LANES = 128
MXU_DIM = 256
N_MOD_ROWS = 16
NA_BLOCK_ROWS = 2
NA_UNION_ROWS = NA_WIN_ROWS + NA_BLOCK_ROWS - 1

F32 = jnp.float32
BF16 = jnp.bfloat16
VMEM_LIMIT = 52 * 1024 * 1024
TOKEN_TILE = 512
ADALN_TILE = 1536
POST_ROW_GROUPS = 2

NT_DIMS = (((1,), (1,)), ((), ()))


def _params(n_axes):
    return pltpu.CompilerParams(dimension_semantics=("arbitrary",) * n_axes,
                                vmem_limit_bytes=VMEM_LIMIT)


def _rms(x, g):
    return x * lax.rsqrt(jnp.mean(x * x, axis=-1, keepdims=True) + EPS) * g


def _lane_first(shape):
    return lax.broadcasted_iota(jnp.int32, shape, len(shape) - 1) < HEAD_DIM


def _pair_queries(x):
    first = _lane_first(x.shape)
    return jnp.where(first, x, 0.0).astype(BF16), jnp.where(first, 0.0, x).astype(BF16)


def _pair_values(v2):
    first = _lane_first(v2.shape)
    ones, zeros = jnp.ones_like(v2), jnp.zeros_like(v2)
    r0 = jnp.concatenate([jnp.where(first, v2, 0.0), jnp.where(first, ones, zeros)], axis=1)
    r1 = jnp.concatenate([jnp.where(first, 0.0, v2), jnp.where(first, zeros, ones)], axis=1)
    return r0.astype(BF16), r1.astype(BF16)


def _pair_values_t(vt2):
    first = lax.broadcasted_iota(jnp.int32, vt2.shape, 0) < HEAD_DIM
    ones, zeros = jnp.ones_like(vt2), jnp.zeros_like(vt2)
    r0 = jnp.concatenate([jnp.where(first, vt2, 0.0), jnp.where(first, ones, zeros)], axis=0)
    r1 = jnp.concatenate([jnp.where(first, 0.0, vt2), jnp.where(first, zeros, ones)], axis=0)
    return r0.astype(BF16), r1.astype(BF16)


def _pair_scores(q_pair, blocks):
    out = []
    for a in range(2):
        scores = []
        for blk in blocks:
            if blk["k_nt"]:
                s = lax.dot_general(q_pair[a], blk["k"], NT_DIMS, preferred_element_type=F32)
            else:
                s = jnp.dot(q_pair[a], blk["k"], preferred_element_type=F32)
            if blk.get("bias") is not None:
                s = s + blk["bias"][a]
            scores.append(s)
        out.append(scores)
    return out


def _pair_finish(all_scores, blocks):
    acc = None
    for a in range(2):
        scores = all_scores[a]
        m = scores[0].max(axis=-1, keepdims=True)
        for s in scores[1:]:
            m = jnp.maximum(m, s.max(axis=-1, keepdims=True))
        for s, blk in zip(scores, blocks):
            p = jnp.exp(s - m).astype(BF16)
            if blk["r_nt"]:
                term = lax.dot_general(p, blk["r"][a], NT_DIMS, preferred_element_type=F32)
            else:
                term = jnp.dot(p, blk["r"][a], preferred_element_type=F32)
            acc = term if acc is None else acc + term
    return acc[:, :LANES] / acc[:, LANES:]


def _adaln_kernel(c_ref, w_ref, b_ref, o_ref):
    c = c_ref[...]
    a = c * jax.nn.sigmoid(c)
    o_ref[...] = jnp.dot(a.astype(BF16), w_ref[...].astype(BF16),
                         preferred_element_type=F32) + b_ref[...]


def _adaln(cvec, w_ada, b_ada):
    tn = ADALN_TILE
    return pl.pallas_call(
        _adaln_kernel,
        grid=(DEPTH, 6 * D_MODEL // tn),
        in_specs=[
            pl.BlockSpec((N_MOD_ROWS, D_MODEL), lambda l, j: (0, 0)),
            pl.BlockSpec((None, D_MODEL, tn), lambda l, j: (l, 0, j)),
            pl.BlockSpec((None, 1, tn), lambda l, j: (l, 0, j)),
        ],
        out_specs=pl.BlockSpec((None, N_MOD_ROWS, tn), lambda l, j: (l, 0, j)),
        out_shape=jax.ShapeDtypeStruct((DEPTH, N_MOD_ROWS, 6 * D_MODEL), F32),
        compiler_params=_params(2),
        name="adaln",
    )(cvec, w_ada, b_ada.reshape(DEPTH, 1, 6 * D_MODEL))


def _na_r0(r, rows):
    return min(max(r - NA_WIN_ROWS // 2, 0), rows - NA_WIN_ROWS)


def _na_union_start(p, rows):
    return min(_na_r0(NA_BLOCK_ROWS * p, rows), rows - NA_UNION_ROWS)


def _na_block_patterns(rows):
    patterns, index = [], []
    for p in range(rows // NA_BLOCK_ROWS):
        start = _na_union_start(p, rows)
        pattern = []
        for a in range(NA_BLOCK_ROWS):
            r = NA_BLOCK_ROWS * p + a
            r0 = _na_r0(r, rows)
            pattern.append(tuple(start + j - r + NA_WIN_ROWS - 1 if r0 <= start + j < r0 + NA_WIN_ROWS else None
                                 for j in range(NA_UNION_ROWS)))
        pattern = tuple(pattern)
        if pattern not in patterns:
            patterns.append(pattern)
        index.append(patterns.index(pattern))
    return patterns, index


def _na_bias_kernel(tbl_ref, o_ref, *, rows):
    l = pl.program_id(0)
    h = pl.program_id(1)
    qi = lax.broadcasted_iota(jnp.int32, (GRID_W, GRID_W), 0)
    ki = lax.broadcasted_iota(jnp.int32, (GRID_W, GRID_W), 1)
    dc = jnp.clip(ki - qi, -(NA_WIN_COLS - 1), NA_WIN_COLS - 1) + NA_WIN_COLS - 1
    c0 = jnp.clip(qi - NA_WIN_COLS // 2, 0, GRID_W - NA_WIN_COLS)
    col_ok = (ki >= c0) & (ki < c0 + NA_WIN_COLS)
    n_dr = 2 * NA_WIN_ROWS - 1
    n_dc = 2 * NA_WIN_COLS - 1
    tiles = []
    for dr in range(n_dr):
        t = jnp.zeros((GRID_W, GRID_W), F32)
        for d in range(n_dc):
            t = jnp.where(dc == d, tbl_ref[((l * NA_HEADS + h) * n_dr + dr) * n_dc + d], t)
        tiles.append(jnp.where(col_ok, t, NEG))
    neg_tile = jnp.full((GRID_W, GRID_W), NEG, F32)
    for i, pattern in enumerate(_na_block_patterns(rows)[0]):
        for a in range(NA_BLOCK_ROWS):
            for j in range(NA_UNION_ROWS):
                dr = pattern[a][j]
                tile = neg_tile if dr is None else tiles[dr]
                o_ref[i, a * GRID_W:(a + 1) * GRID_W, j * GRID_W:(j + 1) * GRID_W] = tile


def _na_bias_expand(na_bias, rows):
    n_blocks = len(_na_block_patterns(rows)[0])
    qn = NA_BLOCK_ROWS * GRID_W
    kn = NA_UNION_ROWS * GRID_W
    return pl.pallas_call(
        functools.partial(_na_bias_kernel, rows=rows),
        grid=(DEPTH, NA_HEADS),
        in_specs=[pl.BlockSpec(memory_space=pltpu.SMEM)],
        out_specs=pl.BlockSpec((None, None, n_blocks, qn, kn), lambda l, h: (l, h, 0, 0, 0)),
        out_shape=jax.ShapeDtypeStruct((DEPTH, NA_HEADS, n_blocks, qn, kn), F32),
        compiler_params=_params(2),
        name="na_bias_expand",
    )(na_bias.reshape(-1))


def _mod_row_map(layer, tiles_per_row, first_row):
    if tiles_per_row is None:
        return lambda i: (layer, first_row, 0, 0)
    return lambda i: (layer, first_row + i // tiles_per_row, 0, 0)


def _inproj_kernel(x_ref, mod_ref, g_ref, wna_ref, wgq_ref, wml_ref, wgr_ref, bgr_ref,
                   zna_ref, zgq_ref, zml_ref, gi_ref, gf_ref, grow_ref):
    x = x_ref[...]
    h = _rms(x, g_ref[0:1, :]) * (1.0 + mod_ref[1:2, :]) + mod_ref[0:1, :]
    hb = h.astype(BF16)
    zna_ref[...] = lax.dot_general(hb, wna_ref[...], NT_DIMS, preferred_element_type=F32)
    zgq_ref[...] = lax.dot_general(hb, wgq_ref[...], NT_DIMS, preferred_element_type=F32)
    zml_ref[...] = lax.dot_general(hb, wml_ref[...], NT_DIMS, preferred_element_type=F32)
    gr = lax.dot_general(wgr_ref[...], hb, NT_DIMS, preferred_element_type=F32) + bgr_ref[...]
    grow_ref[...] = gr
    gc = gr.T
    gi_ref[...] = gc[:, 0:N_SCANS]
    gf_ref[...] = gc[:, N_SCANS:]


def _inproj(x, mods, g_norm, w, layer, tiles_per_row, first_row, tm):
    n = x.shape[0]
    wspec = lambda width: pl.BlockSpec((None, width, D_MODEL), lambda i: (layer, 0, 0))
    return pl.pallas_call(
        _inproj_kernel,
        grid=(n // tm,),
        in_specs=[
            pl.BlockSpec((tm, D_MODEL), lambda i: (i, 0)),
            pl.BlockSpec((None, None, 6, D_MODEL), _mod_row_map(layer, tiles_per_row, first_row)),
            pl.BlockSpec((None, 4, D_MODEL), lambda i: (layer, 0, 0)),
            wspec(ZNA_W), wspec(ZGQ_W), wspec(ZML_W), wspec(N_GATES),
            pl.BlockSpec((None, N_GATES, 1), lambda i: (layer, 0, 0)),
        ],
        out_specs=[
            pl.BlockSpec((tm, ZNA_W), lambda i: (i, 0)),
            pl.BlockSpec((tm, ZGQ_W), lambda i: (i, 0)),
            pl.BlockSpec((tm, ZML_W), lambda i: (i, 0)),
            pl.BlockSpec((tm, N_SCANS), lambda i: (i, 0)),
            pl.BlockSpec((tm, N_SCANS), lambda i: (i, 0)),
            pl.BlockSpec((N_GATES, tm), lambda i: (0, i)),
        ],
        out_shape=[
            jax.ShapeDtypeStruct((n, ZNA_W), F32),
            jax.ShapeDtypeStruct((n, ZGQ_W), F32),
            jax.ShapeDtypeStruct((n, ZML_W), F32),
            jax.ShapeDtypeStruct((n, N_SCANS), F32),
            jax.ShapeDtypeStruct((n, N_SCANS), F32),
            jax.ShapeDtypeStruct((N_GATES, n), F32),
        ],
        compiler_params=_params(1),
        name="inproj",
    )(x, mods, g_norm, w["na"], w["gq"], w["ml"], w["gate_row"], w["b_row"])


def _post_kernel(x_ref, mna_ref, mgq_ref, mml_ref, mod_ref, g_ref, wo_ref, wgu_ref, wd_ref, o_ref):
    rows = x_ref.shape[0] // POST_ROW_GROUPS
    rs = [slice(i * rows, (i + 1) * rows) for i in range(POST_ROW_GROUPS)]
    acc = []
    for r in rs:
        a = jnp.dot(mna_ref[r, :], wo_ref[0:NA_W, :], preferred_element_type=F32)
        a += jnp.dot(mgq_ref[r, :], wo_ref[NA_W:NA_W + GQA_QW, :], preferred_element_type=F32)
        a += jnp.dot(mml_ref[r, :], wo_ref[NA_W + GQA_QW:, :], preferred_element_type=F32)
        acc.append(a)
    x1, gate_up = [], []
    for r, a in zip(rs, acc):
        xr = x_ref[r, :] + mod_ref[2:3, :] * _rms(a, g_ref[1:2, :])
        hb = (_rms(xr, g_ref[2:3, :]) * (1.0 + mod_ref[4:5, :]) + mod_ref[3:4, :]).astype(BF16)
        x1.append(xr)
        gate_up.append((jnp.dot(hb, wgu_ref[:, 0:FF_HIDDEN], preferred_element_type=F32),
                        jnp.dot(hb, wgu_ref[:, FF_HIDDEN:], preferred_element_type=F32)))
    f = []
    for gate, up in gate_up:
        act = (gate * jax.nn.sigmoid(gate) * up).astype(BF16)
        f.append(jnp.dot(act, wd_ref[...], preferred_element_type=F32))
    for r, xr, fr in zip(rs, x1, f):
        o_ref[r, :] = xr + mod_ref[5:6, :] * _rms(fr, g_ref[3:4, :])


def _post(x, mna, mgq, mml, mods, g_norm, w_out, w_gu, w_down, layer, tiles_per_row, first_row, tm):
    n = x.shape[0]
    resident = lambda shape: pl.BlockSpec((None,) + shape, lambda i: (layer, 0, 0), pipeline_mode=pl.Buffered(1))
    return pl.pallas_call(
        _post_kernel,
        grid=(n // tm,),
        in_specs=[
            pl.BlockSpec((tm, D_MODEL), lambda i: (i, 0)),
            pl.BlockSpec((tm, NA_W), lambda i: (i, 0)),
            pl.BlockSpec((tm, GQA_QW), lambda i: (i, 0)),
            pl.BlockSpec((tm, ML_W), lambda i: (i, 0)),
            pl.BlockSpec((None, None, 6, D_MODEL), _mod_row_map(layer, tiles_per_row, first_row)),
            pl.BlockSpec((None, 4, D_MODEL), lambda i: (layer, 0, 0)),
            resident((D_MODEL, D_MODEL)),
            resident((D_MODEL, 2 * FF_HIDDEN)),
            resident((FF_HIDDEN, D_MODEL)),
        ],
        out_specs=pl.BlockSpec((tm, D_MODEL), lambda i: (i, 0)),
        out_shape=jax.ShapeDtypeStruct((n, D_MODEL), F32),
        compiler_params=_params(1),
        name="post",
    )(x, mna, mgq, mml, mods, g_norm, w_out, w_gu, w_down)


def _pair_rms(x, gain, sums_on_mxu=False):
    xsq = x * x
    if sums_on_mxu:
        row = lax.broadcasted_iota(jnp.int32, (LANES, LANES), 0) < HEAD_DIM
        col = lax.broadcasted_iota(jnp.int32, (LANES, LANES), 1) < HEAD_DIM
        ones_blk = jnp.where(row == col, 1.0, 0.0).astype(BF16)
        sums = sum(jnp.dot(p, ones_blk, preferred_element_type=F32) for p in _split3(xsq))
    else:
        first = _lane_first(x.shape)
        sums = jnp.where(first, jnp.sum(jnp.where(first, xsq, 0.0), axis=-1, keepdims=True),
                         jnp.sum(jnp.where(first, 0.0, xsq), axis=-1, keepdims=True))
    return x * lax.rsqrt(sums * (1.0 / HEAD_DIM) + EPS) * gain


def _pair_gain(gqk_ref, row):
    return jnp.concatenate([gqk_ref[row:row + 1, :]] * (LANES // HEAD_DIM), axis=-1)


def _zero_other_layers(ref, layer):
    for other in range(ref.shape[0]):
        if other != layer:
            ref[other] = jnp.zeros(ref.shape[1:], ref.dtype)
    return ref.at[layer]


def _ctx_attn_kernel(zna_ref, zgq_ref, gqk_ref, *rest, layer, creates):
    mna_ref, mgq_ref, kvna_ref, kvgq_ref = rest[-4:]
    if creates:
        kvna_ref, kvgq_ref = _zero_other_layers(kvna_ref, layer), _zero_other_layers(kvgq_ref, layer)

    def store_t(ref, which, pair, x2):
        xt = x2.T
        for a in range(2):
            ref[which, 2 * pair + a] = xt[a * HEAD_DIM:(a + 1) * HEAD_DIM]

    pending = []
    for i in range(NA_HEADS // 2):
        cols = slice(i * LANES, (i + 1) * LANES)
        q2 = zna_ref[:, cols] * QK_SCALE
        k2 = zna_ref[:, NA_W + i * LANES:NA_W + (i + 1) * LANES]
        v2 = zna_ref[:, 2 * NA_W + i * LANES:2 * NA_W + (i + 1) * LANES]
        store_t(kvna_ref, 0, i, k2)
        store_t(kvna_ref, 1, i, v2)
        blocks = [{"k": k2.astype(BF16), "k_nt": True, "r": _pair_values(v2), "r_nt": False}]
        pending.append((mna_ref, cols, _pair_scores(_pair_queries(q2), blocks), blocks))

    gq, gk = _pair_gain(gqk_ref, 0), _pair_gain(gqk_ref, 1)
    k2 = _pair_rms(zgq_ref[:, GQA_QW:GQA_QW + GQA_KW], gk)
    v2 = zgq_ref[:, GQA_QW + GQA_KW:]
    store_t(kvgq_ref, 0, 0, k2)
    store_t(kvgq_ref, 1, 0, v2)
    blocks = [{"k": k2.astype(BF16), "k_nt": True, "r": _pair_values(v2), "r_nt": False}]
    for p in range(GQA_GROUP):
        cols = slice(p * LANES, (p + 1) * LANES)
        q2 = _pair_rms(zgq_ref[:, cols], gq) * QK_SCALE
        pending.append((mgq_ref, cols, _pair_scores(_pair_queries(q2), blocks), blocks))

    for ref, cols, scores, blocks in pending:
        ref[:, cols] = _pair_finish(scores, blocks).astype(BF16)


def _ctx_attn(zna, zgq, g_qk, kv_na_buf, kv_gq_buf, layer, batch, t):
    n = zna.shape[0]
    creates = kv_na_buf is None
    in_specs = [
        pl.BlockSpec((t, ZNA_W), lambda b: (b, 0)),
        pl.BlockSpec((t, ZGQ_W), lambda b: (b, 0)),
        pl.BlockSpec((None, 2, HEAD_DIM), lambda b: (layer, 0, 0)),
    ]
    args = [zna, zgq, g_qk]
    if creates:
        kv_spec = lambda heads: pl.BlockSpec((None, DEPTH, 2, heads, HEAD_DIM, t), lambda b: (b, 0, 0, 0, 0, 0))
        aliases = {}
    else:
        kv_spec = lambda heads: pl.BlockSpec((None, None, 2, heads, HEAD_DIM, t),
                                             lambda b: (b, layer, 0, 0, 0, 0))
        aliases = {len(args): 2, len(args) + 1: 3}
        in_specs += [pl.BlockSpec(memory_space=pl.ANY)] * 2
        args += [kv_na_buf, kv_gq_buf]
    return pl.pallas_call(
        functools.partial(_ctx_attn_kernel, layer=layer, creates=creates),
        grid=(batch,),
        in_specs=in_specs,
        out_specs=[
            pl.BlockSpec((t, NA_W), lambda b: (b, 0)),
            pl.BlockSpec((t, GQA_QW), lambda b: (b, 0)),
            kv_spec(NA_HEADS),
            kv_spec(GQA_KV_HEADS),
        ],
        out_shape=[
            jax.ShapeDtypeStruct((n, NA_W), BF16),
            jax.ShapeDtypeStruct((n, GQA_QW), BF16),
            jax.ShapeDtypeStruct((batch, DEPTH, 2, NA_HEADS, HEAD_DIM, t), F32),
            jax.ShapeDtypeStruct((batch, DEPTH, 2, GQA_KV_HEADS, HEAD_DIM, t), F32),
        ],
        input_output_aliases=aliases,
        compiler_params=_params(1),
        name="ctx_attn",
    )(*args)


def _lat_na_kernel(zna_ref, cache_ref, bias_ref, o_ref, *, rows):
    qn = NA_BLOCK_ROWS * GRID_W
    kn = NA_UNION_ROWS * GRID_W
    n_blocks = rows // NA_BLOCK_ROWS
    pattern_of = _na_block_patterns(rows)[1]
    prepared = []
    for i in range(NA_HEADS // 2):
        cols = slice(i * LANES, (i + 1) * LANES)
        q_pair = _pair_queries(zna_ref[:, cols] * QK_SCALE)
        k2 = zna_ref[:, NA_W + i * LANES:NA_W + (i + 1) * LANES].astype(BF16)
        kc = cache_ref[0, i].astype(BF16)
        scores = []
        for a in range(2):
            s_ctx = jnp.dot(q_pair[a], kc, preferred_element_type=F32)
            s_win = []
            for p in range(n_blocks):
                k0 = _na_union_start(p, rows) * GRID_W
                s_win.append(lax.dot_general(q_pair[a][p * qn:(p + 1) * qn], k2[k0:k0 + kn], NT_DIMS,
                                             preferred_element_type=F32) + bias_ref[2 * i + a, pattern_of[p]])
            scores.append((s_ctx, s_win))
        prepared.append(scores)
    for i in range(NA_HEADS // 2):
        cols = slice(i * LANES, (i + 1) * LANES)
        r = _pair_values(zna_ref[:, 2 * NA_W + i * LANES:2 * NA_W + (i + 1) * LANES])
        rc = _pair_values_t(cache_ref[1, i])
        acc = None
        for a in range(2):
            s_ctx, s_win = prepared[i][a]
            m_win = jnp.concatenate([s.max(axis=-1, keepdims=True) for s in s_win], axis=0)
            m = jnp.maximum(m_win, s_ctx.max(axis=-1, keepdims=True))
            term = lax.dot_general(jnp.exp(s_ctx - m).astype(BF16), rc[a], NT_DIMS, preferred_element_type=F32)
            wins = []
            for p in range(n_blocks):
                k0 = _na_union_start(p, rows) * GRID_W
                pw = jnp.exp(s_win[p] - m[p * qn:(p + 1) * qn]).astype(BF16)
                wins.append(jnp.dot(pw, r[a][k0:k0 + kn], preferred_element_type=F32))
            term = term + jnp.concatenate(wins, axis=0)
            acc = term if acc is None else acc + term
        o_ref[:, cols] = (acc[:, :LANES] / acc[:, LANES:]).astype(BF16)


def _lat_na(zna, cache_t, bias, layer, batch, t):
    n = zna.shape[0]
    rows = t // GRID_W
    past = cache_t.shape[-1]
    n_blocks = len(_na_block_patterns(rows)[0])
    qn = NA_BLOCK_ROWS * GRID_W
    kn = NA_UNION_ROWS * GRID_W
    return pl.pallas_call(
        functools.partial(_lat_na_kernel, rows=rows),
        grid=(batch,),
        in_specs=[
            pl.BlockSpec((t, ZNA_W), lambda b: (b, 0)),
            pl.BlockSpec((None, None, 2, NA_HEADS // 2, LANES, past), lambda b: (b, layer, 0, 0, 0, 0)),
            pl.BlockSpec((None, NA_HEADS, n_blocks, qn, kn), lambda b: (layer, 0, 0, 0, 0)),
        ],
        out_specs=pl.BlockSpec((t, NA_W), lambda b: (b, 0)),
        out_shape=jax.ShapeDtypeStruct((n, NA_W), BF16),
        compiler_params=_params(1),
        name="lat_na",
    )(zna, cache_t, bias)


def _rope_tables(t):
    half = HEAD_DIM // 2
    quarter = half // 2
    inv = 1.0 / (ROPE_BASE ** (jnp.arange(quarter, dtype=F32) / quarter))
    tt = jnp.arange(t)
    row = (tt // GRID_W).astype(F32)
    col = (tt % GRID_W).astype(F32)
    ang_r = row[:, None] * inv[None, :]
    ang_c = col[:, None] * inv[None, :]
    cos = jnp.concatenate([jnp.cos(ang_r)] * 2 + [jnp.cos(ang_c)] * 2, axis=-1)
    sin = jnp.concatenate([-jnp.sin(ang_r), jnp.sin(ang_r), -jnp.sin(ang_c), jnp.sin(ang_c)], axis=-1)
    reps = LANES // HEAD_DIM
    return jnp.tile(cos, (1, reps)), jnp.tile(sin, (1, reps))


def _pair_rope(xn, cos, sin):
    quarter = HEAD_DIM // 4
    lane = lax.broadcasted_iota(jnp.int32, xn.shape, 1)
    lower = (lane & (2 * quarter - 1)) < quarter
    partner = jnp.where(lower, pltpu.roll(xn, LANES - quarter, 1), pltpu.roll(xn, quarter, 1))
    return xn * cos + partner * sin


def _lat_gqa_kernel(zgq_ref, cache_ref, gqk_ref, cos_ref, sin_ref, o_ref):
    cos = cos_ref[...]
    sin = sin_ref[...]
    gq, gk = _pair_gain(gqk_ref, 0), _pair_gain(gqk_ref, 1)
    keys = _pair_rope(_pair_rms(zgq_ref[:, GQA_QW:GQA_QW + GQA_KW], gk, True), cos, sin).astype(BF16)
    blocks = [
        {"k": keys, "k_nt": True, "r": _pair_values(zgq_ref[:, GQA_QW + GQA_KW:]), "r_nt": False},
        {"k": cache_ref[0].astype(BF16), "k_nt": False, "r": _pair_values_t(cache_ref[1]), "r_nt": True},
    ]
    def scores(p):
        x = _pair_rope(_pair_rms(zgq_ref[:, p * LANES:(p + 1) * LANES], gq, True), cos, sin) * QK_SCALE
        return _pair_scores(_pair_queries(x), blocks)

    nxt = scores(0)
    for p in range(GQA_GROUP):
        cur = nxt
        if p + 1 < GQA_GROUP:
            nxt = scores(p + 1)
        o_ref[:, p * LANES:(p + 1) * LANES] = _pair_finish(cur, blocks).astype(BF16)


def _lat_gqa(zgq, cache_t, g_qk, cos, sin, layer, batch, t):
    n = zgq.shape[0]
    past = cache_t.shape[-1]
    return pl.pallas_call(
        _lat_gqa_kernel,
        grid=(batch,),
        in_specs=[
            pl.BlockSpec((t, ZGQ_W), lambda b: (b, 0)),
            pl.BlockSpec((None, None, 2, GQA_KW, past), lambda b: (b, layer, 0, 0, 0)),
            pl.BlockSpec((None, 2, HEAD_DIM), lambda b: (layer, 0, 0)),
            pl.BlockSpec((t, LANES), lambda b: (0, 0)),
            pl.BlockSpec((t, LANES), lambda b: (0, 0)),
        ],
        out_specs=pl.BlockSpec((t, GQA_QW), lambda b: (b, 0)),
        out_shape=jax.ShapeDtypeStruct((n, GQA_QW), BF16),
        compiler_params=_params(1),
        name="lat_gqa",
    )(zgq, cache_t, g_qk, cos, sin)


def _split3(x):
    x1 = x.astype(BF16)
    r1 = x - x1.astype(F32)
    x2 = r1.astype(BF16)
    x3 = (r1 - x2.astype(F32)).astype(BF16)
    return x1, x2, x3


def _log_sigmoid(x):
    return jnp.minimum(x, 0.0) - jnp.log1p(jnp.exp(-jnp.abs(x)))


def _mlstm_select_matrix():
    H = MLSTM_HEADS
    sel = np.zeros((MLSTM_SEL_ROWS, (H // 2) * 2 * MLSTM_TILE_KINDS * LANES), np.float32)
    for j in range(H // 2):
        for d in range(2):
            for q in range(MLSTM_TILE_KINDS):
                for a in range(2):
                    col0 = ((j * 2 + d) * MLSTM_TILE_KINDS + q) * LANES + a * HEAD_DIM
                    sel[q * N_SCANS + d * H + 2 * j + a, col0:col0 + HEAD_DIM] = 1.0
    return sel


def _mlstm_kernel(zml_ref, gi_ref, gf_ref, grow_ref, c0_ref, n0_ref, m0_ref, gml_ref, sel_ref, *rest,
                  t, emit_state, creates_layer):
    tri_s, st_s, dst_s, cst_s, row_s = rest[-5:]
    outs = rest[:-5]
    if emit_state:
        o_ref, cf_ref, nf_ref, mf_ref = outs[-4:]
        if creates_layer is not None:
            cf_ref, nf_ref, mf_ref = (_zero_other_layers(r, creates_layer) for r in (cf_ref, nf_ref, mf_ref))
    else:
        o_ref = outs[-1]
    L = MLSTM_CHUNK
    H = MLSTM_HEADS
    HD = HEAD_DIM
    NP = H // 2
    nc = t // L
    tb = tri_s.shape[-1]

    @pl.when(pl.program_id(0) == 0)
    def _():
        ti = lax.broadcasted_iota(jnp.int32, (tb, tb), 0)
        ui = lax.broadcasted_iota(jnp.int32, (tb, tb), 1)
        same = (ti & -L) == (ui & -L)
        tri_s[0] = jnp.where(same & (ui <= ti), 1.0, 0.0).astype(BF16)
        tri_s[1] = jnp.where(same & (ui >= ti), 1.0, 0.0).astype(BF16)

    lower, upper = tri_s[0], tri_s[1]

    def chunk_sums_cols(x):
        parts = _split3(x)
        pre, suf = [], []
        for i in range(t // tb):
            blk = [p[i * tb:(i + 1) * tb] for p in parts]
            pre.append(sum(jnp.dot(lower, p, preferred_element_type=F32) for p in blk))
            suf.append(sum(jnp.dot(upper, p, preferred_element_type=F32) for p in blk))
        return jnp.concatenate(pre, axis=0), jnp.concatenate(suf, axis=0)

    def chunk_sums_rows(x):
        parts = _split3(x)
        pre, suf = [], []
        for i in range(t // tb):
            blk = [p[:, i * tb:(i + 1) * tb] for p in parts]
            pre.append(sum(jnp.dot(p, upper, preferred_element_type=F32) for p in blk))
            suf.append(sum(jnp.dot(p, lower, preferred_element_type=F32) for p in blk))
        return jnp.concatenate(pre, axis=1), jnp.concatenate(suf, axis=1)

    pre_c, suf_c = chunk_sums_cols(_log_sigmoid(gf_ref[...]))
    lane_c = lax.broadcasted_iota(jnp.int32, (t, N_SCANS), 1)
    b3 = jnp.where(lane_c < H, pre_c, suf_c).reshape(nc, L, N_SCANS)
    i3 = gi_ref[...].reshape(nc, L, N_SCANS)
    fwd3 = lax.broadcasted_iota(jnp.int32, (nc, 1, N_SCANS), 2) < H
    b_end3 = jnp.where(fwd3, b3[:, L - 1:L, :], b3[:, 0:1, :])
    lw_end3 = b_end3 - b3 + i3
    a3 = jnp.max(lw_end3, axis=1, keepdims=True)
    wloc3 = jnp.exp(lw_end3 - a3)

    fwd1 = lax.broadcasted_iota(jnp.int32, (1, N_SCANS), 1) < H
    m = m0_ref[...]
    m_start, carry_decay, contrib_scale = [], [], []
    for j in range(nc):
        a_j = jnp.where(fwd1, a3[j], a3[nc - 1 - j])
        g_j = jnp.where(fwd1, b_end3[j], b_end3[nc - 1 - j])
        m_start.append(m)
        m_next = jnp.maximum(g_j + m, a_j)
        carry_decay.append(jnp.exp(g_j + m - m_next))
        contrib_scale.append(jnp.exp(a_j - m_next))
        m = m_next
    mst3 = jnp.concatenate([jnp.where(fwd1, m_start[c], m_start[nc - 1 - c])[None] for c in range(nc)], axis=0)

    cols = jnp.concatenate([b3.reshape(t, N_SCANS), wloc3.reshape(t, N_SCANS)], axis=1)
    tiles_all = sum(jnp.dot(p, sel_ref[...], preferred_element_type=F32) for p in _split3(cols))
    tile_w = MLSTM_TILE_KINDS * LANES

    def tiles(j, d):
        x = tiles_all[:, (2 * j + d) * tile_w:(2 * j + d + 1) * tile_w]
        return [x[:, q * LANES:(q + 1) * LANES].reshape(nc, L, LANES) for q in range(MLSTM_TILE_KINDS)]

    gr = grow_ref[...]
    pre_r, suf_r = chunk_sums_rows(_log_sigmoid(gr))
    sub_r = lax.broadcasted_iota(jnp.int32, (N_SCANS, t), 0)
    rowv = gr[0:N_SCANS] - jnp.where(sub_r < H, pre_r[N_SCANS:], suf_r[N_SCANS:])
    for j in range(NP):
        for d in range(2):
            e = d * H + 2 * j
            for c in range(nc):
                row_s[2 * j + d, c] = jnp.concatenate(
                    [rowv[e:e + 1, c * L:(c + 1) * L], rowv[e + 1:e + 2, c * L:(c + 1) * L]], axis=1)

    lane_a = lax.broadcasted_iota(jnp.int32, (1, 1, LANES), 2) < HD
    sub_a = lax.broadcasted_iota(jnp.int32, (1, 2 * HD, 1), 1) < HD
    diag = sub_a == lane_a
    diag4 = jnp.concatenate([diag] * 4, axis=2)

    def stack_heads(x3):
        return jnp.concatenate([jnp.where(lane_a, x3, 0.0), jnp.where(lane_a, 0.0, x3)], axis=1)

    def pair_cols(base, j):
        return slice(base + j * LANES, base + (j + 1) * LANES)

    zero_blk = jnp.zeros((HD, HD), F32)
    for j in range(NP):
        cols_d = []
        for d in range(2):
            ca, cb = c0_ref[d, 2 * j].T, c0_ref[d, 2 * j + 1].T
            na = jnp.broadcast_to(n0_ref[d, 2 * j:2 * j + 1, :], (HD, HD)).T
            nb = jnp.broadcast_to(n0_ref[d, 2 * j + 1:2 * j + 2, :], (HD, HD)).T
            top = jnp.concatenate([ca, zero_blk, na, zero_blk], axis=1)
            bot = jnp.concatenate([zero_blk, cb, zero_blk, nb], axis=1)
            cols_d.append(jnp.concatenate([top, bot], axis=0))
        st_s[j] = jnp.concatenate(cols_d, axis=1)

    for j in range(NP):
        k3 = (zml_ref[:, pair_cols(ML_W, j)] * QK_SCALE).reshape(nc, L, LANES).astype(BF16)
        v3 = zml_ref[:, pair_cols(2 * ML_W, j)].reshape(nc, L, LANES)
        rhs = []
        for d in range(2):
            wl = tiles(j, d)[1]
            rhs += [v3 * wl, wl]
        rhs = jnp.concatenate(rhs, axis=2).astype(BF16)
        contrib = jnp.einsum("csk,csn->ckn", k3, rhs, preferred_element_type=F32)
        dst_s[j] = jnp.where(diag4, contrib, 0.0)

    def lane_scale(v, j):
        pieces = []
        for d in range(2):
            sa = jnp.broadcast_to(v[:, d * H + 2 * j:d * H + 2 * j + 1], (1, HD))
            sb = jnp.broadcast_to(v[:, d * H + 2 * j + 1:d * H + 2 * j + 2], (1, HD))
            pieces += [sa, sb, sa, sb]
        return jnp.concatenate(pieces, axis=1)

    for j in range(NP):
        st = st_s[j]
        for step in range(nc):
            cb = nc - 1 - step
            stb = st.astype(BF16)
            cst_s[j, step, :, 0:2 * LANES] = stb[:, 0:2 * LANES]
            cst_s[j, cb, :, 2 * LANES:] = stb[:, 2 * LANES:]
            delta = jnp.concatenate([dst_s[j, step, :, 0:2 * LANES], dst_s[j, cb, :, 2 * LANES:]], axis=1)
            st = lane_scale(carry_decay[step], j) * st + lane_scale(contrib_scale[step], j) * delta
        st_s[j] = st

    sidx = lax.broadcasted_iota(jnp.int32, (1, L, LANES), 2) & (HD - 1)
    tidx = lax.broadcasted_iota(jnp.int32, (1, L, LANES), 1)
    masks = (sidx <= tidx, sidx >= tidx)
    ones_blk = jnp.broadcast_to(jnp.where(diag, 1.0, 0.0).astype(BF16), (nc, 2 * HD, LANES))
    neg_inf = -jnp.inf
    for j in range(NP):
        q3 = zml_ref[:, pair_cols(0, j)].reshape(nc, L, LANES).astype(BF16)
        k3 = (zml_ref[:, pair_cols(ML_W, j)] * QK_SCALE).reshape(nc, L, LANES)
        v3 = zml_ref[:, pair_cols(2 * ML_W, j)].reshape(nc, L, LANES)
        qk = jnp.einsum("ctd,cnd->ctn", q3, stack_heads(k3).astype(BF16), preferred_element_type=F32)
        v_aug = jnp.concatenate([stack_heads(v3).astype(BF16), ones_blk], axis=2)
        out = None
        for d in range(2):
            b_t = tiles(j, d)[0]
            e = d * H + 2 * j
            bm_t = b_t + jnp.where(lane_a, mst3[:, :, e:e + 1], mst3[:, :, e + 1:e + 2])
            logw = jnp.where(masks[d], b_t + row_s[2 * j + d], neg_inf)
            rmax_a = jnp.max(jnp.where(lane_a, logw, neg_inf), axis=-1, keepdims=True)
            rmax_b = jnp.max(jnp.where(lane_a, neg_inf, logw), axis=-1, keepdims=True)
            m_t = jnp.maximum(jnp.where(lane_a, rmax_a, rmax_b), bm_t)
            s = qk * jnp.exp(logw - m_t)
            decay = jnp.exp(bm_t - m_t)
            sv = jnp.einsum("cts,csn->ctn", s.astype(BF16), v_aug, preferred_element_type=F32)
            state = cst_s[j, :, :, 2 * d * LANES:2 * (d + 1) * LANES]
            inter = jnp.einsum("ctk,ckn->ctn", q3, state, preferred_element_type=F32)
            num = sv[:, :, 0:LANES] + decay * inter[:, :, 0:LANES]
            den = sv[:, :, LANES:] + decay * inter[:, :, LANES:]
            h_d = num / jnp.maximum(jnp.abs(den), jnp.exp(-m_t))
            out = h_d if out is None else out + h_d
        cols = pair_cols(0, j)
        og = jax.nn.sigmoid(zml_ref[:, pair_cols(3 * ML_W, j)])
        o_ref[:, cols] = (_pair_rms(out.reshape(t, LANES), gml_ref[:, cols]) * og).astype(BF16)

    if emit_state:
        for j in range(NP):
            st = st_s[j]
            for d in range(2):
                for a in range(2):
                    rows = slice(a * HD, (a + 1) * HD)
                    c0 = 2 * d * LANES + a * HD
                    cf_ref[d, 2 * j + a] = st[rows, c0:c0 + HD].T
                    nf_ref[d, 2 * j + a:2 * j + a + 1, :] = st[rows, c0 + LANES:c0 + LANES + HD].T[0:1, :]
        mf_ref[...] = m


def _mlstm(zml, gi, gf, grow, c0, n0, m0, g_ml, layer, batch, t, state_out, state_layer):
    emit_state = state_out is not None
    creates = emit_state and not isinstance(state_out, tuple)
    n = zml.shape[0]
    H = MLSTM_HEADS
    L = MLSTM_CHUNK
    nc = t // L
    sel = jnp.asarray(_mlstm_select_matrix(), BF16)
    tri_block = min(t, MXU_DIM)
    assert L & (L - 1) == 0 and tri_block % L == 0 and t % tri_block == 0
    if state_layer is None:
        c_spec = pl.BlockSpec((None, 2, H, HEAD_DIM, HEAD_DIM), lambda b: (0, 0, 0, 0, 0))
        n_spec = pl.BlockSpec((None, 2, H, HEAD_DIM), lambda b: (0, 0, 0, 0))
        m_spec = pl.BlockSpec((None, 1, N_SCANS), lambda b: (0, 0, 0))
    else:
        c_spec = pl.BlockSpec((None, None, 2, H, HEAD_DIM, HEAD_DIM), lambda b: (b, state_layer, 0, 0, 0, 0))
        n_spec = pl.BlockSpec((None, None, 2, H, HEAD_DIM), lambda b: (b, state_layer, 0, 0, 0))
        m_spec = pl.BlockSpec((None, None, 1, N_SCANS), lambda b: (b, state_layer, 0, 0))
    in_specs = [
        pl.BlockSpec((t, ZML_W), lambda b: (b, 0)),
        pl.BlockSpec((t, N_SCANS), lambda b: (b, 0)),
        pl.BlockSpec((t, N_SCANS), lambda b: (b, 0)),
        pl.BlockSpec((N_GATES, t), lambda b: (0, b)),
        c_spec, n_spec, m_spec,
        pl.BlockSpec((None, 1, ML_W), lambda b: (layer, 0, 0)),
        pl.BlockSpec(sel.shape, lambda b: (0, 0)),
    ]
    args = [zml, gi, gf, grow, c0, n0, m0, g_ml, sel]
    out_specs = [pl.BlockSpec((t, ML_W), lambda b: (b, 0))]
    out_shape = [jax.ShapeDtypeStruct((n, ML_W), BF16)]
    aliases = {}
    if emit_state:
        tails = [(2, H, HEAD_DIM, HEAD_DIM), (2, H, HEAD_DIM), (1, N_SCANS)]
        for tail in tails:
            zeros = (0,) * len(tail)
            if creates:
                out_specs.append(pl.BlockSpec((None, DEPTH) + tail, lambda b, zeros=zeros: (b, 0) + zeros))
            else:
                out_specs.append(pl.BlockSpec((None, None) + tail, lambda b, zeros=zeros: (b, layer) + zeros))
            out_shape.append(jax.ShapeDtypeStruct((batch, DEPTH) + tail, F32))
        if not creates:
            aliases = {len(args) + i: 1 + i for i in range(len(state_out))}
            in_specs += [pl.BlockSpec(memory_space=pl.ANY)] * len(state_out)
            args += list(state_out)
    return pl.pallas_call(
        functools.partial(_mlstm_kernel, t=t, emit_state=emit_state, creates_layer=layer if creates else None),
        grid=(batch,),
        in_specs=in_specs,
        out_specs=out_specs,
        out_shape=out_shape,
        input_output_aliases=aliases,
        scratch_shapes=[
            pltpu.VMEM((2, tri_block, tri_block), BF16),
            pltpu.VMEM((H // 2, 2 * HEAD_DIM, 4 * LANES), F32),
            pltpu.VMEM((H // 2, nc, 2 * HEAD_DIM, 4 * LANES), F32),
            pltpu.VMEM((H // 2, nc, 2 * HEAD_DIM, 4 * LANES), BF16),
            pltpu.VMEM((H, nc, 1, LANES), F32),
        ],
        compiler_params=_params(1),
        name="mlstm",
    )(*args)


def _layer_path(x, mods, layer, first_row, tiles_row_tokens, weights, mixers):
    per_row = None if tiles_row_tokens is None else tiles_row_tokens // TOKEN_TILE
    g_norm = weights["g_norm"]
    zna, zgq, zml, gi, gf, grow = _inproj(x, mods, g_norm, weights["w_in"], layer, per_row, first_row, TOKEN_TILE)
    mna, mgq, mml, extra = mixers(zna, zgq, zml, gi, gf, grow)
    x = _post(x, mna, mgq, mml, mods, g_norm, weights["w_out"], weights["w_gu"], weights["w_down"], layer,
              per_row, first_row, TOKEN_TILE)
    return x, extra


def _gqa_pair_order():
    return [a * GQA_GROUP + p for p in range(GQA_GROUP) for a in range(GQA_KV_HEADS)]


def _take_blocks(x, axis, base, width, order):
    return jnp.concatenate([lax.slice_in_dim(x, base + width * o, base + width * (o + 1), axis=axis)
                            for o in order], axis=axis)


def kernel(x_prompt, x_sample, cache_na_kv, cache_gqa_kv, state_mlstm_C, state_mlstm_n, state_mlstm_m,
           c, c_ctx, w_in, b_gates, w_out, g_norm, g_qk, g_mlstm, na_bias, w_ada, b_ada, w_gu, w_down):
    batch, seq, _ = x_prompt.shape
    dec_batch, dec_seq, _ = x_sample.shape
    past = cache_na_kv.shape[-2]
    assert dec_batch + 1 <= N_MOD_ROWS and dec_seq % GRID_W == 0 and GQA_KV_HEADS == 2

    cvec = jnp.concatenate([c_ctx[None, :], c, jnp.zeros((N_MOD_ROWS - 1 - dec_batch, D_MODEL), F32)], axis=0)
    mods = _adaln(cvec, w_ada, b_ada).reshape(DEPTH, N_MOD_ROWS, 6, D_MODEL)
    bias = _na_bias_expand(na_bias, dec_seq // GRID_W)
    cos, sin = _rope_tables(dec_seq)

    pair_order = _gqa_pair_order()
    scan_order = [2 * d for d in range(2)]
    H = MLSTM_HEADS
    w_in_t = jnp.swapaxes(w_in, 1, 2)
    w_gi = _take_blocks(w_in_t, 1, OFF_GATES, H, scan_order)
    w_gf = _take_blocks(w_in_t, 1, OFF_GATES + H, H, scan_order)
    b_gi = _take_blocks(b_gates, 1, 0, H, scan_order)
    b_gf = _take_blocks(b_gates, 1, H, H, scan_order)
    w_gq = jnp.concatenate([_take_blocks(w_in_t, 1, OFF_GQ, HEAD_DIM, pair_order),
                            w_in_t[:, OFF_GQ + GQA_QW:OFF_ML]], axis=1)
    w_out_rows = jnp.concatenate([w_out[:, :NA_W], _take_blocks(w_out, 1, NA_W, HEAD_DIM, pair_order),
                                  w_out[:, NA_W + GQA_QW:]], axis=1)
    weights = {
        "g_norm": g_norm,
        "w_in": {
            "na": w_in_t[:, :OFF_GQ].astype(BF16),
            "gq": w_gq.astype(BF16),
            "ml": w_in_t[:, OFF_ML:OFF_GATES].astype(BF16),
            "gate_row": jnp.concatenate([w_gi, w_gf], axis=1).astype(BF16),
            "b_row": jnp.concatenate([b_gi, b_gf], axis=-1)[:, :, None],
        },
        "w_out": w_out_rows.astype(BF16),
        "w_gu": w_gu.astype(BF16),
        "w_down": w_down.astype(BF16),
    }
    g_ml = g_mlstm.reshape(DEPTH, 1, ML_W)
    zero_c = jnp.zeros((1, 2, MLSTM_HEADS, HEAD_DIM, HEAD_DIM), F32)
    zero_n = jnp.zeros((1, 2, MLSTM_HEADS, HEAD_DIM), F32)
    zero_m = jnp.zeros((1, 1, N_SCANS), F32)
    m0_lat = state_mlstm_m.reshape(dec_batch, DEPTH, 1, N_SCANS)
    cache_na_t = jnp.swapaxes(cache_na_kv, -1, -2).reshape(dec_batch, DEPTH, 2, NA_HEADS // 2, LANES, past)
    cache_gq_t = jnp.swapaxes(cache_gqa_kv, -1, -2).reshape(dec_batch, DEPTH, 2, GQA_KW, past)

    xp = x_prompt.reshape(batch * seq, D_MODEL)
    xs = x_sample.reshape(dec_batch * dec_seq, D_MODEL)
    kv_na = kv_gq = None
    states = "create"
    for layer in range(DEPTH):
        def ctx_mixers(zna, zgq, zml, gi, gf, grow, layer=layer, kv_na=kv_na, kv_gq=kv_gq, states=states):
            mna, mgq, kv_na, kv_gq = _ctx_attn(zna, zgq, g_qk, kv_na, kv_gq, layer, batch, seq)
            mml, *states = _mlstm(zml, gi, gf, grow, zero_c, zero_n, zero_m, g_ml, layer, batch, seq,
                                  states, None)
            return mna, mgq, mml, (kv_na, kv_gq, tuple(states))

        def lat_mixers(zna, zgq, zml, gi, gf, grow, layer=layer):
            mna = _lat_na(zna, cache_na_t, bias, layer, dec_batch, dec_seq)
            mgq = _lat_gqa(zgq, cache_gq_t, g_qk, cos, sin, layer, dec_batch, dec_seq)
            (mml,) = _mlstm(zml, gi, gf, grow, state_mlstm_C, state_mlstm_n, m0_lat, g_ml, layer,
                            dec_batch, dec_seq, None, layer)
            return mna, mgq, mml, None

        xp, (kv_na, kv_gq, states) = _layer_path(xp, mods, layer, 0, None, weights, ctx_mixers)
        xs, _ = _layer_path(xs, mods, layer, 1, dec_seq, weights, lat_mixers)

    new_c, new_n, new_m = states
    return (xp.reshape(batch, seq, D_MODEL), xs.reshape(dec_batch, dec_seq, D_MODEL),
            jnp.swapaxes(kv_na, -1, -2), jnp.swapaxes(kv_gq, -1, -2),
            new_c, new_n, new_m.reshape(batch, DEPTH, 2, MLSTM_HEADS))
```

```python
import functools

import jax
import jax.numpy as jnp
import numpy as np
from jax import lax
from jax.experimental import pallas as pl
from jax.experimental.pallas import tpu as pltpu

D_MODEL = 1024
DEPTH = 4
GRID_W = 64
HEAD_DIM = 64
NA_HEADS = 4
GQA_Q_HEADS = 8
GQA_KV_HEADS = 2
GQA_GROUP = GQA_Q_HEADS // GQA_KV_HEADS
MLSTM_HEADS = 4
NA_WIN_ROWS = 8
NA_WIN_COLS = 16
MLSTM_CHUNK = 64
ROPE_BASE = 10000.0
EPS = 1e-6
NEG = -1e30
NA_W = NA_HEADS * HEAD_DIM
GQA_QW = GQA_Q_HEADS * HEAD_DIM
GQA_KW = GQA_KV_HEADS * HEAD_DIM
ML_W = MLSTM_HEADS * HEAD_DIM
N_GATES = 4 * MLSTM_HEADS
N_SCANS = 2 * MLSTM_HEADS
MLSTM_TILE_KINDS = 2
MLSTM_SEL_ROWS = MLSTM_TILE_KINDS * N_SCANS
FF_HIDDEN = ((8 * D_MODEL + 3 * 256 - 1) // (3 * 256)) * 256
QK_SCALE = HEAD_DIM ** -0.5

ZNA_W = 3 * NA_W
ZGQ_W = GQA_QW + 2 * GQA_KW
ZML_W = 4 * ML_W
OFF_GQ = ZNA_W
OFF_ML = ZNA_W + ZGQ_W
OFF_GATES = OFF_ML + ZML_W

LANES = 128
MXU_DIM = 256
N_MOD_ROWS = 16
NA_BLOCK_ROWS = 2
NA_UNION_ROWS = NA_WIN_ROWS + NA_BLOCK_ROWS - 1

F32 = jnp.float32
BF16 = jnp.bfloat16
VMEM_LIMIT = 52 * 1024 * 1024
TOKEN_TILE = 512
ADALN_TILE = 1536
POST_ROW_GROUPS = 2

NT_DIMS = (((1,), (1,)), ((), ()))


def _params(n_axes):
    return pltpu.CompilerParams(dimension_semantics=("arbitrary",) * n_axes,
                                vmem_limit_bytes=VMEM_LIMIT)


def _rms(x, g):
    return x * lax.rsqrt(jnp.mean(x * x, axis=-1, keepdims=True) + EPS) * g


def _lane_first(shape):
    return lax.broadcasted_iota(jnp.int32, shape, len(shape) - 1) < HEAD_DIM


def _pair_queries(x):
    first = _lane_first(x.shape)
    return jnp.where(first, x, 0.0).astype(BF16), jnp.where(first, 0.0, x).astype(BF16)


def _pair_values(v2):
    first = _lane_first(v2.shape)
    ones, zeros = jnp.ones_like(v2), jnp.zeros_like(v2)
    r0 = jnp.concatenate([jnp.where(first, v2, 0.0), jnp.where(first, ones, zeros)], axis=1)
    r1 = jnp.concatenate([jnp.where(first, 0.0, v2), jnp.where(first, zeros, ones)], axis=1)
    return r0.astype(BF16), r1.astype(BF16)


def _pair_values_t(vt2):
    first = lax.broadcasted_iota(jnp.int32, vt2.shape, 0) < HEAD_DIM
    ones, zeros = jnp.ones_like(vt2), jnp.zeros_like(vt2)
    r0 = jnp.concatenate([jnp.where(first, vt2, 0.0), jnp.where(first, ones, zeros)], axis=0)
    r1 = jnp.concatenate([jnp.where(first, 0.0, vt2), jnp.where(first, zeros, ones)], axis=0)
    return r0.astype(BF16), r1.astype(BF16)


def _pair_scores(q_pair, blocks):
    out = []
    for a in range(2):
        scores = []
        for blk in blocks:
            if blk["k_nt"]:
                s = lax.dot_general(q_pair[a], blk["k"], NT_DIMS, preferred_element_type=F32)
            else:
                s = jnp.dot(q_pair[a], blk["k"], preferred_element_type=F32)
            if blk.get("bias") is not None:
                s = s + blk["bias"][a]
            scores.append(s)
        out.append(scores)
    return out


def _pair_finish(all_scores, blocks):
    acc = None
    for a in range(2):
        scores = all_scores[a]
        m = scores[0].max(axis=-1, keepdims=True)
        for s in scores[1:]:
            m = jnp.maximum(m, s.max(axis=-1, keepdims=True))
        for s, blk in zip(scores, blocks):
            p = jnp.exp(s - m).astype(BF16)
            if blk["r_nt"]:
                term = lax.dot_general(p, blk["r"][a], NT_DIMS, preferred_element_type=F32)
            else:
                term = jnp.dot(p, blk["r"][a], preferred_element_type=F32)
            acc = term if acc is None else acc + term
    return acc[:, :LANES] / acc[:, LANES:]


def _adaln_kernel(c_ref, w_ref, b_ref, o_ref):
    c = c_ref[...]
    a = c * jax.nn.sigmoid(c)
    o_ref[...] = jnp.dot(a.astype(BF16), w_ref[...].astype(BF16),
                         preferred_element_type=F32) + b_ref[...]


def _adaln(cvec, w_ada, b_ada):
    tn = ADALN_TILE
    return pl.pallas_call(
        _adaln_kernel,
        grid=(DEPTH, 6 * D_MODEL // tn),
        in_specs=[
            pl.BlockSpec((N_MOD_ROWS, D_MODEL), lambda l, j: (0, 0)),
            pl.BlockSpec((None, D_MODEL, tn), lambda l, j: (l, 0, j)),
            pl.BlockSpec((None, 1, tn), lambda l, j: (l, 0, j)),
        ],
        out_specs=pl.BlockSpec((None, N_MOD_ROWS, tn), lambda l, j: (l, 0, j)),
        out_shape=jax.ShapeDtypeStruct((DEPTH, N_MOD_ROWS, 6 * D_MODEL), F32),
        compiler_params=_params(2),
        name="adaln",
    )(cvec, w_ada, b_ada.reshape(DEPTH, 1, 6 * D_MODEL))


def _na_r0(r, rows):
    return min(max(r - NA_WIN_ROWS // 2, 0), rows - NA_WIN_ROWS)


def _na_union_start(p, rows):
    return min(_na_r0(NA_BLOCK_ROWS * p, rows), rows - NA_UNION_ROWS)


def _na_block_patterns(rows):
    patterns, index = [], []
    for p in range(rows // NA_BLOCK_ROWS):
        start = _na_union_start(p, rows)
        pattern = []
        for a in range(NA_BLOCK_ROWS):
            r = NA_BLOCK_ROWS * p + a
            r0 = _na_r0(r, rows)
            pattern.append(tuple(start + j - r + NA_WIN_ROWS - 1 if r0 <= start + j < r0 + NA_WIN_ROWS else None
                                 for j in range(NA_UNION_ROWS)))
        pattern = tuple(pattern)
        if pattern not in patterns:
            patterns.append(pattern)
        index.append(patterns.index(pattern))
    return patterns, index


def _na_bias_kernel(tbl_ref, o_ref, *, rows):
    l = pl.program_id(0)
    h = pl.program_id(1)
    qi = lax.broadcasted_iota(jnp.int32, (GRID_W, GRID_W), 0)
    ki = lax.broadcasted_iota(jnp.int32, (GRID_W, GRID_W), 1)
    dc = jnp.clip(ki - qi, -(NA_WIN_COLS - 1), NA_WIN_COLS - 1) + NA_WIN_COLS - 1
    c0 = jnp.clip(qi - NA_WIN_COLS // 2, 0, GRID_W - NA_WIN_COLS)
    col_ok = (ki >= c0) & (ki < c0 + NA_WIN_COLS)
    n_dr = 2 * NA_WIN_ROWS - 1
    n_dc = 2 * NA_WIN_COLS - 1
    tiles = []
    for dr in range(n_dr):
        t = jnp.zeros((GRID_W, GRID_W), F32)
        for d in range(n_dc):
            t = jnp.where(dc == d, tbl_ref[((l * NA_HEADS + h) * n_dr + dr) * n_dc + d], t)
        tiles.append(jnp.where(col_ok, t, NEG))
    neg_tile = jnp.full((GRID_W, GRID_W), NEG, F32)
    for i, pattern in enumerate(_na_block_patterns(rows)[0]):
        for a in range(NA_BLOCK_ROWS):
            for j in range(NA_UNION_ROWS):
                dr = pattern[a][j]
                tile = neg_tile if dr is None else tiles[dr]
                o_ref[i, a * GRID_W:(a + 1) * GRID_W, j * GRID_W:(j + 1) * GRID_W] = tile


def _na_bias_expand(na_bias, rows):
    n_blocks = len(_na_block_patterns(rows)[0])
    qn = NA_BLOCK_ROWS * GRID_W
    kn = NA_UNION_ROWS * GRID_W
    return pl.pallas_call(
        functools.partial(_na_bias_kernel, rows=rows),
        grid=(DEPTH, NA_HEADS),
        in_specs=[pl.BlockSpec(memory_space=pltpu.SMEM)],
        out_specs=pl.BlockSpec((None, None, n_blocks, qn, kn), lambda l, h: (l, h, 0, 0, 0)),
        out_shape=jax.ShapeDtypeStruct((DEPTH, NA_HEADS, n_blocks, qn, kn), F32),
        compiler_params=_params(2),
        name="na_bias_expand",
    )(na_bias.reshape(-1))


def _mod_row_map(layer, tiles_per_row, first_row):
    if tiles_per_row is None:
        return lambda i: (layer, first_row, 0, 0)
    return lambda i: (layer, first_row + i // tiles_per_row, 0, 0)


def _inproj_kernel(x_ref, mod_ref, g_ref, wna_ref, wgq_ref, wml0_ref, wml1_ref, wg_ref, bg_ref,
                   zna_ref, zgq_ref, zml_ref, gi_ref, gf_ref, grow_ref):
    x = x_ref[...]
    h = _rms(x, g_ref[0:1, :]) * (1.0 + mod_ref[1:2, :]) + mod_ref[0:1, :]
    hb = h.astype(BF16)
    zna_ref[...] = lax.dot_general(hb, wna_ref[...], NT_DIMS, preferred_element_type=F32)
    zgq_ref[...] = lax.dot_general(hb, wgq_ref[...], NT_DIMS, preferred_element_type=F32)
    half = ZML_W // 2
    zml_ref[:, 0:half] = lax.dot_general(hb, wml0_ref[...], NT_DIMS, preferred_element_type=F32)
    zml_ref[:, half:] = lax.dot_general(hb, wml1_ref[...], NT_DIMS, preferred_element_type=F32)
    g_nat = lax.dot_general(wg_ref[...], hb, NT_DIMS, preferred_element_type=F32) + bg_ref[...]
    H = MLSTM_HEADS
    gr = jnp.concatenate([g_nat[0:H], g_nat[2 * H:3 * H], g_nat[H:2 * H], g_nat[3 * H:]], axis=0)
    grow_ref[...] = gr
    gc = gr.T
    gi_ref[...] = gc[:, 0:N_SCANS]
    gf_ref[...] = gc[:, N_SCANS:]


def _inproj(x, mods, g_norm, w, layer, tiles_per_row, first_row, tm):
    n = x.shape[0]
    ml_half = ZML_W // 2
    assert OFF_ML % ml_half == 0 and OFF_GATES % N_GATES == 0
    wspec = lambda width, block: pl.BlockSpec((None, width, D_MODEL), lambda i: (layer, block, 0))
    return pl.pallas_call(
        _inproj_kernel,
        grid=(n // tm,),
        in_specs=[
            pl.BlockSpec((tm, D_MODEL), lambda i: (i, 0)),
            pl.BlockSpec((None, None, 6, D_MODEL), _mod_row_map(layer, tiles_per_row, first_row)),
            pl.BlockSpec((None, 4, D_MODEL), lambda i: (layer, 0, 0)),
            wspec(ZNA_W, 0), wspec(ZGQ_W, 0), wspec(ml_half, OFF_ML // ml_half),
            wspec(ml_half, OFF_ML // ml_half + 1), wspec(N_GATES, OFF_GATES // N_GATES),
            pl.BlockSpec((None, N_GATES, 1), lambda i: (layer, 0, 0)),
        ],
        out_specs=[
            pl.BlockSpec((tm, ZNA_W), lambda i: (i, 0)),
            pl.BlockSpec((tm, ZGQ_W), lambda i: (i, 0)),
            pl.BlockSpec((tm, ZML_W), lambda i: (i, 0)),
            pl.BlockSpec((tm, N_SCANS), lambda i: (i, 0)),
            pl.BlockSpec((tm, N_SCANS), lambda i: (i, 0)),
            pl.BlockSpec((N_GATES, tm), lambda i: (0, i)),
        ],
        out_shape=[
            jax.ShapeDtypeStruct((n, ZNA_W), F32),
            jax.ShapeDtypeStruct((n, ZGQ_W), F32),
            jax.ShapeDtypeStruct((n, ZML_W), F32),
            jax.ShapeDtypeStruct((n, N_SCANS), F32),
            jax.ShapeDtypeStruct((n, N_SCANS), F32),
            jax.ShapeDtypeStruct((N_GATES, n), F32),
        ],
        compiler_params=_params(1),
        name="inproj",
    )(x, mods, g_norm, w["all"], w["gq"], w["all"], w["all"], w["all"], w["b_row"])


def _post_kernel(x_ref, mna_ref, mgq_ref, mml_ref, mod_ref, g_ref, wo_ref, wgu_ref, wd_ref, o_ref):
    rows = x_ref.shape[0] // POST_ROW_GROUPS
    rs = [slice(i * rows, (i + 1) * rows) for i in range(POST_ROW_GROUPS)]
    acc = []
    for r in rs:
        a = jnp.dot(mna_ref[r, :], wo_ref[0:NA_W, :], preferred_element_type=F32)
        a += jnp.dot(mgq_ref[r, :], wo_ref[NA_W:NA_W + GQA_QW, :], preferred_element_type=F32)
        a += jnp.dot(mml_ref[r, :], wo_ref[NA_W + GQA_QW:, :], preferred_element_type=F32)
        acc.append(a)
    x1, gate_up = [], []
    for r, a in zip(rs, acc):
        xr = x_ref[r, :] + mod_ref[2:3, :] * _rms(a, g_ref[1:2, :])
        hb = (_rms(xr, g_ref[2:3, :]) * (1.0 + mod_ref[4:5, :]) + mod_ref[3:4, :]).astype(BF16)
        x1.append(xr)
        gate_up.append((jnp.dot(hb, wgu_ref[:, 0:FF_HIDDEN], preferred_element_type=F32),
                        jnp.dot(hb, wgu_ref[:, FF_HIDDEN:], preferred_element_type=F32)))
    f = []
    for gate, up in gate_up:
        act = (gate * jax.nn.sigmoid(gate) * up).astype(BF16)
        f.append(jnp.dot(act, wd_ref[...], preferred_element_type=F32))
    for r, xr, fr in zip(rs, x1, f):
        o_ref[r, :] = xr + mod_ref[5:6, :] * _rms(fr, g_ref[3:4, :])


def _post(x, mna, mgq, mml, mods, g_norm, w_out, w_gu, w_down, layer, tiles_per_row, first_row, tm):
    n = x.shape[0]
    resident = lambda shape: pl.BlockSpec((None,) + shape, lambda i: (layer, 0, 0), pipeline_mode=pl.Buffered(1))
    return pl.pallas_call(
        _post_kernel,
        grid=(n // tm,),
        in_specs=[
            pl.BlockSpec((tm, D_MODEL), lambda i: (i, 0)),
            pl.BlockSpec((tm, NA_W), lambda i: (i, 0)),
            pl.BlockSpec((tm, GQA_QW), lambda i: (i, 0)),
            pl.BlockSpec((tm, ML_W), lambda i: (i, 0)),
            pl.BlockSpec((None, None, 6, D_MODEL), _mod_row_map(layer, tiles_per_row, first_row)),
            pl.BlockSpec((None, 4, D_MODEL), lambda i: (layer, 0, 0)),
            resident((D_MODEL, D_MODEL)),
            resident((D_MODEL, 2 * FF_HIDDEN)),
            resident((FF_HIDDEN, D_MODEL)),
        ],
        out_specs=pl.BlockSpec((tm, D_MODEL), lambda i: (i, 0)),
        out_shape=jax.ShapeDtypeStruct((n, D_MODEL), F32),
        compiler_params=_params(1),
        name="post",
    )(x, mna, mgq, mml, mods, g_norm, w_out, w_gu, w_down)


def _pair_rms(x, gain, sums_on_mxu=False):
    xsq = x * x
    if sums_on_mxu:
        row = lax.broadcasted_iota(jnp.int32, (LANES, LANES), 0) < HEAD_DIM
        col = lax.broadcasted_iota(jnp.int32, (LANES, LANES), 1) < HEAD_DIM
        ones_blk = jnp.where(row == col, 1.0, 0.0).astype(BF16)
        sums = sum(jnp.dot(p, ones_blk, preferred_element_type=F32) for p in _split3(xsq))
    else:
        first = _lane_first(x.shape)
        sums = jnp.where(first, jnp.sum(jnp.where(first, xsq, 0.0), axis=-1, keepdims=True),
                         jnp.sum(jnp.where(first, 0.0, xsq), axis=-1, keepdims=True))
    return x * lax.rsqrt(sums * (1.0 / HEAD_DIM) + EPS) * gain


def _pair_gain(gqk_ref, row):
    return jnp.concatenate([gqk_ref[row:row + 1, :]] * (LANES // HEAD_DIM), axis=-1)


def _zero_other_layers(ref, layer):
    for other in range(ref.shape[0]):
        if other != layer:
            ref[other] = jnp.zeros(ref.shape[1:], ref.dtype)
    return ref.at[layer]


def _ctx_attn_kernel(zna_ref, zgq_ref, gqk_ref, *rest, layer, creates):
    mna_ref, mgq_ref, kvna_ref, kvgq_ref = rest[-4:]
    if creates:
        kvna_ref, kvgq_ref = _zero_other_layers(kvna_ref, layer), _zero_other_layers(kvgq_ref, layer)

    def store_t(ref, which, pair, x2):
        xt = x2.T
        for a in range(2):
            ref[which, 2 * pair + a] = xt[a * HEAD_DIM:(a + 1) * HEAD_DIM]

    pending = []
    for i in range(NA_HEADS // 2):
        cols = slice(i * LANES, (i + 1) * LANES)
        q2 = zna_ref[:, cols] * QK_SCALE
        k2 = zna_ref[:, NA_W + i * LANES:NA_W + (i + 1) * LANES]
        v2 = zna_ref[:, 2 * NA_W + i * LANES:2 * NA_W + (i + 1) * LANES]
        store_t(kvna_ref, 0, i, k2)
        store_t(kvna_ref, 1, i, v2)
        blocks = [{"k": k2.astype(BF16), "k_nt": True, "r": _pair_values(v2), "r_nt": False}]
        pending.append((mna_ref, cols, _pair_scores(_pair_queries(q2), blocks), blocks))

    gq, gk = _pair_gain(gqk_ref, 0), _pair_gain(gqk_ref, 1)
    k2 = _pair_rms(zgq_ref[:, GQA_QW:GQA_QW + GQA_KW], gk)
    v2 = zgq_ref[:, GQA_QW + GQA_KW:]
    store_t(kvgq_ref, 0, 0, k2)
    store_t(kvgq_ref, 1, 0, v2)
    blocks = [{"k": k2.astype(BF16), "k_nt": True, "r": _pair_values(v2), "r_nt": False}]
    for p in range(GQA_GROUP):
        cols = slice(p * LANES, (p + 1) * LANES)
        q2 = _pair_rms(zgq_ref[:, cols], gq) * QK_SCALE
        pending.append((mgq_ref, cols, _pair_scores(_pair_queries(q2), blocks), blocks))

    for ref, cols, scores, blocks in pending:
        ref[:, cols] = _pair_finish(scores, blocks).astype(BF16)


def _ctx_attn(zna, zgq, g_qk, kv_na_buf, kv_gq_buf, layer, batch, t):
    n = zna.shape[0]
    creates = kv_na_buf is None
    in_specs = [
        pl.BlockSpec((t, ZNA_W), lambda b: (b, 0)),
        pl.BlockSpec((t, ZGQ_W), lambda b: (b, 0)),
        pl.BlockSpec((None, 2, HEAD_DIM), lambda b: (layer, 0, 0)),
    ]
    args = [zna, zgq, g_qk]
    if creates:
        kv_spec = lambda heads: pl.BlockSpec((None, DEPTH, 2, heads, HEAD_DIM, t), lambda b: (b, 0, 0, 0, 0, 0))
        aliases = {}
    else:
        kv_spec = lambda heads: pl.BlockSpec((None, None, 2, heads, HEAD_DIM, t),
                                             lambda b: (b, layer, 0, 0, 0, 0))
        aliases = {len(args): 2, len(args) + 1: 3}
        in_specs += [pl.BlockSpec(memory_space=pl.ANY)] * 2
        args += [kv_na_buf, kv_gq_buf]
    return pl.pallas_call(
        functools.partial(_ctx_attn_kernel, layer=layer, creates=creates),
        grid=(batch,),
        in_specs=in_specs,
        out_specs=[
            pl.BlockSpec((t, NA_W), lambda b: (b, 0)),
            pl.BlockSpec((t, GQA_QW), lambda b: (b, 0)),
            kv_spec(NA_HEADS),
            kv_spec(GQA_KV_HEADS),
        ],
        out_shape=[
            jax.ShapeDtypeStruct((n, NA_W), BF16),
            jax.ShapeDtypeStruct((n, GQA_QW), BF16),
            jax.ShapeDtypeStruct((batch, DEPTH, 2, NA_HEADS, HEAD_DIM, t), F32),
            jax.ShapeDtypeStruct((batch, DEPTH, 2, GQA_KV_HEADS, HEAD_DIM, t), F32),
        ],
        input_output_aliases=aliases,
        compiler_params=_params(1),
        name="ctx_attn",
    )(*args)


def _lat_na_kernel(zna_ref, cache_ref, bias_ref, o_ref, *, rows):
    qn = NA_BLOCK_ROWS * GRID_W
    kn = NA_UNION_ROWS * GRID_W
    n_blocks = rows // NA_BLOCK_ROWS
    pattern_of = _na_block_patterns(rows)[1]
    prepared = []
    for i in range(NA_HEADS // 2):
        cols = slice(i * LANES, (i + 1) * LANES)
        q_pair = _pair_queries(zna_ref[:, cols] * QK_SCALE)
        k2 = zna_ref[:, NA_W + i * LANES:NA_W + (i + 1) * LANES].astype(BF16)
        kc = cache_ref[0, i].astype(BF16)
        scores = []
        for a in range(2):
            s_ctx = jnp.dot(q_pair[a], kc, preferred_element_type=F32)
            s_win = []
            for p in range(n_blocks):
                k0 = _na_union_start(p, rows) * GRID_W
                s_win.append(lax.dot_general(q_pair[a][p * qn:(p + 1) * qn], k2[k0:k0 + kn], NT_DIMS,
                                             preferred_element_type=F32) + bias_ref[2 * i + a, pattern_of[p]])
            scores.append((s_ctx, s_win))
        prepared.append(scores)
    for i in range(NA_HEADS // 2):
        cols = slice(i * LANES, (i + 1) * LANES)
        r = _pair_values(zna_ref[:, 2 * NA_W + i * LANES:2 * NA_W + (i + 1) * LANES])
        rc = _pair_values_t(cache_ref[1, i])
        acc = None
        for a in range(2):
            s_ctx, s_win = prepared[i][a]
            m_win = jnp.concatenate([s.max(axis=-1, keepdims=True) for s in s_win], axis=0)
            m = jnp.maximum(m_win, s_ctx.max(axis=-1, keepdims=True))
            term = lax.dot_general(jnp.exp(s_ctx - m).astype(BF16), rc[a], NT_DIMS, preferred_element_type=F32)
            wins = []
            for p in range(n_blocks):
                k0 = _na_union_start(p, rows) * GRID_W
                pw = jnp.exp(s_win[p] - m[p * qn:(p + 1) * qn]).astype(BF16)
                wins.append(jnp.dot(pw, r[a][k0:k0 + kn], preferred_element_type=F32))
            term = term + jnp.concatenate(wins, axis=0)
            acc = term if acc is None else acc + term
        o_ref[:, cols] = (acc[:, :LANES] / acc[:, LANES:]).astype(BF16)


def _lat_na(zna, cache_t, bias, layer, batch, t):
    n = zna.shape[0]
    rows = t // GRID_W
    past = cache_t.shape[-1]
    n_blocks = len(_na_block_patterns(rows)[0])
    qn = NA_BLOCK_ROWS * GRID_W
    kn = NA_UNION_ROWS * GRID_W
    return pl.pallas_call(
        functools.partial(_lat_na_kernel, rows=rows),
        grid=(batch,),
        in_specs=[
            pl.BlockSpec((t, ZNA_W), lambda b: (b, 0)),
            pl.BlockSpec((None, None, 2, NA_HEADS // 2, LANES, past), lambda b: (b, layer, 0, 0, 0, 0)),
            pl.BlockSpec((None, NA_HEADS, n_blocks, qn, kn), lambda b: (layer, 0, 0, 0, 0)),
        ],
        out_specs=pl.BlockSpec((t, NA_W), lambda b: (b, 0)),
        out_shape=jax.ShapeDtypeStruct((n, NA_W), BF16),
        compiler_params=_params(1),
        name="lat_na",
    )(zna, cache_t, bias)


def _rope_tables(t):
    half = HEAD_DIM // 2
    quarter = half // 2
    inv = 1.0 / (ROPE_BASE ** (jnp.arange(quarter, dtype=F32) / quarter))
    tt = jnp.arange(t)
    row = (tt // GRID_W).astype(F32)
    col = (tt % GRID_W).astype(F32)
    ang_r = row[:, None] * inv[None, :]
    ang_c = col[:, None] * inv[None, :]
    cos = jnp.concatenate([jnp.cos(ang_r)] * 2 + [jnp.cos(ang_c)] * 2, axis=-1)
    sin = jnp.concatenate([-jnp.sin(ang_r), jnp.sin(ang_r), -jnp.sin(ang_c), jnp.sin(ang_c)], axis=-1)
    reps = LANES // HEAD_DIM
    return jnp.tile(cos, (1, reps)), jnp.tile(sin, (1, reps))


def _pair_rope(xn, cos, sin):
    quarter = HEAD_DIM // 4
    lane = lax.broadcasted_iota(jnp.int32, xn.shape, 1)
    lower = (lane & (2 * quarter - 1)) < quarter
    partner = jnp.where(lower, pltpu.roll(xn, LANES - quarter, 1), pltpu.roll(xn, quarter, 1))
    return xn * cos + partner * sin


def _lat_gqa_kernel(zgq_ref, cache_ref, gqk_ref, cos_ref, sin_ref, o_ref):
    cos = cos_ref[...]
    sin = sin_ref[...]
    gq, gk = _pair_gain(gqk_ref, 0), _pair_gain(gqk_ref, 1)
    keys = _pair_rope(_pair_rms(zgq_ref[:, GQA_QW:GQA_QW + GQA_KW], gk, True), cos, sin).astype(BF16)
    blocks = [
        {"k": keys, "k_nt": True, "r": _pair_values(zgq_ref[:, GQA_QW + GQA_KW:]), "r_nt": False},
        {"k": cache_ref[0].astype(BF16), "k_nt": False, "r": _pair_values_t(cache_ref[1]), "r_nt": True},
    ]
    def scores(p):
        x = _pair_rope(_pair_rms(zgq_ref[:, p * LANES:(p + 1) * LANES], gq, True), cos, sin) * QK_SCALE
        return _pair_scores(_pair_queries(x), blocks)

    nxt = scores(0)
    for p in range(GQA_GROUP):
        cur = nxt
        if p + 1 < GQA_GROUP:
            nxt = scores(p + 1)
        o_ref[:, p * LANES:(p + 1) * LANES] = _pair_finish(cur, blocks).astype(BF16)


def _lat_gqa(zgq, cache_t, g_qk, cos, sin, layer, batch, t):
    n = zgq.shape[0]
    past = cache_t.shape[-1]
    return pl.pallas_call(
        _lat_gqa_kernel,
        grid=(batch,),
        in_specs=[
            pl.BlockSpec((t, ZGQ_W), lambda b: (b, 0)),
            pl.BlockSpec((None, None, 2, GQA_KW, past), lambda b: (b, layer, 0, 0, 0)),
            pl.BlockSpec((None, 2, HEAD_DIM), lambda b: (layer, 0, 0)),
            pl.BlockSpec((t, LANES), lambda b: (0, 0)),
            pl.BlockSpec((t, LANES), lambda b: (0, 0)),
        ],
        out_specs=pl.BlockSpec((t, GQA_QW), lambda b: (b, 0)),
        out_shape=jax.ShapeDtypeStruct((n, GQA_QW), BF16),
        compiler_params=_params(1),
        name="lat_gqa",
    )(zgq, cache_t, g_qk, cos, sin)


def _split3(x):
    x1 = x.astype(BF16)
    r1 = x - x1.astype(F32)
    x2 = r1.astype(BF16)
    x3 = (r1 - x2.astype(F32)).astype(BF16)
    return x1, x2, x3


def _log_sigmoid(x):
    return jnp.minimum(x, 0.0) - jnp.log1p(jnp.exp(-jnp.abs(x)))


def _mlstm_select_matrix():
    H = MLSTM_HEADS
    sel = np.zeros((MLSTM_SEL_ROWS, (H // 2) * 2 * MLSTM_TILE_KINDS * LANES), np.float32)
    for j in range(H // 2):
        for d in range(2):
            for q in range(MLSTM_TILE_KINDS):
                for a in range(2):
                    col0 = ((j * 2 + d) * MLSTM_TILE_KINDS + q) * LANES + a * HEAD_DIM
                    sel[q * N_SCANS + d * H + 2 * j + a, col0:col0 + HEAD_DIM] = 1.0
    return sel


def _mlstm_kernel(zml_ref, gi_ref, gf_ref, grow_ref, c0_ref, n0_ref, m0_ref, gml_ref, sel_ref, *rest,
                  t, emit_state, creates_layer):
    tri_s, st_s, dst_s, cst_s, row_s = rest[-5:]
    outs = rest[:-5]
    if emit_state:
        o_ref, cf_ref, nf_ref, mf_ref = outs[-4:]
        if creates_layer is not None:
            cf_ref, nf_ref, mf_ref = (_zero_other_layers(r, creates_layer) for r in (cf_ref, nf_ref, mf_ref))
    else:
        o_ref = outs[-1]
    L = MLSTM_CHUNK
    H = MLSTM_HEADS
    HD = HEAD_DIM
    NP = H // 2
    nc = t // L
    tb = tri_s.shape[-1]

    @pl.when(pl.program_id(0) == 0)
    def _():
        ti = lax.broadcasted_iota(jnp.int32, (tb, tb), 0)
        ui = lax.broadcasted_iota(jnp.int32, (tb, tb), 1)
        same = (ti & -L) == (ui & -L)
        tri_s[0] = jnp.where(same & (ui <= ti), 1.0, 0.0).astype(BF16)
        tri_s[1] = jnp.where(same & (ui >= ti), 1.0, 0.0).astype(BF16)

    lower, upper = tri_s[0], tri_s[1]

    def chunk_sums_cols(x):
        parts = _split3(x)
        pre, suf = [], []
        for i in range(t // tb):
            blk = [p[i * tb:(i + 1) * tb] for p in parts]
            pre.append(sum(jnp.dot(lower, p, preferred_element_type=F32) for p in blk))
            suf.append(sum(jnp.dot(upper, p, preferred_element_type=F32) for p in blk))
        return jnp.concatenate(pre, axis=0), jnp.concatenate(suf, axis=0)

    def chunk_sums_rows(x):
        parts = _split3(x)
        pre, suf = [], []
        for i in range(t // tb):
            blk = [p[:, i * tb:(i + 1) * tb] for p in parts]
            pre.append(sum(jnp.dot(p, upper, preferred_element_type=F32) for p in blk))
            suf.append(sum(jnp.dot(p, lower, preferred_element_type=F32) for p in blk))
        return jnp.concatenate(pre, axis=1), jnp.concatenate(suf, axis=1)

    pre_c, suf_c = chunk_sums_cols(_log_sigmoid(gf_ref[...]))
    lane_c = lax.broadcasted_iota(jnp.int32, (t, N_SCANS), 1)
    b3 = jnp.where(lane_c < H, pre_c, suf_c).reshape(nc, L, N_SCANS)
    i3 = gi_ref[...].reshape(nc, L, N_SCANS)
    fwd3 = lax.broadcasted_iota(jnp.int32, (nc, 1, N_SCANS), 2) < H
    b_end3 = jnp.where(fwd3, b3[:, L - 1:L, :], b3[:, 0:1, :])
    lw_end3 = b_end3 - b3 + i3
    a3 = jnp.max(lw_end3, axis=1, keepdims=True)
    wloc3 = jnp.exp(lw_end3 - a3)

    fwd1 = lax.broadcasted_iota(jnp.int32, (1, N_SCANS), 1) < H
    m = m0_ref[...]
    m_start, carry_decay, contrib_scale = [], [], []
    for j in range(nc):
        a_j = jnp.where(fwd1, a3[j], a3[nc - 1 - j])
        g_j = jnp.where(fwd1, b_end3[j], b_end3[nc - 1 - j])
        m_start.append(m)
        m_next = jnp.maximum(g_j + m, a_j)
        carry_decay.append(jnp.exp(g_j + m - m_next))
        contrib_scale.append(jnp.exp(a_j - m_next))
        m = m_next
    mst3 = jnp.concatenate([jnp.where(fwd1, m_start[c], m_start[nc - 1 - c])[None] for c in range(nc)], axis=0)

    cols = jnp.concatenate([b3.reshape(t, N_SCANS), wloc3.reshape(t, N_SCANS)], axis=1)
    tiles_all = sum(jnp.dot(p, sel_ref[...], preferred_element_type=F32) for p in _split3(cols))
    tile_w = MLSTM_TILE_KINDS * LANES

    def tiles(j, d):
        x = tiles_all[:, (2 * j + d) * tile_w:(2 * j + d + 1) * tile_w]
        return [x[:, q * LANES:(q + 1) * LANES].reshape(nc, L, LANES) for q in range(MLSTM_TILE_KINDS)]

    gr = grow_ref[...]
    pre_r, suf_r = chunk_sums_rows(_log_sigmoid(gr))
    sub_r = lax.broadcasted_iota(jnp.int32, (N_SCANS, t), 0)
    rowv = gr[0:N_SCANS] - jnp.where(sub_r < H, pre_r[N_SCANS:], suf_r[N_SCANS:])
    for j in range(NP):
        for d in range(2):
            e = d * H + 2 * j
            for c in range(nc):
                row_s[2 * j + d, c] = jnp.concatenate(
                    [rowv[e:e + 1, c * L:(c + 1) * L], rowv[e + 1:e + 2, c * L:(c + 1) * L]], axis=1)

    lane_a = lax.broadcasted_iota(jnp.int32, (1, 1, LANES), 2) < HD
    sub_a = lax.broadcasted_iota(jnp.int32, (1, 2 * HD, 1), 1) < HD
    diag = sub_a == lane_a
    diag4 = jnp.concatenate([diag] * 4, axis=2)

    def stack_heads(x3):
        return jnp.concatenate([jnp.where(lane_a, x3, 0.0), jnp.where(lane_a, 0.0, x3)], axis=1)

    def pair_cols(base, j):
        return slice(base + j * LANES, base + (j + 1) * LANES)

    zero_blk = jnp.zeros((HD, HD), F32)
    for j in range(NP):
        cols_d = []
        for d in range(2):
            ca, cb = c0_ref[d, 2 * j].T, c0_ref[d, 2 * j + 1].T
            na = jnp.broadcast_to(n0_ref[d, 2 * j:2 * j + 1, :], (HD, HD)).T
            nb = jnp.broadcast_to(n0_ref[d, 2 * j + 1:2 * j + 2, :], (HD, HD)).T
            top = jnp.concatenate([ca, zero_blk, na, zero_blk], axis=1)
            bot = jnp.concatenate([zero_blk, cb, zero_blk, nb], axis=1)
            cols_d.append(jnp.concatenate([top, bot], axis=0))
        st_s[j] = jnp.concatenate(cols_d, axis=1)

    for j in range(NP):
        k3 = (zml_ref[:, pair_cols(ML_W, j)] * QK_SCALE).reshape(nc, L, LANES).astype(BF16)
        v3 = zml_ref[:, pair_cols(2 * ML_W, j)].reshape(nc, L, LANES)
        rhs = []
        for d in range(2):
            wl = tiles(j, d)[1]
            rhs += [v3 * wl, wl]
        rhs = jnp.concatenate(rhs, axis=2).astype(BF16)
        contrib = jnp.einsum("csk,csn->ckn", k3, rhs, preferred_element_type=F32)
        dst_s[j] = jnp.where(diag4, contrib, 0.0)

    def lane_scale(v, j):
        pieces = []
        for d in range(2):
            sa = jnp.broadcast_to(v[:, d * H + 2 * j:d * H + 2 * j + 1], (1, HD))
            sb = jnp.broadcast_to(v[:, d * H + 2 * j + 1:d * H + 2 * j + 2], (1, HD))
            pieces += [sa, sb, sa, sb]
        return jnp.concatenate(pieces, axis=1)

    for j in range(NP):
        st = st_s[j]
        for step in range(nc):
            cb = nc - 1 - step
            stb = st.astype(BF16)
            cst_s[j, step, :, 0:2 * LANES] = stb[:, 0:2 * LANES]
            cst_s[j, cb, :, 2 * LANES:] = stb[:, 2 * LANES:]
            delta = jnp.concatenate([dst_s[j, step, :, 0:2 * LANES], dst_s[j, cb, :, 2 * LANES:]], axis=1)
            st = lane_scale(carry_decay[step], j) * st + lane_scale(contrib_scale[step], j) * delta
        st_s[j] = st

    sidx = lax.broadcasted_iota(jnp.int32, (1, L, LANES), 2) & (HD - 1)
    tidx = lax.broadcasted_iota(jnp.int32, (1, L, LANES), 1)
    masks = (sidx <= tidx, sidx >= tidx)
    ones_blk = jnp.broadcast_to(jnp.where(diag, 1.0, 0.0).astype(BF16), (nc, 2 * HD, LANES))
    neg_inf = -jnp.inf
    for j in range(NP):
        q3 = zml_ref[:, pair_cols(0, j)].reshape(nc, L, LANES).astype(BF16)
        k3 = (zml_ref[:, pair_cols(ML_W, j)] * QK_SCALE).reshape(nc, L, LANES)
        v3 = zml_ref[:, pair_cols(2 * ML_W, j)].reshape(nc, L, LANES)
        qk = jnp.einsum("ctd,cnd->ctn", q3, stack_heads(k3).astype(BF16), preferred_element_type=F32)
        v_aug = jnp.concatenate([stack_heads(v3).astype(BF16), ones_blk], axis=2)
        out = None
        for d in range(2):
            b_t = tiles(j, d)[0]
            e = d * H + 2 * j
            bm_t = b_t + jnp.where(lane_a, mst3[:, :, e:e + 1], mst3[:, :, e + 1:e + 2])
            logw = jnp.where(masks[d], b_t + row_s[2 * j + d], neg_inf)
            rmax_a = jnp.max(jnp.where(lane_a, logw, neg_inf), axis=-1, keepdims=True)
            rmax_b = jnp.max(jnp.where(lane_a, neg_inf, logw), axis=-1, keepdims=True)
            m_t = jnp.maximum(jnp.where(lane_a, rmax_a, rmax_b), bm_t)
            s = qk * jnp.exp(logw - m_t)
            decay = jnp.exp(bm_t - m_t)
            sv = jnp.einsum("cts,csn->ctn", s.astype(BF16), v_aug, preferred_element_type=F32)
            state = cst_s[j, :, :, 2 * d * LANES:2 * (d + 1) * LANES]
            inter = jnp.einsum("ctk,ckn->ctn", q3, state, preferred_element_type=F32)
            num = sv[:, :, 0:LANES] + decay * inter[:, :, 0:LANES]
            den = sv[:, :, LANES:] + decay * inter[:, :, LANES:]
            h_d = num / jnp.maximum(jnp.abs(den), jnp.exp(-m_t))
            out = h_d if out is None else out + h_d
        cols = pair_cols(0, j)
        og = jax.nn.sigmoid(zml_ref[:, pair_cols(3 * ML_W, j)])
        o_ref[:, cols] = (_pair_rms(out.reshape(t, LANES), gml_ref[:, cols]) * og).astype(BF16)

    if emit_state:
        for j in range(NP):
            st = st_s[j]
            for d in range(2):
                for a in range(2):
                    rows = slice(a * HD, (a + 1) * HD)
                    c0 = 2 * d * LANES + a * HD
                    cf_ref[d, 2 * j + a] = st[rows, c0:c0 + HD].T
                    nf_ref[d, 2 * j + a:2 * j + a + 1, :] = st[rows, c0 + LANES:c0 + LANES + HD].T[0:1, :]
        mf_ref[...] = m


def _mlstm(zml, gi, gf, grow, c0, n0, m0, g_ml, layer, batch, t, state_out, state_layer):
    emit_state = state_out is not None
    creates = emit_state and not isinstance(state_out, tuple)
    n = zml.shape[0]
    H = MLSTM_HEADS
    L = MLSTM_CHUNK
    nc = t // L
    sel = jnp.asarray(_mlstm_select_matrix(), BF16)
    tri_block = min(t, MXU_DIM)
    assert L & (L - 1) == 0 and tri_block % L == 0 and t % tri_block == 0
    if state_layer is None:
        c_spec = pl.BlockSpec((None, 2, H, HEAD_DIM, HEAD_DIM), lambda b: (0, 0, 0, 0, 0))
        n_spec = pl.BlockSpec((None, 2, H, HEAD_DIM), lambda b: (0, 0, 0, 0))
        m_spec = pl.BlockSpec((None, 1, N_SCANS), lambda b: (0, 0, 0))
    else:
        c_spec = pl.BlockSpec((None, None, 2, H, HEAD_DIM, HEAD_DIM), lambda b: (b, state_layer, 0, 0, 0, 0))
        n_spec = pl.BlockSpec((None, None, 2, H, HEAD_DIM), lambda b: (b, state_layer, 0, 0, 0))
        m_spec = pl.BlockSpec((None, None, 1, N_SCANS), lambda b: (b, state_layer, 0, 0))
    in_specs = [
        pl.BlockSpec((t, ZML_W), lambda b: (b, 0)),
        pl.BlockSpec((t, N_SCANS), lambda b: (b, 0)),
        pl.BlockSpec((t, N_SCANS), lambda b: (b, 0)),
        pl.BlockSpec((N_GATES, t), lambda b: (0, b)),
        c_spec, n_spec, m_spec,
        pl.BlockSpec((None, 1, ML_W), lambda b: (layer, 0, 0)),
        pl.BlockSpec(sel.shape, lambda b: (0, 0)),
    ]
    args = [zml, gi, gf, grow, c0, n0, m0, g_ml, sel]
    out_specs = [pl.BlockSpec((t, ML_W), lambda b: (b, 0))]
    out_shape = [jax.ShapeDtypeStruct((n, ML_W), BF16)]
    aliases = {}
    if emit_state:
        tails = [(2, H, HEAD_DIM, HEAD_DIM), (2, H, HEAD_DIM), (1, N_SCANS)]
        for tail in tails:
            zeros = (0,) * len(tail)
            if creates:
                out_specs.append(pl.BlockSpec((None, DEPTH) + tail, lambda b, zeros=zeros: (b, 0) + zeros))
            else:
                out_specs.append(pl.BlockSpec((None, None) + tail, lambda b, zeros=zeros: (b, layer) + zeros))
            out_shape.append(jax.ShapeDtypeStruct((batch, DEPTH) + tail, F32))
        if not creates:
            aliases = {len(args) + i: 1 + i for i in range(len(state_out))}
            in_specs += [pl.BlockSpec(memory_space=pl.ANY)] * len(state_out)
            args += list(state_out)
    return pl.pallas_call(
        functools.partial(_mlstm_kernel, t=t, emit_state=emit_state, creates_layer=layer if creates else None),
        grid=(batch,),
        in_specs=in_specs,
        out_specs=out_specs,
        out_shape=out_shape,
        input_output_aliases=aliases,
        scratch_shapes=[
            pltpu.VMEM((2, tri_block, tri_block), BF16),
            pltpu.VMEM((H // 2, 2 * HEAD_DIM, 4 * LANES), F32),
            pltpu.VMEM((H // 2, nc, 2 * HEAD_DIM, 4 * LANES), F32),
            pltpu.VMEM((H // 2, nc, 2 * HEAD_DIM, 4 * LANES), BF16),
            pltpu.VMEM((H, nc, 1, LANES), F32),
        ],
        compiler_params=_params(1),
        name="mlstm",
    )(*args)


def _layer_path(x, mods, layer, first_row, tiles_row_tokens, weights, mixers):
    per_row = None if tiles_row_tokens is None else tiles_row_tokens // TOKEN_TILE
    g_norm = weights["g_norm"]
    zna, zgq, zml, gi, gf, grow = _inproj(x, mods, g_norm, weights["w_in"], layer, per_row, first_row, TOKEN_TILE)
    mna, mgq, mml, extra = mixers(zna, zgq, zml, gi, gf, grow)
    x = _post(x, mna, mgq, mml, mods, g_norm, weights["w_out"], weights["w_gu"], weights["w_down"], layer,
              per_row, first_row, TOKEN_TILE)
    return x, extra


def _gqa_pair_order():
    return [a * GQA_GROUP + p for p in range(GQA_GROUP) for a in range(GQA_KV_HEADS)]


def _take_blocks(x, axis, base, width, order):
    return jnp.concatenate([lax.slice_in_dim(x, base + width * o, base + width * (o + 1), axis=axis)
                            for o in order], axis=axis)


def kernel(x_prompt, x_sample, cache_na_kv, cache_gqa_kv, state_mlstm_C, state_mlstm_n, state_mlstm_m,
           c, c_ctx, w_in, b_gates, w_out, g_norm, g_qk, g_mlstm, na_bias, w_ada, b_ada, w_gu, w_down):
    batch, seq, _ = x_prompt.shape
    dec_batch, dec_seq, _ = x_sample.shape
    past = cache_na_kv.shape[-2]
    assert dec_batch + 1 <= N_MOD_ROWS and dec_seq % GRID_W == 0 and GQA_KV_HEADS == 2

    cvec = jnp.concatenate([c_ctx[None, :], c, jnp.zeros((N_MOD_ROWS - 1 - dec_batch, D_MODEL), F32)], axis=0)
    mods = _adaln(cvec, w_ada, b_ada).reshape(DEPTH, N_MOD_ROWS, 6, D_MODEL)
    bias = _na_bias_expand(na_bias, dec_seq // GRID_W)
    cos, sin = _rope_tables(dec_seq)

    pair_order = _gqa_pair_order()
    w_in_t = jnp.swapaxes(w_in, 1, 2).astype(BF16)
    w_gq = jnp.concatenate([_take_blocks(w_in_t, 1, OFF_GQ, HEAD_DIM, pair_order),
                            w_in_t[:, OFF_GQ + GQA_QW:OFF_ML]], axis=1)
    w_out_rows = jnp.concatenate([w_out[:, :NA_W], _take_blocks(w_out, 1, NA_W, HEAD_DIM, pair_order),
                                  w_out[:, NA_W + GQA_QW:]], axis=1)
    weights = {
        "g_norm": g_norm,
        "w_in": {
            "all": w_in_t,
            "gq": w_gq,
            "b_row": b_gates[:, :, None],
        },
        "w_out": w_out_rows.astype(BF16),
        "w_gu": w_gu.astype(BF16),
        "w_down": w_down.astype(BF16),
    }
    g_ml = g_mlstm.reshape(DEPTH, 1, ML_W)
    zero_c = jnp.zeros((1, 2, MLSTM_HEADS, HEAD_DIM, HEAD_DIM), F32)
    zero_n = jnp.zeros((1, 2, MLSTM_HEADS, HEAD_DIM), F32)
    zero_m = jnp.zeros((1, 1, N_SCANS), F32)
    m0_lat = state_mlstm_m.reshape(dec_batch, DEPTH, 1, N_SCANS)
    cache_na_t = jnp.swapaxes(cache_na_kv, -1, -2).reshape(dec_batch, DEPTH, 2, NA_HEADS // 2, LANES, past)
    cache_gq_t = jnp.swapaxes(cache_gqa_kv, -1, -2).reshape(dec_batch, DEPTH, 2, GQA_KW, past)

    xp = x_prompt.reshape(batch * seq, D_MODEL)
    xs = x_sample.reshape(dec_batch * dec_seq, D_MODEL)
    kv_na = kv_gq = None
    states = "create"
    for layer in range(DEPTH):
        def ctx_mixers(zna, zgq, zml, gi, gf, grow, layer=layer, kv_na=kv_na, kv_gq=kv_gq, states=states):
            mna, mgq, kv_na, kv_gq = _ctx_attn(zna, zgq, g_qk, kv_na, kv_gq, layer, batch, seq)
            mml, *states = _mlstm(zml, gi, gf, grow, zero_c, zero_n, zero_m, g_ml, layer, batch, seq,
                                  states, None)
            return mna, mgq, mml, (kv_na, kv_gq, tuple(states))

        def lat_mixers(zna, zgq, zml, gi, gf, grow, layer=layer):
            mna = _lat_na(zna, cache_na_t, bias, layer, dec_batch, dec_seq)
            mgq = _lat_gqa(zgq, cache_gq_t, g_qk, cos, sin, layer, dec_batch, dec_seq)
            (mml,) = _mlstm(zml, gi, gf, grow, state_mlstm_C, state_mlstm_n, m0_lat, g_ml, layer,
                            dec_batch, dec_seq, None, layer)
            return mna, mgq, mml, None

        xp, (kv_na, kv_gq, states) = _layer_path(xp, mods, layer, 0, None, weights, ctx_mixers)
        xs, _ = _layer_path(xs, mods, layer, 1, dec_seq, weights, lat_mixers)

    new_c, new_n, new_m = states
    return (xp.reshape(batch, seq, D_MODEL), xs.reshape(dec_batch, dec_seq, D_MODEL),
            jnp.swapaxes(kv_na, -1, -2), jnp.swapaxes(kv_gq, -1, -2),
            new_c, new_n, new_m.reshape(batch, DEPTH, 2, MLSTM_HEADS))
```

```python
import functools

import jax
import jax.numpy as jnp
import numpy as np
from jax import lax
from jax.experimental import pallas as pl
from jax.experimental.pallas import tpu as pltpu

D_MODEL = 1024
DEPTH = 4
GRID_W = 64
HEAD_DIM = 64
NA_HEADS = 4
GQA_Q_HEADS = 8
GQA_KV_HEADS = 2
GQA_GROUP = GQA_Q_HEADS // GQA_KV_HEADS
MLSTM_HEADS = 4
NA_WIN_ROWS = 8
NA_WIN_COLS = 16
MLSTM_CHUNK = 64
ROPE_BASE = 10000.0
EPS = 1e-6
NEG = -1e30
NA_W = NA_HEADS * HEAD_DIM
GQA_QW = GQA_Q_HEADS * HEAD_DIM
GQA_KW = GQA_KV_HEADS * HEAD_DIM
ML_W = MLSTM_HEADS * HEAD_DIM
N_GATES = 4 * MLSTM_HEADS
N_SCANS = 2 * MLSTM_HEADS
MLSTM_TILE_KINDS = 2
MLSTM_SEL_ROWS = MLSTM_TILE_KINDS * N_SCANS
FF_HIDDEN = ((8 * D_MODEL + 3 * 256 - 1) // (3 * 256)) * 256
QK_SCALE = HEAD_DIM ** -0.5

ZNA_W = 3 * NA_W
ZGQ_W = GQA_QW + 2 * GQA_KW
ZML_W = 4 * ML_W
OFF_GQ = ZNA_W
OFF_ML = ZNA_W + ZGQ_W
OFF_GATES = OFF_ML + ZML_W

LANES = 128
MXU_DIM = 256
N_MOD_ROWS = 16
NA_BLOCK_ROWS = 2
NA_UNION_ROWS = NA_WIN_ROWS + NA_BLOCK_ROWS - 1

F32 = jnp.float32
BF16 = jnp.bfloat16
VMEM_LIMIT = 52 * 1024 * 1024
INPROJ_TILE = 1024
POST_TILE = 512
ADALN_TILE = 1536
POST_ROW_GROUPS = 2

NT_DIMS = (((1,), (1,)), ((), ()))


def _params(n_axes):
    return pltpu.CompilerParams(dimension_semantics=("arbitrary",) * n_axes,
                                vmem_limit_bytes=VMEM_LIMIT)


def _rms(x, g):
    return x * lax.rsqrt(jnp.mean(x * x, axis=-1, keepdims=True) + EPS) * g


def _lane_first(shape):
    return lax.broadcasted_iota(jnp.int32, shape, len(shape) - 1) < HEAD_DIM


def _pair_queries(x):
    first = _lane_first(x.shape)
    return jnp.where(first, x, 0.0).astype(BF16), jnp.where(first, 0.0, x).astype(BF16)


def _pair_values(v2):
    first = _lane_first(v2.shape)
    ones, zeros = jnp.ones_like(v2), jnp.zeros_like(v2)
    r0 = jnp.concatenate([jnp.where(first, v2, 0.0), jnp.where(first, ones, zeros)], axis=1)
    r1 = jnp.concatenate([jnp.where(first, 0.0, v2), jnp.where(first, zeros, ones)], axis=1)
    return r0.astype(BF16), r1.astype(BF16)


def _pair_values_t(vt2):
    first = lax.broadcasted_iota(jnp.int32, vt2.shape, 0) < HEAD_DIM
    ones, zeros = jnp.ones_like(vt2), jnp.zeros_like(vt2)
    r0 = jnp.concatenate([jnp.where(first, vt2, 0.0), jnp.where(first, ones, zeros)], axis=0)
    r1 = jnp.concatenate([jnp.where(first, 0.0, vt2), jnp.where(first, zeros, ones)], axis=0)
    return r0.astype(BF16), r1.astype(BF16)


def _pair_scores(q_pair, blocks):
    out = []
    for a in range(2):
        scores = []
        for blk in blocks:
            if blk["k_nt"]:
                s = lax.dot_general(q_pair[a], blk["k"], NT_DIMS, preferred_element_type=F32)
            else:
                s = jnp.dot(q_pair[a], blk["k"], preferred_element_type=F32)
            if blk.get("bias") is not None:
                s = s + blk["bias"][a]
            scores.append(s)
        out.append(scores)
    return out


def _pair_finish(all_scores, blocks):
    acc = None
    for a in range(2):
        scores = all_scores[a]
        m = scores[0].max(axis=-1, keepdims=True)
        for s in scores[1:]:
            m = jnp.maximum(m, s.max(axis=-1, keepdims=True))
        for s, blk in zip(scores, blocks):
            p = jnp.exp(s - m).astype(BF16)
            if blk["r_nt"]:
                term = lax.dot_general(p, blk["r"][a], NT_DIMS, preferred_element_type=F32)
            else:
                term = jnp.dot(p, blk["r"][a], preferred_element_type=F32)
            acc = term if acc is None else acc + term
    return acc[:, :LANES] / acc[:, LANES:]


def _adaln_kernel(c_ref, w_ref, b_ref, o_ref):
    c = c_ref[...]
    a = c * jax.nn.sigmoid(c)
    o_ref[...] = jnp.dot(a.astype(BF16), w_ref[...].astype(BF16),
                         preferred_element_type=F32) + b_ref[...]


def _adaln(cvec, w_ada, b_ada):
    tn = ADALN_TILE
    return pl.pallas_call(
        _adaln_kernel,
        grid=(DEPTH, 6 * D_MODEL // tn),
        in_specs=[
            pl.BlockSpec((N_MOD_ROWS, D_MODEL), lambda l, j: (0, 0)),
            pl.BlockSpec((None, D_MODEL, tn), lambda l, j: (l, 0, j)),
            pl.BlockSpec((None, 1, tn), lambda l, j: (l, 0, j)),
        ],
        out_specs=pl.BlockSpec((None, N_MOD_ROWS, tn), lambda l, j: (l, 0, j)),
        out_shape=jax.ShapeDtypeStruct((DEPTH, N_MOD_ROWS, 6 * D_MODEL), F32),
        compiler_params=_params(2),
        name="adaln",
    )(cvec, w_ada, b_ada.reshape(DEPTH, 1, 6 * D_MODEL))


def _na_r0(r, rows):
    return min(max(r - NA_WIN_ROWS // 2, 0), rows - NA_WIN_ROWS)


def _na_union_start(p, rows):
    return min(_na_r0(NA_BLOCK_ROWS * p, rows), rows - NA_UNION_ROWS)


def _na_block_patterns(rows):
    patterns, index = [], []
    for p in range(rows // NA_BLOCK_ROWS):
        start = _na_union_start(p, rows)
        pattern = []
        for a in range(NA_BLOCK_ROWS):
            r = NA_BLOCK_ROWS * p + a
            r0 = _na_r0(r, rows)
            pattern.append(tuple(start + j - r + NA_WIN_ROWS - 1 if r0 <= start + j < r0 + NA_WIN_ROWS else None
                                 for j in range(NA_UNION_ROWS)))
        pattern = tuple(pattern)
        if pattern not in patterns:
            patterns.append(pattern)
        index.append(patterns.index(pattern))
    return patterns, index


def _na_bias_kernel(tbl_ref, o_ref, *, rows):
    l = pl.program_id(0)
    h = pl.program_id(1)
    qi = lax.broadcasted_iota(jnp.int32, (GRID_W, GRID_W), 0)
    ki = lax.broadcasted_iota(jnp.int32, (GRID_W, GRID_W), 1)
    dc = jnp.clip(ki - qi, -(NA_WIN_COLS - 1), NA_WIN_COLS - 1) + NA_WIN_COLS - 1
    c0 = jnp.clip(qi - NA_WIN_COLS // 2, 0, GRID_W - NA_WIN_COLS)
    col_ok = (ki >= c0) & (ki < c0 + NA_WIN_COLS)
    n_dr = 2 * NA_WIN_ROWS - 1
    n_dc = 2 * NA_WIN_COLS - 1
    tiles = []
    for dr in range(n_dr):
        t = jnp.zeros((GRID_W, GRID_W), F32)
        for d in range(n_dc):
            t = jnp.where(dc == d, tbl_ref[((l * NA_HEADS + h) * n_dr + dr) * n_dc + d], t)
        tiles.append(jnp.where(col_ok, t, NEG))
    neg_tile = jnp.full((GRID_W, GRID_W), NEG, F32)
    for i, pattern in enumerate(_na_block_patterns(rows)[0]):
        for a in range(NA_BLOCK_ROWS):
            for j in range(NA_UNION_ROWS):
                dr = pattern[a][j]
                tile = neg_tile if dr is None else tiles[dr]
                o_ref[i, a * GRID_W:(a + 1) * GRID_W, j * GRID_W:(j + 1) * GRID_W] = tile


def _na_bias_expand(na_bias, rows):
    n_blocks = len(_na_block_patterns(rows)[0])
    qn = NA_BLOCK_ROWS * GRID_W
    kn = NA_UNION_ROWS * GRID_W
    return pl.pallas_call(
        functools.partial(_na_bias_kernel, rows=rows),
        grid=(DEPTH, NA_HEADS),
        in_specs=[pl.BlockSpec(memory_space=pltpu.SMEM)],
        out_specs=pl.BlockSpec((None, None, n_blocks, qn, kn), lambda l, h: (l, h, 0, 0, 0)),
        out_shape=jax.ShapeDtypeStruct((DEPTH, NA_HEADS, n_blocks, qn, kn), F32),
        compiler_params=_params(2),
        name="na_bias_expand",
    )(na_bias.reshape(-1))


def _mod_row_map(layer, tiles_per_row, first_row):
    if tiles_per_row is None:
        return lambda i: (layer, first_row, 0, 0)
    return lambda i: (layer, first_row + i // tiles_per_row, 0, 0)


def _inproj_kernel(x_ref, mod_ref, g_ref, wna_ref, wgq_ref, wml0_ref, wml1_ref, wg_ref, bg_ref,
                   zna_ref, zgq_ref, zml_ref, gi_ref, gf_ref, grow_ref):
    x = x_ref[...]
    h = _rms(x, g_ref[0:1, :]) * (1.0 + mod_ref[1:2, :]) + mod_ref[0:1, :]
    hb = h.astype(BF16)
    zna_ref[...] = lax.dot_general(hb, wna_ref[...], NT_DIMS, preferred_element_type=F32)
    zgq_ref[...] = lax.dot_general(hb, wgq_ref[...], NT_DIMS, preferred_element_type=F32)
    half = ZML_W // 2
    zml_ref[:, 0:half] = lax.dot_general(hb, wml0_ref[...], NT_DIMS, preferred_element_type=F32)
    zml_ref[:, half:] = lax.dot_general(hb, wml1_ref[...], NT_DIMS, preferred_element_type=F32)
    g_nat = lax.dot_general(wg_ref[...], hb, NT_DIMS, preferred_element_type=F32) + bg_ref[...]
    H = MLSTM_HEADS
    gr = jnp.concatenate([g_nat[0:H], g_nat[2 * H:3 * H], g_nat[H:2 * H], g_nat[3 * H:]], axis=0)
    grow_ref[...] = gr
    gc = gr.T
    gi_ref[...] = gc[:, 0:N_SCANS]
    gf_ref[...] = gc[:, N_SCANS:]


def _inproj(x, mods, g_norm, w, layer, tiles_per_row, first_row, tm):
    n = x.shape[0]
    ml_half = ZML_W // 2
    assert OFF_ML % ml_half == 0 and OFF_GATES % N_GATES == 0
    wspec = lambda width, block: pl.BlockSpec((None, width, D_MODEL), lambda i: (layer, block, 0))
    return pl.pallas_call(
        _inproj_kernel,
        grid=(n // tm,),
        in_specs=[
            pl.BlockSpec((tm, D_MODEL), lambda i: (i, 0)),
            pl.BlockSpec((None, None, 6, D_MODEL), _mod_row_map(layer, tiles_per_row, first_row)),
            pl.BlockSpec((None, 4, D_MODEL), lambda i: (layer, 0, 0)),
            wspec(ZNA_W, 0), wspec(ZGQ_W, 0), wspec(ml_half, OFF_ML // ml_half),
            wspec(ml_half, OFF_ML // ml_half + 1), wspec(N_GATES, OFF_GATES // N_GATES),
            pl.BlockSpec((None, N_GATES, 1), lambda i: (layer, 0, 0)),
        ],
        out_specs=[
            pl.BlockSpec((tm, ZNA_W), lambda i: (i, 0)),
            pl.BlockSpec((tm, ZGQ_W), lambda i: (i, 0)),
            pl.BlockSpec((tm, ZML_W), lambda i: (i, 0)),
            pl.BlockSpec((tm, N_SCANS), lambda i: (i, 0)),
            pl.BlockSpec((tm, N_SCANS), lambda i: (i, 0)),
            pl.BlockSpec((N_GATES, tm), lambda i: (0, i)),
        ],
        out_shape=[
            jax.ShapeDtypeStruct((n, ZNA_W), F32),
            jax.ShapeDtypeStruct((n, ZGQ_W), F32),
            jax.ShapeDtypeStruct((n, ZML_W), F32),
            jax.ShapeDtypeStruct((n, N_SCANS), F32),
            jax.ShapeDtypeStruct((n, N_SCANS), F32),
            jax.ShapeDtypeStruct((N_GATES, n), F32),
        ],
        compiler_params=_params(1),
        name="inproj",
    )(x, mods, g_norm, w["all"], w["gq"], w["all"], w["all"], w["all"], w["b_row"])


def _post_kernel(x_ref, mna_ref, mgq_ref, mml_ref, mod_ref, g_ref, wo_ref, wgu_ref, wd_ref, o_ref):
    rows = x_ref.shape[0] // POST_ROW_GROUPS
    rs = [slice(i * rows, (i + 1) * rows) for i in range(POST_ROW_GROUPS)]
    acc = []
    for r in rs:
        a = jnp.dot(mna_ref[r, :], wo_ref[0:NA_W, :], preferred_element_type=F32)
        a += jnp.dot(mgq_ref[r, :], wo_ref[NA_W:NA_W + GQA_QW, :], preferred_element_type=F32)
        a += jnp.dot(mml_ref[r, :], wo_ref[NA_W + GQA_QW:, :], preferred_element_type=F32)
        acc.append(a)
    x1, gate_up = [], []
    for r, a in zip(rs, acc):
        xr = x_ref[r, :] + mod_ref[2:3, :] * _rms(a, g_ref[1:2, :])
        hb = (_rms(xr, g_ref[2:3, :]) * (1.0 + mod_ref[4:5, :]) + mod_ref[3:4, :]).astype(BF16)
        x1.append(xr)
        gate_up.append((jnp.dot(hb, wgu_ref[:, 0:FF_HIDDEN], preferred_element_type=F32),
                        jnp.dot(hb, wgu_ref[:, FF_HIDDEN:], preferred_element_type=F32)))
    f = []
    for gate, up in gate_up:
        act = (gate * jax.nn.sigmoid(gate) * up).astype(BF16)
        f.append(jnp.dot(act, wd_ref[...], preferred_element_type=F32))
    for r, xr, fr in zip(rs, x1, f):
        o_ref[r, :] = xr + mod_ref[5:6, :] * _rms(fr, g_ref[3:4, :])


def _post(x, mna, mgq, mml, mods, g_norm, w_out, w_gu, w_down, layer, tiles_per_row, first_row, tm):
    n = x.shape[0]
    resident = lambda shape: pl.BlockSpec((None,) + shape, lambda i: (layer, 0, 0), pipeline_mode=pl.Buffered(1))
    return pl.pallas_call(
        _post_kernel,
        grid=(n // tm,),
        in_specs=[
            pl.BlockSpec((tm, D_MODEL), lambda i: (i, 0)),
            pl.BlockSpec((tm, NA_W), lambda i: (i, 0)),
            pl.BlockSpec((tm, GQA_QW), lambda i: (i, 0)),
            pl.BlockSpec((tm, ML_W), lambda i: (i, 0)),
            pl.BlockSpec((None, None, 6, D_MODEL), _mod_row_map(layer, tiles_per_row, first_row)),
            pl.BlockSpec((None, 4, D_MODEL), lambda i: (layer, 0, 0)),
            resident((D_MODEL, D_MODEL)),
            resident((D_MODEL, 2 * FF_HIDDEN)),
            resident((FF_HIDDEN, D_MODEL)),
        ],
        out_specs=pl.BlockSpec((tm, D_MODEL), lambda i: (i, 0)),
        out_shape=jax.ShapeDtypeStruct((n, D_MODEL), F32),
        compiler_params=_params(1),
        name="post",
    )(x, mna, mgq, mml, mods, g_norm, w_out, w_gu, w_down)


def _pair_rms(x, gain, sums_on_mxu=False):
    xsq = x * x
    if sums_on_mxu:
        row = lax.broadcasted_iota(jnp.int32, (LANES, LANES), 0) < HEAD_DIM
        col = lax.broadcasted_iota(jnp.int32, (LANES, LANES), 1) < HEAD_DIM
        ones_blk = jnp.where(row == col, 1.0, 0.0).astype(BF16)
        sums = sum(jnp.dot(p, ones_blk, preferred_element_type=F32) for p in _split3(xsq))
    else:
        first = _lane_first(x.shape)
        sums = jnp.where(first, jnp.sum(jnp.where(first, xsq, 0.0), axis=-1, keepdims=True),
                         jnp.sum(jnp.where(first, 0.0, xsq), axis=-1, keepdims=True))
    return x * lax.rsqrt(sums * (1.0 / HEAD_DIM) + EPS) * gain


def _pair_gain(gqk_ref, row):
    return jnp.concatenate([gqk_ref[row:row + 1, :]] * (LANES // HEAD_DIM), axis=-1)


def _zero_other_layers(ref, layer):
    for other in range(ref.shape[0]):
        if other != layer:
            ref[other] = jnp.zeros(ref.shape[1:], ref.dtype)
    return ref.at[layer]


def _ctx_attn_kernel(zna_ref, zgq_ref, gqk_ref, *rest, layer, creates):
    mna_ref, mgq_ref, kvna_ref, kvgq_ref = rest[-4:]
    if creates:
        kvna_ref, kvgq_ref = _zero_other_layers(kvna_ref, layer), _zero_other_layers(kvgq_ref, layer)

    def store_t(ref, which, pair, x2):
        xt = x2.T
        for a in range(2):
            ref[which, 2 * pair + a] = xt[a * HEAD_DIM:(a + 1) * HEAD_DIM]

    pending = []
    for i in range(NA_HEADS // 2):
        cols = slice(i * LANES, (i + 1) * LANES)
        q2 = zna_ref[:, cols] * QK_SCALE
        k2 = zna_ref[:, NA_W + i * LANES:NA_W + (i + 1) * LANES]
        v2 = zna_ref[:, 2 * NA_W + i * LANES:2 * NA_W + (i + 1) * LANES]
        store_t(kvna_ref, 0, i, k2)
        store_t(kvna_ref, 1, i, v2)
        blocks = [{"k": k2.astype(BF16), "k_nt": True, "r": _pair_values(v2), "r_nt": False}]
        pending.append((mna_ref, cols, _pair_scores(_pair_queries(q2), blocks), blocks))

    gq, gk = _pair_gain(gqk_ref, 0), _pair_gain(gqk_ref, 1)
    k2 = _pair_rms(zgq_ref[:, GQA_QW:GQA_QW + GQA_KW], gk)
    v2 = zgq_ref[:, GQA_QW + GQA_KW:]
    store_t(kvgq_ref, 0, 0, k2)
    store_t(kvgq_ref, 1, 0, v2)
    blocks = [{"k": k2.astype(BF16), "k_nt": True, "r": _pair_values(v2), "r_nt": False}]
    for p in range(GQA_GROUP):
        cols = slice(p * LANES, (p + 1) * LANES)
        q2 = _pair_rms(zgq_ref[:, cols], gq) * QK_SCALE
        pending.append((mgq_ref, cols, _pair_scores(_pair_queries(q2), blocks), blocks))

    for ref, cols, scores, blocks in pending:
        ref[:, cols] = _pair_finish(scores, blocks).astype(BF16)


def _ctx_attn(zna, zgq, g_qk, kv_na_buf, kv_gq_buf, layer, batch, t):
    n = zna.shape[0]
    creates = kv_na_buf is None
    in_specs = [
        pl.BlockSpec((t, ZNA_W), lambda b: (b, 0)),
        pl.BlockSpec((t, ZGQ_W), lambda b: (b, 0)),
        pl.BlockSpec((None, 2, HEAD_DIM), lambda b: (layer, 0, 0)),
    ]
    args = [zna, zgq, g_qk]
    if creates:
        kv_spec = lambda heads: pl.BlockSpec((None, DEPTH, 2, heads, HEAD_DIM, t), lambda b: (b, 0, 0, 0, 0, 0))
        aliases = {}
    else:
        kv_spec = lambda heads: pl.BlockSpec((None, None, 2, heads, HEAD_DIM, t),
                                             lambda b: (b, layer, 0, 0, 0, 0))
        aliases = {len(args): 2, len(args) + 1: 3}
        in_specs += [pl.BlockSpec(memory_space=pl.ANY)] * 2
        args += [kv_na_buf, kv_gq_buf]
    return pl.pallas_call(
        functools.partial(_ctx_attn_kernel, layer=layer, creates=creates),
        grid=(batch,),
        in_specs=in_specs,
        out_specs=[
            pl.BlockSpec((t, NA_W), lambda b: (b, 0)),
            pl.BlockSpec((t, GQA_QW), lambda b: (b, 0)),
            kv_spec(NA_HEADS),
            kv_spec(GQA_KV_HEADS),
        ],
        out_shape=[
            jax.ShapeDtypeStruct((n, NA_W), BF16),
            jax.ShapeDtypeStruct((n, GQA_QW), BF16),
            jax.ShapeDtypeStruct((batch, DEPTH, 2, NA_HEADS, HEAD_DIM, t), F32),
            jax.ShapeDtypeStruct((batch, DEPTH, 2, GQA_KV_HEADS, HEAD_DIM, t), F32),
        ],
        input_output_aliases=aliases,
        compiler_params=_params(1),
        name="ctx_attn",
    )(*args)


def _lat_na_kernel(zna_ref, cache_ref, bias_ref, o_ref, *, rows):
    qn = NA_BLOCK_ROWS * GRID_W
    kn = NA_UNION_ROWS * GRID_W
    n_blocks = rows // NA_BLOCK_ROWS
    pattern_of = _na_block_patterns(rows)[1]
    prepared = []
    for i in range(NA_HEADS // 2):
        cols = slice(i * LANES, (i + 1) * LANES)
        q_pair = _pair_queries(zna_ref[:, cols] * QK_SCALE)
        k2 = zna_ref[:, NA_W + i * LANES:NA_W + (i + 1) * LANES].astype(BF16)
        kc = cache_ref[0, i].astype(BF16)
        scores = []
        for a in range(2):
            s_ctx = jnp.dot(q_pair[a], kc, preferred_element_type=F32)
            s_win = []
            for p in range(n_blocks):
                k0 = _na_union_start(p, rows) * GRID_W
                s_win.append(lax.dot_general(q_pair[a][p * qn:(p + 1) * qn], k2[k0:k0 + kn], NT_DIMS,
                                             preferred_element_type=F32) + bias_ref[2 * i + a, pattern_of[p]])
            scores.append((s_ctx, s_win))
        prepared.append(scores)
    for i in range(NA_HEADS // 2):
        cols = slice(i * LANES, (i + 1) * LANES)
        r = _pair_values(zna_ref[:, 2 * NA_W + i * LANES:2 * NA_W + (i + 1) * LANES])
        rc = _pair_values_t(cache_ref[1, i])
        acc = None
        for a in range(2):
            s_ctx, s_win = prepared[i][a]
            m_win = jnp.concatenate([s.max(axis=-1, keepdims=True) for s in s_win], axis=0)
            m = jnp.maximum(m_win, s_ctx.max(axis=-1, keepdims=True))
            term = lax.dot_general(jnp.exp(s_ctx - m).astype(BF16), rc[a], NT_DIMS, preferred_element_type=F32)
            wins = []
            for p in range(n_blocks):
                k0 = _na_union_start(p, rows) * GRID_W
                pw = jnp.exp(s_win[p] - m[p * qn:(p + 1) * qn]).astype(BF16)
                wins.append(jnp.dot(pw, r[a][k0:k0 + kn], preferred_element_type=F32))
            term = term + jnp.concatenate(wins, axis=0)
            acc = term if acc is None else acc + term
        o_ref[:, cols] = (acc[:, :LANES] / acc[:, LANES:]).astype(BF16)


def _lat_na(zna, cache_t, bias, layer, batch, t):
    n = zna.shape[0]
    rows = t // GRID_W
    past = cache_t.shape[-1]
    n_blocks = len(_na_block_patterns(rows)[0])
    qn = NA_BLOCK_ROWS * GRID_W
    kn = NA_UNION_ROWS * GRID_W
    return pl.pallas_call(
        functools.partial(_lat_na_kernel, rows=rows),
        grid=(batch,),
        in_specs=[
            pl.BlockSpec((t, ZNA_W), lambda b: (b, 0)),
            pl.BlockSpec((None, None, 2, NA_HEADS // 2, LANES, past), lambda b: (b, layer, 0, 0, 0, 0)),
            pl.BlockSpec((None, NA_HEADS, n_blocks, qn, kn), lambda b: (layer, 0, 0, 0, 0)),
        ],
        out_specs=pl.BlockSpec((t, NA_W), lambda b: (b, 0)),
        out_shape=jax.ShapeDtypeStruct((n, NA_W), BF16),
        compiler_params=_params(1),
        name="lat_na",
    )(zna, cache_t, bias)


def _rope_tables(t):
    half = HEAD_DIM // 2
    quarter = half // 2
    inv = 1.0 / (ROPE_BASE ** (jnp.arange(quarter, dtype=F32) / quarter))
    tt = jnp.arange(t)
    row = (tt // GRID_W).astype(F32)
    col = (tt % GRID_W).astype(F32)
    ang_r = row[:, None] * inv[None, :]
    ang_c = col[:, None] * inv[None, :]
    cos = jnp.concatenate([jnp.cos(ang_r)] * 2 + [jnp.cos(ang_c)] * 2, axis=-1)
    sin = jnp.concatenate([-jnp.sin(ang_r), jnp.sin(ang_r), -jnp.sin(ang_c), jnp.sin(ang_c)], axis=-1)
    reps = LANES // HEAD_DIM
    return jnp.tile(cos, (1, reps)), jnp.tile(sin, (1, reps))


def _pair_rope(xn, cos, sin):
    quarter = HEAD_DIM // 4
    lane = lax.broadcasted_iota(jnp.int32, xn.shape, 1)
    lower = (lane & (2 * quarter - 1)) < quarter
    partner = jnp.where(lower, pltpu.roll(xn, LANES - quarter, 1), pltpu.roll(xn, quarter, 1))
    return xn * cos + partner * sin


def _lat_gqa_kernel(zgq_ref, cache_ref, gqk_ref, cos_ref, sin_ref, o_ref):
    cos = cos_ref[...]
    sin = sin_ref[...]
    gq, gk = _pair_gain(gqk_ref, 0), _pair_gain(gqk_ref, 1)
    keys = _pair_rope(_pair_rms(zgq_ref[:, GQA_QW:GQA_QW + GQA_KW], gk, True), cos, sin).astype(BF16)
    blocks = [
        {"k": keys, "k_nt": True, "r": _pair_values(zgq_ref[:, GQA_QW + GQA_KW:]), "r_nt": False},
        {"k": cache_ref[0].astype(BF16), "k_nt": False, "r": _pair_values_t(cache_ref[1]), "r_nt": True},
    ]
    def scores(p):
        x = _pair_rope(_pair_rms(zgq_ref[:, p * LANES:(p + 1) * LANES], gq, True), cos, sin) * QK_SCALE
        return _pair_scores(_pair_queries(x), blocks)

    nxt = scores(0)
    for p in range(GQA_GROUP):
        cur = nxt
        if p + 1 < GQA_GROUP:
            nxt = scores(p + 1)
        o_ref[:, p * LANES:(p + 1) * LANES] = _pair_finish(cur, blocks).astype(BF16)


def _lat_gqa(zgq, cache_t, g_qk, cos, sin, layer, batch, t):
    n = zgq.shape[0]
    past = cache_t.shape[-1]
    return pl.pallas_call(
        _lat_gqa_kernel,
        grid=(batch,),
        in_specs=[
            pl.BlockSpec((t, ZGQ_W), lambda b: (b, 0)),
            pl.BlockSpec((None, None, 2, GQA_KW, past), lambda b: (b, layer, 0, 0, 0)),
            pl.BlockSpec((None, 2, HEAD_DIM), lambda b: (layer, 0, 0)),
            pl.BlockSpec((t, LANES), lambda b: (0, 0)),
            pl.BlockSpec((t, LANES), lambda b: (0, 0)),
        ],
        out_specs=pl.BlockSpec((t, GQA_QW), lambda b: (b, 0)),
        out_shape=jax.ShapeDtypeStruct((n, GQA_QW), BF16),
        compiler_params=_params(1),
        name="lat_gqa",
    )(zgq, cache_t, g_qk, cos, sin)


def _split3(x):
    x1 = x.astype(BF16)
    r1 = x - x1.astype(F32)
    x2 = r1.astype(BF16)
    x3 = (r1 - x2.astype(F32)).astype(BF16)
    return x1, x2, x3


def _log_sigmoid(x):
    return jnp.minimum(x, 0.0) - jnp.log1p(jnp.exp(-jnp.abs(x)))


def _mlstm_select_matrix():
    H = MLSTM_HEADS
    sel = np.zeros((MLSTM_SEL_ROWS, (H // 2) * 2 * MLSTM_TILE_KINDS * LANES), np.float32)
    for j in range(H // 2):
        for d in range(2):
            for q in range(MLSTM_TILE_KINDS):
                for a in range(2):
                    col0 = ((j * 2 + d) * MLSTM_TILE_KINDS + q) * LANES + a * HEAD_DIM
                    sel[q * N_SCANS + d * H + 2 * j + a, col0:col0 + HEAD_DIM] = 1.0
    return sel


def _mlstm_kernel(zml_ref, gi_ref, gf_ref, grow_ref, c0_ref, n0_ref, m0_ref, gml_ref, sel_ref, *rest,
                  t, emit_state, creates_layer):
    tri_s, st_s, dst_s, cst_s, row_s = rest[-5:]
    outs = rest[:-5]
    if emit_state:
        o_ref, cf_ref, nf_ref, mf_ref = outs[-4:]
        if creates_layer is not None:
            cf_ref, nf_ref, mf_ref = (_zero_other_layers(r, creates_layer) for r in (cf_ref, nf_ref, mf_ref))
    else:
        o_ref = outs[-1]
    L = MLSTM_CHUNK
    H = MLSTM_HEADS
    HD = HEAD_DIM
    NP = H // 2
    nc = t // L
    tb = tri_s.shape[-1]

    @pl.when(pl.program_id(0) == 0)
    def _():
        ti = lax.broadcasted_iota(jnp.int32, (tb, tb), 0)
        ui = lax.broadcasted_iota(jnp.int32, (tb, tb), 1)
        same = (ti & -L) == (ui & -L)
        tri_s[0] = jnp.where(same & (ui <= ti), 1.0, 0.0).astype(BF16)
        tri_s[1] = jnp.where(same & (ui >= ti), 1.0, 0.0).astype(BF16)

    lower, upper = tri_s[0], tri_s[1]

    def chunk_sums_cols(x):
        parts = _split3(x)
        pre, suf = [], []
        for i in range(t // tb):
            blk = [p[i * tb:(i + 1) * tb] for p in parts]
            pre.append(sum(jnp.dot(lower, p, preferred_element_type=F32) for p in blk))
            suf.append(sum(jnp.dot(upper, p, preferred_element_type=F32) for p in blk))
        return jnp.concatenate(pre, axis=0), jnp.concatenate(suf, axis=0)

    def chunk_sums_rows(x):
        parts = _split3(x)
        pre, suf = [], []
        for i in range(t // tb):
            blk = [p[:, i * tb:(i + 1) * tb] for p in parts]
            pre.append(sum(jnp.dot(p, upper, preferred_element_type=F32) for p in blk))
            suf.append(sum(jnp.dot(p, lower, preferred_element_type=F32) for p in blk))
        return jnp.concatenate(pre, axis=1), jnp.concatenate(suf, axis=1)

    pre_c, suf_c = chunk_sums_cols(_log_sigmoid(gf_ref[...]))
    lane_c = lax.broadcasted_iota(jnp.int32, (t, N_SCANS), 1)
    b3 = jnp.where(lane_c < H, pre_c, suf_c).reshape(nc, L, N_SCANS)
    i3 = gi_ref[...].reshape(nc, L, N_SCANS)
    fwd3 = lax.broadcasted_iota(jnp.int32, (nc, 1, N_SCANS), 2) < H
    b_end3 = jnp.where(fwd3, b3[:, L - 1:L, :], b3[:, 0:1, :])
    lw_end3 = b_end3 - b3 + i3
    a3 = jnp.max(lw_end3, axis=1, keepdims=True)
    wloc3 = jnp.exp(lw_end3 - a3)

    fwd1 = lax.broadcasted_iota(jnp.int32, (1, N_SCANS), 1) < H
    m = m0_ref[...]
    m_start, carry_decay, contrib_scale = [], [], []
    for j in range(nc):
        a_j = jnp.where(fwd1, a3[j], a3[nc - 1 - j])
        g_j = jnp.where(fwd1, b_end3[j], b_end3[nc - 1 - j])
        m_start.append(m)
        m_next = jnp.maximum(g_j + m, a_j)
        carry_decay.append(jnp.exp(g_j + m - m_next))
        contrib_scale.append(jnp.exp(a_j - m_next))
        m = m_next
    mst3 = jnp.concatenate([jnp.where(fwd1, m_start[c], m_start[nc - 1 - c])[None] for c in range(nc)], axis=0)

    cols = jnp.concatenate([b3.reshape(t, N_SCANS), wloc3.reshape(t, N_SCANS)], axis=1)
    tiles_all = sum(jnp.dot(p, sel_ref[...], preferred_element_type=F32) for p in _split3(cols))
    tile_w = MLSTM_TILE_KINDS * LANES

    def tiles(j, d):
        x = tiles_all[:, (2 * j + d) * tile_w:(2 * j + d + 1) * tile_w]
        return [x[:, q * LANES:(q + 1) * LANES].reshape(nc, L, LANES) for q in range(MLSTM_TILE_KINDS)]

    gr = grow_ref[...]
    pre_r, suf_r = chunk_sums_rows(_log_sigmoid(gr))
    sub_r = lax.broadcasted_iota(jnp.int32, (N_SCANS, t), 0)
    rowv = gr[0:N_SCANS] - jnp.where(sub_r < H, pre_r[N_SCANS:], suf_r[N_SCANS:])
    for j in range(NP):
        for d in range(2):
            e = d * H + 2 * j
            for c in range(nc):
                row_s[2 * j + d, c] = jnp.concatenate(
                    [rowv[e:e + 1, c * L:(c + 1) * L], rowv[e + 1:e + 2, c * L:(c + 1) * L]], axis=1)

    lane_a = lax.broadcasted_iota(jnp.int32, (1, 1, LANES), 2) < HD
    sub_a = lax.broadcasted_iota(jnp.int32, (1, 2 * HD, 1), 1) < HD
    diag = sub_a == lane_a
    diag4 = jnp.concatenate([diag] * 4, axis=2)

    def stack_heads(x3):
        return jnp.concatenate([jnp.where(lane_a, x3, 0.0), jnp.where(lane_a, 0.0, x3)], axis=1)

    def pair_cols(base, j):
        return slice(base + j * LANES, base + (j + 1) * LANES)

    zero_blk = jnp.zeros((HD, HD), F32)
    for j in range(NP):
        cols_d = []
        for d in range(2):
            ca, cb = c0_ref[d, 2 * j].T, c0_ref[d, 2 * j + 1].T
            na = jnp.broadcast_to(n0_ref[d, 2 * j:2 * j + 1, :], (HD, HD)).T
            nb = jnp.broadcast_to(n0_ref[d, 2 * j + 1:2 * j + 2, :], (HD, HD)).T
            top = jnp.concatenate([ca, zero_blk, na, zero_blk], axis=1)
            bot = jnp.concatenate([zero_blk, cb, zero_blk, nb], axis=1)
            cols_d.append(jnp.concatenate([top, bot], axis=0))
        st_s[j] = jnp.concatenate(cols_d, axis=1)

    for j in range(NP):
        k3 = (zml_ref[:, pair_cols(ML_W, j)] * QK_SCALE).reshape(nc, L, LANES).astype(BF16)
        v3 = zml_ref[:, pair_cols(2 * ML_W, j)].reshape(nc, L, LANES)
        rhs = []
        for d in range(2):
            wl = tiles(j, d)[1]
            rhs += [v3 * wl, wl]
        rhs = jnp.concatenate(rhs, axis=2).astype(BF16)
        contrib = jnp.einsum("csk,csn->ckn", k3, rhs, preferred_element_type=F32)
        dst_s[j] = jnp.where(diag4, contrib, 0.0)

    def lane_scale(v, j):
        pieces = []
        for d in range(2):
            sa = jnp.broadcast_to(v[:, d * H + 2 * j:d * H + 2 * j + 1], (1, HD))
            sb = jnp.broadcast_to(v[:, d * H + 2 * j + 1:d * H + 2 * j + 2], (1, HD))
            pieces += [sa, sb, sa, sb]
        return jnp.concatenate(pieces, axis=1)

    for j in range(NP):
        st = st_s[j]
        for step in range(nc):
            cb = nc - 1 - step
            stb = st.astype(BF16)
            cst_s[j, step, :, 0:2 * LANES] = stb[:, 0:2 * LANES]
            cst_s[j, cb, :, 2 * LANES:] = stb[:, 2 * LANES:]
            delta = jnp.concatenate([dst_s[j, step, :, 0:2 * LANES], dst_s[j, cb, :, 2 * LANES:]], axis=1)
            st = lane_scale(carry_decay[step], j) * st + lane_scale(contrib_scale[step], j) * delta
        st_s[j] = st

    sidx = lax.broadcasted_iota(jnp.int32, (1, L, LANES), 2) & (HD - 1)
    tidx = lax.broadcasted_iota(jnp.int32, (1, L, LANES), 1)
    masks = (sidx <= tidx, sidx >= tidx)
    ones_blk = jnp.broadcast_to(jnp.where(diag, 1.0, 0.0).astype(BF16), (nc, 2 * HD, LANES))
    neg_inf = -jnp.inf
    for j in range(NP):
        q3 = zml_ref[:, pair_cols(0, j)].reshape(nc, L, LANES).astype(BF16)
        k3 = (zml_ref[:, pair_cols(ML_W, j)] * QK_SCALE).reshape(nc, L, LANES)
        v3 = zml_ref[:, pair_cols(2 * ML_W, j)].reshape(nc, L, LANES)
        qk = jnp.einsum("ctd,cnd->ctn", q3, stack_heads(k3).astype(BF16), preferred_element_type=F32)
        v_aug = jnp.concatenate([stack_heads(v3).astype(BF16), ones_blk], axis=2)
        out = None
        for d in range(2):
            b_t = tiles(j, d)[0]
            e = d * H + 2 * j
            bm_t = b_t + jnp.where(lane_a, mst3[:, :, e:e + 1], mst3[:, :, e + 1:e + 2])
            logw = jnp.where(masks[d], b_t + row_s[2 * j + d], neg_inf)
            rmax_a = jnp.max(jnp.where(lane_a, logw, neg_inf), axis=-1, keepdims=True)
            rmax_b = jnp.max(jnp.where(lane_a, neg_inf, logw), axis=-1, keepdims=True)
            m_t = jnp.maximum(jnp.where(lane_a, rmax_a, rmax_b), bm_t)
            s = qk * jnp.exp(logw - m_t)
            decay = jnp.exp(bm_t - m_t)
            sv = jnp.einsum("cts,csn->ctn", s.astype(BF16), v_aug, preferred_element_type=F32)
            state = cst_s[j, :, :, 2 * d * LANES:2 * (d + 1) * LANES]
            inter = jnp.einsum("ctk,ckn->ctn", q3, state, preferred_element_type=F32)
            num = sv[:, :, 0:LANES] + decay * inter[:, :, 0:LANES]
            den = sv[:, :, LANES:] + decay * inter[:, :, LANES:]
            h_d = num / jnp.maximum(jnp.abs(den), jnp.exp(-m_t))
            out = h_d if out is None else out + h_d
        cols = pair_cols(0, j)
        og = jax.nn.sigmoid(zml_ref[:, pair_cols(3 * ML_W, j)])
        o_ref[:, cols] = (_pair_rms(out.reshape(t, LANES), gml_ref[:, cols]) * og).astype(BF16)

    if emit_state:
        for j in range(NP):
            st = st_s[j]
            for d in range(2):
                for a in range(2):
                    rows = slice(a * HD, (a + 1) * HD)
                    c0 = 2 * d * LANES + a * HD
                    cf_ref[d, 2 * j + a] = st[rows, c0:c0 + HD].T
                    nf_ref[d, 2 * j + a:2 * j + a + 1, :] = st[rows, c0 + LANES:c0 + LANES + HD].T[0:1, :]
        mf_ref[...] = m


def _mlstm(zml, gi, gf, grow, c0, n0, m0, g_ml, layer, batch, t, state_out, state_layer):
    emit_state = state_out is not None
    creates = emit_state and not isinstance(state_out, tuple)
    n = zml.shape[0]
    H = MLSTM_HEADS
    L = MLSTM_CHUNK
    nc = t // L
    sel = jnp.asarray(_mlstm_select_matrix(), BF16)
    tri_block = min(t, MXU_DIM)
    assert L & (L - 1) == 0 and tri_block % L == 0 and t % tri_block == 0
    if state_layer is None:
        c_spec = pl.BlockSpec((None, 2, H, HEAD_DIM, HEAD_DIM), lambda b: (0, 0, 0, 0, 0))
        n_spec = pl.BlockSpec((None, 2, H, HEAD_DIM), lambda b: (0, 0, 0, 0))
        m_spec = pl.BlockSpec((None, 1, N_SCANS), lambda b: (0, 0, 0))
    else:
        c_spec = pl.BlockSpec((None, None, 2, H, HEAD_DIM, HEAD_DIM), lambda b: (b, state_layer, 0, 0, 0, 0))
        n_spec = pl.BlockSpec((None, None, 2, H, HEAD_DIM), lambda b: (b, state_layer, 0, 0, 0))
        m_spec = pl.BlockSpec((None, None, 1, N_SCANS), lambda b: (b, state_layer, 0, 0))
    in_specs = [
        pl.BlockSpec((t, ZML_W), lambda b: (b, 0)),
        pl.BlockSpec((t, N_SCANS), lambda b: (b, 0)),
        pl.BlockSpec((t, N_SCANS), lambda b: (b, 0)),
        pl.BlockSpec((N_GATES, t), lambda b: (0, b)),
        c_spec, n_spec, m_spec,
        pl.BlockSpec((None, 1, ML_W), lambda b: (layer, 0, 0)),
        pl.BlockSpec(sel.shape, lambda b: (0, 0)),
    ]
    args = [zml, gi, gf, grow, c0, n0, m0, g_ml, sel]
    out_specs = [pl.BlockSpec((t, ML_W), lambda b: (b, 0))]
    out_shape = [jax.ShapeDtypeStruct((n, ML_W), BF16)]
    aliases = {}
    if emit_state:
        tails = [(2, H, HEAD_DIM, HEAD_DIM), (2, H, HEAD_DIM), (1, N_SCANS)]
        for tail in tails:
            zeros = (0,) * len(tail)
            if creates:
                out_specs.append(pl.BlockSpec((None, DEPTH) + tail, lambda b, zeros=zeros: (b, 0) + zeros))
            else:
                out_specs.append(pl.BlockSpec((None, None) + tail, lambda b, zeros=zeros: (b, layer) + zeros))
            out_shape.append(jax.ShapeDtypeStruct((batch, DEPTH) + tail, F32))
        if not creates:
            aliases = {len(args) + i: 1 + i for i in range(len(state_out))}
            in_specs += [pl.BlockSpec(memory_space=pl.ANY)] * len(state_out)
            args += list(state_out)
    return pl.pallas_call(
        functools.partial(_mlstm_kernel, t=t, emit_state=emit_state, creates_layer=layer if creates else None),
        grid=(batch,),
        in_specs=in_specs,
        out_specs=out_specs,
        out_shape=out_shape,
        input_output_aliases=aliases,
        scratch_shapes=[
            pltpu.VMEM((2, tri_block, tri_block), BF16),
            pltpu.VMEM((H // 2, 2 * HEAD_DIM, 4 * LANES), F32),
            pltpu.VMEM((H // 2, nc, 2 * HEAD_DIM, 4 * LANES), F32),
            pltpu.VMEM((H // 2, nc, 2 * HEAD_DIM, 4 * LANES), BF16),
            pltpu.VMEM((H, nc, 1, LANES), F32),
        ],
        compiler_params=_params(1),
        name="mlstm",
    )(*args)


def _layer_path(x, mods, layer, first_row, tiles_row_tokens, weights, mixers):
    per_row = lambda tile: None if tiles_row_tokens is None else tiles_row_tokens // tile
    g_norm = weights["g_norm"]
    zna, zgq, zml, gi, gf, grow = _inproj(x, mods, g_norm, weights["w_in"], layer, per_row(INPROJ_TILE),
                                          first_row, INPROJ_TILE)
    mna, mgq, mml, extra = mixers(zna, zgq, zml, gi, gf, grow)
    x = _post(x, mna, mgq, mml, mods, g_norm, weights["w_out"], weights["w_gu"], weights["w_down"], layer,
              per_row(POST_TILE), first_row, POST_TILE)
    return x, extra


def _gqa_pair_order():
    return [a * GQA_GROUP + p for p in range(GQA_GROUP) for a in range(GQA_KV_HEADS)]


def _take_blocks(x, axis, base, width, order):
    return jnp.concatenate([lax.slice_in_dim(x, base + width * o, base + width * (o + 1), axis=axis)
                            for o in order], axis=axis)


def kernel(x_prompt, x_sample, cache_na_kv, cache_gqa_kv, state_mlstm_C, state_mlstm_n, state_mlstm_m,
           c, c_ctx, w_in, b_gates, w_out, g_norm, g_qk, g_mlstm, na_bias, w_ada, b_ada, w_gu, w_down):
    batch, seq, _ = x_prompt.shape
    dec_batch, dec_seq, _ = x_sample.shape
    past = cache_na_kv.shape[-2]
    assert dec_batch + 1 <= N_MOD_ROWS and dec_seq % GRID_W == 0 and GQA_KV_HEADS == 2

    cvec = jnp.concatenate([c_ctx[None, :], c, jnp.zeros((N_MOD_ROWS - 1 - dec_batch, D_MODEL), F32)], axis=0)
    mods = _adaln(cvec, w_ada, b_ada).reshape(DEPTH, N_MOD_ROWS, 6, D_MODEL)
    bias = _na_bias_expand(na_bias, dec_seq // GRID_W)
    cos, sin = _rope_tables(dec_seq)

    pair_order = _gqa_pair_order()
    w_in_t = jnp.swapaxes(w_in, 1, 2).astype(BF16)
    w_gq = jnp.concatenate([_take_blocks(w_in_t, 1, OFF_GQ, HEAD_DIM, pair_order),
                            w_in_t[:, OFF_GQ + GQA_QW:OFF_ML]], axis=1)
    w_out_rows = jnp.concatenate([w_out[:, :NA_W], _take_blocks(w_out, 1, NA_W, HEAD_DIM, pair_order),
                                  w_out[:, NA_W + GQA_QW:]], axis=1)
    weights = {
        "g_norm": g_norm,
        "w_in": {
            "all": w_in_t,
            "gq": w_gq,
            "b_row": b_gates[:, :, None],
        },
        "w_out": w_out_rows.astype(BF16),
        "w_gu": w_gu.astype(BF16),
        "w_down": w_down.astype(BF16),
    }
    g_ml = g_mlstm.reshape(DEPTH, 1, ML_W)
    zero_c = jnp.zeros((1, 2, MLSTM_HEADS, HEAD_DIM, HEAD_DIM), F32)
    zero_n = jnp.zeros((1, 2, MLSTM_HEADS, HEAD_DIM), F32)
    zero_m = jnp.zeros((1, 1, N_SCANS), F32)
    m0_lat = state_mlstm_m.reshape(dec_batch, DEPTH, 1, N_SCANS)
    cache_na_t = jnp.swapaxes(cache_na_kv, -1, -2).reshape(dec_batch, DEPTH, 2, NA_HEADS // 2, LANES, past)
    cache_gq_t = jnp.swapaxes(cache_gqa_kv, -1, -2).reshape(dec_batch, DEPTH, 2, GQA_KW, past)

    xp = x_prompt.reshape(batch * seq, D_MODEL)
    xs = x_sample.reshape(dec_batch * dec_seq, D_MODEL)
    kv_na = kv_gq = None
    states = "create"
    for layer in range(DEPTH):
        def ctx_mixers(zna, zgq, zml, gi, gf, grow, layer=layer, kv_na=kv_na, kv_gq=kv_gq, states=states):
            mna, mgq, kv_na, kv_gq = _ctx_attn(zna, zgq, g_qk, kv_na, kv_gq, layer, batch, seq)
            mml, *states = _mlstm(zml, gi, gf, grow, zero_c, zero_n, zero_m, g_ml, layer, batch, seq,
                                  states, None)
            return mna, mgq, mml, (kv_na, kv_gq, tuple(states))

        def lat_mixers(zna, zgq, zml, gi, gf, grow, layer=layer):
            mna = _lat_na(zna, cache_na_t, bias, layer, dec_batch, dec_seq)
            mgq = _lat_gqa(zgq, cache_gq_t, g_qk, cos, sin, layer, dec_batch, dec_seq)
            (mml,) = _mlstm(zml, gi, gf, grow, state_mlstm_C, state_mlstm_n, m0_lat, g_ml, layer,
                            dec_batch, dec_seq, None, layer)
            return mna, mgq, mml, None

        xp, (kv_na, kv_gq, states) = _layer_path(xp, mods, layer, 0, None, weights, ctx_mixers)
        xs, _ = _layer_path(xs, mods, layer, 1, dec_seq, weights, lat_mixers)

    new_c, new_n, new_m = states
    return (xp.reshape(batch, seq, D_MODEL), xs.reshape(dec_batch, dec_seq, D_MODEL),
            jnp.swapaxes(kv_na, -1, -2), jnp.swapaxes(kv_gq, -1, -2),
            new_c, new_n, new_m.reshape(batch, DEPTH, 2, MLSTM_HEADS))
```

```python
import functools

import jax
import jax.numpy as jnp
import numpy as np
from jax import lax
from jax.experimental import pallas as pl
from jax.experimental.pallas import tpu as pltpu

D_MODEL = 1024
DEPTH = 4
GRID_W = 64
HEAD_DIM = 64
NA_HEADS = 4
GQA_Q_HEADS = 8
GQA_KV_HEADS = 2
GQA_GROUP = GQA_Q_HEADS // GQA_KV_HEADS
MLSTM_HEADS = 4
NA_WIN_ROWS = 8
NA_WIN_COLS = 16
MLSTM_CHUNK = 64
ROPE_BASE = 10000.0
EPS = 1e-6
NEG = -1e30
NA_W = NA_HEADS * HEAD_DIM
GQA_QW = GQA_Q_HEADS * HEAD_DIM
GQA_KW = GQA_KV_HEADS * HEAD_DIM
ML_W = MLSTM_HEADS * HEAD_DIM
N_GATES = 4 * MLSTM_HEADS
N_SCANS = 2 * MLSTM_HEADS
MLSTM_TILE_KINDS = 2
MLSTM_SEL_ROWS = MLSTM_TILE_KINDS * N_SCANS
FF_HIDDEN = ((8 * D_MODEL + 3 * 256 - 1) // (3 * 256)) * 256
QK_SCALE = HEAD_DIM ** -0.5

ZNA_W = 3 * NA_W
ZGQ_W = GQA_QW + 2 * GQA_KW
ZML_W = 4 * ML_W
OFF_GQ = ZNA_W
OFF_ML = ZNA_W + ZGQ_W
OFF_GATES = OFF_ML + ZML_W

LANES = 128
MXU_DIM = 256
N_MOD_ROWS = 16
NA_BLOCK_ROWS = 2
NA_UNION_ROWS = NA_WIN_ROWS + NA_BLOCK_ROWS - 1

F32 = jnp.float32
BF16 = jnp.bfloat16
VMEM_LIMIT = 52 * 1024 * 1024
INPROJ_TILE = 1024
INPROJ_MIN_STEPS = 8
POST_TILE = 512
ADALN_TILE = 3072
POST_ROW_GROUPS = 2

NT_DIMS = (((1,), (1,)), ((), ()))


def _params(n_axes):
    return pltpu.CompilerParams(dimension_semantics=("arbitrary",) * n_axes,
                                vmem_limit_bytes=VMEM_LIMIT)


def _rms(x, g):
    return x * lax.rsqrt(jnp.mean(x * x, axis=-1, keepdims=True) + EPS) * g


def _lane_first(shape):
    return lax.broadcasted_iota(jnp.int32, shape, len(shape) - 1) < HEAD_DIM


def _pair_queries(x):
    first = _lane_first(x.shape)
    return jnp.where(first, x, 0.0).astype(BF16), jnp.where(first, 0.0, x).astype(BF16)


def _pair_values(v2):
    first = _lane_first(v2.shape)
    ones, zeros = jnp.ones_like(v2), jnp.zeros_like(v2)
    r0 = jnp.concatenate([jnp.where(first, v2, 0.0), jnp.where(first, ones, zeros)], axis=1)
    r1 = jnp.concatenate([jnp.where(first, 0.0, v2), jnp.where(first, zeros, ones)], axis=1)
    return r0.astype(BF16), r1.astype(BF16)


def _pair_values_t(vt2):
    first = lax.broadcasted_iota(jnp.int32, vt2.shape, 0) < HEAD_DIM
    ones, zeros = jnp.ones_like(vt2), jnp.zeros_like(vt2)
    r0 = jnp.concatenate([jnp.where(first, vt2, 0.0), jnp.where(first, ones, zeros)], axis=0)
    r1 = jnp.concatenate([jnp.where(first, 0.0, vt2), jnp.where(first, zeros, ones)], axis=0)
    return r0.astype(BF16), r1.astype(BF16)


def _pair_scores(q_pair, blocks):
    out = []
    for a in range(2):
        scores = []
        for blk in blocks:
            if blk["k_nt"]:
                s = lax.dot_general(q_pair[a], blk["k"], NT_DIMS, preferred_element_type=F32)
            else:
                s = jnp.dot(q_pair[a], blk["k"], preferred_element_type=F32)
            if blk.get("bias") is not None:
                s = s + blk["bias"][a]
            scores.append(s)
        out.append(scores)
    return out


def _pair_finish(all_scores, blocks):
    acc = None
    for a in range(2):
        scores = all_scores[a]
        m = scores[0].max(axis=-1, keepdims=True)
        for s in scores[1:]:
            m = jnp.maximum(m, s.max(axis=-1, keepdims=True))
        for s, blk in zip(scores, blocks):
            p = jnp.exp(s - m).astype(BF16)
            if blk["r_nt"]:
                term = lax.dot_general(p, blk["r"][a], NT_DIMS, preferred_element_type=F32)
            else:
                term = jnp.dot(p, blk["r"][a], preferred_element_type=F32)
            acc = term if acc is None else acc + term
    return acc[:, :LANES] / acc[:, LANES:]


def _adaln_kernel(c_ref, w_ref, b_ref, o_ref):
    c = c_ref[...]
    a = c * jax.nn.sigmoid(c)
    o_ref[...] = jnp.dot(a.astype(BF16), w_ref[...].astype(BF16),
                         preferred_element_type=F32) + b_ref[...]


def _adaln(cvec, w_ada, b_ada):
    tn = ADALN_TILE
    return pl.pallas_call(
        _adaln_kernel,
        grid=(DEPTH, 6 * D_MODEL // tn),
        in_specs=[
            pl.BlockSpec((N_MOD_ROWS, D_MODEL), lambda l, j: (0, 0)),
            pl.BlockSpec((None, D_MODEL, tn), lambda l, j: (l, 0, j)),
            pl.BlockSpec((None, 1, tn), lambda l, j: (l, 0, j)),
        ],
        out_specs=pl.BlockSpec((None, N_MOD_ROWS, tn), lambda l, j: (l, 0, j)),
        out_shape=jax.ShapeDtypeStruct((DEPTH, N_MOD_ROWS, 6 * D_MODEL), F32),
        compiler_params=_params(2),
        name="adaln",
    )(cvec, w_ada, b_ada.reshape(DEPTH, 1, 6 * D_MODEL))


def _na_r0(r, rows):
    return min(max(r - NA_WIN_ROWS // 2, 0), rows - NA_WIN_ROWS)


def _na_union_start(p, rows):
    return min(_na_r0(NA_BLOCK_ROWS * p, rows), rows - NA_UNION_ROWS)


def _na_block_patterns(rows):
    patterns, index = [], []
    for p in range(rows // NA_BLOCK_ROWS):
        start = _na_union_start(p, rows)
        pattern = []
        for a in range(NA_BLOCK_ROWS):
            r = NA_BLOCK_ROWS * p + a
            r0 = _na_r0(r, rows)
            pattern.append(tuple(start + j - r + NA_WIN_ROWS - 1 if r0 <= start + j < r0 + NA_WIN_ROWS else None
                                 for j in range(NA_UNION_ROWS)))
        pattern = tuple(pattern)
        if pattern not in patterns:
            patterns.append(pattern)
        index.append(patterns.index(pattern))
    return patterns, index


def _na_bias_kernel(tbl_ref, o_ref, *, rows):
    l = pl.program_id(0)
    h = pl.program_id(1)
    qi = lax.broadcasted_iota(jnp.int32, (GRID_W, GRID_W), 0)
    ki = lax.broadcasted_iota(jnp.int32, (GRID_W, GRID_W), 1)
    dc = jnp.clip(ki - qi, -(NA_WIN_COLS - 1), NA_WIN_COLS - 1) + NA_WIN_COLS - 1
    c0 = jnp.clip(qi - NA_WIN_COLS // 2, 0, GRID_W - NA_WIN_COLS)
    col_ok = (ki >= c0) & (ki < c0 + NA_WIN_COLS)
    n_dr = 2 * NA_WIN_ROWS - 1
    n_dc = 2 * NA_WIN_COLS - 1
    tiles = []
    for dr in range(n_dr):
        t = jnp.zeros((GRID_W, GRID_W), F32)
        for d in range(n_dc):
            t = jnp.where(dc == d, tbl_ref[((l * NA_HEADS + h) * n_dr + dr) * n_dc + d], t)
        tiles.append(jnp.where(col_ok, t, NEG))
    neg_tile = jnp.full((GRID_W, GRID_W), NEG, F32)
    for i, pattern in enumerate(_na_block_patterns(rows)[0]):
        for a in range(NA_BLOCK_ROWS):
            for j in range(NA_UNION_ROWS):
                dr = pattern[a][j]
                tile = neg_tile if dr is None else tiles[dr]
                o_ref[i, a * GRID_W:(a + 1) * GRID_W, j * GRID_W:(j + 1) * GRID_W] = tile


def _na_bias_expand(na_bias, rows):
    n_blocks = len(_na_block_patterns(rows)[0])
    qn = NA_BLOCK_ROWS * GRID_W
    kn = NA_UNION_ROWS * GRID_W
    return pl.pallas_call(
        functools.partial(_na_bias_kernel, rows=rows),
        grid=(DEPTH, NA_HEADS),
        in_specs=[pl.BlockSpec(memory_space=pltpu.SMEM)],
        out_specs=pl.BlockSpec((None, None, n_blocks, qn, kn), lambda l, h: (l, h, 0, 0, 0)),
        out_shape=jax.ShapeDtypeStruct((DEPTH, NA_HEADS, n_blocks, qn, kn), F32),
        compiler_params=_params(2),
        name="na_bias_expand",
    )(na_bias.reshape(-1))


def _mod_row_map(layer, tiles_per_row, first_row):
    if tiles_per_row is None:
        return lambda i: (layer, first_row, 0, 0)
    return lambda i: (layer, first_row + i // tiles_per_row, 0, 0)


def _inproj_kernel(x_ref, mod_ref, g_ref, wna_ref, wgq_ref, wml0_ref, wml1_ref, wg_ref, bg_ref,
                   zna_ref, zgq_ref, zml_ref, gi_ref, gf_ref, grow_ref):
    x = x_ref[...]
    h = _rms(x, g_ref[0:1, :]) * (1.0 + mod_ref[1:2, :]) + mod_ref[0:1, :]
    hb = h.astype(BF16)
    zna_ref[...] = lax.dot_general(hb, wna_ref[...], NT_DIMS, preferred_element_type=F32)
    zgq_ref[...] = lax.dot_general(hb, wgq_ref[...], NT_DIMS, preferred_element_type=F32)
    half = ZML_W // 2
    zml_ref[:, 0:half] = lax.dot_general(hb, wml0_ref[...], NT_DIMS, preferred_element_type=F32)
    zml_ref[:, half:] = lax.dot_general(hb, wml1_ref[...], NT_DIMS, preferred_element_type=F32)
    g_nat = lax.dot_general(wg_ref[...], hb, NT_DIMS, preferred_element_type=F32) + bg_ref[...]
    H = MLSTM_HEADS
    gr = jnp.concatenate([g_nat[0:H], g_nat[2 * H:3 * H], g_nat[H:2 * H], g_nat[3 * H:]], axis=0)
    grow_ref[...] = gr
    gc = gr.T
    gi_ref[...] = gc[:, 0:N_SCANS]
    gf_ref[...] = gc[:, N_SCANS:]


def _inproj(x, mods, g_norm, w, layer, tiles_per_row, first_row, tm):
    n = x.shape[0]
    ml_half = ZML_W // 2
    assert OFF_ML % ml_half == 0 and OFF_GATES % N_GATES == 0
    wspec = lambda width, block: pl.BlockSpec((None, width, D_MODEL), lambda i: (layer, block, 0))
    return pl.pallas_call(
        _inproj_kernel,
        grid=(n // tm,),
        in_specs=[
            pl.BlockSpec((tm, D_MODEL), lambda i: (i, 0)),
            pl.BlockSpec((None, None, 6, D_MODEL), _mod_row_map(layer, tiles_per_row, first_row)),
            pl.BlockSpec((None, 4, D_MODEL), lambda i: (layer, 0, 0)),
            wspec(ZNA_W, 0), wspec(ZGQ_W, 0), wspec(ml_half, OFF_ML // ml_half),
            wspec(ml_half, OFF_ML // ml_half + 1), wspec(N_GATES, OFF_GATES // N_GATES),
            pl.BlockSpec((None, N_GATES, 1), lambda i: (layer, 0, 0)),
        ],
        out_specs=[
            pl.BlockSpec((tm, ZNA_W), lambda i: (i, 0)),
            pl.BlockSpec((tm, ZGQ_W), lambda i: (i, 0)),
            pl.BlockSpec((tm, ZML_W), lambda i: (i, 0)),
            pl.BlockSpec((tm, N_SCANS), lambda i: (i, 0)),
            pl.BlockSpec((tm, N_SCANS), lambda i: (i, 0)),
            pl.BlockSpec((N_GATES, tm), lambda i: (0, i)),
        ],
        out_shape=[
            jax.ShapeDtypeStruct((n, ZNA_W), F32),
            jax.ShapeDtypeStruct((n, ZGQ_W), F32),
            jax.ShapeDtypeStruct((n, ZML_W), F32),
            jax.ShapeDtypeStruct((n, N_SCANS), F32),
            jax.ShapeDtypeStruct((n, N_SCANS), F32),
            jax.ShapeDtypeStruct((N_GATES, n), F32),
        ],
        compiler_params=_params(1),
        name="inproj",
    )(x, mods, g_norm, w["all"], w["gq"], w["all"], w["all"], w["all"], w["b_row"])


def _post_kernel(x_ref, mna_ref, mgq_ref, mml_ref, mod_ref, g_ref, wo_ref, wgu_ref, wd_ref, o_ref):
    rows = x_ref.shape[0] // POST_ROW_GROUPS
    rs = [slice(i * rows, (i + 1) * rows) for i in range(POST_ROW_GROUPS)]
    acc = []
    for r in rs:
        a = jnp.dot(mna_ref[r, :], wo_ref[0:NA_W, :], preferred_element_type=F32)
        a += jnp.dot(mgq_ref[r, :], wo_ref[NA_W:NA_W + GQA_QW, :], preferred_element_type=F32)
        a += jnp.dot(mml_ref[r, :], wo_ref[NA_W + GQA_QW:, :], preferred_element_type=F32)
        acc.append(a)
    x1, gate_up = [], []
    for r, a in zip(rs, acc):
        xr = x_ref[r, :] + mod_ref[2:3, :] * _rms(a, g_ref[1:2, :])
        hb = (_rms(xr, g_ref[2:3, :]) * (1.0 + mod_ref[4:5, :]) + mod_ref[3:4, :]).astype(BF16)
        x1.append(xr)
        gate_up.append((jnp.dot(hb, wgu_ref[:, 0:FF_HIDDEN], preferred_element_type=F32),
                        jnp.dot(hb, wgu_ref[:, FF_HIDDEN:], preferred_element_type=F32)))
    f = []
    for gate, up in gate_up:
        act = (gate * jax.nn.sigmoid(gate) * up).astype(BF16)
        f.append(jnp.dot(act, wd_ref[...], preferred_element_type=F32))
    for r, xr, fr in zip(rs, x1, f):
        o_ref[r, :] = xr + mod_ref[5:6, :] * _rms(fr, g_ref[3:4, :])


def _post(x, mna, mgq, mml, mods, g_norm, w_out, w_gu, w_down, layer, tiles_per_row, first_row, tm):
    n = x.shape[0]
    resident = lambda shape: pl.BlockSpec((None,) + shape, lambda i: (layer, 0, 0), pipeline_mode=pl.Buffered(1))
    return pl.pallas_call(
        _post_kernel,
        grid=(n // tm,),
        in_specs=[
            pl.BlockSpec((tm, D_MODEL), lambda i: (i, 0)),
            pl.BlockSpec((tm, NA_W), lambda i: (i, 0)),
            pl.BlockSpec((tm, GQA_QW), lambda i: (i, 0)),
            pl.BlockSpec((tm, ML_W), lambda i: (i, 0)),
            pl.BlockSpec((None, None, 6, D_MODEL), _mod_row_map(layer, tiles_per_row, first_row)),
            pl.BlockSpec((None, 4, D_MODEL), lambda i: (layer, 0, 0)),
            resident((D_MODEL, D_MODEL)),
            resident((D_MODEL, 2 * FF_HIDDEN)),
            resident((FF_HIDDEN, D_MODEL)),
        ],
        out_specs=pl.BlockSpec((tm, D_MODEL), lambda i: (i, 0)),
        out_shape=jax.ShapeDtypeStruct((n, D_MODEL), F32),
        compiler_params=_params(1),
        name="post",
    )(x, mna, mgq, mml, mods, g_norm, w_out, w_gu, w_down)


def _pair_rms(x, gain, sums_on_mxu=False):
    xsq = x * x
    if sums_on_mxu:
        row = lax.broadcasted_iota(jnp.int32, (LANES, LANES), 0) < HEAD_DIM
        col = lax.broadcasted_iota(jnp.int32, (LANES, LANES), 1) < HEAD_DIM
        ones_blk = jnp.where(row == col, 1.0, 0.0).astype(BF16)
        sums = sum(jnp.dot(p, ones_blk, preferred_element_type=F32) for p in _split3(xsq))
    else:
        first = _lane_first(x.shape)
        sums = jnp.where(first, jnp.sum(jnp.where(first, xsq, 0.0), axis=-1, keepdims=True),
                         jnp.sum(jnp.where(first, 0.0, xsq), axis=-1, keepdims=True))
    return x * lax.rsqrt(sums * (1.0 / HEAD_DIM) + EPS) * gain


def _pair_gain(gqk_ref, row):
    return jnp.concatenate([gqk_ref[row:row + 1, :]] * (LANES // HEAD_DIM), axis=-1)


def _zero_other_layers(ref, layer):
    for other in range(ref.shape[0]):
        if other != layer:
            ref[other] = jnp.zeros(ref.shape[1:], ref.dtype)
    return ref.at[layer]


def _ctx_attn_kernel(zna_ref, zgq_ref, gqk_ref, *rest, layer, creates):
    mna_ref, mgq_ref, kvna_ref, kvgq_ref = rest[-4:]
    if creates:
        kvna_ref, kvgq_ref = _zero_other_layers(kvna_ref, layer), _zero_other_layers(kvgq_ref, layer)

    def store_t(ref, which, pair, x2):
        xt = x2.T
        for a in range(2):
            ref[which, 2 * pair + a] = xt[a * HEAD_DIM:(a + 1) * HEAD_DIM]

    pending = []
    for i in range(NA_HEADS // 2):
        cols = slice(i * LANES, (i + 1) * LANES)
        q2 = zna_ref[:, cols] * QK_SCALE
        k2 = zna_ref[:, NA_W + i * LANES:NA_W + (i + 1) * LANES]
        v2 = zna_ref[:, 2 * NA_W + i * LANES:2 * NA_W + (i + 1) * LANES]
        store_t(kvna_ref, 0, i, k2)
        store_t(kvna_ref, 1, i, v2)
        blocks = [{"k": k2.astype(BF16), "k_nt": True, "r": _pair_values(v2), "r_nt": False}]
        pending.append((mna_ref, cols, _pair_scores(_pair_queries(q2), blocks), blocks))

    gq, gk = _pair_gain(gqk_ref, 0), _pair_gain(gqk_ref, 1)
    k2 = _pair_rms(zgq_ref[:, GQA_QW:GQA_QW + GQA_KW], gk)
    v2 = zgq_ref[:, GQA_QW + GQA_KW:]
    store_t(kvgq_ref, 0, 0, k2)
    store_t(kvgq_ref, 1, 0, v2)
    blocks = [{"k": k2.astype(BF16), "k_nt": True, "r": _pair_values(v2), "r_nt": False}]
    for p in range(GQA_GROUP):
        cols = slice(p * LANES, (p + 1) * LANES)
        q2 = _pair_rms(zgq_ref[:, cols], gq) * QK_SCALE
        pending.append((mgq_ref, cols, _pair_scores(_pair_queries(q2), blocks), blocks))

    for ref, cols, scores, blocks in pending:
        ref[:, cols] = _pair_finish(scores, blocks).astype(BF16)


def _ctx_attn(zna, zgq, g_qk, kv_na_buf, kv_gq_buf, layer, batch, t):
    n = zna.shape[0]
    creates = kv_na_buf is None
    in_specs = [
        pl.BlockSpec((t, ZNA_W), lambda b: (b, 0)),
        pl.BlockSpec((t, ZGQ_W), lambda b: (b, 0)),
        pl.BlockSpec((None, 2, HEAD_DIM), lambda b: (layer, 0, 0)),
    ]
    args = [zna, zgq, g_qk]
    if creates:
        kv_spec = lambda heads: pl.BlockSpec((None, DEPTH, 2, heads, HEAD_DIM, t), lambda b: (b, 0, 0, 0, 0, 0))
        aliases = {}
    else:
        kv_spec = lambda heads: pl.BlockSpec((None, None, 2, heads, HEAD_DIM, t),
                                             lambda b: (b, layer, 0, 0, 0, 0))
        aliases = {len(args): 2, len(args) + 1: 3}
        in_specs += [pl.BlockSpec(memory_space=pl.ANY)] * 2
        args += [kv_na_buf, kv_gq_buf]
    return pl.pallas_call(
        functools.partial(_ctx_attn_kernel, layer=layer, creates=creates),
        grid=(batch,),
        in_specs=in_specs,
        out_specs=[
            pl.BlockSpec((t, NA_W), lambda b: (b, 0)),
            pl.BlockSpec((t, GQA_QW), lambda b: (b, 0)),
            kv_spec(NA_HEADS),
            kv_spec(GQA_KV_HEADS),
        ],
        out_shape=[
            jax.ShapeDtypeStruct((n, NA_W), BF16),
            jax.ShapeDtypeStruct((n, GQA_QW), BF16),
            jax.ShapeDtypeStruct((batch, DEPTH, 2, NA_HEADS, HEAD_DIM, t), F32),
            jax.ShapeDtypeStruct((batch, DEPTH, 2, GQA_KV_HEADS, HEAD_DIM, t), F32),
        ],
        input_output_aliases=aliases,
        compiler_params=_params(1),
        name="ctx_attn",
    )(*args)


def _lat_na_kernel(zna_ref, cache_ref, bias_ref, o_ref, *, rows):
    qn = NA_BLOCK_ROWS * GRID_W
    kn = NA_UNION_ROWS * GRID_W
    n_blocks = rows // NA_BLOCK_ROWS
    pattern_of = _na_block_patterns(rows)[1]
    prepared = []
    for i in range(NA_HEADS // 2):
        cols = slice(i * LANES, (i + 1) * LANES)
        q_pair = _pair_queries(zna_ref[:, cols] * QK_SCALE)
        k2 = zna_ref[:, NA_W + i * LANES:NA_W + (i + 1) * LANES].astype(BF16)
        kc = cache_ref[0, i].astype(BF16)
        scores = []
        for a in range(2):
            s_ctx = jnp.dot(q_pair[a], kc, preferred_element_type=F32)
            s_win = []
            for p in range(n_blocks):
                k0 = _na_union_start(p, rows) * GRID_W
                s_win.append(lax.dot_general(q_pair[a][p * qn:(p + 1) * qn], k2[k0:k0 + kn], NT_DIMS,
                                             preferred_element_type=F32) + bias_ref[2 * i + a, pattern_of[p]])
            scores.append((s_ctx, s_win))
        prepared.append(scores)
    for i in range(NA_HEADS // 2):
        cols = slice(i * LANES, (i + 1) * LANES)
        r = _pair_values(zna_ref[:, 2 * NA_W + i * LANES:2 * NA_W + (i + 1) * LANES])
        rc = _pair_values_t(cache_ref[1, i])
        acc = None
        for a in range(2):
            s_ctx, s_win = prepared[i][a]
            m_win = jnp.concatenate([s.max(axis=-1, keepdims=True) for s in s_win], axis=0)
            m = jnp.maximum(m_win, s_ctx.max(axis=-1, keepdims=True))
            term = lax.dot_general(jnp.exp(s_ctx - m).astype(BF16), rc[a], NT_DIMS, preferred_element_type=F32)
            wins = []
            for p in range(n_blocks):
                k0 = _na_union_start(p, rows) * GRID_W
                pw = jnp.exp(s_win[p] - m[p * qn:(p + 1) * qn]).astype(BF16)
                wins.append(jnp.dot(pw, r[a][k0:k0 + kn], preferred_element_type=F32))
            term = term + jnp.concatenate(wins, axis=0)
            acc = term if acc is None else acc + term
        o_ref[:, cols] = (acc[:, :LANES] / acc[:, LANES:]).astype(BF16)


def _lat_na(zna, cache_t, bias, layer, batch, t):
    n = zna.shape[0]
    rows = t // GRID_W
    past = cache_t.shape[-1]
    n_blocks = len(_na_block_patterns(rows)[0])
    qn = NA_BLOCK_ROWS * GRID_W
    kn = NA_UNION_ROWS * GRID_W
    return pl.pallas_call(
        functools.partial(_lat_na_kernel, rows=rows),
        grid=(batch,),
        in_specs=[
            pl.BlockSpec((t, ZNA_W), lambda b: (b, 0)),
            pl.BlockSpec((None, None, 2, NA_HEADS // 2, LANES, past), lambda b: (b, layer, 0, 0, 0, 0)),
            pl.BlockSpec((None, NA_HEADS, n_blocks, qn, kn), lambda b: (layer, 0, 0, 0, 0)),
        ],
        out_specs=pl.BlockSpec((t, NA_W), lambda b: (b, 0)),
        out_shape=jax.ShapeDtypeStruct((n, NA_W), BF16),
        compiler_params=_params(1),
        name="lat_na",
    )(zna, cache_t, bias)


def _rope_tables(t):
    half = HEAD_DIM // 2
    quarter = half // 2
    inv = 1.0 / (ROPE_BASE ** (jnp.arange(quarter, dtype=F32) / quarter))
    tt = jnp.arange(t)
    row = (tt // GRID_W).astype(F32)
    col = (tt % GRID_W).astype(F32)
    ang_r = row[:, None] * inv[None, :]
    ang_c = col[:, None] * inv[None, :]
    cos = jnp.concatenate([jnp.cos(ang_r)] * 2 + [jnp.cos(ang_c)] * 2, axis=-1)
    sin = jnp.concatenate([-jnp.sin(ang_r), jnp.sin(ang_r), -jnp.sin(ang_c), jnp.sin(ang_c)], axis=-1)
    reps = LANES // HEAD_DIM
    return jnp.tile(cos, (1, reps)), jnp.tile(sin, (1, reps))


def _pair_rope(xn, cos, sin):
    quarter = HEAD_DIM // 4
    lane = lax.broadcasted_iota(jnp.int32, xn.shape, 1)
    lower = (lane & (2 * quarter - 1)) < quarter
    partner = jnp.where(lower, pltpu.roll(xn, LANES - quarter, 1), pltpu.roll(xn, quarter, 1))
    return xn * cos + partner * sin


def _lat_gqa_kernel(zgq_ref, cache_ref, gqk_ref, cos_ref, sin_ref, o_ref):
    cos = cos_ref[...]
    sin = sin_ref[...]
    gq, gk = _pair_gain(gqk_ref, 0), _pair_gain(gqk_ref, 1)
    keys = _pair_rope(_pair_rms(zgq_ref[:, GQA_QW:GQA_QW + GQA_KW], gk, True), cos, sin).astype(BF16)
    blocks = [
        {"k": keys, "k_nt": True, "r": _pair_values(zgq_ref[:, GQA_QW + GQA_KW:]), "r_nt": False},
        {"k": cache_ref[0].astype(BF16), "k_nt": False, "r": _pair_values_t(cache_ref[1]), "r_nt": True},
    ]
    def scores(p):
        x = _pair_rope(_pair_rms(zgq_ref[:, p * LANES:(p + 1) * LANES], gq, True), cos, sin) * QK_SCALE
        return _pair_scores(_pair_queries(x), blocks)

    nxt = scores(0)
    for p in range(GQA_GROUP):
        cur = nxt
        if p + 1 < GQA_GROUP:
            nxt = scores(p + 1)
        o_ref[:, p * LANES:(p + 1) * LANES] = _pair_finish(cur, blocks).astype(BF16)


def _lat_gqa(zgq, cache_t, g_qk, cos, sin, layer, batch, t):
    n = zgq.shape[0]
    past = cache_t.shape[-1]
    return pl.pallas_call(
        _lat_gqa_kernel,
        grid=(batch,),
        in_specs=[
            pl.BlockSpec((t, ZGQ_W), lambda b: (b, 0)),
            pl.BlockSpec((None, None, 2, GQA_KW, past), lambda b: (b, layer, 0, 0, 0)),
            pl.BlockSpec((None, 2, HEAD_DIM), lambda b: (layer, 0, 0)),
            pl.BlockSpec((t, LANES), lambda b: (0, 0)),
            pl.BlockSpec((t, LANES), lambda b: (0, 0)),
        ],
        out_specs=pl.BlockSpec((t, GQA_QW), lambda b: (b, 0)),
        out_shape=jax.ShapeDtypeStruct((n, GQA_QW), BF16),
        compiler_params=_params(1),
        name="lat_gqa",
    )(zgq, cache_t, g_qk, cos, sin)


def _split3(x):
    x1 = x.astype(BF16)
    r1 = x - x1.astype(F32)
    x2 = r1.astype(BF16)
    x3 = (r1 - x2.astype(F32)).astype(BF16)
    return x1, x2, x3


def _log_sigmoid(x):
    return jnp.minimum(x, 0.0) - jnp.log1p(jnp.exp(-jnp.abs(x)))


def _mlstm_select_matrix():
    H = MLSTM_HEADS
    sel = np.zeros((MLSTM_SEL_ROWS, (H // 2) * 2 * MLSTM_TILE_KINDS * LANES), np.float32)
    for j in range(H // 2):
        for d in range(2):
            for q in range(MLSTM_TILE_KINDS):
                for a in range(2):
                    col0 = ((j * 2 + d) * MLSTM_TILE_KINDS + q) * LANES + a * HEAD_DIM
                    sel[q * N_SCANS + d * H + 2 * j + a, col0:col0 + HEAD_DIM] = 1.0
    return sel


def _mlstm_kernel(zml_ref, gi_ref, gf_ref, grow_ref, c0_ref, n0_ref, m0_ref, gml_ref, sel_ref, *rest,
                  t, emit_state, creates_layer):
    tri_s, st_s, dst_s, cst_s, row_s = rest[-5:]
    outs = rest[:-5]
    if emit_state:
        o_ref, cf_ref, nf_ref, mf_ref = outs[-4:]
        if creates_layer is not None:
            cf_ref, nf_ref, mf_ref = (_zero_other_layers(r, creates_layer) for r in (cf_ref, nf_ref, mf_ref))
    else:
        o_ref = outs[-1]
    L = MLSTM_CHUNK
    H = MLSTM_HEADS
    HD = HEAD_DIM
    NP = H // 2
    nc = t // L
    tb = tri_s.shape[-1]

    @pl.when(pl.program_id(0) == 0)
    def _():
        ti = lax.broadcasted_iota(jnp.int32, (tb, tb), 0)
        ui = lax.broadcasted_iota(jnp.int32, (tb, tb), 1)
        same = (ti & -L) == (ui & -L)
        tri_s[0] = jnp.where(same & (ui <= ti), 1.0, 0.0).astype(BF16)
        tri_s[1] = jnp.where(same & (ui >= ti), 1.0, 0.0).astype(BF16)

    lower, upper = tri_s[0], tri_s[1]

    def chunk_sums_cols(x):
        parts = _split3(x)
        pre, suf = [], []
        for i in range(t // tb):
            blk = [p[i * tb:(i + 1) * tb] for p in parts]
            pre.append(sum(jnp.dot(lower, p, preferred_element_type=F32) for p in blk))
            suf.append(sum(jnp.dot(upper, p, preferred_element_type=F32) for p in blk))
        return jnp.concatenate(pre, axis=0), jnp.concatenate(suf, axis=0)

    def chunk_sums_rows(x):
        parts = _split3(x)
        pre, suf = [], []
        for i in range(t // tb):
            blk = [p[:, i * tb:(i + 1) * tb] for p in parts]
            pre.append(sum(jnp.dot(p, upper, preferred_element_type=F32) for p in blk))
            suf.append(sum(jnp.dot(p, lower, preferred_element_type=F32) for p in blk))
        return jnp.concatenate(pre, axis=1), jnp.concatenate(suf, axis=1)

    pre_c, suf_c = chunk_sums_cols(_log_sigmoid(gf_ref[...]))
    lane_c = lax.broadcasted_iota(jnp.int32, (t, N_SCANS), 1)
    b3 = jnp.where(lane_c < H, pre_c, suf_c).reshape(nc, L, N_SCANS)
    i3 = gi_ref[...].reshape(nc, L, N_SCANS)
    fwd3 = lax.broadcasted_iota(jnp.int32, (nc, 1, N_SCANS), 2) < H
    b_end3 = jnp.where(fwd3, b3[:, L - 1:L, :], b3[:, 0:1, :])
    lw_end3 = b_end3 - b3 + i3
    a3 = jnp.max(lw_end3, axis=1, keepdims=True)
    wloc3 = jnp.exp(lw_end3 - a3)

    fwd1 = lax.broadcasted_iota(jnp.int32, (1, N_SCANS), 1) < H
    m = m0_ref[...]
    m_start, carry_decay, contrib_scale = [], [], []
    for j in range(nc):
        a_j = jnp.where(fwd1, a3[j], a3[nc - 1 - j])
        g_j = jnp.where(fwd1, b_end3[j], b_end3[nc - 1 - j])
        m_start.append(m)
        m_next = jnp.maximum(g_j + m, a_j)
        carry_decay.append(jnp.exp(g_j + m - m_next))
        contrib_scale.append(jnp.exp(a_j - m_next))
        m = m_next
    mst3 = jnp.concatenate([jnp.where(fwd1, m_start[c], m_start[nc - 1 - c])[None] for c in range(nc)], axis=0)

    cols = jnp.concatenate([b3.reshape(t, N_SCANS), wloc3.reshape(t, N_SCANS)], axis=1)
    tiles_all = sum(jnp.dot(p, sel_ref[...], preferred_element_type=F32) for p in _split3(cols))
    tile_w = MLSTM_TILE_KINDS * LANES

    def tiles(j, d):
        x = tiles_all[:, (2 * j + d) * tile_w:(2 * j + d + 1) * tile_w]
        return [x[:, q * LANES:(q + 1) * LANES].reshape(nc, L, LANES) for q in range(MLSTM_TILE_KINDS)]

    gr = grow_ref[...]
    pre_r, suf_r = chunk_sums_rows(_log_sigmoid(gr))
    sub_r = lax.broadcasted_iota(jnp.int32, (N_SCANS, t), 0)
    rowv = gr[0:N_SCANS] - jnp.where(sub_r < H, pre_r[N_SCANS:], suf_r[N_SCANS:])
    for j in range(NP):
        for d in range(2):
            e = d * H + 2 * j
            for c in range(nc):
                row_s[2 * j + d, c] = jnp.concatenate(
                    [rowv[e:e + 1, c * L:(c + 1) * L], rowv[e + 1:e + 2, c * L:(c + 1) * L]], axis=1)

    lane_a = lax.broadcasted_iota(jnp.int32, (1, 1, LANES), 2) < HD
    sub_a = lax.broadcasted_iota(jnp.int32, (1, 2 * HD, 1), 1) < HD
    diag = sub_a == lane_a
    diag4 = jnp.concatenate([diag] * 4, axis=2)

    def stack_heads(x3):
        return jnp.concatenate([jnp.where(lane_a, x3, 0.0), jnp.where(lane_a, 0.0, x3)], axis=1)

    def pair_cols(base, j):
        return slice(base + j * LANES, base + (j + 1) * LANES)

    zero_blk = jnp.zeros((HD, HD), F32)
    for j in range(NP):
        cols_d = []
        for d in range(2):
            ca, cb = c0_ref[d, 2 * j].T, c0_ref[d, 2 * j + 1].T
            na = jnp.broadcast_to(n0_ref[d, 2 * j:2 * j + 1, :], (HD, HD)).T
            nb = jnp.broadcast_to(n0_ref[d, 2 * j + 1:2 * j + 2, :], (HD, HD)).T
            top = jnp.concatenate([ca, zero_blk, na, zero_blk], axis=1)
            bot = jnp.concatenate([zero_blk, cb, zero_blk, nb], axis=1)
            cols_d.append(jnp.concatenate([top, bot], axis=0))
        st_s[j] = jnp.concatenate(cols_d, axis=1)

    for j in range(NP):
        k3 = (zml_ref[:, pair_cols(ML_W, j)] * QK_SCALE).reshape(nc, L, LANES).astype(BF16)
        v3 = zml_ref[:, pair_cols(2 * ML_W, j)].reshape(nc, L, LANES)
        rhs = []
        for d in range(2):
            wl = tiles(j, d)[1]
            rhs += [v3 * wl, wl]
        rhs = jnp.concatenate(rhs, axis=2).astype(BF16)
        contrib = jnp.einsum("csk,csn->ckn", k3, rhs, preferred_element_type=F32)
        dst_s[j] = jnp.where(diag4, contrib, 0.0)

    def lane_scale(v, j):
        pieces = []
        for d in range(2):
            sa = jnp.broadcast_to(v[:, d * H + 2 * j:d * H + 2 * j + 1], (1, HD))
            sb = jnp.broadcast_to(v[:, d * H + 2 * j + 1:d * H + 2 * j + 2], (1, HD))
            pieces += [sa, sb, sa, sb]
        return jnp.concatenate(pieces, axis=1)

    for j in range(NP):
        st = st_s[j]
        for step in range(nc):
            cb = nc - 1 - step
            stb = st.astype(BF16)
            cst_s[j, step, :, 0:2 * LANES] = stb[:, 0:2 * LANES]
            cst_s[j, cb, :, 2 * LANES:] = stb[:, 2 * LANES:]
            delta = jnp.concatenate([dst_s[j, step, :, 0:2 * LANES], dst_s[j, cb, :, 2 * LANES:]], axis=1)
            st = lane_scale(carry_decay[step], j) * st + lane_scale(contrib_scale[step], j) * delta
        st_s[j] = st

    sidx = lax.broadcasted_iota(jnp.int32, (1, L, LANES), 2) & (HD - 1)
    tidx = lax.broadcasted_iota(jnp.int32, (1, L, LANES), 1)
    masks = (sidx <= tidx, sidx >= tidx)
    ones_blk = jnp.broadcast_to(jnp.where(diag, 1.0, 0.0).astype(BF16), (nc, 2 * HD, LANES))
    neg_inf = -jnp.inf
    for j in range(NP):
        q3 = zml_ref[:, pair_cols(0, j)].reshape(nc, L, LANES).astype(BF16)
        k3 = (zml_ref[:, pair_cols(ML_W, j)] * QK_SCALE).reshape(nc, L, LANES)
        v3 = zml_ref[:, pair_cols(2 * ML_W, j)].reshape(nc, L, LANES)
        qk = jnp.einsum("ctd,cnd->ctn", q3, stack_heads(k3).astype(BF16), preferred_element_type=F32)
        v_aug = jnp.concatenate([stack_heads(v3).astype(BF16), ones_blk], axis=2)
        out = None
        for d in range(2):
            b_t = tiles(j, d)[0]
            e = d * H + 2 * j
            bm_t = b_t + jnp.where(lane_a, mst3[:, :, e:e + 1], mst3[:, :, e + 1:e + 2])
            logw = jnp.where(masks[d], b_t + row_s[2 * j + d], neg_inf)
            rmax_a = jnp.max(jnp.where(lane_a, logw, neg_inf), axis=-1, keepdims=True)
            rmax_b = jnp.max(jnp.where(lane_a, neg_inf, logw), axis=-1, keepdims=True)
            m_t = jnp.maximum(jnp.where(lane_a, rmax_a, rmax_b), bm_t)
            s = qk * jnp.exp(logw - m_t)
            decay = jnp.exp(bm_t - m_t)
            sv = jnp.einsum("cts,csn->ctn", s.astype(BF16), v_aug, preferred_element_type=F32)
            state = cst_s[j, :, :, 2 * d * LANES:2 * (d + 1) * LANES]
            inter = jnp.einsum("ctk,ckn->ctn", q3, state, preferred_element_type=F32)
            num = sv[:, :, 0:LANES] + decay * inter[:, :, 0:LANES]
            den = sv[:, :, LANES:] + decay * inter[:, :, LANES:]
            h_d = num / jnp.maximum(jnp.abs(den), jnp.exp(-m_t))
            out = h_d if out is None else out + h_d
        cols = pair_cols(0, j)
        og = jax.nn.sigmoid(zml_ref[:, pair_cols(3 * ML_W, j)])
        o_ref[:, cols] = (_pair_rms(out.reshape(t, LANES), gml_ref[:, cols]) * og).astype(BF16)

    if emit_state:
        for j in range(NP):
            st = st_s[j]
            for d in range(2):
                for a in range(2):
                    rows = slice(a * HD, (a + 1) * HD)
                    c0 = 2 * d * LANES + a * HD
                    cf_ref[d, 2 * j + a] = st[rows, c0:c0 + HD].T
                    nf_ref[d, 2 * j + a:2 * j + a + 1, :] = st[rows, c0 + LANES:c0 + LANES + HD].T[0:1, :]
        mf_ref[...] = m


def _mlstm(zml, gi, gf, grow, c0, n0, m0, g_ml, layer, batch, t, state_out, state_layer):
    emit_state = state_out is not None
    creates = emit_state and not isinstance(state_out, tuple)
    n = zml.shape[0]
    H = MLSTM_HEADS
    L = MLSTM_CHUNK
    nc = t // L
    sel = jnp.asarray(_mlstm_select_matrix(), BF16)
    tri_block = min(t, MXU_DIM)
    assert L & (L - 1) == 0 and tri_block % L == 0 and t % tri_block == 0
    if state_layer is None:
        c_spec = pl.BlockSpec((None, 2, H, HEAD_DIM, HEAD_DIM), lambda b: (0, 0, 0, 0, 0))
        n_spec = pl.BlockSpec((None, 2, H, HEAD_DIM), lambda b: (0, 0, 0, 0))
        m_spec = pl.BlockSpec((None, 1, N_SCANS), lambda b: (0, 0, 0))
    else:
        c_spec = pl.BlockSpec((None, None, 2, H, HEAD_DIM, HEAD_DIM), lambda b: (b, state_layer, 0, 0, 0, 0))
        n_spec = pl.BlockSpec((None, None, 2, H, HEAD_DIM), lambda b: (b, state_layer, 0, 0, 0))
        m_spec = pl.BlockSpec((None, None, 1, N_SCANS), lambda b: (b, state_layer, 0, 0))
    in_specs = [
        pl.BlockSpec((t, ZML_W), lambda b: (b, 0)),
        pl.BlockSpec((t, N_SCANS), lambda b: (b, 0)),
        pl.BlockSpec((t, N_SCANS), lambda b: (b, 0)),
        pl.BlockSpec((N_GATES, t), lambda b: (0, b)),
        c_spec, n_spec, m_spec,
        pl.BlockSpec((None, 1, ML_W), lambda b: (layer, 0, 0)),
        pl.BlockSpec(sel.shape, lambda b: (0, 0)),
    ]
    args = [zml, gi, gf, grow, c0, n0, m0, g_ml, sel]
    out_specs = [pl.BlockSpec((t, ML_W), lambda b: (b, 0))]
    out_shape = [jax.ShapeDtypeStruct((n, ML_W), BF16)]
    aliases = {}
    if emit_state:
        tails = [(2, H, HEAD_DIM, HEAD_DIM), (2, H, HEAD_DIM), (1, N_SCANS)]
        for tail in tails:
            zeros = (0,) * len(tail)
            if creates:
                out_specs.append(pl.BlockSpec((None, DEPTH) + tail, lambda b, zeros=zeros: (b, 0) + zeros))
            else:
                out_specs.append(pl.BlockSpec((None, None) + tail, lambda b, zeros=zeros: (b, layer) + zeros))
            out_shape.append(jax.ShapeDtypeStruct((batch, DEPTH) + tail, F32))
        if not creates:
            aliases = {len(args) + i: 1 + i for i in range(len(state_out))}
            in_specs += [pl.BlockSpec(memory_space=pl.ANY)] * len(state_out)
            args += list(state_out)
    return pl.pallas_call(
        functools.partial(_mlstm_kernel, t=t, emit_state=emit_state, creates_layer=layer if creates else None),
        grid=(batch,),
        in_specs=in_specs,
        out_specs=out_specs,
        out_shape=out_shape,
        input_output_aliases=aliases,
        scratch_shapes=[
            pltpu.VMEM((2, tri_block, tri_block), BF16),
            pltpu.VMEM((H // 2, 2 * HEAD_DIM, 4 * LANES), F32),
            pltpu.VMEM((H // 2, nc, 2 * HEAD_DIM, 4 * LANES), F32),
            pltpu.VMEM((H // 2, nc, 2 * HEAD_DIM, 4 * LANES), BF16),
            pltpu.VMEM((H, nc, 1, LANES), F32),
        ],
        compiler_params=_params(1),
        name="mlstm",
    )(*args)


def _layer_path(x, mods, layer, first_row, tiles_row_tokens, weights, mixers):
    per_row = lambda tile: None if tiles_row_tokens is None else tiles_row_tokens // tile
    g_norm = weights["g_norm"]
    in_tile = INPROJ_TILE if x.shape[0] >= INPROJ_MIN_STEPS * INPROJ_TILE else INPROJ_TILE // 2
    zna, zgq, zml, gi, gf, grow = _inproj(x, mods, g_norm, weights["w_in"], layer, per_row(in_tile),
                                          first_row, in_tile)
    mna, mgq, mml, extra = mixers(zna, zgq, zml, gi, gf, grow)
    x = _post(x, mna, mgq, mml, mods, g_norm, weights["w_out"], weights["w_gu"], weights["w_down"], layer,
              per_row(POST_TILE), first_row, POST_TILE)
    return x, extra


def _gqa_pair_order():
    return [a * GQA_GROUP + p for p in range(GQA_GROUP) for a in range(GQA_KV_HEADS)]


def _take_blocks(x, axis, base, width, order):
    return jnp.concatenate([lax.slice_in_dim(x, base + width * o, base + width * (o + 1), axis=axis)
                            for o in order], axis=axis)


def kernel(x_prompt, x_sample, cache_na_kv, cache_gqa_kv, state_mlstm_C, state_mlstm_n, state_mlstm_m,
           c, c_ctx, w_in, b_gates, w_out, g_norm, g_qk, g_mlstm, na_bias, w_ada, b_ada, w_gu, w_down):
    batch, seq, _ = x_prompt.shape
    dec_batch, dec_seq, _ = x_sample.shape
    past = cache_na_kv.shape[-2]
    assert dec_batch + 1 <= N_MOD_ROWS and dec_seq % GRID_W == 0 and GQA_KV_HEADS == 2

    cvec = jnp.concatenate([c_ctx[None, :], c, jnp.zeros((N_MOD_ROWS - 1 - dec_batch, D_MODEL), F32)], axis=0)
    mods = _adaln(cvec, w_ada, b_ada).reshape(DEPTH, N_MOD_ROWS, 6, D_MODEL)
    bias = _na_bias_expand(na_bias, dec_seq // GRID_W)
    cos, sin = _rope_tables(dec_seq)

    pair_order = _gqa_pair_order()
    w_in_t = jnp.swapaxes(w_in, 1, 2).astype(BF16)
    w_gq = jnp.concatenate([_take_blocks(w_in_t, 1, OFF_GQ, HEAD_DIM, pair_order),
                            w_in_t[:, OFF_GQ + GQA_QW:OFF_ML]], axis=1)
    w_out_rows = jnp.concatenate([w_out[:, :NA_W], _take_blocks(w_out, 1, NA_W, HEAD_DIM, pair_order),
                                  w_out[:, NA_W + GQA_QW:]], axis=1)
    weights = {
        "g_norm": g_norm,
        "w_in": {
            "all": w_in_t,
            "gq": w_gq,
            "b_row": b_gates[:, :, None],
        },
        "w_out": w_out_rows.astype(BF16),
        "w_gu": w_gu.astype(BF16),
        "w_down": w_down.astype(BF16),
    }
    g_ml = g_mlstm.reshape(DEPTH, 1, ML_W)
    zero_c = jnp.zeros((1, 2, MLSTM_HEADS, HEAD_DIM, HEAD_DIM), F32)
    zero_n = jnp.zeros((1, 2, MLSTM_HEADS, HEAD_DIM), F32)
    zero_m = jnp.zeros((1, 1, N_SCANS), F32)
    m0_lat = state_mlstm_m.reshape(dec_batch, DEPTH, 1, N_SCANS)
    cache_na_t = jnp.swapaxes(cache_na_kv, -1, -2).reshape(dec_batch, DEPTH, 2, NA_HEADS // 2, LANES, past)
    cache_gq_t = jnp.swapaxes(cache_gqa_kv, -1, -2).reshape(dec_batch, DEPTH, 2, GQA_KW, past)

    xp = x_prompt.reshape(batch * seq, D_MODEL)
    xs = x_sample.reshape(dec_batch * dec_seq, D_MODEL)
    kv_na = kv_gq = None
    states = "create"
    for layer in range(DEPTH):
        def ctx_mixers(zna, zgq, zml, gi, gf, grow, layer=layer, kv_na=kv_na, kv_gq=kv_gq, states=states):
            mna, mgq, kv_na, kv_gq = _ctx_attn(zna, zgq, g_qk, kv_na, kv_gq, layer, batch, seq)
            mml, *states = _mlstm(zml, gi, gf, grow, zero_c, zero_n, zero_m, g_ml, layer, batch, seq,
                                  states, None)
            return mna, mgq, mml, (kv_na, kv_gq, tuple(states))

        def lat_mixers(zna, zgq, zml, gi, gf, grow, layer=layer):
            mna = _lat_na(zna, cache_na_t, bias, layer, dec_batch, dec_seq)
            mgq = _lat_gqa(zgq, cache_gq_t, g_qk, cos, sin, layer, dec_batch, dec_seq)
            (mml,) = _mlstm(zml, gi, gf, grow, state_mlstm_C, state_mlstm_n, m0_lat, g_ml, layer,
                            dec_batch, dec_seq, None, layer)
            return mna, mgq, mml, None

        xp, (kv_na, kv_gq, states) = _layer_path(xp, mods, layer, 0, None, weights, ctx_mixers)
        xs, _ = _layer_path(xs, mods, layer, 1, dec_seq, weights, lat_mixers)

    new_c, new_n, new_m = states
    return (xp.reshape(batch, seq, D_MODEL), xs.reshape(dec_batch, dec_seq, D_MODEL),
            jnp.swapaxes(kv_na, -1, -2), jnp.swapaxes(kv_gq, -1, -2),
            new_c, new_n, new_m.reshape(batch, DEPTH, 2, MLSTM_HEADS))
```

```python
import functools

import jax
import jax.numpy as jnp
import numpy as np
from jax import lax
from jax.experimental import pallas as pl
from jax.experimental.pallas import tpu as pltpu

D_MODEL = 1024
DEPTH = 4
GRID_W = 64
HEAD_DIM = 64
NA_HEADS = 4
GQA_Q_HEADS = 8
GQA_KV_HEADS = 2
GQA_GROUP = GQA_Q_HEADS // GQA_KV_HEADS
MLSTM_HEADS = 4
NA_WIN_ROWS = 8
NA_WIN_COLS = 16
MLSTM_CHUNK = 64
ROPE_BASE = 10000.0
EPS = 1e-6
NEG = -1e30
NA_W = NA_HEADS * HEAD_DIM
GQA_QW = GQA_Q_HEADS * HEAD_DIM
GQA_KW = GQA_KV_HEADS * HEAD_DIM
ML_W = MLSTM_HEADS * HEAD_DIM
N_GATES = 4 * MLSTM_HEADS
N_SCANS = 2 * MLSTM_HEADS
MLSTM_TILE_KINDS = 2
MLSTM_SEL_ROWS = MLSTM_TILE_KINDS * N_SCANS
FF_HIDDEN = ((8 * D_MODEL + 3 * 256 - 1) // (3 * 256)) * 256
QK_SCALE = HEAD_DIM ** -0.5

ZNA_W = 3 * NA_W
ZGQ_W = GQA_QW + 2 * GQA_KW
ZML_W = 4 * ML_W
OFF_GQ = ZNA_W
OFF_ML = ZNA_W + ZGQ_W
OFF_GATES = OFF_ML + ZML_W

LANES = 128
MXU_DIM = 256
N_MOD_ROWS = 16
NA_BLOCK_ROWS = 2
NA_UNION_ROWS = NA_WIN_ROWS + NA_BLOCK_ROWS - 1

F32 = jnp.float32
BF16 = jnp.bfloat16
VMEM_LIMIT = 52 * 1024 * 1024
INPROJ_TILE = 1024
INPROJ_MIN_STEPS = 8
POST_TILE = 512
ADALN_TILE = 3072
POST_ROW_GROUPS = 2

NT_DIMS = (((1,), (1,)), ((), ()))


def _params(n_axes):
    return pltpu.CompilerParams(dimension_semantics=("arbitrary",) * n_axes,
                                vmem_limit_bytes=VMEM_LIMIT)


def _rms(x, g):
    return x * lax.rsqrt(jnp.mean(x * x, axis=-1, keepdims=True) + EPS) * g


def _lane_first(shape):
    return lax.broadcasted_iota(jnp.int32, shape, len(shape) - 1) < HEAD_DIM


def _pair_queries(x):
    first = _lane_first(x.shape)
    return jnp.where(first, x, 0.0).astype(BF16), jnp.where(first, 0.0, x).astype(BF16)


def _pair_values(v2):
    first = _lane_first(v2.shape)
    ones, zeros = jnp.ones_like(v2), jnp.zeros_like(v2)
    r0 = jnp.concatenate([jnp.where(first, v2, 0.0), jnp.where(first, ones, zeros)], axis=1)
    r1 = jnp.concatenate([jnp.where(first, 0.0, v2), jnp.where(first, zeros, ones)], axis=1)
    return r0.astype(BF16), r1.astype(BF16)


def _pair_values_t(vt2):
    first = lax.broadcasted_iota(jnp.int32, vt2.shape, 0) < HEAD_DIM
    ones, zeros = jnp.ones_like(vt2), jnp.zeros_like(vt2)
    r0 = jnp.concatenate([jnp.where(first, vt2, 0.0), jnp.where(first, ones, zeros)], axis=0)
    r1 = jnp.concatenate([jnp.where(first, 0.0, vt2), jnp.where(first, zeros, ones)], axis=0)
    return r0.astype(BF16), r1.astype(BF16)


def _pair_scores(q_pair, blocks):
    out = []
    for a in range(2):
        scores = []
        for blk in blocks:
            if blk["k_nt"]:
                s = lax.dot_general(q_pair[a], blk["k"], NT_DIMS, preferred_element_type=F32)
            else:
                s = jnp.dot(q_pair[a], blk["k"], preferred_element_type=F32)
            if blk.get("bias") is not None:
                s = s + blk["bias"][a]
            scores.append(s)
        out.append(scores)
    return out


def _pair_finish(all_scores, blocks):
    acc = None
    for a in range(2):
        scores = all_scores[a]
        m = scores[0].max(axis=-1, keepdims=True)
        for s in scores[1:]:
            m = jnp.maximum(m, s.max(axis=-1, keepdims=True))
        for s, blk in zip(scores, blocks):
            p = jnp.exp(s - m).astype(BF16)
            if blk["r_nt"]:
                term = lax.dot_general(p, blk["r"][a], NT_DIMS, preferred_element_type=F32)
            else:
                term = jnp.dot(p, blk["r"][a], preferred_element_type=F32)
            acc = term if acc is None else acc + term
    return acc[:, :LANES] / acc[:, LANES:]


def _adaln_kernel(c_ref, w_ref, b_ref, o_ref):
    c = c_ref[...]
    a = c * jax.nn.sigmoid(c)
    o_ref[...] = jnp.dot(a.astype(BF16), w_ref[...].astype(BF16),
                         preferred_element_type=F32) + b_ref[...]


def _adaln(cvec, w_ada, b_ada):
    tn = ADALN_TILE
    return pl.pallas_call(
        _adaln_kernel,
        grid=(DEPTH, 6 * D_MODEL // tn),
        in_specs=[
            pl.BlockSpec((N_MOD_ROWS, D_MODEL), lambda l, j: (0, 0)),
            pl.BlockSpec((None, D_MODEL, tn), lambda l, j: (l, 0, j)),
            pl.BlockSpec((None, 1, tn), lambda l, j: (l, 0, j)),
        ],
        out_specs=pl.BlockSpec((None, N_MOD_ROWS, tn), lambda l, j: (l, 0, j)),
        out_shape=jax.ShapeDtypeStruct((DEPTH, N_MOD_ROWS, 6 * D_MODEL), F32),
        compiler_params=_params(2),
        name="adaln",
    )(cvec, w_ada, b_ada.reshape(DEPTH, 1, 6 * D_MODEL))


def _na_r0(r, rows):
    return min(max(r - NA_WIN_ROWS // 2, 0), rows - NA_WIN_ROWS)


def _na_union_start(p, rows):
    return min(_na_r0(NA_BLOCK_ROWS * p, rows), rows - NA_UNION_ROWS)


def _na_block_patterns(rows):
    patterns, index = [], []
    for p in range(rows // NA_BLOCK_ROWS):
        start = _na_union_start(p, rows)
        pattern = []
        for a in range(NA_BLOCK_ROWS):
            r = NA_BLOCK_ROWS * p + a
            r0 = _na_r0(r, rows)
            pattern.append(tuple(start + j - r + NA_WIN_ROWS - 1 if r0 <= start + j < r0 + NA_WIN_ROWS else None
                                 for j in range(NA_UNION_ROWS)))
        pattern = tuple(pattern)
        if pattern not in patterns:
            patterns.append(pattern)
        index.append(patterns.index(pattern))
    return patterns, index


def _na_bias_kernel(tbl_ref, o_ref, *, rows):
    l = pl.program_id(0)
    h = pl.program_id(1)
    qi = lax.broadcasted_iota(jnp.int32, (GRID_W, GRID_W), 0)
    ki = lax.broadcasted_iota(jnp.int32, (GRID_W, GRID_W), 1)
    dc = jnp.clip(ki - qi, -(NA_WIN_COLS - 1), NA_WIN_COLS - 1) + NA_WIN_COLS - 1
    c0 = jnp.clip(qi - NA_WIN_COLS // 2, 0, GRID_W - NA_WIN_COLS)
    col_ok = (ki >= c0) & (ki < c0 + NA_WIN_COLS)
    n_dr = 2 * NA_WIN_ROWS - 1
    n_dc = 2 * NA_WIN_COLS - 1
    tiles = []
    for dr in range(n_dr):
        t = jnp.zeros((GRID_W, GRID_W), F32)
        for d in range(n_dc):
            t = jnp.where(dc == d, tbl_ref[((l * NA_HEADS + h) * n_dr + dr) * n_dc + d], t)
        tiles.append(jnp.where(col_ok, t, NEG))
    neg_tile = jnp.full((GRID_W, GRID_W), NEG, F32)
    for i, pattern in enumerate(_na_block_patterns(rows)[0]):
        for a in range(NA_BLOCK_ROWS):
            for j in range(NA_UNION_ROWS):
                dr = pattern[a][j]
                tile = neg_tile if dr is None else tiles[dr]
                o_ref[i, a * GRID_W:(a + 1) * GRID_W, j * GRID_W:(j + 1) * GRID_W] = tile


def _na_bias_expand(na_bias, rows):
    n_blocks = len(_na_block_patterns(rows)[0])
    qn = NA_BLOCK_ROWS * GRID_W
    kn = NA_UNION_ROWS * GRID_W
    return pl.pallas_call(
        functools.partial(_na_bias_kernel, rows=rows),
        grid=(DEPTH, NA_HEADS),
        in_specs=[pl.BlockSpec(memory_space=pltpu.SMEM)],
        out_specs=pl.BlockSpec((None, None, n_blocks, qn, kn), lambda l, h: (l, h, 0, 0, 0)),
        out_shape=jax.ShapeDtypeStruct((DEPTH, NA_HEADS, n_blocks, qn, kn), F32),
        compiler_params=_params(2),
        name="na_bias_expand",
    )(na_bias.reshape(-1))


def _mod_row_map(layer, tiles_per_row, first_row):
    if tiles_per_row is None:
        return lambda i: (layer, first_row, 0, 0)
    return lambda i: (layer, first_row + i // tiles_per_row, 0, 0)


def _inproj_kernel(x_ref, mod_ref, g_ref, wna_ref, wgq_ref, wml0_ref, wml1_ref, wg_ref, bg_ref,
                   zna_ref, zgq_ref, zml_ref, gi_ref, gf_ref, grow_ref):
    x = x_ref[...]
    h = _rms(x, g_ref[0:1, :]) * (1.0 + mod_ref[1:2, :]) + mod_ref[0:1, :]
    hb = h.astype(BF16)
    zna_ref[...] = lax.dot_general(hb, wna_ref[...], NT_DIMS, preferred_element_type=F32)
    zgq_ref[...] = lax.dot_general(hb, wgq_ref[...], NT_DIMS, preferred_element_type=F32)
    half = ZML_W // 2
    zml_ref[:, 0:half] = lax.dot_general(hb, wml0_ref[...], NT_DIMS, preferred_element_type=F32)
    zml_ref[:, half:] = lax.dot_general(hb, wml1_ref[...], NT_DIMS, preferred_element_type=F32)
    g_nat = lax.dot_general(wg_ref[...], hb, NT_DIMS, preferred_element_type=F32) + bg_ref[...]
    H = MLSTM_HEADS
    gr = jnp.concatenate([g_nat[0:H], g_nat[2 * H:3 * H], g_nat[H:2 * H], g_nat[3 * H:]], axis=0)
    grow_ref[...] = gr
    gc = gr.T
    gi_ref[...] = gc[:, 0:N_SCANS]
    gf_ref[...] = gc[:, N_SCANS:]


def _inproj(x, mods, g_norm, w, layer, tiles_per_row, first_row, tm):
    n = x.shape[0]
    ml_half = ZML_W // 2
    assert OFF_ML % ml_half == 0 and OFF_GATES % N_GATES == 0
    wspec = lambda width, block: pl.BlockSpec((None, width, D_MODEL), lambda i: (layer, block, 0))
    return pl.pallas_call(
        _inproj_kernel,
        grid=(n // tm,),
        in_specs=[
            pl.BlockSpec((tm, D_MODEL), lambda i: (i, 0)),
            pl.BlockSpec((None, None, 6, D_MODEL), _mod_row_map(layer, tiles_per_row, first_row)),
            pl.BlockSpec((None, 4, D_MODEL), lambda i: (layer, 0, 0)),
            wspec(ZNA_W, 0), wspec(ZGQ_W, 0), wspec(ml_half, OFF_ML // ml_half),
            wspec(ml_half, OFF_ML // ml_half + 1), wspec(N_GATES, OFF_GATES // N_GATES),
            pl.BlockSpec((None, N_GATES, 1), lambda i: (layer, 0, 0)),
        ],
        out_specs=[
            pl.BlockSpec((tm, ZNA_W), lambda i: (i, 0)),
            pl.BlockSpec((tm, ZGQ_W), lambda i: (i, 0)),
            pl.BlockSpec((tm, ZML_W), lambda i: (i, 0)),
            pl.BlockSpec((tm, N_SCANS), lambda i: (i, 0)),
            pl.BlockSpec((tm, N_SCANS), lambda i: (i, 0)),
            pl.BlockSpec((N_GATES, tm), lambda i: (0, i)),
        ],
        out_shape=[
            jax.ShapeDtypeStruct((n, ZNA_W), F32),
            jax.ShapeDtypeStruct((n, ZGQ_W), F32),
            jax.ShapeDtypeStruct((n, ZML_W), F32),
            jax.ShapeDtypeStruct((n, N_SCANS), F32),
            jax.ShapeDtypeStruct((n, N_SCANS), F32),
            jax.ShapeDtypeStruct((N_GATES, n), F32),
        ],
        compiler_params=_params(1),
        name="inproj",
    )(x, mods, g_norm, w["all"], w["gq"], w["all"], w["all"], w["all"], w["b_row"])


def _post_kernel(x_ref, mna_ref, mgq_ref, mml_ref, mod_ref, g_ref, wo_ref, wgu_ref, wd_ref, o_ref):
    rows = x_ref.shape[0] // POST_ROW_GROUPS
    rs = [slice(i * rows, (i + 1) * rows) for i in range(POST_ROW_GROUPS)]
    acc = []
    for r in rs:
        a = jnp.dot(mna_ref[r, :], wo_ref[0:NA_W, :], preferred_element_type=F32)
        a += jnp.dot(mgq_ref[r, :], wo_ref[NA_W:NA_W + GQA_QW, :], preferred_element_type=F32)
        a += jnp.dot(mml_ref[r, :], wo_ref[NA_W + GQA_QW:, :], preferred_element_type=F32)
        acc.append(a)
    x1, gate_up = [], []
    for r, a in zip(rs, acc):
        xr = x_ref[r, :] + mod_ref[2:3, :] * _rms(a, g_ref[1:2, :])
        hb = (_rms(xr, g_ref[2:3, :]) * (1.0 + mod_ref[4:5, :]) + mod_ref[3:4, :]).astype(BF16)
        x1.append(xr)
        gate_up.append((jnp.dot(hb, wgu_ref[:, 0:FF_HIDDEN], preferred_element_type=F32),
                        jnp.dot(hb, wgu_ref[:, FF_HIDDEN:], preferred_element_type=F32)))
    f = []
    for gate, up in gate_up:
        act = (gate * jax.nn.sigmoid(gate) * up).astype(BF16)
        f.append(jnp.dot(act, wd_ref[...], preferred_element_type=F32))
    for r, xr, fr in zip(rs, x1, f):
        o_ref[r, :] = xr + mod_ref[5:6, :] * _rms(fr, g_ref[3:4, :])


def _post(x, mna, mgq, mml, mods, g_norm, w_out, w_gu, w_down, layer, tiles_per_row, first_row, tm):
    n = x.shape[0]
    resident = lambda shape: pl.BlockSpec((None,) + shape, lambda i: (layer, 0, 0), pipeline_mode=pl.Buffered(1))
    return pl.pallas_call(
        _post_kernel,
        grid=(n // tm,),
        in_specs=[
            pl.BlockSpec((tm, D_MODEL), lambda i: (i, 0)),
            pl.BlockSpec((tm, NA_W), lambda i: (i, 0)),
            pl.BlockSpec((tm, GQA_QW), lambda i: (i, 0)),
            pl.BlockSpec((tm, ML_W), lambda i: (i, 0)),
            pl.BlockSpec((None, None, 6, D_MODEL), _mod_row_map(layer, tiles_per_row, first_row)),
            pl.BlockSpec((None, 4, D_MODEL), lambda i: (layer, 0, 0)),
            resident((D_MODEL, D_MODEL)),
            resident((D_MODEL, 2 * FF_HIDDEN)),
            resident((FF_HIDDEN, D_MODEL)),
        ],
        out_specs=pl.BlockSpec((tm, D_MODEL), lambda i: (i, 0)),
        out_shape=jax.ShapeDtypeStruct((n, D_MODEL), F32),
        compiler_params=_params(1),
        name="post",
    )(x, mna, mgq, mml, mods, g_norm, w_out, w_gu, w_down)


def _pair_rms(x, gain, sums_on_mxu=False):
    xsq = x * x
    if sums_on_mxu:
        row = lax.broadcasted_iota(jnp.int32, (LANES, LANES), 0) < HEAD_DIM
        col = lax.broadcasted_iota(jnp.int32, (LANES, LANES), 1) < HEAD_DIM
        ones_blk = jnp.where(row == col, 1.0, 0.0).astype(BF16)
        sums = sum(jnp.dot(p, ones_blk, preferred_element_type=F32) for p in _split3(xsq))
    else:
        first = _lane_first(x.shape)
        sums = jnp.where(first, jnp.sum(jnp.where(first, xsq, 0.0), axis=-1, keepdims=True),
                         jnp.sum(jnp.where(first, 0.0, xsq), axis=-1, keepdims=True))
    return x * lax.rsqrt(sums * (1.0 / HEAD_DIM) + EPS) * gain


def _pair_gain(gqk_ref, row):
    return jnp.concatenate([gqk_ref[row:row + 1, :]] * (LANES // HEAD_DIM), axis=-1)


def _zero_other_layers(ref, layer):
    for other in range(ref.shape[0]):
        if other != layer:
            ref[other] = jnp.zeros(ref.shape[1:], ref.dtype)
    return ref.at[layer]


def _ctx_attn_kernel(zna_ref, zgq_ref, gqk_ref, *rest, layer, creates):
    mna_ref, mgq_ref, kvna_ref, kvgq_ref = rest[-4:]
    if creates:
        kvna_ref, kvgq_ref = _zero_other_layers(kvna_ref, layer), _zero_other_layers(kvgq_ref, layer)

    def store_t(ref, which, pair, x2):
        xt = x2.T
        for a in range(2):
            ref[which, 2 * pair + a] = xt[a * HEAD_DIM:(a + 1) * HEAD_DIM]

    pending = []
    for i in range(NA_HEADS // 2):
        cols = slice(i * LANES, (i + 1) * LANES)
        q2 = zna_ref[:, cols] * QK_SCALE
        k2 = zna_ref[:, NA_W + i * LANES:NA_W + (i + 1) * LANES]
        v2 = zna_ref[:, 2 * NA_W + i * LANES:2 * NA_W + (i + 1) * LANES]
        store_t(kvna_ref, 0, i, k2)
        store_t(kvna_ref, 1, i, v2)
        blocks = [{"k": k2.astype(BF16), "k_nt": True, "r": _pair_values(v2), "r_nt": False}]
        pending.append((mna_ref, cols, _pair_scores(_pair_queries(q2), blocks), blocks))

    gq, gk = _pair_gain(gqk_ref, 0), _pair_gain(gqk_ref, 1)
    k2 = _pair_rms(zgq_ref[:, GQA_QW:GQA_QW + GQA_KW], gk)
    v2 = zgq_ref[:, GQA_QW + GQA_KW:]
    store_t(kvgq_ref, 0, 0, k2)
    store_t(kvgq_ref, 1, 0, v2)
    blocks = [{"k": k2.astype(BF16), "k_nt": True, "r": _pair_values(v2), "r_nt": False}]
    for p in range(GQA_GROUP):
        cols = slice(p * LANES, (p + 1) * LANES)
        q2 = _pair_rms(zgq_ref[:, cols], gq) * QK_SCALE
        pending.append((mgq_ref, cols, _pair_scores(_pair_queries(q2), blocks), blocks))

    for ref, cols, scores, blocks in pending:
        ref[:, cols] = _pair_finish(scores, blocks).astype(BF16)


def _ctx_attn(zna, zgq, g_qk, kv_na_buf, kv_gq_buf, layer, batch, t):
    n = zna.shape[0]
    creates = kv_na_buf is None
    in_specs = [
        pl.BlockSpec((t, ZNA_W), lambda b: (b, 0)),
        pl.BlockSpec((t, ZGQ_W), lambda b: (b, 0)),
        pl.BlockSpec((None, 2, HEAD_DIM), lambda b: (layer, 0, 0)),
    ]
    args = [zna, zgq, g_qk]
    if creates:
        kv_spec = lambda heads: pl.BlockSpec((None, DEPTH, 2, heads, HEAD_DIM, t), lambda b: (b, 0, 0, 0, 0, 0))
        aliases = {}
    else:
        kv_spec = lambda heads: pl.BlockSpec((None, None, 2, heads, HEAD_DIM, t),
                                             lambda b: (b, layer, 0, 0, 0, 0))
        aliases = {len(args): 2, len(args) + 1: 3}
        in_specs += [pl.BlockSpec(memory_space=pl.ANY)] * 2
        args += [kv_na_buf, kv_gq_buf]
    return pl.pallas_call(
        functools.partial(_ctx_attn_kernel, layer=layer, creates=creates),
        grid=(batch,),
        in_specs=in_specs,
        out_specs=[
            pl.BlockSpec((t, NA_W), lambda b: (b, 0)),
            pl.BlockSpec((t, GQA_QW), lambda b: (b, 0)),
            kv_spec(NA_HEADS),
            kv_spec(GQA_KV_HEADS),
        ],
        out_shape=[
            jax.ShapeDtypeStruct((n, NA_W), BF16),
            jax.ShapeDtypeStruct((n, GQA_QW), BF16),
            jax.ShapeDtypeStruct((batch, DEPTH, 2, NA_HEADS, HEAD_DIM, t), F32),
            jax.ShapeDtypeStruct((batch, DEPTH, 2, GQA_KV_HEADS, HEAD_DIM, t), F32),
        ],
        input_output_aliases=aliases,
        compiler_params=_params(1),
        name="ctx_attn",
    )(*args)


def _lat_na_kernel(zna_ref, cache_ref, bias_ref, o_ref, *, rows):
    qn = NA_BLOCK_ROWS * GRID_W
    kn = NA_UNION_ROWS * GRID_W
    n_blocks = rows // NA_BLOCK_ROWS
    pattern_of = _na_block_patterns(rows)[1]
    prepared = []
    for i in range(NA_HEADS // 2):
        cols = slice(i * LANES, (i + 1) * LANES)
        q_pair = _pair_queries(zna_ref[:, cols] * QK_SCALE)
        k2 = zna_ref[:, NA_W + i * LANES:NA_W + (i + 1) * LANES].astype(BF16)
        kc = cache_ref[0, i].astype(BF16)
        scores = []
        for a in range(2):
            s_ctx = jnp.dot(q_pair[a], kc, preferred_element_type=F32)
            s_win = []
            for p in range(n_blocks):
                k0 = _na_union_start(p, rows) * GRID_W
                s_win.append(lax.dot_general(q_pair[a][p * qn:(p + 1) * qn], k2[k0:k0 + kn], NT_DIMS,
                                             preferred_element_type=F32) + bias_ref[2 * i + a, pattern_of[p]])
            scores.append((s_ctx, s_win))
        prepared.append(scores)
    for i in range(NA_HEADS // 2):
        cols = slice(i * LANES, (i + 1) * LANES)
        r = _pair_values(zna_ref[:, 2 * NA_W + i * LANES:2 * NA_W + (i + 1) * LANES])
        rc = _pair_values_t(cache_ref[1, i])
        acc = None
        for a in range(2):
            s_ctx, s_win = prepared[i][a]
            m_win = jnp.concatenate([s.max(axis=-1, keepdims=True) for s in s_win], axis=0)
            m = jnp.maximum(m_win, s_ctx.max(axis=-1, keepdims=True))
            term = lax.dot_general(jnp.exp(s_ctx - m).astype(BF16), rc[a], NT_DIMS, preferred_element_type=F32)
            wins = []
            for p in range(n_blocks):
                k0 = _na_union_start(p, rows) * GRID_W
                pw = jnp.exp(s_win[p] - m[p * qn:(p + 1) * qn]).astype(BF16)
                wins.append(jnp.dot(pw, r[a][k0:k0 + kn], preferred_element_type=F32))
            term = term + jnp.concatenate(wins, axis=0)
            acc = term if acc is None else acc + term
        o_ref[:, cols] = (acc[:, :LANES] / acc[:, LANES:]).astype(BF16)


def _rope_tables(t):
    half = HEAD_DIM // 2
    quarter = half // 2
    inv = 1.0 / (ROPE_BASE ** (jnp.arange(quarter, dtype=F32) / quarter))
    tt = jnp.arange(t)
    row = (tt // GRID_W).astype(F32)
    col = (tt % GRID_W).astype(F32)
    ang_r = row[:, None] * inv[None, :]
    ang_c = col[:, None] * inv[None, :]
    cos = jnp.concatenate([jnp.cos(ang_r)] * 2 + [jnp.cos(ang_c)] * 2, axis=-1)
    sin = jnp.concatenate([-jnp.sin(ang_r), jnp.sin(ang_r), -jnp.sin(ang_c), jnp.sin(ang_c)], axis=-1)
    reps = LANES // HEAD_DIM
    return jnp.tile(cos, (1, reps)), jnp.tile(sin, (1, reps))


def _pair_rope(xn, cos, sin):
    quarter = HEAD_DIM // 4
    lane = lax.broadcasted_iota(jnp.int32, xn.shape, 1)
    lower = (lane & (2 * quarter - 1)) < quarter
    partner = jnp.where(lower, pltpu.roll(xn, LANES - quarter, 1), pltpu.roll(xn, quarter, 1))
    return xn * cos + partner * sin


def _lat_gqa_kernel(zgq_ref, cache_ref, gqk_ref, cos_ref, sin_ref, o_ref):
    cos = cos_ref[...]
    sin = sin_ref[...]
    gq, gk = _pair_gain(gqk_ref, 0), _pair_gain(gqk_ref, 1)
    keys = _pair_rope(_pair_rms(zgq_ref[:, GQA_QW:GQA_QW + GQA_KW], gk, True), cos, sin).astype(BF16)
    blocks = [
        {"k": keys, "k_nt": True, "r": _pair_values(zgq_ref[:, GQA_QW + GQA_KW:]), "r_nt": False},
        {"k": cache_ref[0].astype(BF16), "k_nt": False, "r": _pair_values_t(cache_ref[1]), "r_nt": True},
    ]
    def scores(p):
        x = _pair_rope(_pair_rms(zgq_ref[:, p * LANES:(p + 1) * LANES], gq, True), cos, sin) * QK_SCALE
        return _pair_scores(_pair_queries(x), blocks)

    nxt = scores(0)
    for p in range(GQA_GROUP):
        cur = nxt
        if p + 1 < GQA_GROUP:
            nxt = scores(p + 1)
        o_ref[:, p * LANES:(p + 1) * LANES] = _pair_finish(cur, blocks).astype(BF16)


def _lat_attn_kernel(zna_ref, cna_ref, bias_ref, zgq_ref, cgq_ref, gqk_ref, cos_ref, sin_ref, ona_ref, ogq_ref,
                     *, rows):
    _lat_na_kernel(zna_ref, cna_ref, bias_ref, ona_ref, rows=rows)
    _lat_gqa_kernel(zgq_ref, cgq_ref, gqk_ref, cos_ref, sin_ref, ogq_ref)


def _lat_attn(zna, zgq, cache_na_t, cache_gq_t, bias, g_qk, cos, sin, layer, batch, t):
    n = zna.shape[0]
    rows = t // GRID_W
    past = cache_na_t.shape[-1]
    n_patterns = len(_na_block_patterns(rows)[0])
    qn = NA_BLOCK_ROWS * GRID_W
    kn = NA_UNION_ROWS * GRID_W
    return pl.pallas_call(
        functools.partial(_lat_attn_kernel, rows=rows),
        grid=(batch,),
        in_specs=[
            pl.BlockSpec((t, ZNA_W), lambda b: (b, 0)),
            pl.BlockSpec((None, None, 2, NA_HEADS // 2, LANES, past), lambda b: (b, layer, 0, 0, 0, 0)),
            pl.BlockSpec((None, NA_HEADS, n_patterns, qn, kn), lambda b: (layer, 0, 0, 0, 0)),
            pl.BlockSpec((t, ZGQ_W), lambda b: (b, 0)),
            pl.BlockSpec((None, None, 2, GQA_KW, past), lambda b: (b, layer, 0, 0, 0)),
            pl.BlockSpec((None, 2, HEAD_DIM), lambda b: (layer, 0, 0)),
            pl.BlockSpec((t, LANES), lambda b: (0, 0)),
            pl.BlockSpec((t, LANES), lambda b: (0, 0)),
        ],
        out_specs=[pl.BlockSpec((t, NA_W), lambda b: (b, 0)), pl.BlockSpec((t, GQA_QW), lambda b: (b, 0))],
        out_shape=[jax.ShapeDtypeStruct((n, NA_W), BF16), jax.ShapeDtypeStruct((n, GQA_QW), BF16)],
        compiler_params=_params(1),
        name="lat_attn",
    )(zna, cache_na_t, bias, zgq, cache_gq_t, g_qk, cos, sin)


def _split3(x):
    x1 = x.astype(BF16)
    r1 = x - x1.astype(F32)
    x2 = r1.astype(BF16)
    x3 = (r1 - x2.astype(F32)).astype(BF16)
    return x1, x2, x3


def _log_sigmoid(x):
    return jnp.minimum(x, 0.0) - jnp.log1p(jnp.exp(-jnp.abs(x)))


def _mlstm_select_matrix():
    H = MLSTM_HEADS
    sel = np.zeros((MLSTM_SEL_ROWS, (H // 2) * 2 * MLSTM_TILE_KINDS * LANES), np.float32)
    for j in range(H // 2):
        for d in range(2):
            for q in range(MLSTM_TILE_KINDS):
                for a in range(2):
                    col0 = ((j * 2 + d) * MLSTM_TILE_KINDS + q) * LANES + a * HEAD_DIM
                    sel[q * N_SCANS + d * H + 2 * j + a, col0:col0 + HEAD_DIM] = 1.0
    return sel


def _mlstm_kernel(zml_ref, gi_ref, gf_ref, grow_ref, c0_ref, n0_ref, m0_ref, gml_ref, sel_ref, *rest,
                  t, emit_state, creates_layer):
    tri_s, st_s, dst_s, cst_s, row_s = rest[-5:]
    outs = rest[:-5]
    if emit_state:
        o_ref, cf_ref, nf_ref, mf_ref = outs[-4:]
        if creates_layer is not None:
            cf_ref, nf_ref, mf_ref = (_zero_other_layers(r, creates_layer) for r in (cf_ref, nf_ref, mf_ref))
    else:
        o_ref = outs[-1]
    L = MLSTM_CHUNK
    H = MLSTM_HEADS
    HD = HEAD_DIM
    NP = H // 2
    nc = t // L
    tb = tri_s.shape[-1]

    @pl.when(pl.program_id(0) == 0)
    def _():
        ti = lax.broadcasted_iota(jnp.int32, (tb, tb), 0)
        ui = lax.broadcasted_iota(jnp.int32, (tb, tb), 1)
        same = (ti & -L) == (ui & -L)
        tri_s[0] = jnp.where(same & (ui <= ti), 1.0, 0.0).astype(BF16)
        tri_s[1] = jnp.where(same & (ui >= ti), 1.0, 0.0).astype(BF16)

    lower, upper = tri_s[0], tri_s[1]

    def chunk_sums_cols(x):
        parts = _split3(x)
        pre, suf = [], []
        for i in range(t // tb):
            blk = [p[i * tb:(i + 1) * tb] for p in parts]
            pre.append(sum(jnp.dot(lower, p, preferred_element_type=F32) for p in blk))
            suf.append(sum(jnp.dot(upper, p, preferred_element_type=F32) for p in blk))
        return jnp.concatenate(pre, axis=0), jnp.concatenate(suf, axis=0)

    def chunk_sums_rows(x):
        parts = _split3(x)
        pre, suf = [], []
        for i in range(t // tb):
            blk = [p[:, i * tb:(i + 1) * tb] for p in parts]
            pre.append(sum(jnp.dot(p, upper, preferred_element_type=F32) for p in blk))
            suf.append(sum(jnp.dot(p, lower, preferred_element_type=F32) for p in blk))
        return jnp.concatenate(pre, axis=1), jnp.concatenate(suf, axis=1)

    pre_c, suf_c = chunk_sums_cols(_log_sigmoid(gf_ref[...]))
    lane_c = lax.broadcasted_iota(jnp.int32, (t, N_SCANS), 1)
    b3 = jnp.where(lane_c < H, pre_c, suf_c).reshape(nc, L, N_SCANS)
    i3 = gi_ref[...].reshape(nc, L, N_SCANS)
    fwd3 = lax.broadcasted_iota(jnp.int32, (nc, 1, N_SCANS), 2) < H
    b_end3 = jnp.where(fwd3, b3[:, L - 1:L, :], b3[:, 0:1, :])
    lw_end3 = b_end3 - b3 + i3
    a3 = jnp.max(lw_end3, axis=1, keepdims=True)
    wloc3 = jnp.exp(lw_end3 - a3)

    fwd1 = lax.broadcasted_iota(jnp.int32, (1, N_SCANS), 1) < H
    m = m0_ref[...]
    m_start, carry_decay, contrib_scale = [], [], []
    for j in range(nc):
        a_j = jnp.where(fwd1, a3[j], a3[nc - 1 - j])
        g_j = jnp.where(fwd1, b_end3[j], b_end3[nc - 1 - j])
        m_start.append(m)
        m_next = jnp.maximum(g_j + m, a_j)
        carry_decay.append(jnp.exp(g_j + m - m_next))
        contrib_scale.append(jnp.exp(a_j - m_next))
        m = m_next
    mst3 = jnp.concatenate([jnp.where(fwd1, m_start[c], m_start[nc - 1 - c])[None] for c in range(nc)], axis=0)

    cols = jnp.concatenate([b3.reshape(t, N_SCANS), wloc3.reshape(t, N_SCANS)], axis=1)
    tiles_all = sum(jnp.dot(p, sel_ref[...], preferred_element_type=F32) for p in _split3(cols))
    tile_w = MLSTM_TILE_KINDS * LANES

    def tiles(j, d):
        x = tiles_all[:, (2 * j + d) * tile_w:(2 * j + d + 1) * tile_w]
        return [x[:, q * LANES:(q + 1) * LANES].reshape(nc, L, LANES) for q in range(MLSTM_TILE_KINDS)]

    gr = grow_ref[...]
    pre_r, suf_r = chunk_sums_rows(_log_sigmoid(gr))
    sub_r = lax.broadcasted_iota(jnp.int32, (N_SCANS, t), 0)
    rowv = gr[0:N_SCANS] - jnp.where(sub_r < H, pre_r[N_SCANS:], suf_r[N_SCANS:])
    for j in range(NP):
        for d in range(2):
            e = d * H + 2 * j
            for c in range(nc):
                row_s[2 * j + d, c] = jnp.concatenate(
                    [rowv[e:e + 1, c * L:(c + 1) * L], rowv[e + 1:e + 2, c * L:(c + 1) * L]], axis=1)

    lane_a = lax.broadcasted_iota(jnp.int32, (1, 1, LANES), 2) < HD
    sub_a = lax.broadcasted_iota(jnp.int32, (1, 2 * HD, 1), 1) < HD
    diag = sub_a == lane_a
    diag4 = jnp.concatenate([diag] * 4, axis=2)

    def stack_heads(x3):
        return jnp.concatenate([jnp.where(lane_a, x3, 0.0), jnp.where(lane_a, 0.0, x3)], axis=1)

    def pair_cols(base, j):
        return slice(base + j * LANES, base + (j + 1) * LANES)

    zero_blk = jnp.zeros((HD, HD), F32)
    for j in range(NP):
        cols_d = []
        for d in range(2):
            ca, cb = c0_ref[d, 2 * j].T, c0_ref[d, 2 * j + 1].T
            na = jnp.broadcast_to(n0_ref[d, 2 * j:2 * j + 1, :], (HD, HD)).T
            nb = jnp.broadcast_to(n0_ref[d, 2 * j + 1:2 * j + 2, :], (HD, HD)).T
            top = jnp.concatenate([ca, zero_blk, na, zero_blk], axis=1)
            bot = jnp.concatenate([zero_blk, cb, zero_blk, nb], axis=1)
            cols_d.append(jnp.concatenate([top, bot], axis=0))
        st_s[j] = jnp.concatenate(cols_d, axis=1)

    for j in range(NP):
        k3 = (zml_ref[:, pair_cols(ML_W, j)] * QK_SCALE).reshape(nc, L, LANES).astype(BF16)
        v3 = zml_ref[:, pair_cols(2 * ML_W, j)].reshape(nc, L, LANES)
        rhs = []
        for d in range(2):
            wl = tiles(j, d)[1]
            rhs += [v3 * wl, wl]
        rhs = jnp.concatenate(rhs, axis=2).astype(BF16)
        contrib = jnp.einsum("csk,csn->ckn", k3, rhs, preferred_element_type=F32)
        dst_s[j] = jnp.where(diag4, contrib, 0.0)

    def lane_scale(v, j):
        pieces = []
        for d in range(2):
            sa = jnp.broadcast_to(v[:, d * H + 2 * j:d * H + 2 * j + 1], (1, HD))
            sb = jnp.broadcast_to(v[:, d * H + 2 * j + 1:d * H + 2 * j + 2], (1, HD))
            pieces += [sa, sb, sa, sb]
        return jnp.concatenate(pieces, axis=1)

    for j in range(NP):
        st = st_s[j]
        for step in range(nc):
            cb = nc - 1 - step
            stb = st.astype(BF16)
            cst_s[j, step, :, 0:2 * LANES] = stb[:, 0:2 * LANES]
            cst_s[j, cb, :, 2 * LANES:] = stb[:, 2 * LANES:]
            delta = jnp.concatenate([dst_s[j, step, :, 0:2 * LANES], dst_s[j, cb, :, 2 * LANES:]], axis=1)
            st = lane_scale(carry_decay[step], j) * st + lane_scale(contrib_scale[step], j) * delta
        st_s[j] = st

    sidx = lax.broadcasted_iota(jnp.int32, (1, L, LANES), 2) & (HD - 1)
    tidx = lax.broadcasted_iota(jnp.int32, (1, L, LANES), 1)
    masks = (sidx <= tidx, sidx >= tidx)
    ones_blk = jnp.broadcast_to(jnp.where(diag, 1.0, 0.0).astype(BF16), (nc, 2 * HD, LANES))
    neg_inf = -jnp.inf
    for j in range(NP):
        q3 = zml_ref[:, pair_cols(0, j)].reshape(nc, L, LANES).astype(BF16)
        k3 = (zml_ref[:, pair_cols(ML_W, j)] * QK_SCALE).reshape(nc, L, LANES)
        v3 = zml_ref[:, pair_cols(2 * ML_W, j)].reshape(nc, L, LANES)
        qk = jnp.einsum("ctd,cnd->ctn", q3, stack_heads(k3).astype(BF16), preferred_element_type=F32)
        v_aug = jnp.concatenate([stack_heads(v3).astype(BF16), ones_blk], axis=2)
        out = None
        for d in range(2):
            b_t = tiles(j, d)[0]
            e = d * H + 2 * j
            bm_t = b_t + jnp.where(lane_a, mst3[:, :, e:e + 1], mst3[:, :, e + 1:e + 2])
            logw = jnp.where(masks[d], b_t + row_s[2 * j + d], neg_inf)
            rmax_a = jnp.max(jnp.where(lane_a, logw, neg_inf), axis=-1, keepdims=True)
            rmax_b = jnp.max(jnp.where(lane_a, neg_inf, logw), axis=-1, keepdims=True)
            m_t = jnp.maximum(jnp.where(lane_a, rmax_a, rmax_b), bm_t)
            s = qk * jnp.exp(logw - m_t)
            decay = jnp.exp(bm_t - m_t)
            sv = jnp.einsum("cts,csn->ctn", s.astype(BF16), v_aug, preferred_element_type=F32)
            state = cst_s[j, :, :, 2 * d * LANES:2 * (d + 1) * LANES]
            inter = jnp.einsum("ctk,ckn->ctn", q3, state, preferred_element_type=F32)
            num = sv[:, :, 0:LANES] + decay * inter[:, :, 0:LANES]
            den = sv[:, :, LANES:] + decay * inter[:, :, LANES:]
            h_d = num / jnp.maximum(jnp.abs(den), jnp.exp(-m_t))
            out = h_d if out is None else out + h_d
        cols = pair_cols(0, j)
        og = jax.nn.sigmoid(zml_ref[:, pair_cols(3 * ML_W, j)])
        o_ref[:, cols] = (_pair_rms(out.reshape(t, LANES), gml_ref[:, cols]) * og).astype(BF16)

    if emit_state:
        for j in range(NP):
            st = st_s[j]
            for d in range(2):
                for a in range(2):
                    rows = slice(a * HD, (a + 1) * HD)
                    c0 = 2 * d * LANES + a * HD
                    cf_ref[d, 2 * j + a] = st[rows, c0:c0 + HD].T
                    nf_ref[d, 2 * j + a:2 * j + a + 1, :] = st[rows, c0 + LANES:c0 + LANES + HD].T[0:1, :]
        mf_ref[...] = m


def _mlstm(zml, gi, gf, grow, c0, n0, m0, g_ml, layer, batch, t, state_out, state_layer):
    emit_state = state_out is not None
    creates = emit_state and not isinstance(state_out, tuple)
    n = zml.shape[0]
    H = MLSTM_HEADS
    L = MLSTM_CHUNK
    nc = t // L
    sel = jnp.asarray(_mlstm_select_matrix(), BF16)
    tri_block = min(t, MXU_DIM)
    assert L & (L - 1) == 0 and tri_block % L == 0 and t % tri_block == 0
    if state_layer is None:
        c_spec = pl.BlockSpec((None, 2, H, HEAD_DIM, HEAD_DIM), lambda b: (0, 0, 0, 0, 0))
        n_spec = pl.BlockSpec((None, 2, H, HEAD_DIM), lambda b: (0, 0, 0, 0))
        m_spec = pl.BlockSpec((None, 1, N_SCANS), lambda b: (0, 0, 0))
    else:
        c_spec = pl.BlockSpec((None, None, 2, H, HEAD_DIM, HEAD_DIM), lambda b: (b, state_layer, 0, 0, 0, 0))
        n_spec = pl.BlockSpec((None, None, 2, H, HEAD_DIM), lambda b: (b, state_layer, 0, 0, 0))
        m_spec = pl.BlockSpec((None, None, 1, N_SCANS), lambda b: (b, state_layer, 0, 0))
    in_specs = [
        pl.BlockSpec((t, ZML_W), lambda b: (b, 0)),
        pl.BlockSpec((t, N_SCANS), lambda b: (b, 0)),
        pl.BlockSpec((t, N_SCANS), lambda b: (b, 0)),
        pl.BlockSpec((N_GATES, t), lambda b: (0, b)),
        c_spec, n_spec, m_spec,
        pl.BlockSpec((None, 1, ML_W), lambda b: (layer, 0, 0)),
        pl.BlockSpec(sel.shape, lambda b: (0, 0)),
    ]
    args = [zml, gi, gf, grow, c0, n0, m0, g_ml, sel]
    out_specs = [pl.BlockSpec((t, ML_W), lambda b: (b, 0))]
    out_shape = [jax.ShapeDtypeStruct((n, ML_W), BF16)]
    aliases = {}
    if emit_state:
        tails = [(2, H, HEAD_DIM, HEAD_DIM), (2, H, HEAD_DIM), (1, N_SCANS)]
        for tail in tails:
            zeros = (0,) * len(tail)
            if creates:
                out_specs.append(pl.BlockSpec((None, DEPTH) + tail, lambda b, zeros=zeros: (b, 0) + zeros))
            else:
                out_specs.append(pl.BlockSpec((None, None) + tail, lambda b, zeros=zeros: (b, layer) + zeros))
            out_shape.append(jax.ShapeDtypeStruct((batch, DEPTH) + tail, F32))
        if not creates:
            aliases = {len(args) + i: 1 + i for i in range(len(state_out))}
            in_specs += [pl.BlockSpec(memory_space=pl.ANY)] * len(state_out)
            args += list(state_out)
    return pl.pallas_call(
        functools.partial(_mlstm_kernel, t=t, emit_state=emit_state, creates_layer=layer if creates else None),
        grid=(batch,),
        in_specs=in_specs,
        out_specs=out_specs,
        out_shape=out_shape,
        input_output_aliases=aliases,
        scratch_shapes=[
            pltpu.VMEM((2, tri_block, tri_block), BF16),
            pltpu.VMEM((H // 2, 2 * HEAD_DIM, 4 * LANES), F32),
            pltpu.VMEM((H // 2, nc, 2 * HEAD_DIM, 4 * LANES), F32),
            pltpu.VMEM((H // 2, nc, 2 * HEAD_DIM, 4 * LANES), BF16),
            pltpu.VMEM((H, nc, 1, LANES), F32),
        ],
        compiler_params=_params(1),
        name="mlstm",
    )(*args)


def _layer_path(x, mods, layer, first_row, tiles_row_tokens, weights, mixers):
    per_row = lambda tile: None if tiles_row_tokens is None else tiles_row_tokens // tile
    g_norm = weights["g_norm"]
    in_tile = INPROJ_TILE if x.shape[0] >= INPROJ_MIN_STEPS * INPROJ_TILE else INPROJ_TILE // 2
    zna, zgq, zml, gi, gf, grow = _inproj(x, mods, g_norm, weights["w_in"], layer, per_row(in_tile),
                                          first_row, in_tile)
    mna, mgq, mml, extra = mixers(zna, zgq, zml, gi, gf, grow)
    x = _post(x, mna, mgq, mml, mods, g_norm, weights["w_out"], weights["w_gu"], weights["w_down"], layer,
              per_row(POST_TILE), first_row, POST_TILE)
    return x, extra


def _gqa_pair_order():
    return [a * GQA_GROUP + p for p in range(GQA_GROUP) for a in range(GQA_KV_HEADS)]


def _take_blocks(x, axis, base, width, order):
    return jnp.concatenate([lax.slice_in_dim(x, base + width * o, base + width * (o + 1), axis=axis)
                            for o in order], axis=axis)


def kernel(x_prompt, x_sample, cache_na_kv, cache_gqa_kv, state_mlstm_C, state_mlstm_n, state_mlstm_m,
           c, c_ctx, w_in, b_gates, w_out, g_norm, g_qk, g_mlstm, na_bias, w_ada, b_ada, w_gu, w_down):
    batch, seq, _ = x_prompt.shape
    dec_batch, dec_seq, _ = x_sample.shape
    past = cache_na_kv.shape[-2]
    assert dec_batch + 1 <= N_MOD_ROWS and dec_seq % GRID_W == 0 and GQA_KV_HEADS == 2

    cvec = jnp.concatenate([c_ctx[None, :], c, jnp.zeros((N_MOD_ROWS - 1 - dec_batch, D_MODEL), F32)], axis=0)
    mods = _adaln(cvec, w_ada, b_ada).reshape(DEPTH, N_MOD_ROWS, 6, D_MODEL)
    bias = _na_bias_expand(na_bias, dec_seq // GRID_W)
    cos, sin = _rope_tables(dec_seq)

    pair_order = _gqa_pair_order()
    w_in_t = jnp.swapaxes(w_in, 1, 2).astype(BF16)
    w_gq = jnp.concatenate([_take_blocks(w_in_t, 1, OFF_GQ, HEAD_DIM, pair_order),
                            w_in_t[:, OFF_GQ + GQA_QW:OFF_ML]], axis=1)
    w_out_rows = jnp.concatenate([w_out[:, :NA_W], _take_blocks(w_out, 1, NA_W, HEAD_DIM, pair_order),
                                  w_out[:, NA_W + GQA_QW:]], axis=1)
    weights = {
        "g_norm": g_norm,
        "w_in": {
            "all": w_in_t,
            "gq": w_gq,
            "b_row": b_gates[:, :, None],
        },
        "w_out": w_out_rows.astype(BF16),
        "w_gu": w_gu.astype(BF16),
        "w_down": w_down.astype(BF16),
    }
    g_ml = g_mlstm.reshape(DEPTH, 1, ML_W)
    zero_c = jnp.zeros((1, 2, MLSTM_HEADS, HEAD_DIM, HEAD_DIM), F32)
    zero_n = jnp.zeros((1, 2, MLSTM_HEADS, HEAD_DIM), F32)
    zero_m = jnp.zeros((1, 1, N_SCANS), F32)
    m0_lat = state_mlstm_m.reshape(dec_batch, DEPTH, 1, N_SCANS)
    cache_na_t = jnp.swapaxes(cache_na_kv, -1, -2).reshape(dec_batch, DEPTH, 2, NA_HEADS // 2, LANES, past)
    cache_gq_t = jnp.swapaxes(cache_gqa_kv, -1, -2).reshape(dec_batch, DEPTH, 2, GQA_KW, past)

    xp = x_prompt.reshape(batch * seq, D_MODEL)
    xs = x_sample.reshape(dec_batch * dec_seq, D_MODEL)
    kv_na = kv_gq = None
    states = "create"
    for layer in range(DEPTH):
        def ctx_mixers(zna, zgq, zml, gi, gf, grow, layer=layer, kv_na=kv_na, kv_gq=kv_gq, states=states):
            mna, mgq, kv_na, kv_gq = _ctx_attn(zna, zgq, g_qk, kv_na, kv_gq, layer, batch, seq)
            mml, *states = _mlstm(zml, gi, gf, grow, zero_c, zero_n, zero_m, g_ml, layer, batch, seq,
                                  states, None)
            return mna, mgq, mml, (kv_na, kv_gq, tuple(states))

        def lat_mixers(zna, zgq, zml, gi, gf, grow, layer=layer):
            mna, mgq = _lat_attn(zna, zgq, cache_na_t, cache_gq_t, bias, g_qk, cos, sin, layer, dec_batch, dec_seq)
            (mml,) = _mlstm(zml, gi, gf, grow, state_mlstm_C, state_mlstm_n, m0_lat, g_ml, layer,
                            dec_batch, dec_seq, None, layer)
            return mna, mgq, mml, None

        xp, (kv_na, kv_gq, states) = _layer_path(xp, mods, layer, 0, None, weights, ctx_mixers)
        xs, _ = _layer_path(xs, mods, layer, 1, dec_seq, weights, lat_mixers)

    new_c, new_n, new_m = states
    return (xp.reshape(batch, seq, D_MODEL), xs.reshape(dec_batch, dec_seq, D_MODEL),
            jnp.swapaxes(kv_na, -1, -2), jnp.swapaxes(kv_gq, -1, -2),
            new_c, new_n, new_m.reshape(batch, DEPTH, 2, MLSTM_HEADS))
```
